```python
import math
import jax, jax.numpy as jnp
from jax import lax
import numpy as np

D_MODEL = 1024
BATCH = 8
SEQ = 4096
DEPTH = 2

NSA_Q_HEADS = 8
NSA_KV_HEADS = 2
HEAD_DIM = 64
NSA_GROUP = NSA_Q_HEADS // NSA_KV_HEADS
CMP_LEN = 32
CMP_STRIDE = 16
CMP_HIDDEN = 256
SEL_LEN = 64
SEL_TOPN = 16
WINDOW = 512
NSA_QBLOCK = 64
ROPE_THETA = 10000.0
Q_WIDTH = NSA_Q_HEADS * HEAD_DIM
KV_WIDTH = NSA_KV_HEADS * HEAD_DIM
GM_GROUPS = 8
GM_GROUP_DIM = 64
GM_WIDTH = GM_GROUPS * GM_GROUP_DIM
GM_CHUNK = 128
N_EXPERTS = 256
TOP_K = 8
D_EXPERT = 256
D_SHARED = 256
ROUTE_SCALE = 2.5
MOE_BLOCK = 128
MOE_BLOCKS_PER_STEP = 8
SPLIT_SIZES = (Q_WIDTH, KV_WIDTH, KV_WIDTH, KV_WIDTH, KV_WIDTH, KV_WIDTH, KV_WIDTH, NSA_Q_HEADS * 3, GM_WIDTH, GM_WIDTH, D_MODEL, D_MODEL)
IN_COLS = Q_WIDTH + 6 * KV_WIDTH + NSA_Q_HEADS * 3 + 2 * GM_WIDTH + 2 * D_MODEL
EPS = 1e-6

kernel_name = "hybrid_nsa_gmlp_moe_adaln"


def _rms(x):
    xf = x.astype(jnp.float32)
    return (xf * lax.rsqrt(jnp.mean(xf * xf, -1, keepdims=True) + EPS)).astype(x.dtype)


def _rope(x, pos):
    half = HEAD_DIM // 2
    inv = ROPE_THETA ** (-jnp.arange(half, dtype=jnp.float32) / half)
    ang = pos.astype(jnp.float32)[..., None] * inv
    cos = jnp.cos(ang)[:, :, None, :]
    sin = jnp.sin(ang)[:, :, None, :]
    x1 = x[..., :half].astype(jnp.float32)
    x2 = x[..., half:].astype(jnp.float32)
    return jnp.concatenate([x1 * cos - x2 * sin, x2 * cos + x1 * sin], -1).astype(x.dtype)


def _masked_softmax(s, mask):
    s = jnp.where(mask, s.astype(jnp.float32), -jnp.inf)
    m = jnp.max(s, -1, keepdims=True)
    m = jnp.where(jnp.isfinite(m), m, 0.0)
    e = jnp.where(mask, jnp.exp(s - m), 0.0)
    return e / jnp.maximum(jnp.sum(e, -1, keepdims=True), 1e-30)


def _nsa(q, k_cmp, v_cmp, k_slc, v_slc, k_win, v_win, gate_logits, pos, q_gain, k_gain, pos_k, pos_v, w1k, w2k, w1v, w2v):
    B, S = q.shape[:2]
    Hkv, G, hd, QB = NSA_KV_HEADS, NSA_GROUP, HEAD_DIM, NSA_QBLOCK
    q = _rms(q) * q_gain
    q_rot = _rope(q, pos)
    k_slc = _rope(_rms(k_slc) * k_gain, pos)
    k_win = _rope(_rms(k_win) * k_gain, pos)

    n_cmp = (S - CMP_LEN) // CMP_STRIDE + 1
    blk = np.arange(n_cmp)[:, None] * CMP_STRIDE + np.arange(CMP_LEN)[None, :]

    def compress(t, pe, w1, w2):
        tb = t[:, blk] + pe[None, None, :, None, :]
        tb = jnp.swapaxes(tb, 2, 3).reshape(B, n_cmp, Hkv, CMP_LEN * hd)
        return jax.nn.gelu(tb @ w1) @ w2

    kc = _rms(compress(k_cmp, pos_k, w1k, w2k)) * k_gain
    vc = compress(v_cmp, pos_v, w1v, w2v)
    cmp_start = blk[:, 0]
    cmp_end = blk[:, -1]

    n_sel = S // SEL_LEN
    top_n = min(SEL_TOPN, n_sel)
    sel_start = np.arange(n_sel) * SEL_LEN
    overlap = jnp.asarray(((cmp_start[:, None] <= sel_start[None, :] + SEL_LEN - 1) & (cmp_end[:, None] >= sel_start[None, :])).astype(np.float32))
    kb = k_slc.reshape(B, n_sel, SEL_LEN, Hkv, hd).transpose(0, 3, 1, 2, 4)
    vb = v_slc.reshape(B, n_sel, SEL_LEN, Hkv, hd).transpose(0, 3, 1, 2, 4)

    kw = jnp.pad(k_win, ((0, 0), (WINDOW, 0), (0, 0), (0, 0)))
    vw = jnp.pad(v_win, ((0, 0), (WINDOW, 0), (0, 0), (0, 0)))

    g = jax.nn.sigmoid(gate_logits.astype(jnp.float32)).astype(q.dtype)
    scale = HEAD_DIM ** -0.5
    cmp_end_j = jnp.asarray(cmp_end)
    bi = jnp.arange(B)[:, None, None, None]
    hi = jnp.arange(Hkv)[None, :, None, None]
    jsel = jnp.arange(n_sel)

    def block(q0):
        t = q0 + jnp.arange(QB)
        qn = lax.dynamic_slice_in_dim(q, q0, QB, 1).reshape(B, QB, Hkv, G, hd)
        qr = lax.dynamic_slice_in_dim(q_rot, q0, QB, 1).reshape(B, QB, Hkv, G, hd)
        gb = lax.dynamic_slice_in_dim(g, q0, QB, 1)

        s_c = jnp.einsum('bqkgd,bnkd->bkgqn', qn, kc) * scale
        p_c = _masked_softmax(s_c, cmp_end_j[None, :] <= t[:, None])
        o_c = jnp.einsum('bkgqn,bnkd->bqkgd', p_c.astype(vc.dtype), vc)

        imp = jnp.einsum('bkgqn,nj->bkqj', p_c, overlap)
        cur = t // SEL_LEN
        valid = jsel[None, :] <= cur[:, None]
        forced = (jsel[None, :] == 0) | (jsel[None, :] == cur[:, None]) | (jsel[None, :] == cur[:, None] - 1)
        rank = jnp.where(forced, jnp.inf, jnp.where(valid, imp, -jnp.inf))
        _, idx = lax.top_k(rank, top_n)
        ks = kb[bi, hi, idx]
        vs = vb[bi, hi, idx].reshape(B, Hkv, QB, top_n * SEL_LEN, hd)
        tok = idx[..., None] * SEL_LEN + jnp.arange(SEL_LEN)
        m_s = (tok <= t[None, None, :, None, None]).reshape(B, Hkv, 1, QB, top_n * SEL_LEN)
        s_s = jnp.einsum('bqkgd,bkqnld->bkgqnl', qr, ks).reshape(B, Hkv, G, QB, top_n * SEL_LEN) * scale
        p_s = _masked_softmax(s_s, m_s)
        o_s = jnp.einsum('bkgqm,bkqmd->bqkgd', p_s.astype(vs.dtype), vs)

        kwin = lax.dynamic_slice_in_dim(kw, q0, WINDOW + QB, 1)
        vwin = lax.dynamic_slice_in_dim(vw, q0, WINDOW + QB, 1)
        kpos = q0 - WINDOW + jnp.arange(WINDOW + QB)
        m_w = (kpos[None, :] <= t[:, None]) & (kpos[None, :] > t[:, None] - WINDOW) & (kpos[None, :] >= 0)
        s_w = jnp.einsum('bqkgd,bmkd->bkgqm', qr, kwin) * scale
        p_w = _masked_softmax(s_w, m_w)
        o_w = jnp.einsum('bkgqm,bmkd->bqkgd', p_w.astype(vwin.dtype), vwin)

        shp = (B, QB, NSA_Q_HEADS, hd)
        return (gb[..., 0:1] * o_c.reshape(shp) + gb[..., 1:2] * o_s.reshape(shp) + gb[..., 2:3] * o_w.reshape(shp))

    out = lax.map(block, jnp.arange(S // QB) * QB)
    return jnp.moveaxis(out, 0, 1).reshape(B, S, Q_WIDTH)


def _gmlp(u, v, ln_g, ln_b, ws, bs):
    B, S, _ = v.shape
    vf = v.astype(jnp.float32)
    mu = jnp.mean(vf, -1, keepdims=True)
    var = jnp.mean(jnp.square(vf - mu), -1, keepdims=True)
    vn = ((vf - mu) * lax.rsqrt(var + EPS)).astype(v.dtype) * ln_g + ln_b
    vc = vn.reshape(B, S // GM_CHUNK, GM_CHUNK, GM_GROUPS, GM_GROUP_DIM)
    causal = jnp.tril(jnp.ones((GM_CHUNK, GM_CHUNK), ws.dtype))
    mixed = jnp.einsum('gts,bnsgc->bntgc', ws * causal, vc) + jnp.swapaxes(bs, 0, 1)[None, None, :, :, None]
    return u * mixed.reshape(B, S, GM_WIDTH)


def _moe(h, w_router, b_router, w_gate_e, w_up_e, w_down_e, w_gate_sh, w_up_sh, w_down_sh):
    T, D = h.shape
    aff = jax.nn.sigmoid(jnp.dot(h, w_router).astype(jnp.float32))
    _, idx = lax.top_k(aff + b_router.astype(jnp.float32), TOP_K)
    top = jnp.take_along_axis(aff, idx, -1)
    wts = top / jnp.sum(top, -1, keepdims=True) * ROUTE_SCALE

    n_as = T * TOP_K
    flat_e = idx.reshape(-1)
    order = jnp.argsort(flat_e)
    se = flat_e[order]
    tok = (order // TOP_K).astype(jnp.int32)
    counts = jnp.bincount(flat_e, length=N_EXPERTS)
    padded = (counts + MOE_BLOCK - 1) // MOE_BLOCK * MOE_BLOCK
    pad_end = jnp.cumsum(padded)
    pad_start = pad_end - padded
    grp_start = jnp.cumsum(counts) - counts
    dest = pad_start[se] + jnp.arange(n_as) - grp_start[se]
    step_rows = MOE_BLOCK * MOE_BLOCKS_PER_STEP
    n_rows = -(-(n_as + N_EXPERTS * MOE_BLOCK) // step_rows) * step_rows
    row_tok = jnp.zeros((n_rows,), jnp.int32).at[dest].set(tok)
    row_w = jnp.zeros((n_rows,), jnp.float32).at[dest].set(wts.reshape(-1)[order])
    blk_e = jnp.minimum(jnp.searchsorted(pad_end, jnp.arange(n_rows // MOE_BLOCK) * MOE_BLOCK, side='right'), N_EXPERTS - 1)
    n_steps = n_rows // step_rows
    xs = (row_tok.reshape(n_steps, MOE_BLOCKS_PER_STEP, MOE_BLOCK),
          row_w.reshape(n_steps, MOE_BLOCKS_PER_STEP, MOE_BLOCK),
          blk_e.reshape(n_steps, MOE_BLOCKS_PER_STEP))

    def step(acc, blk):
        rt, rw, be = blk
        xb = h[rt]
        a = jnp.einsum('pnd,pdf->pnf', xb, w_gate_e[be])
        b = jnp.einsum('pnd,pdf->pnf', xb, w_up_e[be])
        y = jnp.einsum('pnf,pfd->pnd', jax.nn.silu(a) * b, w_down_e[be])
        acc = acc.at[rt.reshape(-1)].add((y.astype(jnp.float32) * rw[..., None]).reshape(-1, D))
        return acc, None

    routed, _ = lax.scan(step, jnp.zeros((T, D), jnp.float32), xs)
    shared = (jax.nn.silu(h @ w_gate_sh) * (h @ w_up_sh)) @ w_down_sh
    return routed.astype(h.dtype) + shared


def setup_inputs(seed: int = 0) -> dict:
    key = jax.random.key(seed)
    ks = jax.random.split(key, 32)
    L, D, hd = DEPTH, D_MODEL, HEAD_DIM
    nrm = lambda k, shp, s: jax.random.normal(k, shp, jnp.float32) * s
    return {
        "x": nrm(ks[0], (BATCH, SEQ, D), 1.0),
        "c": nrm(ks[1], (BATCH, D), 1.0),
        "positions": jnp.broadcast_to(jnp.arange(SEQ, dtype=jnp.int32), (BATCH, SEQ)),
        "w_mod": nrm(ks[2], (L, D, 6 * D), 0.5 * D ** -0.5),
        "b_mod": nrm(ks[3], (L, 6 * D), 0.02),
        "w_in": nrm(ks[4], (L, D, IN_COLS), D ** -0.5),
        "q_gain": 1.0 + nrm(ks[5], (L, hd), 0.02),
        "k_gain": 1.0 + nrm(ks[6], (L, hd), 0.02),
        "cmp_pos_k": nrm(ks[7], (L, CMP_LEN, hd), 0.02),
        "cmp_pos_v": nrm(ks[8], (L, CMP_LEN, hd), 0.02),
        "cmp_w1_k": nrm(ks[9], (L, CMP_LEN * hd, CMP_HIDDEN), (CMP_LEN * hd) ** -0.5),
        "cmp_w2_k": nrm(ks[10], (L, CMP_HIDDEN, hd), CMP_HIDDEN ** -0.5),
        "cmp_w1_v": nrm(ks[11], (L, CMP_LEN * hd, CMP_HIDDEN), (CMP_LEN * hd) ** -0.5),
        "cmp_w2_v": nrm(ks[12], (L, CMP_HIDDEN, hd), CMP_HIDDEN ** -0.5),
        "gm_ln_g": 1.0 + nrm(ks[13], (L, GM_WIDTH), 0.02),
        "gm_ln_b": nrm(ks[14], (L, GM_WIDTH), 0.02),
        "gm_ws": nrm(ks[15], (L, GM_GROUPS, GM_CHUNK, GM_CHUNK), GM_CHUNK ** -0.5),
        "gm_bs": 1.0 + nrm(ks[16], (L, GM_GROUPS, GM_CHUNK), 0.02),
        "w_proj_nsa": nrm(ks[17], (L, Q_WIDTH, D), Q_WIDTH ** -0.5),
        "w_proj_gm": nrm(ks[18], (L, GM_WIDTH, D), GM_WIDTH ** -0.5),
        "w_out": nrm(ks[19], (L, D, D), D ** -0.5),
        "w_router": nrm(ks[20], (L, D, N_EXPERTS), D ** -0.5),
        "b_router": nrm(ks[21], (L, N_EXPERTS), 0.01),
        "w_gate_e": nrm(ks[22], (L, N_EXPERTS, D, D_EXPERT), D ** -0.5),
        "w_up_e": nrm(ks[23], (L, N_EXPERTS, D, D_EXPERT), D ** -0.5),
        "w_down_e": nrm(ks[24], (L, N_EXPERTS, D_EXPERT, D), D_EXPERT ** -0.5),
        "w_gate_sh": nrm(ks[25], (L, D, D_SHARED), D ** -0.5),
        "w_up_sh": nrm(ks[26], (L, D, D_SHARED), D ** -0.5),
        "w_down_sh": nrm(ks[27], (L, D_SHARED, D), D_SHARED ** -0.5),
    }


def reference(x, c, positions, w_mod, b_mod, w_in, q_gain, k_gain, cmp_pos_k, cmp_pos_v, cmp_w1_k, cmp_w2_k, cmp_w1_v, cmp_w2_v, gm_ln_g, gm_ln_b, gm_ws, gm_bs, w_proj_nsa, w_proj_gm, w_out, w_router, b_router, w_gate_e, w_up_e, w_down_e, w_gate_sh, w_up_sh, w_down_sh):
    B, S, D = x.shape
    split_pts = [int(p) for p in np.cumsum(SPLIT_SIZES)[:-1]]
    kv_shape = (B, S, NSA_KV_HEADS, HEAD_DIM)
    for l in range(DEPTH):
        mod = jax.nn.silu(c) @ w_mod[l] + b_mod[l]
        sh_a, sc_a, gt_a, sh_f, sc_f, gt_f = jnp.split(mod, 6, axis=-1)

        h = _rms(x) * (1.0 + sc_a[:, None, :]) + sh_a[:, None, :]
        z = h @ w_in[l]
        (zq, zkc, zvc, zks, zvs, zkw, zvw, zg, zu, zv, zga, zgb) = jnp.split(z, split_pts, axis=-1)
        o_nsa = _nsa(zq.reshape(B, S, NSA_Q_HEADS, HEAD_DIM),
                     zkc.reshape(kv_shape), zvc.reshape(kv_shape),
                     zks.reshape(kv_shape), zvs.reshape(kv_shape),
                     zkw.reshape(kv_shape), zvw.reshape(kv_shape),
                     zg.reshape(B, S, NSA_Q_HEADS, 3), positions,
                     q_gain[l], k_gain[l], cmp_pos_k[l], cmp_pos_v[l],
                     cmp_w1_k[l], cmp_w2_k[l], cmp_w1_v[l], cmp_w2_v[l])
        o_gm = _gmlp(jax.nn.gelu(zu), jax.nn.gelu(zv), gm_ln_g[l], gm_ln_b[l], gm_ws[l], gm_bs[l])
        y_a = o_nsa @ w_proj_nsa[l]
        y_b = o_gm @ w_proj_gm[l]
        merged = jax.nn.sigmoid(zga) * y_a + jax.nn.sigmoid(zgb) * y_b
        x = x + gt_a[:, None, :] * (merged @ w_out[l])

        h2 = _rms(x) * (1.0 + sc_f[:, None, :]) + sh_f[:, None, :]
        y = _moe(h2.reshape(B * S, D), w_router[l], b_router[l], w_gate_e[l], w_up_e[l], w_down_e[l],
                 w_gate_sh[l], w_up_sh[l], w_down_sh[l]).reshape(B, S, D)
        x = x + gt_f[:, None, :] * y
    return x
```

```python
import functools

import jax
import jax.numpy as jnp
import numpy as np
from jax import lax
from jax.experimental import pallas as pl
from jax.experimental.pallas import tpu as pltpu

D_MODEL = 1024
NSA_Q_HEADS = 8
NSA_KV_HEADS = 2
HEAD_DIM = 64
NSA_GROUP = NSA_Q_HEADS // NSA_KV_HEADS
CMP_LEN = 32
CMP_STRIDE = 16
CMP_HIDDEN = 256
SEL_LEN = 64
SEL_TOPN = 16
WINDOW = 512
ROPE_THETA = 10000.0
Q_WIDTH = NSA_Q_HEADS * HEAD_DIM
KV_WIDTH = NSA_KV_HEADS * HEAD_DIM
GM_GROUPS = 8
GM_GROUP_DIM = 64
GM_WIDTH = GM_GROUPS * GM_GROUP_DIM
GM_CHUNK = 128
N_EXPERTS = 256
TOP_K = 8
D_EXPERT = 256
D_SHARED = 256
ROUTE_SCALE = 2.5
EPS = 1e-6

LANES = 128
SEL_BIAS_WIDTH = 64
MASK_NEG = -1e30
SEL_NEG = -30000.0
MOE_ROWS = 256
VMEM_LIMIT = 56 * 1024 * 1024

F32 = jnp.float32
BF16 = jnp.bfloat16
HI = lax.Precision.HIGHEST


def _cparams(*sem):
    return pltpu.CompilerParams(dimension_semantics=sem, vmem_limit_bytes=VMEM_LIMIT)


def _dot(a, b, **kw):
    return jnp.dot(a, b, preferred_element_type=F32, **kw)


def _dot_nt(a, b, **kw):
    return lax.dot_general(a, b, (((1,), (1,)), ((), ())), preferred_element_type=F32, **kw)


def _gelu(x):
    return 0.5 * x * (1.0 + jnp.tanh(0.7978845608028654 * (x + 0.044715 * (x * x * x))))


def _sigmoid(x):
    return 1.0 / (1.0 + jnp.exp(-x))


def _silu(x):
    return x * _sigmoid(x)


def _mod_kernel(c_ref, w_ref, b_ref, o_ref):
    c = c_ref[...]
    o_ref[0] = _dot(_silu(c), w_ref[0], precision=HI) + b_ref[0]


def _mod_call(c, w_mod, b_mod):
    L, D, N = w_mod.shape
    B = c.shape[0]
    tn = 1536
    return pl.pallas_call(
        _mod_kernel,
        grid=(L, N // tn),
        in_specs=[pl.BlockSpec((B, D), lambda l, j: (0, 0)),
                  pl.BlockSpec((1, D, tn), lambda l, j: (l, 0, j)),
                  pl.BlockSpec((1, 1, tn), lambda l, j: (l, 0, j))],
        out_specs=pl.BlockSpec((1, B, tn), lambda l, j: (l, 0, j)),
        out_shape=jax.ShapeDtypeStruct((L, B, N), F32),
        compiler_params=_cparams("arbitrary", "arbitrary"),
        name="mod",
    )(c, w_mod, b_mod.reshape(L, 1, N))


_C_Q = 0
_C_K = 512
_C_KC = 768
_C_VC = 896
_C_VS = 1024
_C_VW = 1152
_C_G = 1280
_C_U = 1536
_C_V = 2048
_C_GA = 2560
_C_GB = 3584
IN_COLS_P = 4608


def _head_norm(z, bd):
    ms = _dot(z * z, bd, precision=HI)
    return z * lax.rsqrt(ms + EPS)


def _rope(z, cos, sin_lo, sin_hi):
    w = z.shape[-1]
    half = HEAD_DIM // 2
    return z * cos + pltpu.roll(z, w - half, 1) * sin_lo + pltpu.roll(z, half, 1) * sin_hi


def _tile_lanes(t, n):
    return t if n == 1 else jnp.concatenate([t] * n, axis=-1)


def _inproj_kernel(x_ref, sc_ref, sh_ref, w_ref, bdq_ref, bdk_ref, qg_ref, kg_ref,
                   cos_ref, sl_ref, shi_ref, lng_ref, lnb_ref,
                   qn_ref, qr_ref, ks_ref, kw_ref, vs_ref, vw_ref, kc_ref, vc_ref,
                   g_ref, u_ref, v_ref, ga_ref, gb_ref):
    tm = x_ref.shape[1]
    x = x_ref[0]
    ms = jnp.mean(x * x, axis=-1, keepdims=True)
    h = (x * lax.rsqrt(ms + EPS)) * (1.0 + sc_ref[0]) + sh_ref[0]
    hb = h.astype(BF16)

    def mm(lo, width):
        return _dot(hb, w_ref[:, lo:lo + width])

    cos, sl, shi = cos_ref[0], sl_ref[0], shi_ref[0]

    zq = mm(_C_Q, Q_WIDTH)
    qn = _head_norm(zq, bdq_ref[...]) * qg_ref[...]
    qr = _rope(qn, _tile_lanes(cos, 4), _tile_lanes(sl, 4), _tile_lanes(shi, 4))
    qn_ref[0] = qn.astype(BF16)
    qr_ref[0] = qr.astype(BF16)

    zk = mm(_C_K, 2 * KV_WIDTH)
    kn = _head_norm(zk, bdk_ref[...]) * kg_ref[...]
    kr = _rope(kn, _tile_lanes(cos, 2), _tile_lanes(sl, 2), _tile_lanes(shi, 2))
    lane = lax.broadcasted_iota(jnp.int32, (tm, LANES), 1)
    tok = pl.program_id(1) * tm + lax.broadcasted_iota(jnp.int32, (tm, LANES), 0)
    onehot = jnp.where(lane - HEAD_DIM == tok // SEL_LEN, 1.0, 0.0)
    ones_col = jnp.where(lane == HEAD_DIM, 1.0, 0.0)
    low = lane < HEAD_DIM
    zvs = mm(_C_VS, KV_WIDTH)
    zvw = mm(_C_VW, KV_WIDTH)
    zkc = mm(_C_KC, KV_WIDTH)
    zvc = mm(_C_VC, KV_WIDTH)
    for kv in range(NSA_KV_HEADS):
        def head(a):
            return a if kv == 0 else pltpu.roll(a, HEAD_DIM, 1)
        ks_ref[0, kv] = jnp.where(low, head(kr[:, :KV_WIDTH]), onehot).astype(BF16)
        kw_ref[0, kv] = jnp.where(low, head(kr[:, KV_WIDTH:]), 0.0).astype(BF16)
        vs_ref[0, kv] = jnp.where(low, head(zvs), ones_col).astype(BF16)
        vw_ref[0, kv] = jnp.where(low, head(zvw), ones_col).astype(BF16)
        kc_ref[0, kv] = head(zkc)[:, :HEAD_DIM]
        vc_ref[0, kv] = head(zvc)[:, :HEAD_DIM]

    zg = mm(_C_G, 2 * LANES)
    sg = _sigmoid(zg)
    g_ref[0, 0] = sg[:, :LANES]
    g_ref[0, 1] = sg[:, LANES:]

    u_ref[0] = _gelu(mm(_C_U, GM_WIDTH)).astype(BF16)
    gv = _gelu(mm(_C_V, GM_WIDTH))
    mu = jnp.mean(gv, axis=-1, keepdims=True)
    cen = gv - mu
    var = jnp.mean(cen * cen, axis=-1, keepdims=True)
    v_ref[0] = ((cen * lax.rsqrt(var + EPS)) * lng_ref[...] + lnb_ref[...]).astype(BF16)

    ga_ref[0] = _sigmoid(mm(_C_GA, D_MODEL)).astype(BF16)
    gb_ref[0] = _sigmoid(mm(_C_GB, D_MODEL)).astype(BF16)


def _block_diag_mean(width):
    idx = np.arange(width) // HEAD_DIM
    return jnp.asarray((idx[:, None] == idx[None, :]).astype(np.float32) / HEAD_DIM)


def _inproj_call(x, sc, sh, w_p, qg, kg, cos, sl, shi, lng, lnb, tm):
    B, S, D = x.shape
    H = NSA_KV_HEADS
    full = lambda *shape: pl.BlockSpec(shape, lambda b, i: (0,) * len(shape))
    tok3 = lambda w: pl.BlockSpec((1, tm, w), lambda b, i: (b, i, 0))
    per_b = pl.BlockSpec((1, 1, D), lambda b, i: (b, 0, 0))
    kv4 = lambda w: pl.BlockSpec((1, H, tm, w), lambda b, i: (b, 0, i, 0))
    sds = jax.ShapeDtypeStruct
    out_shape = [
        sds((B, S, Q_WIDTH), BF16), sds((B, S, Q_WIDTH), BF16),
        sds((B, H, S, LANES), BF16), sds((B, H, S, LANES), BF16),
        sds((B, H, S, LANES), BF16), sds((B, H, S, LANES), BF16),
        sds((B, H, S, HEAD_DIM), F32), sds((B, H, S, HEAD_DIM), F32),
        sds((B, H, S, LANES), F32),
        sds((B, S, GM_WIDTH), BF16), sds((B, S, GM_WIDTH), BF16),
        sds((B, S, D), BF16), sds((B, S, D), BF16),
    ]
    out_specs = [
        tok3(Q_WIDTH), tok3(Q_WIDTH), kv4(LANES), kv4(LANES), kv4(LANES), kv4(LANES),
        kv4(HEAD_DIM), kv4(HEAD_DIM), kv4(LANES),
        tok3(GM_WIDTH), tok3(GM_WIDTH), tok3(D), tok3(D),
    ]
    return pl.pallas_call(
        _inproj_kernel,
        grid=(B, S // tm),
        in_specs=[tok3(D), per_b, per_b, full(D, IN_COLS_P),
                  full(Q_WIDTH, Q_WIDTH), full(2 * KV_WIDTH, 2 * KV_WIDTH),
                  full(1, Q_WIDTH), full(1, 2 * KV_WIDTH),
                  tok3(LANES), tok3(LANES), tok3(LANES),
                  full(1, GM_WIDTH), full(1, GM_WIDTH)],
        out_specs=out_specs,
        out_shape=out_shape,
        compiler_params=_cparams("arbitrary", "arbitrary"),
        name="in_proj",
    )(x, sc, sh, w_p, _block_diag_mean(Q_WIDTH), _block_diag_mean(2 * KV_WIDTH), qg, kg,
      cos, sl, shi, lng, lnb)


def _compress_kernel(kr_ref, vr_ref, w1k_ref, w2k_ref, pek_ref, w1v_ref, w2v_ref, pev_ref,
                     kg_ref, kc_ref, vc_ref):
    nc = kr_ref.shape[2]
    half = CMP_STRIDE * HEAD_DIM

    def mlp(raw, w1_ref, w2_ref, pe_ref):
        a = raw.astype(BF16)
        top = _dot(a, w1_ref[:half, :])
        bot = _dot(a, w1_ref[half:, :])
        pe = jnp.broadcast_to(pe_ref[...], (8, 2 * half)).astype(BF16)
        pe_row = _dot(pe, w1_ref[...])[0:1, :]
        hid = top + pltpu.roll(bot, nc - 1, 0) + pe_row
        return _dot(_gelu(hid).astype(BF16), w2_ref[...])

    kc = mlp(kr_ref[0, 0], w1k_ref, w2k_ref, pek_ref)
    ms = jnp.mean(kc * kc, axis=-1, keepdims=True)
    kc_ref[0, 0] = (kc * lax.rsqrt(ms + EPS) * kg_ref[...]).astype(BF16)
    vc_ref[0, 0] = mlp(vr_ref[0, 0], w1v_ref, w2v_ref, pev_ref).astype(BF16)


def _compress_call(kc_raw, vc_raw, w1k, w2k, pek, w1v, w2v, pev, kg):
    B, H, S, hd = kc_raw.shape
    nc = S // CMP_STRIDE
    feat = CMP_STRIDE * hd
    raw = pl.BlockSpec((1, 1, nc, feat), lambda b, h: (b, h, 0, 0))
    full = lambda *shape: pl.BlockSpec(shape, lambda b, h: (0,) * len(shape))
    out = pl.BlockSpec((1, 1, nc, hd), lambda b, h: (b, h, 0, 0))
    return pl.pallas_call(
        _compress_kernel,
        grid=(B, H),
        in_specs=[raw, raw, full(2 * feat, CMP_HIDDEN), full(CMP_HIDDEN, hd), full(1, 2 * feat),
                  full(2 * feat, CMP_HIDDEN), full(CMP_HIDDEN, hd), full(1, 2 * feat),
                  full(1, hd)],
        out_specs=[out, out],
        out_shape=[jax.ShapeDtypeStruct((B, H, nc, hd), BF16)] * 2,
        compiler_params=_cparams("arbitrary", "arbitrary"),
        name="compress",
    )(kc_raw.reshape(B, H, nc, feat), vc_raw.reshape(B, H, nc, feat),
      w1k, w2k, pek, w1v, w2v, pev, kg)


def _group_rows(a):
    return jnp.concatenate([a[:, g * HEAD_DIM:(g + 1) * HEAD_DIM] for g in range(NSA_GROUP)], axis=0)


def _nsa_cmp_kernel(qn_ref, qr_ref, kc_ref, vc_ref, ovl_ref, oc_ref, qa_ref, *, n_sel):
    tq = qn_ref.shape[1]
    nc = kc_ref.shape[2]
    G = NSA_GROUP
    q0 = pl.program_id(2) * tq
    q4 = _group_rows(qn_ref[0])
    s = _dot_nt(q4, kc_ref[0, 0])
    row = lax.broadcasted_iota(jnp.int32, (G, tq, nc), 1).reshape(G * tq, nc)
    col = lax.broadcasted_iota(jnp.int32, (G * tq, nc), 1)
    vis = col * CMP_STRIDE + (CMP_LEN - 1) <= q0 + row
    s = jnp.where(vis, s, MASK_NEG)
    m = jnp.max(s, axis=-1, keepdims=True)
    e = jnp.where(vis, jnp.exp(s - m), 0.0)
    p = e / jnp.maximum(jnp.sum(e, axis=-1, keepdims=True), 1e-30)
    oc = _dot(p.astype(BF16), vc_ref[0, 0])
    oc_ref[0, 0] = oc.reshape(G, tq, HEAD_DIM)

    psum = p[0:tq] + p[tq:2 * tq] + p[2 * tq:3 * tq] + p[3 * tq:4 * tq]
    imp = _dot_nt(ovl_ref[...], psum, precision=HI)[:SEL_BIAS_WIDTH]
    blk = lax.broadcasted_iota(jnp.int32, (SEL_BIAS_WIDTH, tq), 0)
    cur = (q0 + lax.broadcasted_iota(jnp.int32, (SEL_BIAS_WIDTH, tq), 1)) // SEL_LEN
    valid = blk <= cur
    forced = (blk == 0) | (blk == cur) | (blk == cur - 1)
    cand = valid & jnp.logical_not(forced)
    n_forced = jnp.minimum(cur, 2) + 1
    val = jnp.where(cand, imp, -1.0)
    cnt = jnp.zeros((SEL_BIAS_WIDTH, tq), F32)
    for j in range(n_sel):
        vj = jnp.broadcast_to(val[j:j + 1, :], (SEL_BIAS_WIDTH, tq))
        cnt = cnt + jnp.where(blk > j, jnp.where(vj >= val, 1.0, 0.0), jnp.where(vj > val, 1.0, 0.0))
    free = (min(SEL_TOPN, n_sel) - n_forced).astype(F32)
    sel = (forced & valid) | (cand & (cnt < free))
    sel_t = jnp.concatenate([jnp.where(sel, 1.0, 0.0), jnp.zeros((LANES - SEL_BIAS_WIDTH, tq), F32)], axis=0)
    sel_q = sel_t.T
    bias = jnp.where(sel_q > 0.5, 0.0, SEL_NEG)
    lane = lax.broadcasted_iota(jnp.int32, (tq, LANES), 1)
    bias_hi = pltpu.roll(bias, SEL_BIAS_WIDTH, 1)
    qr = qr_ref[0]
    for g in range(G):
        qg = qr[:, g * HEAD_DIM:(g + 1) * HEAD_DIM].astype(F32)
        qg = jnp.concatenate([qg, qg], axis=-1)
        qa_ref[0, 0, g] = jnp.where(lane < HEAD_DIM, qg, bias_hi).astype(BF16)


def _overlap_t(S):
    n_cmp = (S - CMP_LEN) // CMP_STRIDE + 1
    nc = S // CMP_STRIDE
    n_sel = S // SEL_LEN
    start = np.arange(nc) * CMP_STRIDE
    end = start + CMP_LEN - 1
    sel_start = np.arange(n_sel) * SEL_LEN
    ov = (start[None, :] <= sel_start[:, None] + SEL_LEN - 1) & (end[None, :] >= sel_start[:, None])
    ov = ov & (np.arange(nc) < n_cmp)[None, :]
    out = np.zeros((LANES, nc), np.float32)
    out[:n_sel] = ov.astype(np.float32)
    return jnp.asarray(out)


def _nsa_cmp_call(qn, qr, kc, vc, tq):
    B, S, _ = qn.shape
    H, G = NSA_KV_HEADS, NSA_GROUP
    nc = kc.shape[2]
    n_sel = S // SEL_LEN
    assert n_sel <= SEL_BIAS_WIDTH and nc % LANES == 0
    qspec = pl.BlockSpec((1, tq, G * HEAD_DIM), lambda b, h, i: (b, i, h))
    cspec = pl.BlockSpec((1, 1, nc, HEAD_DIM), lambda b, h, i: (b, h, 0, 0))
    return pl.pallas_call(
        functools.partial(_nsa_cmp_kernel, n_sel=n_sel),
        grid=(B, H, S // tq),
        in_specs=[qspec, qspec, cspec, cspec, pl.BlockSpec((LANES, nc), lambda b, h, i: (0, 0))],
        out_specs=[pl.BlockSpec((1, 1, G, tq, HEAD_DIM), lambda b, h, i: (b, h, 0, i, 0)),
                   pl.BlockSpec((1, 1, G, tq, LANES), lambda b, h, i: (b, h, 0, i, 0))],
        out_shape=[jax.ShapeDtypeStruct((B, H, G, S, HEAD_DIM), F32),
                   jax.ShapeDtypeStruct((B, H, G, S, LANES), BF16)],
        compiler_params=_cparams("arbitrary", "arbitrary", "arbitrary"),
        name="nsa_cmp",
    )(qn, qr, kc, vc, _overlap_t(S))


def _nsa_flash_kernel(qa_ref, ks_ref, vs_ref, kw_ref, vw_ref, oc_ref, g_ref, o_ref, m_scr, acc_scr):
    G = NSA_GROUP
    tq = qa_ref.shape[3]
    R = G * tq
    tk = tq
    i = pl.program_id(2)
    q = qa_ref[0, 0].reshape(R, LANES)
    rowq = lax.broadcasted_iota(jnp.int32, (G, tq, tk), 1).reshape(R, tk)
    colk = lax.broadcasted_iota(jnp.int32, (R, tk), 1)

    def reset():
        m_scr[...] = jnp.full((R, LANES), MASK_NEG, F32)
        acc_scr[...] = jnp.zeros((R, LANES), F32)

    def step(k_ref, v_ref, j, mask):
        off = pl.multiple_of(j * tk, tk)
        k = k_ref[0, 0, pl.ds(off, tk), :]
        v = v_ref[0, 0, pl.ds(off, tk), :]
        s = _dot_nt(q, k)
        if mask is not None:
            s = jnp.where(mask, s, MASK_NEG)
        m_prev = m_scr[...]
        m_new = jnp.maximum(m_prev, jnp.max(s, axis=-1, keepdims=True))
        alpha = jnp.exp(m_prev - m_new)
        p = jnp.exp(s - m_new[:, 0:1])
        acc_scr[...] = alpha * acc_scr[...] + _dot(p.astype(BF16), v)
        m_scr[...] = m_new

    def result():
        acc = acc_scr[...]
        return acc[:, :HEAD_DIM] / acc[:, HEAD_DIM:HEAD_DIM + 1]

    reset()

    def sel_body(j, carry):
        step(ks_ref, vs_ref, j, None)
        return carry

    lax.fori_loop(0, i, sel_body, 0)
    step(ks_ref, vs_ref, i, colk <= rowq)
    o_sel = result()

    reset()
    assert WINDOW == 2 * tk

    @pl.when(i >= 2)
    def _():
        step(kw_ref, vw_ref, i - 2, colk > rowq)

    @pl.when(i >= 1)
    def _():
        step(kw_ref, vw_ref, i - 1, None)

    step(kw_ref, vw_ref, i, colk <= rowq)
    o_win = result()

    o_cmp = oc_ref[0, 0].reshape(R, HEAD_DIM)
    gates = g_ref[0, 0]
    outs = []
    for g in range(G):
        r = slice(g * tq, (g + 1) * tq)
        outs.append(gates[:, 3 * g:3 * g + 1] * o_cmp[r]
                    + gates[:, 3 * g + 1:3 * g + 2] * o_sel[r]
                    + gates[:, 3 * g + 2:3 * g + 3] * o_win[r])
    o_ref[0] = jnp.concatenate(outs, axis=-1).astype(BF16)


def _nsa_flash_call(qa, ks, vs, kw, vw, oc, gates):
    B, H, G, S, _ = qa.shape
    tq = WINDOW // 2
    R = G * tq
    kv = pl.BlockSpec((1, 1, S, LANES), lambda b, h, i: (b, h, 0, 0))
    return pl.pallas_call(
        _nsa_flash_kernel,
        grid=(B, H, S // tq),
        in_specs=[pl.BlockSpec((1, 1, G, tq, LANES), lambda b, h, i: (b, h, 0, i, 0)),
                  kv, kv, kv, kv,
                  pl.BlockSpec((1, 1, G, tq, HEAD_DIM), lambda b, h, i: (b, h, 0, i, 0)),
                  pl.BlockSpec((1, 1, tq, LANES), lambda b, h, i: (b, h, i, 0))],
        out_specs=pl.BlockSpec((1, tq, G * HEAD_DIM), lambda b, h, i: (b, i, h)),
        out_shape=jax.ShapeDtypeStruct((B, S, Q_WIDTH), BF16),
        scratch_shapes=[pltpu.VMEM((R, LANES), F32), pltpu.VMEM((R, LANES), F32)],
        compiler_params=_cparams("arbitrary", "arbitrary", "arbitrary"),
        name="nsa_flash",
    )(qa, ks, vs, kw, vw, oc, gates)


def _gmlp_kernel(u_ref, v_ref, ws_ref, bs_ref, o_ref):
    tm = u_ref.shape[1]
    C = GM_CHUNK
    r = lax.broadcasted_iota(jnp.int32, (C, C), 0)
    c = lax.broadcasted_iota(jnp.int32, (C, C), 1)
    causal = c <= r
    ws = [jnp.where(causal, ws_ref[g], 0.0).astype(BF16) for g in range(GM_GROUPS)]
    for n in range(tm // C):
        rows = slice(n * C, (n + 1) * C)
        vn = v_ref[0, rows, :]
        mixed = jnp.concatenate(
            [_dot(ws[g], vn[:, g * GM_GROUP_DIM:(g + 1) * GM_GROUP_DIM]) for g in range(GM_GROUPS)],
            axis=-1)
        o_ref[0, rows, :] = (u_ref[0, rows, :].astype(F32) * (mixed + bs_ref[...])).astype(BF16)


def _gmlp_call(u, v, ws, bs_full, tm):
    B, S, W = u.shape
    tok = pl.BlockSpec((1, tm, W), lambda b, i: (b, i, 0))
    return pl.pallas_call(
        _gmlp_kernel,
        grid=(B, S // tm),
        in_specs=[tok, tok,
                  pl.BlockSpec((GM_GROUPS, GM_CHUNK, GM_CHUNK), lambda b, i: (0, 0, 0)),
                  pl.BlockSpec((GM_CHUNK, W), lambda b, i: (0, 0))],
        out_specs=tok,
        out_shape=jax.ShapeDtypeStruct((B, S, W), BF16),
        compiler_params=_cparams("arbitrary", "arbitrary"),
        name="gmlp",
    )(u, v, ws, bs_full)


def _mixout_kernel(on_ref, og_ref, ga_ref, gb_ref, x_ref, gt_ref, sc_ref, sh_ref,
                   wpn_ref, wpg_ref, wo_ref, wr_ref, wgs_ref, wus_ref, wds_ref,
                   x1_ref, h2_ref, lg_ref, shd_ref):
    ya = _dot(on_ref[0], wpn_ref[...])
    yb = _dot(og_ref[0], wpg_ref[...])
    merged = ga_ref[0].astype(F32) * ya + gb_ref[0].astype(F32) * yb
    x1 = x_ref[0] + gt_ref[0] * _dot(merged.astype(BF16), wo_ref[...])
    x1_ref[0] = x1
    ms = jnp.mean(x1 * x1, axis=-1, keepdims=True)
    h2 = (x1 * lax.rsqrt(ms + EPS)) * (1.0 + sc_ref[0]) + sh_ref[0]
    lg_ref[0] = _dot(h2, wr_ref[...], precision=HI)
    hb = h2.astype(BF16)
    h2_ref[0] = hb
    act = _silu(_dot(hb, wgs_ref[...])) * _dot(hb, wus_ref[...])
    shd_ref[0] = _dot(act.astype(BF16), wds_ref[...])


def _mixout_call(o_nsa, o_gm, ga, gb, x, gt, sc, sh, wpn, wpg, wo, wr, wgs, wus, wds, tm):
    B, S, D = x.shape
    tok = lambda w: pl.BlockSpec((1, tm, w), lambda b, i: (b, i, 0))
    per_b = pl.BlockSpec((1, 1, D), lambda b, i: (b, 0, 0))
    full = lambda a: pl.BlockSpec(a.shape, lambda b, i: (0,) * a.ndim)
    sds = jax.ShapeDtypeStruct
    return pl.pallas_call(
        _mixout_kernel,
        grid=(B, S // tm),
        in_specs=[tok(Q_WIDTH), tok(GM_WIDTH), tok(D), tok(D), tok(D), per_b, per_b, per_b,
                  full(wpn), full(wpg), full(wo), full(wr), full(wgs), full(wus), full(wds)],
        out_specs=[tok(D), tok(D), tok(N_EXPERTS), tok(D)],
        out_shape=[sds((B, S, D), F32), sds((B, S, D), BF16), sds((B, S, N_EXPERTS), F32),
                   sds((B, S, D), F32)],
        compiler_params=_cparams("arbitrary", "arbitrary"),
        name="mix_out",
    )(o_nsa, o_gm, ga, gb, x, gt, sc, sh, wpn, wpg, wo, wr, wgs, wus, wds)


def _route_kernel(lg_ref, br_ref, idx_ref, wt_ref, rank_ref, cnt_ref, run_scr):
    tm, E = lg_ref.shape

    @pl.when(pl.program_id(0) == 0)
    def _():
        run_scr[...] = jnp.zeros_like(run_scr)

    aff = _sigmoid(lg_ref[...])
    work = aff + br_ref[...]
    lane = lax.broadcasted_iota(jnp.int32, (tm, E), 1).astype(F32)
    picked = jnp.zeros((tm, E), F32)
    idxs, tops = [], []
    for _ in range(TOP_K):
        m = jnp.max(work, axis=-1, keepdims=True)
        idx = jnp.min(jnp.where(work == m, lane, float(E)), axis=-1, keepdims=True)
        hit = lane == idx
        tops.append(jnp.sum(jnp.where(hit, aff, 0.0), axis=-1, keepdims=True))
        idxs.append(idx)
        picked = jnp.where(hit, 1.0, picked)
        work = jnp.where(hit, -jnp.inf, work)
    total = tops[0]
    for t in tops[1:]:
        total = total + t
    r = lax.broadcasted_iota(jnp.int32, (tm, tm), 0)
    c = lax.broadcasted_iota(jnp.int32, (tm, tm), 1)
    before = _dot(jnp.where(c < r, 1.0, 0.0).astype(BF16), picked.astype(BF16)) + run_scr[...]
    ranks = [jnp.sum(jnp.where(lane == idx, before, 0.0), axis=-1, keepdims=True) for idx in idxs]
    run_scr[...] = run_scr[...] + jnp.sum(picked, axis=0, keepdims=True)
    cnt_ref[...] = run_scr[...]
    lane_k = lax.broadcasted_iota(jnp.int32, (tm, LANES), 1)
    idx_o = jnp.zeros((tm, LANES), jnp.int32)
    wt_o = jnp.zeros((tm, LANES), F32)
    rank_o = jnp.zeros((tm, LANES), jnp.int32)
    for k in range(TOP_K):
        idx_o = jnp.where(lane_k == k, idxs[k].astype(jnp.int32), idx_o)
        wt_o = jnp.where(lane_k == k, tops[k] / total * ROUTE_SCALE, wt_o)
        rank_o = jnp.where(lane_k == k, ranks[k].astype(jnp.int32), rank_o)
    idx_ref[...] = idx_o
    wt_ref[...] = wt_o
    rank_ref[...] = rank_o


def _route_call(logits, b_router, tm):
    T, E = logits.shape
    tok = lambda w: pl.BlockSpec((tm, w), lambda i: (i, 0))
    one = pl.BlockSpec((1, E), lambda i: (0, 0))
    sds = jax.ShapeDtypeStruct
    return pl.pallas_call(
        _route_kernel,
        grid=(T // tm,),
        in_specs=[tok(E), one],
        out_specs=[tok(LANES), tok(LANES), tok(LANES), one],
        out_shape=[sds((T, LANES), jnp.int32), sds((T, LANES), F32), sds((T, LANES), jnp.int32),
                   sds((1, E), F32)],
        scratch_shapes=[pltpu.VMEM((1, E), F32)],
        compiler_params=_cparams("arbitrary"),
        name="route",
    )(logits, b_router.reshape(1, E))


def _expert_kernel(be_ref, nb_ref, x_ref, rw_ref, wg_ref, wu_ref, wd_ref, y_ref):
    @pl.when(pl.program_id(0) < nb_ref[0])
    def _():
        x = x_ref[...]
        a = _dot(x, wg_ref[0].astype(BF16))
        b = _dot(x, wu_ref[0].astype(BF16))
        y = _dot((_silu(a) * b).astype(BF16), wd_ref[0].astype(BF16))
        y_ref[...] = (y * rw_ref[...]).astype(BF16)

    @pl.when(pl.program_id(0) >= nb_ref[0])
    def _():
        y_ref[...] = jnp.zeros_like(y_ref)


def _expert_call(blk_e, n_used, xs, row_w, wg, wu, wd):
    n_rows, D = xs.shape
    rows = MOE_ROWS
    wspec = lambda shape: pl.BlockSpec((1,) + shape, lambda i, be, nb: (be[i], 0, 0))
    grid_spec = pltpu.PrefetchScalarGridSpec(
        num_scalar_prefetch=2,
        grid=(n_rows // rows,),
        in_specs=[pl.BlockSpec((rows, D), lambda i, be, nb: (i, 0)),
                  pl.BlockSpec((rows, 1), lambda i, be, nb: (i, 0)),
                  wspec((D, D_EXPERT)), wspec((D, D_EXPERT)), wspec((D_EXPERT, D))],
        out_specs=pl.BlockSpec((rows, D), lambda i, be, nb: (i, 0)),
    )
    return pl.pallas_call(
        _expert_kernel,
        grid_spec=grid_spec,
        out_shape=jax.ShapeDtypeStruct((n_rows, D), BF16),
        compiler_params=_cparams("arbitrary"),
        name="expert",
    )(blk_e, n_used, xs, row_w, wg, wu, wd)


def _combine_kernel(y_ref, shd_ref, x_ref, gt_ref, o_ref):
    routed = y_ref[0].astype(F32)
    for k in range(1, TOP_K):
        routed = routed + y_ref[k].astype(F32)
    o_ref[0] = x_ref[0] + gt_ref[0] * (routed + shd_ref[0])


def _combine_call(yk, shared, x1, gt, tm):
    B, S, D = x1.shape
    nb = S // tm
    tok = pl.BlockSpec((1, tm, D), lambda b, i: (b, i, 0))
    return pl.pallas_call(
        _combine_kernel,
        grid=(B, nb),
        in_specs=[pl.BlockSpec((TOP_K, tm, D), lambda b, i: (0, b * nb + i, 0)),
                  tok, tok, pl.BlockSpec((1, 1, D), lambda b, i: (b, 0, 0))],
        out_specs=tok,
        out_shape=jax.ShapeDtypeStruct((B, S, D), F32),
        compiler_params=_cparams("arbitrary", "arbitrary"),
        name="combine",
    )(yk, shared, x1, gt)


def _rope_tables(positions):
    half = HEAD_DIM // 2
    inv = ROPE_THETA ** (-jnp.arange(half, dtype=F32) / half)
    ang = positions.astype(F32)[..., None] * inv
    cos, sin, zero = jnp.cos(ang), jnp.sin(ang), jnp.zeros_like(ang)
    cos_t = jnp.concatenate([cos, cos] * 2, axis=-1)
    sin_lo = jnp.concatenate([-sin, zero] * 2, axis=-1)
    sin_hi = jnp.concatenate([zero, sin] * 2, axis=-1)
    return cos_t, sin_lo, sin_hi


def _reorder_w_in(w):
    o = np.cumsum([0, Q_WIDTH] + [KV_WIDTH] * 6 + [NSA_Q_HEADS * 3, GM_WIDTH, GM_WIDTH, D_MODEL, D_MODEL])
    q, kc, vc, ks, vs, kw, vw, g, u, v, ga, gb = [w[:, o[i]:o[i + 1]] for i in range(12)]
    per = NSA_GROUP * 3
    pad = jnp.zeros((w.shape[0], LANES - per), w.dtype)
    return jnp.concatenate([q, ks, kw, kc, vc, vs, vw, g[:, :per], pad, g[:, per:], pad, u, v, ga, gb], axis=1)


def _dispatch_plan(idx, rank, counts, n_rows):
    rows = MOE_ROWS
    T = idx.shape[0]
    counts = counts.astype(jnp.int32)
    padded = (counts + rows - 1) // rows * rows
    pad_end = jnp.cumsum(padded)
    pad_start = pad_end - padded
    dest = pad_start[idx] + rank
    blk_e = jnp.minimum(jnp.searchsorted(pad_end, jnp.arange(n_rows // rows) * rows, side='right'),
                        N_EXPERTS - 1).astype(jnp.int32)
    n_used = (pad_end[-1:] // rows).astype(jnp.int32)
    return dest, blk_e, n_used


def kernel(x, c, positions, w_mod, b_mod, w_in, q_gain, k_gain, cmp_pos_k, cmp_pos_v, cmp_w1_k, cmp_w2_k, cmp_w1_v, cmp_w2_v, gm_ln_g, gm_ln_b, gm_ws, gm_bs, w_proj_nsa, w_proj_gm, w_out, w_router, b_router, w_gate_e, w_up_e, w_down_e, w_gate_sh, w_up_sh, w_down_sh):
    B, S, D = x.shape
    L = w_mod.shape[0]
    T = B * S
    tm = 256
    scale = HEAD_DIM ** -0.5
    cos_t, sin_lo, sin_hi = _rope_tables(positions)
    mod = _mod_call(c, w_mod, b_mod)
    n_rows = T * TOP_K + N_EXPERTS * MOE_ROWS
    tok_ids = jnp.broadcast_to(jnp.arange(T, dtype=jnp.int32)[:, None], (T, TOP_K))

    for l in range(L):
        sh_a, sc_a, gt_a, sh_f, sc_f, gt_f = [mod[l, :, i * D:(i + 1) * D].reshape(B, 1, D) for i in range(6)]
        qg = (jnp.tile(q_gain[l], NSA_Q_HEADS) * scale).reshape(1, Q_WIDTH)
        kg = jnp.tile(k_gain[l], 2 * NSA_KV_HEADS).reshape(1, 2 * KV_WIDTH)
        (qn, qr, ks, kw, vs, vw, kc_raw, vc_raw, gates, u, v, ga, gb) = _inproj_call(
            x, sc_a, sh_a, _reorder_w_in(w_in[l]).astype(BF16), qg, kg, cos_t, sin_lo, sin_hi,
            gm_ln_g[l].reshape(1, GM_WIDTH), gm_ln_b[l].reshape(1, GM_WIDTH), tm)
        kc, vc = _compress_call(
            kc_raw, vc_raw, cmp_w1_k[l].astype(BF16), cmp_w2_k[l].astype(BF16), cmp_pos_k[l].reshape(1, -1),
            cmp_w1_v[l].astype(BF16), cmp_w2_v[l].astype(BF16), cmp_pos_v[l].reshape(1, -1),
            k_gain[l].reshape(1, HEAD_DIM))
        o_cmp, q_aug = _nsa_cmp_call(qn, qr, kc, vc, WINDOW // 2)
        o_nsa = _nsa_flash_call(q_aug, ks, vs, kw, vw, o_cmp, gates)
        bs_full = jnp.repeat(gm_bs[l].T, GM_GROUP_DIM, axis=1)
        o_gm = _gmlp_call(u, v, gm_ws[l], bs_full, 512)
        x1, h2, logits, shared = _mixout_call(
            o_nsa, o_gm, ga, gb, x, gt_a, sc_f, sh_f,
            w_proj_nsa[l].astype(BF16), w_proj_gm[l].astype(BF16), w_out[l].astype(BF16), w_router[l],
            w_gate_sh[l].astype(BF16), w_up_sh[l].astype(BF16), w_down_sh[l].astype(BF16), tm)
        idx, wts, rank, counts = _route_call(logits.reshape(T, N_EXPERTS), b_router[l], tm)
        idx, wts, rank = idx[:, :TOP_K], wts[:, :TOP_K], rank[:, :TOP_K]
        dest, blk_e, n_used = _dispatch_plan(idx, rank, counts[0], n_rows)
        flat = dest.reshape(-1)
        row_tok = jnp.zeros((n_rows,), jnp.int32).at[flat].set(tok_ids.reshape(-1))
        row_w = jnp.zeros((n_rows,), F32).at[flat].set(wts.reshape(-1))
        xs = jnp.take(h2.reshape(T, D), row_tok, axis=0)
        y = _expert_call(blk_e, n_used, xs, row_w.reshape(n_rows, 1), w_gate_e[l], w_up_e[l], w_down_e[l])
        yk = jnp.take(y, dest.T.reshape(-1), axis=0).reshape(TOP_K, T, D)
        x = _combine_call(yk, shared, x1, gt_f, tm)
    return x
```

```python
import functools

import jax
import jax.numpy as jnp
import numpy as np
from jax import lax
from jax.experimental import pallas as pl
from jax.experimental.pallas import tpu as pltpu

D_MODEL = 1024
NSA_Q_HEADS = 8
NSA_KV_HEADS = 2
HEAD_DIM = 64
NSA_GROUP = NSA_Q_HEADS // NSA_KV_HEADS
CMP_LEN = 32
CMP_STRIDE = 16
CMP_HIDDEN = 256
SEL_LEN = 64
SEL_TOPN = 16
WINDOW = 512
ROPE_THETA = 10000.0
Q_WIDTH = NSA_Q_HEADS * HEAD_DIM
KV_WIDTH = NSA_KV_HEADS * HEAD_DIM
GM_GROUPS = 8
GM_GROUP_DIM = 64
GM_WIDTH = GM_GROUPS * GM_GROUP_DIM
GM_CHUNK = 128
N_EXPERTS = 256
TOP_K = 8
D_EXPERT = 256
D_SHARED = 256
ROUTE_SCALE = 2.5
EPS = 1e-6

LANES = 128
SEL_BIAS_WIDTH = 64
MASK_NEG = -1e30
SEL_NEG = -30000.0
MOE_ROWS = 256
VMEM_LIMIT = 56 * 1024 * 1024

F32 = jnp.float32
BF16 = jnp.bfloat16
HI = lax.Precision.HIGHEST


def _cparams(*sem):
    return pltpu.CompilerParams(dimension_semantics=sem, vmem_limit_bytes=VMEM_LIMIT)


def _dot(a, b, **kw):
    return jnp.dot(a, b, preferred_element_type=F32, **kw)


def _dot_nt(a, b, **kw):
    return lax.dot_general(a, b, (((1,), (1,)), ((), ())), preferred_element_type=F32, **kw)


def _gelu(x):
    return 0.5 * x * (1.0 + jnp.tanh(0.7978845608028654 * (x + 0.044715 * (x * x * x))))


def _sigmoid(x):
    return 1.0 / (1.0 + jnp.exp(-x))


def _silu(x):
    return x * _sigmoid(x)


_HI_MASK = np.uint32(0xFFFF0000)


def _pack_halves(a):
    w = a.shape[1] // 2
    bits = lax.bitcast_convert_type(a.astype(BF16).astype(F32), jnp.uint32)
    return (bits[:, w:] & _HI_MASK) | (bits[:, :w] >> 16)


def _unpack_halves(words):
    lo = lax.bitcast_convert_type(words << 16, F32)
    hi = lax.bitcast_convert_type(words & _HI_MASK, F32)
    return lo, hi


def _mod_kernel(c_ref, w_ref, b_ref, o_ref):
    c = c_ref[...]
    o_ref[0] = _dot(_silu(c), w_ref[0], precision=HI) + b_ref[0]


def _mod_call(c, w_mod, b_mod):
    L, D, N = w_mod.shape
    B = c.shape[0]
    tn = 1536
    return pl.pallas_call(
        _mod_kernel,
        grid=(L, N // tn),
        in_specs=[pl.BlockSpec((B, D), lambda l, j: (0, 0)),
                  pl.BlockSpec((1, D, tn), lambda l, j: (l, 0, j)),
                  pl.BlockSpec((1, 1, tn), lambda l, j: (l, 0, j))],
        out_specs=pl.BlockSpec((1, B, tn), lambda l, j: (l, 0, j)),
        out_shape=jax.ShapeDtypeStruct((L, B, N), F32),
        compiler_params=_cparams("arbitrary", "arbitrary"),
        name="mod",
    )(c, w_mod, b_mod.reshape(L, 1, N))


_C_Q = 0
_C_K = 512
_C_KC = 768
_C_VC = 896
_C_VS = 1024
_C_VW = 1152
_C_G = 1280
_C_U = 1536
_C_V = 2048
_C_GA = 2560
_C_GB = 3584
IN_COLS_P = 4608


def _head_norm(z, bd):
    ms = _dot(z * z, bd, precision=HI)
    return z * lax.rsqrt(ms + EPS)


def _rope(z, cos, sin_lo, sin_hi):
    w = z.shape[-1]
    half = HEAD_DIM // 2
    return z * cos + pltpu.roll(z, w - half, 1) * sin_lo + pltpu.roll(z, half, 1) * sin_hi


def _tile_lanes(t, n):
    return t if n == 1 else jnp.concatenate([t] * n, axis=-1)


def _inproj_kernel(x_ref, sc_ref, sh_ref, w_ref, bdq_ref, bdk_ref, qg_ref, kg_ref,
                   cos_ref, sl_ref, shi_ref, lng_ref, lnb_ref,
                   qn_ref, qr_ref, ks_ref, kw_ref, vs_ref, vw_ref, kc_ref, vc_ref,
                   g_ref, u_ref, v_ref, ga_ref, gb_ref):
    tm = x_ref.shape[1]
    x = x_ref[0]
    ms = jnp.mean(x * x, axis=-1, keepdims=True)
    h = (x * lax.rsqrt(ms + EPS)) * (1.0 + sc_ref[0]) + sh_ref[0]
    hb = h.astype(BF16)

    def mm(lo, width):
        return _dot(hb, w_ref[:, lo:lo + width])

    cos, sl, shi = cos_ref[0], sl_ref[0], shi_ref[0]

    zq = mm(_C_Q, Q_WIDTH)
    qn = _head_norm(zq, bdq_ref[...]) * qg_ref[...]
    qr = _rope(qn, _tile_lanes(cos, 4), _tile_lanes(sl, 4), _tile_lanes(shi, 4))
    qn_ref[0] = qn.astype(BF16)
    qr_ref[0] = qr.astype(BF16)

    zk = mm(_C_K, 2 * KV_WIDTH)
    kn = _head_norm(zk, bdk_ref[...]) * kg_ref[...]
    kr = _rope(kn, _tile_lanes(cos, 2), _tile_lanes(sl, 2), _tile_lanes(shi, 2))
    lane = lax.broadcasted_iota(jnp.int32, (tm, LANES), 1)
    tok = pl.program_id(1) * tm + lax.broadcasted_iota(jnp.int32, (tm, LANES), 0)
    onehot = jnp.where(lane - HEAD_DIM == tok // SEL_LEN, 1.0, 0.0)
    ones_col = jnp.where(lane == HEAD_DIM, 1.0, 0.0)
    low = lane < HEAD_DIM
    zvs = mm(_C_VS, KV_WIDTH)
    zvw = mm(_C_VW, KV_WIDTH)
    zkc = mm(_C_KC, KV_WIDTH)
    zvc = mm(_C_VC, KV_WIDTH)
    for kv in range(NSA_KV_HEADS):
        def head(a):
            return a if kv == 0 else pltpu.roll(a, HEAD_DIM, 1)
        ks_ref[0, kv] = jnp.where(low, head(kr[:, :KV_WIDTH]), onehot).astype(BF16)
        kw_ref[0, kv] = jnp.where(low, head(kr[:, KV_WIDTH:]), 0.0).astype(BF16)
        vs_ref[0, kv] = jnp.where(low, head(zvs), ones_col).astype(BF16)
        vw_ref[0, kv] = jnp.where(low, head(zvw), ones_col).astype(BF16)
        kc_ref[0, kv] = head(zkc)[:, :HEAD_DIM]
        vc_ref[0, kv] = head(zvc)[:, :HEAD_DIM]

    zg = mm(_C_G, 2 * LANES)
    sg = _sigmoid(zg)
    g_ref[0, 0] = sg[:, :LANES]
    g_ref[0, 1] = sg[:, LANES:]

    u_ref[0] = _gelu(mm(_C_U, GM_WIDTH)).astype(BF16)
    gv = _gelu(mm(_C_V, GM_WIDTH))
    mu = jnp.mean(gv, axis=-1, keepdims=True)
    cen = gv - mu
    var = jnp.mean(cen * cen, axis=-1, keepdims=True)
    v_ref[0] = ((cen * lax.rsqrt(var + EPS)) * lng_ref[...] + lnb_ref[...]).astype(BF16)

    ga_ref[0] = _sigmoid(mm(_C_GA, D_MODEL)).astype(BF16)
    gb_ref[0] = _sigmoid(mm(_C_GB, D_MODEL)).astype(BF16)


def _block_diag_mean(width):
    idx = np.arange(width) // HEAD_DIM
    return jnp.asarray((idx[:, None] == idx[None, :]).astype(np.float32) / HEAD_DIM)


def _inproj_call(x, sc, sh, w_p, qg, kg, cos, sl, shi, lng, lnb, tm):
    B, S, D = x.shape
    H = NSA_KV_HEADS
    full = lambda *shape: pl.BlockSpec(shape, lambda b, i: (0,) * len(shape))
    tok3 = lambda w: pl.BlockSpec((1, tm, w), lambda b, i: (b, i, 0))
    per_b = pl.BlockSpec((1, 1, D), lambda b, i: (b, 0, 0))
    kv4 = lambda w: pl.BlockSpec((1, H, tm, w), lambda b, i: (b, 0, i, 0))
    sds = jax.ShapeDtypeStruct
    out_shape = [
        sds((B, S, Q_WIDTH), BF16), sds((B, S, Q_WIDTH), BF16),
        sds((B, H, S, LANES), BF16), sds((B, H, S, LANES), BF16),
        sds((B, H, S, LANES), BF16), sds((B, H, S, LANES), BF16),
        sds((B, H, S, HEAD_DIM), F32), sds((B, H, S, HEAD_DIM), F32),
        sds((B, H, S, LANES), F32),
        sds((B, S, GM_WIDTH), BF16), sds((B, S, GM_WIDTH), BF16),
        sds((B, S, D), BF16), sds((B, S, D), BF16),
    ]
    out_specs = [
        tok3(Q_WIDTH), tok3(Q_WIDTH), kv4(LANES), kv4(LANES), kv4(LANES), kv4(LANES),
        kv4(HEAD_DIM), kv4(HEAD_DIM), kv4(LANES),
        tok3(GM_WIDTH), tok3(GM_WIDTH), tok3(D), tok3(D),
    ]
    return pl.pallas_call(
        _inproj_kernel,
        grid=(B, S // tm),
        in_specs=[tok3(D), per_b, per_b, full(D, IN_COLS_P),
                  full(Q_WIDTH, Q_WIDTH), full(2 * KV_WIDTH, 2 * KV_WIDTH),
                  full(1, Q_WIDTH), full(1, 2 * KV_WIDTH),
                  tok3(LANES), tok3(LANES), tok3(LANES),
                  full(1, GM_WIDTH), full(1, GM_WIDTH)],
        out_specs=out_specs,
        out_shape=out_shape,
        compiler_params=_cparams("arbitrary", "arbitrary"),
        name="in_proj",
    )(x, sc, sh, w_p, _block_diag_mean(Q_WIDTH), _block_diag_mean(2 * KV_WIDTH), qg, kg,
      cos, sl, shi, lng, lnb)


def _compress_kernel(kr_ref, vr_ref, w1k_ref, w2k_ref, pek_ref, w1v_ref, w2v_ref, pev_ref,
                     kg_ref, kc_ref, vc_ref):
    nc = kr_ref.shape[2]
    half = CMP_STRIDE * HEAD_DIM

    def mlp(raw, w1_ref, w2_ref, pe_ref):
        a = raw.astype(BF16)
        top = _dot(a, w1_ref[:half, :])
        bot = _dot(a, w1_ref[half:, :])
        pe = jnp.broadcast_to(pe_ref[...], (8, 2 * half)).astype(BF16)
        pe_row = _dot(pe, w1_ref[...])[0:1, :]
        hid = top + pltpu.roll(bot, nc - 1, 0) + pe_row
        return _dot(_gelu(hid).astype(BF16), w2_ref[...])

    kc = mlp(kr_ref[0, 0], w1k_ref, w2k_ref, pek_ref)
    ms = jnp.mean(kc * kc, axis=-1, keepdims=True)
    kc_ref[0, 0] = (kc * lax.rsqrt(ms + EPS) * kg_ref[...]).astype(BF16)
    vc_ref[0, 0] = mlp(vr_ref[0, 0], w1v_ref, w2v_ref, pev_ref).astype(BF16)


def _compress_call(kc_raw, vc_raw, w1k, w2k, pek, w1v, w2v, pev, kg):
    B, H, S, hd = kc_raw.shape
    nc = S // CMP_STRIDE
    feat = CMP_STRIDE * hd
    raw = pl.BlockSpec((1, 1, nc, feat), lambda b, h: (b, h, 0, 0))
    full = lambda *shape: pl.BlockSpec(shape, lambda b, h: (0,) * len(shape))
    out = pl.BlockSpec((1, 1, nc, hd), lambda b, h: (b, h, 0, 0))
    return pl.pallas_call(
        _compress_kernel,
        grid=(B, H),
        in_specs=[raw, raw, full(2 * feat, CMP_HIDDEN), full(CMP_HIDDEN, hd), full(1, 2 * feat),
                  full(2 * feat, CMP_HIDDEN), full(CMP_HIDDEN, hd), full(1, 2 * feat),
                  full(1, hd)],
        out_specs=[out, out],
        out_shape=[jax.ShapeDtypeStruct((B, H, nc, hd), BF16)] * 2,
        compiler_params=_cparams("arbitrary", "arbitrary"),
        name="compress",
    )(kc_raw.reshape(B, H, nc, feat), vc_raw.reshape(B, H, nc, feat),
      w1k, w2k, pek, w1v, w2v, pev, kg)


def _group_rows(a):
    return jnp.concatenate([a[:, g * HEAD_DIM:(g + 1) * HEAD_DIM] for g in range(NSA_GROUP)], axis=0)


def _nsa_cmp_kernel(qn_ref, qr_ref, kc_ref, vc_ref, ovl_ref, oc_ref, qa_ref, *, n_sel):
    tq = qn_ref.shape[1]
    nc = kc_ref.shape[2]
    G = NSA_GROUP
    q0 = pl.program_id(2) * tq
    q4 = _group_rows(qn_ref[0])
    s = _dot_nt(q4, kc_ref[0, 0])
    row = lax.broadcasted_iota(jnp.int32, (G, tq, nc), 1).reshape(G * tq, nc)
    col = lax.broadcasted_iota(jnp.int32, (G * tq, nc), 1)
    vis = col * CMP_STRIDE + (CMP_LEN - 1) <= q0 + row
    s = jnp.where(vis, s, MASK_NEG)
    m = jnp.max(s, axis=-1, keepdims=True)
    e = jnp.where(vis, jnp.exp(s - m), 0.0)
    p = e / jnp.maximum(jnp.sum(e, axis=-1, keepdims=True), 1e-30)
    oc = _dot(p.astype(BF16), vc_ref[0, 0])
    oc_ref[0, 0] = oc.reshape(G, tq, HEAD_DIM)

    psum = p[0:tq] + p[tq:2 * tq] + p[2 * tq:3 * tq] + p[3 * tq:4 * tq]
    imp = _dot_nt(ovl_ref[...], psum, precision=HI)[:SEL_BIAS_WIDTH]
    blk = lax.broadcasted_iota(jnp.int32, (SEL_BIAS_WIDTH, tq), 0)
    cur = (q0 + lax.broadcasted_iota(jnp.int32, (SEL_BIAS_WIDTH, tq), 1)) // SEL_LEN
    valid = blk <= cur
    forced = (blk == 0) | (blk == cur) | (blk == cur - 1)
    cand = valid & jnp.logical_not(forced)
    n_forced = jnp.minimum(cur, 2) + 1
    val = jnp.where(cand, imp, -1.0)
    cnt = jnp.zeros((SEL_BIAS_WIDTH, tq), F32)
    for j in range(n_sel):
        vj = jnp.broadcast_to(val[j:j + 1, :], (SEL_BIAS_WIDTH, tq))
        cnt = cnt + jnp.where(blk > j, jnp.where(vj >= val, 1.0, 0.0), jnp.where(vj > val, 1.0, 0.0))
    free = (min(SEL_TOPN, n_sel) - n_forced).astype(F32)
    sel = (forced & valid) | (cand & (cnt < free))
    sel_t = jnp.concatenate([jnp.where(sel, 1.0, 0.0), jnp.zeros((LANES - SEL_BIAS_WIDTH, tq), F32)], axis=0)
    sel_q = sel_t.T
    bias = jnp.where(sel_q > 0.5, 0.0, SEL_NEG)
    lane = lax.broadcasted_iota(jnp.int32, (tq, LANES), 1)
    bias_hi = pltpu.roll(bias, SEL_BIAS_WIDTH, 1)
    qr = qr_ref[0]
    for g in range(G):
        qg = qr[:, g * HEAD_DIM:(g + 1) * HEAD_DIM].astype(F32)
        qg = jnp.concatenate([qg, qg], axis=-1)
        qa_ref[0, 0, g] = jnp.where(lane < HEAD_DIM, qg, bias_hi).astype(BF16)


def _overlap_t(S):
    n_cmp = (S - CMP_LEN) // CMP_STRIDE + 1
    nc = S // CMP_STRIDE
    n_sel = S // SEL_LEN
    start = np.arange(nc) * CMP_STRIDE
    end = start + CMP_LEN - 1
    sel_start = np.arange(n_sel) * SEL_LEN
    ov = (start[None, :] <= sel_start[:, None] + SEL_LEN - 1) & (end[None, :] >= sel_start[:, None])
    ov = ov & (np.arange(nc) < n_cmp)[None, :]
    out = np.zeros((LANES, nc), np.float32)
    out[:n_sel] = ov.astype(np.float32)
    return jnp.asarray(out)


def _nsa_cmp_call(qn, qr, kc, vc, tq):
    B, S, _ = qn.shape
    H, G = NSA_KV_HEADS, NSA_GROUP
    nc = kc.shape[2]
    n_sel = S // SEL_LEN
    assert n_sel <= SEL_BIAS_WIDTH and nc % LANES == 0
    qspec = pl.BlockSpec((1, tq, G * HEAD_DIM), lambda b, h, i: (b, i, h))
    cspec = pl.BlockSpec((1, 1, nc, HEAD_DIM), lambda b, h, i: (b, h, 0, 0))
    return pl.pallas_call(
        functools.partial(_nsa_cmp_kernel, n_sel=n_sel),
        grid=(B, H, S // tq),
        in_specs=[qspec, qspec, cspec, cspec, pl.BlockSpec((LANES, nc), lambda b, h, i: (0, 0))],
        out_specs=[pl.BlockSpec((1, 1, G, tq, HEAD_DIM), lambda b, h, i: (b, h, 0, i, 0)),
                   pl.BlockSpec((1, 1, G, tq, LANES), lambda b, h, i: (b, h, 0, i, 0))],
        out_shape=[jax.ShapeDtypeStruct((B, H, G, S, HEAD_DIM), F32),
                   jax.ShapeDtypeStruct((B, H, G, S, LANES), BF16)],
        compiler_params=_cparams("arbitrary", "arbitrary", "arbitrary"),
        name="nsa_cmp",
    )(qn, qr, kc, vc, _overlap_t(S))


def _nsa_flash_kernel(qa_ref, ks_ref, vs_ref, kw_ref, vw_ref, oc_ref, g_ref, o_ref, m_scr, acc_scr):
    G = NSA_GROUP
    tq = qa_ref.shape[3]
    R = G * tq
    tk = tq
    i = pl.program_id(2)
    q = qa_ref[0, 0].reshape(R, LANES)
    rowq = lax.broadcasted_iota(jnp.int32, (G, tq, tk), 1).reshape(R, tk)
    colk = lax.broadcasted_iota(jnp.int32, (R, tk), 1)

    def reset():
        m_scr[...] = jnp.full((R, LANES), MASK_NEG, F32)
        acc_scr[...] = jnp.zeros((R, LANES), F32)

    def step(k_ref, v_ref, j, mask):
        off = pl.multiple_of(j * tk, tk)
        k = k_ref[0, 0, pl.ds(off, tk), :]
        v = v_ref[0, 0, pl.ds(off, tk), :]
        s = _dot_nt(q, k)
        if mask is not None:
            s = jnp.where(mask, s, MASK_NEG)
        m_prev = m_scr[...]
        m_new = jnp.maximum(m_prev, jnp.max(s, axis=-1, keepdims=True))
        alpha = jnp.exp(m_prev - m_new)
        p = jnp.exp(s - m_new[:, 0:1])
        acc_scr[...] = alpha * acc_scr[...] + _dot(p.astype(BF16), v)
        m_scr[...] = m_new

    def result():
        acc = acc_scr[...]
        return acc[:, :HEAD_DIM] / acc[:, HEAD_DIM:HEAD_DIM + 1]

    reset()

    def sel_body(j, carry):
        step(ks_ref, vs_ref, j, None)
        return carry

    lax.fori_loop(0, i, sel_body, 0)
    step(ks_ref, vs_ref, i, colk <= rowq)
    o_sel = result()

    reset()
    assert WINDOW == 2 * tk

    @pl.when(i >= 2)
    def _():
        step(kw_ref, vw_ref, i - 2, colk > rowq)

    @pl.when(i >= 1)
    def _():
        step(kw_ref, vw_ref, i - 1, None)

    step(kw_ref, vw_ref, i, colk <= rowq)
    o_win = result()

    o_cmp = oc_ref[0, 0].reshape(R, HEAD_DIM)
    gates = g_ref[0, 0]
    outs = []
    for g in range(G):
        r = slice(g * tq, (g + 1) * tq)
        outs.append(gates[:, 3 * g:3 * g + 1] * o_cmp[r]
                    + gates[:, 3 * g + 1:3 * g + 2] * o_sel[r]
                    + gates[:, 3 * g + 2:3 * g + 3] * o_win[r])
    o_ref[0] = jnp.concatenate(outs, axis=-1).astype(BF16)


def _nsa_flash_call(qa, ks, vs, kw, vw, oc, gates):
    B, H, G, S, _ = qa.shape
    tq = WINDOW // 2
    R = G * tq
    kv = pl.BlockSpec((1, 1, S, LANES), lambda b, h, i: (b, h, 0, 0))
    return pl.pallas_call(
        _nsa_flash_kernel,
        grid=(B, H, S // tq),
        in_specs=[pl.BlockSpec((1, 1, G, tq, LANES), lambda b, h, i: (b, h, 0, i, 0)),
                  kv, kv, kv, kv,
                  pl.BlockSpec((1, 1, G, tq, HEAD_DIM), lambda b, h, i: (b, h, 0, i, 0)),
                  pl.BlockSpec((1, 1, tq, LANES), lambda b, h, i: (b, h, i, 0))],
        out_specs=pl.BlockSpec((1, tq, G * HEAD_DIM), lambda b, h, i: (b, i, h)),
        out_shape=jax.ShapeDtypeStruct((B, S, Q_WIDTH), BF16),
        scratch_shapes=[pltpu.VMEM((R, LANES), F32), pltpu.VMEM((R, LANES), F32)],
        compiler_params=_cparams("arbitrary", "arbitrary", "arbitrary"),
        name="nsa_flash",
    )(qa, ks, vs, kw, vw, oc, gates)


def _gmlp_kernel(u_ref, v_ref, ws_ref, bs_ref, o_ref):
    tm = u_ref.shape[1]
    C = GM_CHUNK
    r = lax.broadcasted_iota(jnp.int32, (C, C), 0)
    c = lax.broadcasted_iota(jnp.int32, (C, C), 1)
    causal = c <= r
    ws = [jnp.where(causal, ws_ref[g], 0.0).astype(BF16) for g in range(GM_GROUPS)]
    for n in range(tm // C):
        rows = slice(n * C, (n + 1) * C)
        vn = v_ref[0, rows, :]
        mixed = jnp.concatenate(
            [_dot(ws[g], vn[:, g * GM_GROUP_DIM:(g + 1) * GM_GROUP_DIM]) for g in range(GM_GROUPS)],
            axis=-1)
        o_ref[0, rows, :] = (u_ref[0, rows, :].astype(F32) * (mixed + bs_ref[...])).astype(BF16)


def _gmlp_call(u, v, ws, bs_full, tm):
    B, S, W = u.shape
    tok = pl.BlockSpec((1, tm, W), lambda b, i: (b, i, 0))
    return pl.pallas_call(
        _gmlp_kernel,
        grid=(B, S // tm),
        in_specs=[tok, tok,
                  pl.BlockSpec((GM_GROUPS, GM_CHUNK, GM_CHUNK), lambda b, i: (0, 0, 0)),
                  pl.BlockSpec((GM_CHUNK, W), lambda b, i: (0, 0))],
        out_specs=tok,
        out_shape=jax.ShapeDtypeStruct((B, S, W), BF16),
        compiler_params=_cparams("arbitrary", "arbitrary"),
        name="gmlp",
    )(u, v, ws, bs_full)


def _mixout_kernel(on_ref, og_ref, ga_ref, gb_ref, x_ref, gt_ref, sc_ref, sh_ref,
                   wpn_ref, wpg_ref, wo_ref, wr_ref, wgs_ref, wus_ref, wds_ref,
                   x1_ref, h2_ref, lg_ref, shd_ref):
    ya = _dot(on_ref[0], wpn_ref[...])
    yb = _dot(og_ref[0], wpg_ref[...])
    merged = ga_ref[0].astype(F32) * ya + gb_ref[0].astype(F32) * yb
    x1 = x_ref[0] + gt_ref[0] * _dot(merged.astype(BF16), wo_ref[...])
    x1_ref[0] = x1
    ms = jnp.mean(x1 * x1, axis=-1, keepdims=True)
    h2 = (x1 * lax.rsqrt(ms + EPS)) * (1.0 + sc_ref[0]) + sh_ref[0]
    lg_ref[0] = _dot(h2, wr_ref[...], precision=HI)
    hb = h2.astype(BF16)
    h2_ref[0] = _pack_halves(hb)
    act = _silu(_dot(hb, wgs_ref[...])) * _dot(hb, wus_ref[...])
    shd_ref[0] = _dot(act.astype(BF16), wds_ref[...])


def _mixout_call(o_nsa, o_gm, ga, gb, x, gt, sc, sh, wpn, wpg, wo, wr, wgs, wus, wds, tm):
    B, S, D = x.shape
    tok = lambda w: pl.BlockSpec((1, tm, w), lambda b, i: (b, i, 0))
    per_b = pl.BlockSpec((1, 1, D), lambda b, i: (b, 0, 0))
    full = lambda a: pl.BlockSpec(a.shape, lambda b, i: (0,) * a.ndim)
    sds = jax.ShapeDtypeStruct
    return pl.pallas_call(
        _mixout_kernel,
        grid=(B, S // tm),
        in_specs=[tok(Q_WIDTH), tok(GM_WIDTH), tok(D), tok(D), tok(D), per_b, per_b, per_b,
                  full(wpn), full(wpg), full(wo), full(wr), full(wgs), full(wus), full(wds)],
        out_specs=[tok(D), tok(D // 2), tok(N_EXPERTS), tok(D)],
        out_shape=[sds((B, S, D), F32), sds((B, S, D // 2), jnp.uint32), sds((B, S, N_EXPERTS), F32),
                   sds((B, S, D), F32)],
        compiler_params=_cparams("arbitrary", "arbitrary"),
        name="mix_out",
    )(o_nsa, o_gm, ga, gb, x, gt, sc, sh, wpn, wpg, wo, wr, wgs, wus, wds)


def _route_kernel(lg_ref, br_ref, idx_ref, wt_ref, rank_ref, cnt_ref, run_scr):
    tm, E = lg_ref.shape

    @pl.when(pl.program_id(0) == 0)
    def _():
        run_scr[...] = jnp.zeros_like(run_scr)

    aff = _sigmoid(lg_ref[...])
    work = aff + br_ref[...]
    lane = lax.broadcasted_iota(jnp.int32, (tm, E), 1).astype(F32)
    picked = jnp.zeros((tm, E), F32)
    idxs, tops = [], []
    for _ in range(TOP_K):
        m = jnp.max(work, axis=-1, keepdims=True)
        idx = jnp.min(jnp.where(work == m, lane, float(E)), axis=-1, keepdims=True)
        hit = lane == idx
        tops.append(jnp.sum(jnp.where(hit, aff, 0.0), axis=-1, keepdims=True))
        idxs.append(idx)
        picked = jnp.where(hit, 1.0, picked)
        work = jnp.where(hit, -jnp.inf, work)
    total = tops[0]
    for t in tops[1:]:
        total = total + t
    r = lax.broadcasted_iota(jnp.int32, (tm, tm), 0)
    c = lax.broadcasted_iota(jnp.int32, (tm, tm), 1)
    before = _dot(jnp.where(c < r, 1.0, 0.0).astype(BF16), picked.astype(BF16)) + run_scr[...]
    ranks = [jnp.sum(jnp.where(lane == idx, before, 0.0), axis=-1, keepdims=True) for idx in idxs]
    run_scr[...] = run_scr[...] + jnp.sum(picked, axis=0, keepdims=True)
    cnt_ref[...] = run_scr[...]
    lane_k = lax.broadcasted_iota(jnp.int32, (tm, LANES), 1)
    idx_o = jnp.zeros((tm, LANES), jnp.int32)
    wt_o = jnp.zeros((tm, LANES), F32)
    rank_o = jnp.zeros((tm, LANES), jnp.int32)
    for k in range(TOP_K):
        idx_o = jnp.where(lane_k == k, idxs[k].astype(jnp.int32), idx_o)
        wt_o = jnp.where(lane_k == k, tops[k] / total * ROUTE_SCALE, wt_o)
        rank_o = jnp.where(lane_k == k, ranks[k].astype(jnp.int32), rank_o)
    idx_ref[...] = idx_o
    wt_ref[...] = wt_o
    rank_ref[...] = rank_o


def _route_call(logits, b_router, tm):
    T, E = logits.shape
    tok = lambda w: pl.BlockSpec((tm, w), lambda i: (i, 0))
    one = pl.BlockSpec((1, E), lambda i: (0, 0))
    sds = jax.ShapeDtypeStruct
    return pl.pallas_call(
        _route_kernel,
        grid=(T // tm,),
        in_specs=[tok(E), one],
        out_specs=[tok(LANES), tok(LANES), tok(LANES), one],
        out_shape=[sds((T, LANES), jnp.int32), sds((T, LANES), F32), sds((T, LANES), jnp.int32),
                   sds((1, E), F32)],
        scratch_shapes=[pltpu.VMEM((1, E), F32)],
        compiler_params=_cparams("arbitrary"),
        name="route",
    )(logits, b_router.reshape(1, E))


def _slot_kernel(idx_ref, rank_ref, ps_ref, o_ref):
    tm = idx_ref.shape[0]
    E = ps_ref.shape[1]
    lane = lax.broadcasted_iota(jnp.int32, (tm, E), 1)
    lane_k = lax.broadcasted_iota(jnp.int32, (tm, LANES), 1)
    idx, rank, ps = idx_ref[...], rank_ref[...], ps_ref[...]
    out = jnp.zeros((tm, LANES), jnp.int32)
    for k in range(TOP_K):
        base = jnp.sum(jnp.where(lane == idx[:, k:k + 1], ps, 0.0), axis=-1, keepdims=True)
        out = jnp.where(lane_k == k, base.astype(jnp.int32) + rank[:, k:k + 1], out)
    o_ref[...] = out[:, :TOP_K]


def _slot_call(idx, rank, pad_start, tm):
    T = idx.shape[0]
    E = pad_start.shape[0]
    tok = pl.BlockSpec((tm, LANES), lambda i: (i, 0))
    return pl.pallas_call(
        _slot_kernel,
        grid=(T // tm,),
        in_specs=[tok, tok, pl.BlockSpec((1, E), lambda i: (0, 0))],
        out_specs=pl.BlockSpec((tm, TOP_K), lambda i: (i, 0)),
        out_shape=jax.ShapeDtypeStruct((T, TOP_K), jnp.int32),
        compiler_params=_cparams("arbitrary"),
        name="slot",
    )(idx, rank, pad_start.astype(F32).reshape(1, E))


def _dispatch_kernel(bv_ref, slot_ref, h_ref, xs_ref, zbuf, sem, zsem):
    tm = h_ref.shape[0]
    rows = zbuf.shape[0]

    @pl.when(pl.program_id(0) == 0)
    def _():
        zbuf[...] = jnp.zeros_like(zbuf)

        def zero_copy(i):
            return pltpu.make_async_copy(zbuf, xs_ref.at[pl.ds(pl.multiple_of(i * rows, rows), rows), :], zsem)

        def start(i, carry):
            @pl.when(bv_ref[i] < rows)
            def _():
                zero_copy(i).start()
            return carry

        def wait(i, carry):
            @pl.when(bv_ref[i] < rows)
            def _():
                zero_copy(i).wait()
            return carry

        lax.fori_loop(0, bv_ref.shape[0], start, 0)
        lax.fori_loop(0, bv_ref.shape[0], wait, 0)

    def body(r, carry):
        for k in range(TOP_K):
            d = slot_ref[r * TOP_K + k]
            pltpu.make_async_copy(h_ref.at[pl.ds(r, 1), :], xs_ref.at[pl.ds(d, 1), :], sem).start()
        return carry

    lax.fori_loop(0, tm, body, 0)
    for k in range(TOP_K):
        pltpu.make_async_copy(h_ref, xs_ref.at[pl.ds(0, tm), :], sem).wait()


def _dispatch_call(blk_valid, slots, h2p, n_rows, tm):
    T, W = h2p.shape
    grid_spec = pltpu.PrefetchScalarGridSpec(
        num_scalar_prefetch=1,
        grid=(T // tm,),
        in_specs=[pl.BlockSpec((tm * TOP_K,), lambda i, bv: (i,), memory_space=pltpu.SMEM),
                  pl.BlockSpec((tm, W), lambda i, bv: (i, 0))],
        out_specs=pl.BlockSpec(memory_space=pl.ANY),
        scratch_shapes=[pltpu.VMEM((MOE_ROWS, W), jnp.uint32), pltpu.SemaphoreType.DMA, pltpu.SemaphoreType.DMA],
    )
    return pl.pallas_call(
        _dispatch_kernel,
        grid_spec=grid_spec,
        out_shape=jax.ShapeDtypeStruct((n_rows, W), jnp.uint32),
        compiler_params=pltpu.CompilerParams(dimension_semantics=("arbitrary",), vmem_limit_bytes=VMEM_LIMIT,
                                             has_side_effects=True),
        name="dispatch",
    )(blk_valid, slots, h2p)


def _expert_kernel(be_ref, nb_ref, x_ref, wg_ref, wu_ref, wd_ref, y_ref):
    i = pl.program_id(0)

    @pl.when(i < nb_ref[0])
    def _():
        lo, hi = _unpack_halves(x_ref[...])
        x = jnp.concatenate([lo, hi], axis=1).astype(BF16)
        a = _dot(x, wg_ref[0, 0].astype(BF16))
        b = _dot(x, wu_ref[0, 0].astype(BF16))
        y = _dot((_silu(a) * b).astype(BF16), wd_ref[0, 0].astype(BF16))
        y_ref[...] = _pack_halves(y)

    @pl.when(i >= nb_ref[0])
    def _():
        y_ref[...] = jnp.zeros_like(y_ref)


def _expert_call(layer, blk_e, n_used, xs, wg, wu, wd):
    n_rows, W = xs.shape
    rows = MOE_ROWS
    D, F = wg.shape[2], wg.shape[3]
    wspec = lambda a, b: pl.BlockSpec((1, 1, a, b), lambda i, be, nb: (layer, be[i], 0, 0))
    grid_spec = pltpu.PrefetchScalarGridSpec(
        num_scalar_prefetch=2,
        grid=(n_rows // rows,),
        in_specs=[pl.BlockSpec((rows, W), lambda i, be, nb: (jnp.minimum(i, nb[0] - 1), 0)),
                  wspec(D, F), wspec(D, F), wspec(F, D)],
        out_specs=pl.BlockSpec((rows, W), lambda i, be, nb: (i, 0)),
    )
    return pl.pallas_call(
        _expert_kernel,
        grid_spec=grid_spec,
        out_shape=jax.ShapeDtypeStruct((n_rows, W), jnp.uint32),
        compiler_params=_cparams("arbitrary"),
        name="expert",
    )(blk_e, n_used, xs, wg, wu, wd)


def _combine_kernel(slot_ref, wt_ref, shd_ref, x_ref, gt_ref, y_ref, o_ref, ybuf, sem):
    tm = x_ref.shape[0]

    def body(r, carry):
        for k in range(TOP_K):
            d = slot_ref[r * TOP_K + k]
            pltpu.make_async_copy(y_ref.at[pl.ds(d, 1), :], ybuf.at[k, pl.ds(r, 1), :], sem).start()
        return carry

    lax.fori_loop(0, tm, body, 0)
    for k in range(TOP_K):
        pltpu.make_async_copy(y_ref.at[pl.ds(0, tm), :], ybuf.at[k], sem).wait()
    wt = wt_ref[...]
    acc_lo = acc_hi = None
    for k in range(TOP_K):
        lo, hi = _unpack_halves(ybuf[k])
        w = wt[:, k:k + 1]
        acc_lo = w * lo if k == 0 else acc_lo + w * lo
        acc_hi = w * hi if k == 0 else acc_hi + w * hi
    routed = jnp.concatenate([acc_lo, acc_hi], axis=1)
    o_ref[...] = x_ref[...] + gt_ref[0] * (routed + shd_ref[...])


def _combine_call(slots, wts, shared, x1, gt, y, tm):
    T, D = x1.shape
    B = gt.shape[0]
    per_b = T // B // tm
    tok = lambda w: pl.BlockSpec((tm, w), lambda i: (i, 0))
    return pl.pallas_call(
        _combine_kernel,
        grid=(T // tm,),
        in_specs=[pl.BlockSpec((tm * TOP_K,), lambda i: (i,), memory_space=pltpu.SMEM),
                  tok(LANES), tok(D), tok(D),
                  pl.BlockSpec((1, 1, D), lambda i: (i // per_b, 0, 0)),
                  pl.BlockSpec(memory_space=pl.ANY)],
        out_specs=tok(D),
        out_shape=jax.ShapeDtypeStruct((T, D), F32),
        scratch_shapes=[pltpu.VMEM((TOP_K, tm, y.shape[1]), jnp.uint32), pltpu.SemaphoreType.DMA],
        compiler_params=_cparams("arbitrary"),
        name="combine",
    )(slots, wts, shared, x1, gt, y)


def _rope_tables(positions):
    half = HEAD_DIM // 2
    inv = ROPE_THETA ** (-jnp.arange(half, dtype=F32) / half)
    ang = positions.astype(F32)[..., None] * inv
    cos, sin, zero = jnp.cos(ang), jnp.sin(ang), jnp.zeros_like(ang)
    cos_t = jnp.concatenate([cos, cos] * 2, axis=-1)
    sin_lo = jnp.concatenate([-sin, zero] * 2, axis=-1)
    sin_hi = jnp.concatenate([zero, sin] * 2, axis=-1)
    return cos_t, sin_lo, sin_hi


def _reorder_w_in(w):
    o = np.cumsum([0, Q_WIDTH] + [KV_WIDTH] * 6 + [NSA_Q_HEADS * 3, GM_WIDTH, GM_WIDTH, D_MODEL, D_MODEL])
    q, kc, vc, ks, vs, kw, vw, g, u, v, ga, gb = [w[:, o[i]:o[i + 1]] for i in range(12)]
    per = NSA_GROUP * 3
    pad = jnp.zeros((w.shape[0], LANES - per), w.dtype)
    return jnp.concatenate([q, ks, kw, kc, vc, vs, vw, g[:, :per], pad, g[:, per:], pad, u, v, ga, gb], axis=1)


def _block_plan(counts, n_rows):
    rows = MOE_ROWS
    counts = counts.astype(jnp.int32)
    padded = (counts + rows - 1) // rows * rows
    pad_end = jnp.cumsum(padded)
    pad_start = pad_end - padded
    blk_start = jnp.arange(n_rows // rows, dtype=jnp.int32) * rows
    blk_e = jnp.minimum(jnp.sum((pad_end[None, :] <= blk_start[:, None]).astype(jnp.int32), axis=1), N_EXPERTS - 1)
    own = blk_e[:, None] == jnp.arange(N_EXPERTS, dtype=jnp.int32)[None, :]
    e_count = jnp.sum(jnp.where(own, counts[None, :], 0), axis=1)
    e_start = jnp.sum(jnp.where(own, pad_start[None, :], 0), axis=1)
    blk_valid = jnp.clip(e_count - (blk_start - e_start), 0, rows).astype(jnp.int32)
    n_used = (pad_end[-1:] // rows).astype(jnp.int32)
    return pad_start, blk_e.astype(jnp.int32), blk_valid, n_used


def kernel(x, c, positions, w_mod, b_mod, w_in, q_gain, k_gain, cmp_pos_k, cmp_pos_v, cmp_w1_k, cmp_w2_k, cmp_w1_v, cmp_w2_v, gm_ln_g, gm_ln_b, gm_ws, gm_bs, w_proj_nsa, w_proj_gm, w_out, w_router, b_router, w_gate_e, w_up_e, w_down_e, w_gate_sh, w_up_sh, w_down_sh):
    B, S, D = x.shape
    L = w_mod.shape[0]
    T = B * S
    tm = 256
    scale = HEAD_DIM ** -0.5
    cos_t, sin_lo, sin_hi = _rope_tables(positions)
    mod = _mod_call(c, w_mod, b_mod)
    n_rows = T * TOP_K + N_EXPERTS * MOE_ROWS

    for l in range(L):
        sh_a, sc_a, gt_a, sh_f, sc_f, gt_f = [mod[l, :, i * D:(i + 1) * D].reshape(B, 1, D) for i in range(6)]
        qg = (jnp.tile(q_gain[l], NSA_Q_HEADS) * scale).reshape(1, Q_WIDTH)
        kg = jnp.tile(k_gain[l], 2 * NSA_KV_HEADS).reshape(1, 2 * KV_WIDTH)
        (qn, qr, ks, kw, vs, vw, kc_raw, vc_raw, gates, u, v, ga, gb) = _inproj_call(
            x, sc_a, sh_a, _reorder_w_in(w_in[l]).astype(BF16), qg, kg, cos_t, sin_lo, sin_hi,
            gm_ln_g[l].reshape(1, GM_WIDTH), gm_ln_b[l].reshape(1, GM_WIDTH), tm)
        kc, vc = _compress_call(
            kc_raw, vc_raw, cmp_w1_k[l].astype(BF16), cmp_w2_k[l].astype(BF16), cmp_pos_k[l].reshape(1, -1),
            cmp_w1_v[l].astype(BF16), cmp_w2_v[l].astype(BF16), cmp_pos_v[l].reshape(1, -1),
            k_gain[l].reshape(1, HEAD_DIM))
        o_cmp, q_aug = _nsa_cmp_call(qn, qr, kc, vc, WINDOW // 2)
        o_nsa = _nsa_flash_call(q_aug, ks, vs, kw, vw, o_cmp, gates)
        bs_full = jnp.repeat(gm_bs[l].T, GM_GROUP_DIM, axis=1)
        o_gm = _gmlp_call(u, v, gm_ws[l], bs_full, 512)
        x1, h2, logits, shared = _mixout_call(
            o_nsa, o_gm, ga, gb, x, gt_a, sc_f, sh_f,
            w_proj_nsa[l].astype(BF16), w_proj_gm[l].astype(BF16), w_out[l].astype(BF16), w_router[l],
            w_gate_sh[l].astype(BF16), w_up_sh[l].astype(BF16), w_down_sh[l].astype(BF16), tm)
        idx, wts, rank, counts = _route_call(logits.reshape(T, N_EXPERTS), b_router[l], tm)
        pad_start, blk_e, blk_valid, n_used = _block_plan(counts[0], n_rows)
        slots = _slot_call(idx, rank, pad_start, tm).reshape(T * TOP_K)
        xs = _dispatch_call(blk_valid, slots, h2.reshape(T, D // 2), n_rows, tm)
        y = _expert_call(l, blk_e, n_used, xs, w_gate_e, w_up_e, w_down_e)
        x = _combine_call(slots, wts, shared.reshape(T, D), x1.reshape(T, D), gt_f, y, tm).reshape(B, S, D)
    return x
```

```python
import functools

import jax
import jax.numpy as jnp
import numpy as np
from jax import lax
from jax.experimental import pallas as pl
from jax.experimental.pallas import tpu as pltpu

D_MODEL = 1024
NSA_Q_HEADS = 8
NSA_KV_HEADS = 2
HEAD_DIM = 64
NSA_GROUP = NSA_Q_HEADS // NSA_KV_HEADS
CMP_LEN = 32
CMP_STRIDE = 16
CMP_HIDDEN = 256
SEL_LEN = 64
SEL_TOPN = 16
WINDOW = 512
ROPE_THETA = 10000.0
Q_WIDTH = NSA_Q_HEADS * HEAD_DIM
KV_WIDTH = NSA_KV_HEADS * HEAD_DIM
GM_GROUPS = 8
GM_GROUP_DIM = 64
GM_WIDTH = GM_GROUPS * GM_GROUP_DIM
GM_CHUNK = 128
N_EXPERTS = 256
TOP_K = 8
D_EXPERT = 256
D_SHARED = 256
ROUTE_SCALE = 2.5
EPS = 1e-6

LANES = 128
SEL_BIAS_WIDTH = 64
MASK_NEG = -1e30
SEL_NEG = -30000.0
MOE_ROWS = 256
VMEM_LIMIT = 56 * 1024 * 1024

F32 = jnp.float32
BF16 = jnp.bfloat16
HI = lax.Precision.HIGHEST


def _cparams(*sem):
    return pltpu.CompilerParams(dimension_semantics=sem, vmem_limit_bytes=VMEM_LIMIT)


def _dot(a, b, **kw):
    return jnp.dot(a, b, preferred_element_type=F32, **kw)


def _dot_nt(a, b, **kw):
    return lax.dot_general(a, b, (((1,), (1,)), ((), ())), preferred_element_type=F32, **kw)


def _gelu(x):
    return 0.5 * x * (1.0 + jnp.tanh(0.7978845608028654 * (x + 0.044715 * (x * x * x))))


def _sigmoid(x):
    return 1.0 / (1.0 + jnp.exp(-x))


def _silu(x):
    return x * _sigmoid(x)


_HI_MASK = np.uint32(0xFFFF0000)


def _pack_halves(a):
    w = a.shape[1] // 2
    bits = lax.bitcast_convert_type(a.astype(BF16).astype(F32), jnp.uint32)
    return (bits[:, w:] & _HI_MASK) | (bits[:, :w] >> 16)


def _unpack_halves(words):
    lo = lax.bitcast_convert_type(words << 16, F32)
    hi = lax.bitcast_convert_type(words & _HI_MASK, F32)
    return lo, hi


def _mod_kernel(c_ref, w_ref, b_ref, o_ref):
    c = c_ref[...]
    o_ref[0] = _dot(_silu(c), w_ref[0], precision=HI) + b_ref[0]


def _mod_call(c, w_mod, b_mod):
    L, D, N = w_mod.shape
    B = c.shape[0]
    tn = 1536
    return pl.pallas_call(
        _mod_kernel,
        grid=(L, N // tn),
        in_specs=[pl.BlockSpec((B, D), lambda l, j: (0, 0)),
                  pl.BlockSpec((1, D, tn), lambda l, j: (l, 0, j)),
                  pl.BlockSpec((1, 1, tn), lambda l, j: (l, 0, j))],
        out_specs=pl.BlockSpec((1, B, tn), lambda l, j: (l, 0, j)),
        out_shape=jax.ShapeDtypeStruct((L, B, N), F32),
        compiler_params=_cparams("arbitrary", "arbitrary"),
        name="mod",
    )(c, w_mod, b_mod.reshape(L, 1, N))


_C_Q = 0
_C_K = 512
_C_KC = 768
_C_VC = 896
_C_VS = 1024
_C_VW = 1152
_C_G = 1280
_C_U = 1536
_C_V = 2048
_C_GA = 2560
_C_GB = 3584
IN_COLS_P = 4608


def _head_norm(z, bd):
    ms = _dot(z * z, bd, precision=HI)
    return z * lax.rsqrt(ms + EPS)


def _rope(z, cos, sin_lo, sin_hi):
    w = z.shape[-1]
    half = HEAD_DIM // 2
    return z * cos + pltpu.roll(z, w - half, 1) * sin_lo + pltpu.roll(z, half, 1) * sin_hi


def _tile_lanes(t, n):
    return t if n == 1 else jnp.concatenate([t] * n, axis=-1)


def _inproj_kernel(x_ref, sc_ref, sh_ref, w_ref, bdq_ref, bdk_ref, qg_ref, kg_ref,
                   cos_ref, sl_ref, shi_ref, lng_ref, lnb_ref,
                   qn_ref, qr_ref, ks_ref, kw_ref, vs_ref, vw_ref, kc_ref, vc_ref,
                   g_ref, u_ref, v_ref, ga_ref, gb_ref):
    tm = x_ref.shape[1]
    x = x_ref[0]
    ms = jnp.mean(x * x, axis=-1, keepdims=True)
    h = (x * lax.rsqrt(ms + EPS)) * (1.0 + sc_ref[0]) + sh_ref[0]
    hb = h.astype(BF16)

    def mm(lo, width):
        return _dot(hb, w_ref[:, lo:lo + width])

    cos, sl, shi = cos_ref[0], sl_ref[0], shi_ref[0]

    zq = mm(_C_Q, Q_WIDTH)
    qn = _head_norm(zq, bdq_ref[...]) * qg_ref[...]
    qr = _rope(qn, _tile_lanes(cos, 4), _tile_lanes(sl, 4), _tile_lanes(shi, 4))
    qn_ref[0] = qn.astype(BF16)
    qr_ref[0] = qr.astype(BF16)

    zk = mm(_C_K, 2 * KV_WIDTH)
    kn = _head_norm(zk, bdk_ref[...]) * kg_ref[...]
    kr = _rope(kn, _tile_lanes(cos, 2), _tile_lanes(sl, 2), _tile_lanes(shi, 2))
    lane = lax.broadcasted_iota(jnp.int32, (tm, LANES), 1)
    tok = pl.program_id(1) * tm + lax.broadcasted_iota(jnp.int32, (tm, LANES), 0)
    onehot = jnp.where(lane - HEAD_DIM == tok // SEL_LEN, 1.0, 0.0)
    ones_col = jnp.where(lane == HEAD_DIM, 1.0, 0.0)
    low = lane < HEAD_DIM
    zvs = mm(_C_VS, KV_WIDTH)
    zvw = mm(_C_VW, KV_WIDTH)
    zkc = mm(_C_KC, KV_WIDTH)
    zvc = mm(_C_VC, KV_WIDTH)
    for kv in range(NSA_KV_HEADS):
        def head(a):
            return a if kv == 0 else pltpu.roll(a, HEAD_DIM, 1)
        ks_ref[0, kv] = jnp.where(low, head(kr[:, :KV_WIDTH]), onehot).astype(BF16)
        kw_ref[0, kv] = jnp.where(low, head(kr[:, KV_WIDTH:]), 0.0).astype(BF16)
        vs_ref[0, kv] = jnp.where(low, head(zvs), ones_col).astype(BF16)
        vw_ref[0, kv] = jnp.where(low, head(zvw), ones_col).astype(BF16)
        kc_ref[0, kv] = head(zkc)[:, :HEAD_DIM]
        vc_ref[0, kv] = head(zvc)[:, :HEAD_DIM]

    zg = mm(_C_G, 2 * LANES)
    sg = _sigmoid(zg)
    g_ref[0, 0] = sg[:, :LANES]
    g_ref[0, 1] = sg[:, LANES:]

    u_ref[0] = _gelu(mm(_C_U, GM_WIDTH)).astype(BF16)
    gv = _gelu(mm(_C_V, GM_WIDTH))
    mu = jnp.mean(gv, axis=-1, keepdims=True)
    cen = gv - mu
    var = jnp.mean(cen * cen, axis=-1, keepdims=True)
    v_ref[0] = ((cen * lax.rsqrt(var + EPS)) * lng_ref[...] + lnb_ref[...]).astype(BF16)

    ga_ref[0] = _sigmoid(mm(_C_GA, D_MODEL)).astype(BF16)
    gb_ref[0] = _sigmoid(mm(_C_GB, D_MODEL)).astype(BF16)


def _block_diag_mean(width):
    idx = np.arange(width) // HEAD_DIM
    return jnp.asarray((idx[:, None] == idx[None, :]).astype(np.float32) / HEAD_DIM)


def _inproj_call(x, sc, sh, w_p, qg, kg, cos, sl, shi, lng, lnb, tm):
    B, S, D = x.shape
    H = NSA_KV_HEADS
    full = lambda *shape: pl.BlockSpec(shape, lambda b, i: (0,) * len(shape))
    tok3 = lambda w: pl.BlockSpec((1, tm, w), lambda b, i: (b, i, 0))
    per_b = pl.BlockSpec((1, 1, D), lambda b, i: (b, 0, 0))
    kv4 = lambda w: pl.BlockSpec((1, H, tm, w), lambda b, i: (b, 0, i, 0))
    sds = jax.ShapeDtypeStruct
    out_shape = [
        sds((B, S, Q_WIDTH), BF16), sds((B, S, Q_WIDTH), BF16),
        sds((B, H, S, LANES), BF16), sds((B, H, S, LANES), BF16),
        sds((B, H, S, LANES), BF16), sds((B, H, S, LANES), BF16),
        sds((B, H, S, HEAD_DIM), F32), sds((B, H, S, HEAD_DIM), F32),
        sds((B, H, S, LANES), F32),
        sds((B, S, GM_WIDTH), BF16), sds((B, S, GM_WIDTH), BF16),
        sds((B, S, D), BF16), sds((B, S, D), BF16),
    ]
    out_specs = [
        tok3(Q_WIDTH), tok3(Q_WIDTH), kv4(LANES), kv4(LANES), kv4(LANES), kv4(LANES),
        kv4(HEAD_DIM), kv4(HEAD_DIM), kv4(LANES),
        tok3(GM_WIDTH), tok3(GM_WIDTH), tok3(D), tok3(D),
    ]
    return pl.pallas_call(
        _inproj_kernel,
        grid=(B, S // tm),
        in_specs=[tok3(D), per_b, per_b, full(D, IN_COLS_P),
                  full(Q_WIDTH, Q_WIDTH), full(2 * KV_WIDTH, 2 * KV_WIDTH),
                  full(1, Q_WIDTH), full(1, 2 * KV_WIDTH),
                  tok3(LANES), tok3(LANES), tok3(LANES),
                  full(1, GM_WIDTH), full(1, GM_WIDTH)],
        out_specs=out_specs,
        out_shape=out_shape,
        compiler_params=_cparams("arbitrary", "arbitrary"),
        name="in_proj",
    )(x, sc, sh, w_p, _block_diag_mean(Q_WIDTH), _block_diag_mean(2 * KV_WIDTH), qg, kg,
      cos, sl, shi, lng, lnb)


def _compress_kernel(kr_ref, vr_ref, w1k_ref, w2k_ref, pek_ref, w1v_ref, w2v_ref, pev_ref,
                     kg_ref, kc_ref, vc_ref):
    nc = kr_ref.shape[2]
    half = CMP_STRIDE * HEAD_DIM

    def mlp(raw, w1_ref, w2_ref, pe_ref):
        a = raw.astype(BF16)
        top = _dot(a, w1_ref[:half, :])
        bot = _dot(a, w1_ref[half:, :])
        pe = jnp.broadcast_to(pe_ref[...], (8, 2 * half)).astype(BF16)
        pe_row = _dot(pe, w1_ref[...])[0:1, :]
        hid = top + pltpu.roll(bot, nc - 1, 0) + pe_row
        return _dot(_gelu(hid).astype(BF16), w2_ref[...])

    kc = mlp(kr_ref[0, 0], w1k_ref, w2k_ref, pek_ref)
    ms = jnp.mean(kc * kc, axis=-1, keepdims=True)
    kc_ref[0, 0] = (kc * lax.rsqrt(ms + EPS) * kg_ref[...]).astype(BF16)
    vc_ref[0, 0] = mlp(vr_ref[0, 0], w1v_ref, w2v_ref, pev_ref).astype(BF16)


def _compress_call(kc_raw, vc_raw, w1k, w2k, pek, w1v, w2v, pev, kg):
    B, H, S, hd = kc_raw.shape
    nc = S // CMP_STRIDE
    feat = CMP_STRIDE * hd
    raw = pl.BlockSpec((1, 1, nc, feat), lambda b, h: (b, h, 0, 0))
    full = lambda *shape: pl.BlockSpec(shape, lambda b, h: (0,) * len(shape))
    out = pl.BlockSpec((1, 1, nc, hd), lambda b, h: (b, h, 0, 0))
    return pl.pallas_call(
        _compress_kernel,
        grid=(B, H),
        in_specs=[raw, raw, full(2 * feat, CMP_HIDDEN), full(CMP_HIDDEN, hd), full(1, 2 * feat),
                  full(2 * feat, CMP_HIDDEN), full(CMP_HIDDEN, hd), full(1, 2 * feat),
                  full(1, hd)],
        out_specs=[out, out],
        out_shape=[jax.ShapeDtypeStruct((B, H, nc, hd), BF16)] * 2,
        compiler_params=_cparams("arbitrary", "arbitrary"),
        name="compress",
    )(kc_raw.reshape(B, H, nc, feat), vc_raw.reshape(B, H, nc, feat),
      w1k, w2k, pek, w1v, w2v, pev, kg)


def _group_rows(a):
    return jnp.concatenate([a[:, g * HEAD_DIM:(g + 1) * HEAD_DIM] for g in range(NSA_GROUP)], axis=0)


def _nsa_cmp_kernel(qn_ref, qr_ref, kc_ref, vc_ref, ovl_ref, oc_ref, qa_ref, *, n_sel):
    tq = qn_ref.shape[1]
    nc = kc_ref.shape[2]
    G = NSA_GROUP
    q0 = pl.program_id(2) * tq
    q4 = _group_rows(qn_ref[0])
    s = _dot_nt(q4, kc_ref[0, 0])
    row = lax.broadcasted_iota(jnp.int32, (G, tq, nc), 1).reshape(G * tq, nc)
    col = lax.broadcasted_iota(jnp.int32, (G * tq, nc), 1)
    vis = col * CMP_STRIDE + (CMP_LEN - 1) <= q0 + row
    s = jnp.where(vis, s, MASK_NEG)
    m = jnp.max(s, axis=-1, keepdims=True)
    e = jnp.where(vis, jnp.exp(s - m), 0.0)
    p = e / jnp.maximum(jnp.sum(e, axis=-1, keepdims=True), 1e-30)
    oc = _dot(p.astype(BF16), vc_ref[0, 0])
    oc_ref[0, 0] = oc.reshape(G, tq, HEAD_DIM)

    psum = p[0:tq] + p[tq:2 * tq] + p[2 * tq:3 * tq] + p[3 * tq:4 * tq]
    imp = _dot_nt(ovl_ref[...], psum, precision=HI)[:SEL_BIAS_WIDTH]
    blk = lax.broadcasted_iota(jnp.int32, (SEL_BIAS_WIDTH, tq), 0)
    cur = (q0 + lax.broadcasted_iota(jnp.int32, (SEL_BIAS_WIDTH, tq), 1)) // SEL_LEN
    valid = blk <= cur
    forced = (blk == 0) | (blk == cur) | (blk == cur - 1)
    cand = valid & jnp.logical_not(forced)
    n_forced = jnp.minimum(cur, 2) + 1
    val = jnp.where(cand, imp, -1.0)
    cnt = jnp.zeros((SEL_BIAS_WIDTH, tq), F32)
    for j in range(n_sel):
        vj = jnp.broadcast_to(val[j:j + 1, :], (SEL_BIAS_WIDTH, tq))
        cnt = cnt + jnp.where(blk > j, jnp.where(vj >= val, 1.0, 0.0), jnp.where(vj > val, 1.0, 0.0))
    free = (min(SEL_TOPN, n_sel) - n_forced).astype(F32)
    sel = (forced & valid) | (cand & (cnt < free))
    sel_t = jnp.concatenate([jnp.where(sel, 1.0, 0.0), jnp.zeros((LANES - SEL_BIAS_WIDTH, tq), F32)], axis=0)
    sel_q = sel_t.T
    bias = jnp.where(sel_q > 0.5, 0.0, SEL_NEG)
    lane = lax.broadcasted_iota(jnp.int32, (tq, LANES), 1)
    bias_hi = pltpu.roll(bias, SEL_BIAS_WIDTH, 1)
    qr = qr_ref[0]
    for g in range(G):
        qg = qr[:, g * HEAD_DIM:(g + 1) * HEAD_DIM].astype(F32)
        qg = jnp.concatenate([qg, qg], axis=-1)
        qa_ref[0, 0, g] = jnp.where(lane < HEAD_DIM, qg, bias_hi).astype(BF16)


def _overlap_t(S):
    n_cmp = (S - CMP_LEN) // CMP_STRIDE + 1
    nc = S // CMP_STRIDE
    n_sel = S // SEL_LEN
    start = np.arange(nc) * CMP_STRIDE
    end = start + CMP_LEN - 1
    sel_start = np.arange(n_sel) * SEL_LEN
    ov = (start[None, :] <= sel_start[:, None] + SEL_LEN - 1) & (end[None, :] >= sel_start[:, None])
    ov = ov & (np.arange(nc) < n_cmp)[None, :]
    out = np.zeros((LANES, nc), np.float32)
    out[:n_sel] = ov.astype(np.float32)
    return jnp.asarray(out)


def _nsa_cmp_call(qn, qr, kc, vc, tq):
    B, S, _ = qn.shape
    H, G = NSA_KV_HEADS, NSA_GROUP
    nc = kc.shape[2]
    n_sel = S // SEL_LEN
    assert n_sel <= SEL_BIAS_WIDTH and nc % LANES == 0
    qspec = pl.BlockSpec((1, tq, G * HEAD_DIM), lambda b, h, i: (b, i, h))
    cspec = pl.BlockSpec((1, 1, nc, HEAD_DIM), lambda b, h, i: (b, h, 0, 0))
    return pl.pallas_call(
        functools.partial(_nsa_cmp_kernel, n_sel=n_sel),
        grid=(B, H, S // tq),
        in_specs=[qspec, qspec, cspec, cspec, pl.BlockSpec((LANES, nc), lambda b, h, i: (0, 0))],
        out_specs=[pl.BlockSpec((1, 1, G, tq, HEAD_DIM), lambda b, h, i: (b, h, 0, i, 0)),
                   pl.BlockSpec((1, 1, G, tq, LANES), lambda b, h, i: (b, h, 0, i, 0))],
        out_shape=[jax.ShapeDtypeStruct((B, H, G, S, HEAD_DIM), F32),
                   jax.ShapeDtypeStruct((B, H, G, S, LANES), BF16)],
        compiler_params=_cparams("arbitrary", "arbitrary", "arbitrary"),
        name="nsa_cmp",
    )(qn, qr, kc, vc, _overlap_t(S))


SEL_CHUNK = 4


def _nsa_flash_kernel(qa_ref, ks_ref, vs_ref, kw_ref, vw_ref, oc_ref, g_ref, o_ref, m_scr, acc_scr):
    G = NSA_GROUP
    tq = qa_ref.shape[3]
    R = G * tq
    tk = tq
    i = pl.program_id(2)
    q0 = i * tq

    def rows_of(ref, j0, nt):
        return ref[0, 0, pl.ds(pl.multiple_of(j0 * tk, tk), nt * tk), :]

    def attend(k_ref, v_ref, j0, nt, visible, state):
        k, v = rows_of(k_ref, j0, nt), rows_of(v_ref, j0, nt)
        ss = [_dot_nt(qa_ref[0, 0, g], k) for g in range(G)]
        if visible is not None:
            qpos = q0 + lax.broadcasted_iota(jnp.int32, (tq, nt * tk), 0)
            kpos = j0 * tk + lax.broadcasted_iota(jnp.int32, (tq, nt * tk), 1)
            mask = visible(qpos, kpos)
        ps, alphas = [], []
        for g in range(G):
            s = ss[g] if visible is None else jnp.where(mask, ss[g], MASK_NEG)
            cols = [s[:, c * LANES:(c + 1) * LANES] for c in range(nt * tk // LANES)]
            m_new = jnp.max(functools.reduce(jnp.maximum, cols), axis=-1, keepdims=True)
            if state:
                m_prev = m_scr[g]
                m_new = jnp.maximum(m_prev, m_new)
                alphas.append(jnp.exp(m_prev - m_new))
                m_scr[g] = m_new
            ps.append(jnp.concatenate([jnp.exp(c - m_new) for c in cols], axis=-1).astype(BF16))
        if not state:
            return [_dot(ps[g], v) for g in range(G)]
        for g in range(G):
            acc_scr[g] = alphas[g] * acc_scr[g] + _dot(ps[g], v)

    def normalise(acc):
        return acc[:, :HEAD_DIM] / acc[:, HEAD_DIM:HEAD_DIM + 1]

    m_scr[...] = jnp.full((G, tq, LANES), MASK_NEG, F32)
    acc_scr[...] = jnp.zeros((G, tq, LANES), F32)
    causal = lambda qpos, kpos: kpos <= qpos

    def sel_body(c, carry):
        attend(ks_ref, vs_ref, c * SEL_CHUNK, SEL_CHUNK, None, True)
        return carry

    n_full = i // SEL_CHUNK
    lax.fori_loop(0, n_full, sel_body, 0)
    attend(ks_ref, vs_ref, n_full * SEL_CHUNK, 2, causal, True)

    @pl.when(i % SEL_CHUNK >= 2)
    def _():
        attend(ks_ref, vs_ref, n_full * SEL_CHUNK + 2, 2, causal, True)

    o_sel = normalise(acc_scr[...].reshape(R, LANES))

    assert WINDOW == 2 * tk
    band = lambda qpos, kpos: (kpos <= qpos) & (kpos > qpos - WINDOW)
    o_win = normalise(jnp.concatenate(attend(kw_ref, vw_ref, jnp.maximum(i - 2, 0), 3, band, False), axis=0))

    o_cmp = oc_ref[0, 0].reshape(R, HEAD_DIM)
    gates = g_ref[0, 0]
    outs = []
    for g in range(G):
        r = slice(g * tq, (g + 1) * tq)
        outs.append(gates[:, 3 * g:3 * g + 1] * o_cmp[r]
                    + gates[:, 3 * g + 1:3 * g + 2] * o_sel[r]
                    + gates[:, 3 * g + 2:3 * g + 3] * o_win[r])
    o_ref[0] = jnp.concatenate(outs, axis=-1).astype(BF16)


def _nsa_flash_call(qa, ks, vs, kw, vw, oc, gates):
    B, H, G, S, _ = qa.shape
    tq = WINDOW // 2
    assert (S // tq) % SEL_CHUNK == 0
    kv = pl.BlockSpec((1, 1, S, LANES), lambda b, h, i: (b, h, 0, 0))
    return pl.pallas_call(
        _nsa_flash_kernel,
        grid=(B, H, S // tq),
        in_specs=[pl.BlockSpec((1, 1, G, tq, LANES), lambda b, h, i: (b, h, 0, i, 0)),
                  kv, kv, kv, kv,
                  pl.BlockSpec((1, 1, G, tq, HEAD_DIM), lambda b, h, i: (b, h, 0, i, 0)),
                  pl.BlockSpec((1, 1, tq, LANES), lambda b, h, i: (b, h, i, 0))],
        out_specs=pl.BlockSpec((1, tq, G * HEAD_DIM), lambda b, h, i: (b, i, h)),
        out_shape=jax.ShapeDtypeStruct((B, S, Q_WIDTH), BF16),
        scratch_shapes=[pltpu.VMEM((G, tq, LANES), F32), pltpu.VMEM((G, tq, LANES), F32)],
        compiler_params=_cparams("arbitrary", "arbitrary", "arbitrary"),
        name="nsa_flash",
    )(qa, ks, vs, kw, vw, oc, gates)


def _gmlp_kernel(u_ref, v_ref, ws_ref, bs_ref, o_ref):
    tm = u_ref.shape[1]
    C = GM_CHUNK
    r = lax.broadcasted_iota(jnp.int32, (C, C), 0)
    c = lax.broadcasted_iota(jnp.int32, (C, C), 1)
    causal = c <= r
    ws = [jnp.where(causal, ws_ref[g], 0.0).astype(BF16) for g in range(GM_GROUPS)]
    for n in range(tm // C):
        rows = slice(n * C, (n + 1) * C)
        vn = v_ref[0, rows, :]
        mixed = jnp.concatenate(
            [_dot(ws[g], vn[:, g * GM_GROUP_DIM:(g + 1) * GM_GROUP_DIM]) for g in range(GM_GROUPS)],
            axis=-1)
        o_ref[0, rows, :] = (u_ref[0, rows, :].astype(F32) * (mixed + bs_ref[...])).astype(BF16)


def _gmlp_call(u, v, ws, bs_full, tm):
    B, S, W = u.shape
    tok = pl.BlockSpec((1, tm, W), lambda b, i: (b, i, 0))
    return pl.pallas_call(
        _gmlp_kernel,
        grid=(B, S // tm),
        in_specs=[tok, tok,
                  pl.BlockSpec((GM_GROUPS, GM_CHUNK, GM_CHUNK), lambda b, i: (0, 0, 0)),
                  pl.BlockSpec((GM_CHUNK, W), lambda b, i: (0, 0))],
        out_specs=tok,
        out_shape=jax.ShapeDtypeStruct((B, S, W), BF16),
        compiler_params=_cparams("arbitrary", "arbitrary"),
        name="gmlp",
    )(u, v, ws, bs_full)


def _mixout_kernel(on_ref, og_ref, ga_ref, gb_ref, x_ref, gt_ref, sc_ref, sh_ref,
                   wpn_ref, wpg_ref, wo_ref, wr_ref, wgs_ref, wus_ref, wds_ref,
                   x1_ref, h2_ref, lg_ref, shd_ref):
    ya = _dot(on_ref[0], wpn_ref[...])
    yb = _dot(og_ref[0], wpg_ref[...])
    merged = ga_ref[0].astype(F32) * ya + gb_ref[0].astype(F32) * yb
    x1 = x_ref[0] + gt_ref[0] * _dot(merged.astype(BF16), wo_ref[...])
    x1_ref[0] = x1
    ms = jnp.mean(x1 * x1, axis=-1, keepdims=True)
    h2 = (x1 * lax.rsqrt(ms + EPS)) * (1.0 + sc_ref[0]) + sh_ref[0]
    lg_ref[0] = _dot(h2, wr_ref[...], precision=HI)
    hb = h2.astype(BF16)
    h2_ref[0] = _pack_halves(hb)
    act = _silu(_dot(hb, wgs_ref[...])) * _dot(hb, wus_ref[...])
    shd_ref[0] = _dot(act.astype(BF16), wds_ref[...])


def _mixout_call(o_nsa, o_gm, ga, gb, x, gt, sc, sh, wpn, wpg, wo, wr, wgs, wus, wds, tm):
    B, S, D = x.shape
    tok = lambda w: pl.BlockSpec((1, tm, w), lambda b, i: (b, i, 0))
    per_b = pl.BlockSpec((1, 1, D), lambda b, i: (b, 0, 0))
    full = lambda a: pl.BlockSpec(a.shape, lambda b, i: (0,) * a.ndim)
    sds = jax.ShapeDtypeStruct
    return pl.pallas_call(
        _mixout_kernel,
        grid=(B, S // tm),
        in_specs=[tok(Q_WIDTH), tok(GM_WIDTH), tok(D), tok(D), tok(D), per_b, per_b, per_b,
                  full(wpn), full(wpg), full(wo), full(wr), full(wgs), full(wus), full(wds)],
        out_specs=[tok(D), tok(D // 2), tok(N_EXPERTS), tok(D)],
        out_shape=[sds((B, S, D), F32), sds((B, S, D // 2), jnp.uint32), sds((B, S, N_EXPERTS), F32),
                   sds((B, S, D), F32)],
        compiler_params=_cparams("arbitrary", "arbitrary"),
        name="mix_out",
    )(o_nsa, o_gm, ga, gb, x, gt, sc, sh, wpn, wpg, wo, wr, wgs, wus, wds)


def _route_kernel(lg_ref, br_ref, idx_ref, wt_ref, rank_ref, cnt_ref, run_scr):
    tm, E = lg_ref.shape

    @pl.when(pl.program_id(0) == 0)
    def _():
        run_scr[...] = jnp.zeros_like(run_scr)

    aff = _sigmoid(lg_ref[...])
    work = aff + br_ref[...]
    lane = lax.broadcasted_iota(jnp.int32, (tm, E), 1).astype(F32)
    picked = jnp.zeros((tm, E), F32)
    idxs, tops = [], []
    for _ in range(TOP_K):
        m = jnp.max(work, axis=-1, keepdims=True)
        idx = jnp.min(jnp.where(work == m, lane, float(E)), axis=-1, keepdims=True)
        hit = lane == idx
        tops.append(jnp.sum(jnp.where(hit, aff, 0.0), axis=-1, keepdims=True))
        idxs.append(idx)
        picked = jnp.where(hit, 1.0, picked)
        work = jnp.where(hit, -jnp.inf, work)
    total = tops[0]
    for t in tops[1:]:
        total = total + t
    r = lax.broadcasted_iota(jnp.int32, (tm, tm), 0)
    c = lax.broadcasted_iota(jnp.int32, (tm, tm), 1)
    before = _dot(jnp.where(c < r, 1.0, 0.0).astype(BF16), picked.astype(BF16)) + run_scr[...]
    ranks = [jnp.sum(jnp.where(lane == idx, before, 0.0), axis=-1, keepdims=True) for idx in idxs]
    run_scr[...] = run_scr[...] + jnp.sum(picked, axis=0, keepdims=True)
    cnt_ref[...] = run_scr[...]
    lane_k = lax.broadcasted_iota(jnp.int32, (tm, LANES), 1)
    idx_o = jnp.zeros((tm, LANES), jnp.int32)
    wt_o = jnp.zeros((tm, LANES), F32)
    rank_o = jnp.zeros((tm, LANES), jnp.int32)
    for k in range(TOP_K):
        idx_o = jnp.where(lane_k == k, idxs[k].astype(jnp.int32), idx_o)
        wt_o = jnp.where(lane_k == k, tops[k] / total * ROUTE_SCALE, wt_o)
        rank_o = jnp.where(lane_k == k, ranks[k].astype(jnp.int32), rank_o)
    idx_ref[...] = idx_o
    wt_ref[...] = wt_o
    rank_ref[...] = rank_o


def _route_call(logits, b_router, tm):
    T, E = logits.shape
    tok = lambda w: pl.BlockSpec((tm, w), lambda i: (i, 0))
    one = pl.BlockSpec((1, E), lambda i: (0, 0))
    sds = jax.ShapeDtypeStruct
    return pl.pallas_call(
        _route_kernel,
        grid=(T // tm,),
        in_specs=[tok(E), one],
        out_specs=[tok(LANES), tok(LANES), tok(LANES), one],
        out_shape=[sds((T, LANES), jnp.int32), sds((T, LANES), F32), sds((T, LANES), jnp.int32),
                   sds((1, E), F32)],
        scratch_shapes=[pltpu.VMEM((1, E), F32)],
        compiler_params=_cparams("arbitrary"),
        name="route",
    )(logits, b_router.reshape(1, E))


def _slot_kernel(idx_ref, rank_ref, ps_ref, o_ref):
    tm = idx_ref.shape[0]
    E = ps_ref.shape[1]
    lane = lax.broadcasted_iota(jnp.int32, (tm, E), 1)
    lane_k = lax.broadcasted_iota(jnp.int32, (tm, LANES), 1)
    idx, rank, ps = idx_ref[...], rank_ref[...], ps_ref[...]
    out = jnp.zeros((tm, LANES), jnp.int32)
    for k in range(TOP_K):
        base = jnp.sum(jnp.where(lane == idx[:, k:k + 1], ps, 0.0), axis=-1, keepdims=True)
        out = jnp.where(lane_k == k, base.astype(jnp.int32) + rank[:, k:k + 1], out)
    o_ref[...] = out[:, :TOP_K]


def _slot_call(idx, rank, pad_start, tm):
    T = idx.shape[0]
    E = pad_start.shape[0]
    tok = pl.BlockSpec((tm, LANES), lambda i: (i, 0))
    return pl.pallas_call(
        _slot_kernel,
        grid=(T // tm,),
        in_specs=[tok, tok, pl.BlockSpec((1, E), lambda i: (0, 0))],
        out_specs=pl.BlockSpec((tm, TOP_K), lambda i: (i, 0)),
        out_shape=jax.ShapeDtypeStruct((T, TOP_K), jnp.int32),
        compiler_params=_cparams("arbitrary"),
        name="slot",
    )(idx, rank, pad_start.astype(F32).reshape(1, E))


def _dispatch_kernel(bv_ref, slot_ref, h_ref, xs_ref, zbuf, sem, zsem):
    tm = h_ref.shape[0]
    rows = zbuf.shape[0]

    @pl.when(pl.program_id(0) == 0)
    def _():
        zbuf[...] = jnp.zeros_like(zbuf)

        def zero_copy(i):
            return pltpu.make_async_copy(zbuf, xs_ref.at[pl.ds(pl.multiple_of(i * rows, rows), rows), :], zsem)

        def start(i, carry):
            @pl.when(bv_ref[i] < rows)
            def _():
                zero_copy(i).start()
            return carry

        def wait(i, carry):
            @pl.when(bv_ref[i] < rows)
            def _():
                zero_copy(i).wait()
            return carry

        lax.fori_loop(0, bv_ref.shape[0], start, 0)
        lax.fori_loop(0, bv_ref.shape[0], wait, 0)

    def body(r, carry):
        for k in range(TOP_K):
            d = slot_ref[r * TOP_K + k]
            pltpu.make_async_copy(h_ref.at[pl.ds(r, 1), :], xs_ref.at[pl.ds(d, 1), :], sem).start()
        return carry

    lax.fori_loop(0, tm, body, 0)
    for k in range(TOP_K):
        pltpu.make_async_copy(h_ref, xs_ref.at[pl.ds(0, tm), :], sem).wait()


def _dispatch_call(blk_valid, slots, h2p, n_rows, tm):
    T, W = h2p.shape
    grid_spec = pltpu.PrefetchScalarGridSpec(
        num_scalar_prefetch=1,
        grid=(T // tm,),
        in_specs=[pl.BlockSpec((tm * TOP_K,), lambda i, bv: (i,), memory_space=pltpu.SMEM),
                  pl.BlockSpec((tm, W), lambda i, bv: (i, 0))],
        out_specs=pl.BlockSpec(memory_space=pl.ANY),
        scratch_shapes=[pltpu.VMEM((MOE_ROWS, W), jnp.uint32), pltpu.SemaphoreType.DMA, pltpu.SemaphoreType.DMA],
    )
    return pl.pallas_call(
        _dispatch_kernel,
        grid_spec=grid_spec,
        out_shape=jax.ShapeDtypeStruct((n_rows, W), jnp.uint32),
        compiler_params=pltpu.CompilerParams(dimension_semantics=("arbitrary",), vmem_limit_bytes=VMEM_LIMIT,
                                             has_side_effects=True),
        name="dispatch",
    )(blk_valid, slots, h2p)


def _expert_kernel(be_ref, nb_ref, x_ref, wg_ref, wu_ref, wd_ref, y_ref):
    i = pl.program_id(0)

    @pl.when(i < nb_ref[0])
    def _():
        lo, hi = _unpack_halves(x_ref[...])
        x = jnp.concatenate([lo, hi], axis=1).astype(BF16)
        a = _dot(x, wg_ref[0, 0].astype(BF16))
        b = _dot(x, wu_ref[0, 0].astype(BF16))
        y = _dot((_silu(a) * b).astype(BF16), wd_ref[0, 0].astype(BF16))
        y_ref[...] = _pack_halves(y)

    @pl.when(i >= nb_ref[0])
    def _():
        y_ref[...] = jnp.zeros_like(y_ref)


def _expert_call(layer, blk_e, n_used, xs, wg, wu, wd):
    n_rows, W = xs.shape
    rows = MOE_ROWS
    D, F = wg.shape[2], wg.shape[3]
    wspec = lambda a, b: pl.BlockSpec((1, 1, a, b), lambda i, be, nb: (layer, be[i], 0, 0))
    grid_spec = pltpu.PrefetchScalarGridSpec(
        num_scalar_prefetch=2,
        grid=(n_rows // rows,),
        in_specs=[pl.BlockSpec((rows, W), lambda i, be, nb: (jnp.minimum(i, nb[0] - 1), 0)),
                  wspec(D, F), wspec(D, F), wspec(F, D)],
        out_specs=pl.BlockSpec((rows, W), lambda i, be, nb: (i, 0)),
    )
    return pl.pallas_call(
        _expert_kernel,
        grid_spec=grid_spec,
        out_shape=jax.ShapeDtypeStruct((n_rows, W), jnp.uint32),
        compiler_params=_cparams("arbitrary"),
        name="expert",
    )(blk_e, n_used, xs, wg, wu, wd)


def _combine_kernel(slot_ref, wt_ref, shd_ref, x_ref, gt_ref, y_ref, o_ref, ybuf, sem):
    tm = x_ref.shape[0]

    def body(r, carry):
        for k in range(TOP_K):
            d = slot_ref[r * TOP_K + k]
            pltpu.make_async_copy(y_ref.at[pl.ds(d, 1), :], ybuf.at[k, pl.ds(r, 1), :], sem).start()
        return carry

    lax.fori_loop(0, tm, body, 0)
    for k in range(TOP_K):
        pltpu.make_async_copy(y_ref.at[pl.ds(0, tm), :], ybuf.at[k], sem).wait()
    wt = wt_ref[...]
    acc_lo = acc_hi = None
    for k in range(TOP_K):
        lo, hi = _unpack_halves(ybuf[k])
        w = wt[:, k:k + 1]
        acc_lo = w * lo if k == 0 else acc_lo + w * lo
        acc_hi = w * hi if k == 0 else acc_hi + w * hi
    routed = jnp.concatenate([acc_lo, acc_hi], axis=1)
    o_ref[...] = x_ref[...] + gt_ref[0] * (routed + shd_ref[...])


def _combine_call(slots, wts, shared, x1, gt, y, tm):
    T, D = x1.shape
    B = gt.shape[0]
    per_b = T // B // tm
    tok = lambda w: pl.BlockSpec((tm, w), lambda i: (i, 0))
    return pl.pallas_call(
        _combine_kernel,
        grid=(T // tm,),
        in_specs=[pl.BlockSpec((tm * TOP_K,), lambda i: (i,), memory_space=pltpu.SMEM),
                  tok(LANES), tok(D), tok(D),
                  pl.BlockSpec((1, 1, D), lambda i: (i // per_b, 0, 0)),
                  pl.BlockSpec(memory_space=pl.ANY)],
        out_specs=tok(D),
        out_shape=jax.ShapeDtypeStruct((T, D), F32),
        scratch_shapes=[pltpu.VMEM((TOP_K, tm, y.shape[1]), jnp.uint32), pltpu.SemaphoreType.DMA],
        compiler_params=_cparams("arbitrary"),
        name="combine",
    )(slots, wts, shared, x1, gt, y)


def _rope_tables(positions):
    half = HEAD_DIM // 2
    inv = ROPE_THETA ** (-jnp.arange(half, dtype=F32) / half)
    ang = positions.astype(F32)[..., None] * inv
    cos, sin, zero = jnp.cos(ang), jnp.sin(ang), jnp.zeros_like(ang)
    cos_t = jnp.concatenate([cos, cos] * 2, axis=-1)
    sin_lo = jnp.concatenate([-sin, zero] * 2, axis=-1)
    sin_hi = jnp.concatenate([zero, sin] * 2, axis=-1)
    return cos_t, sin_lo, sin_hi


def _reorder_w_in(w):
    o = np.cumsum([0, Q_WIDTH] + [KV_WIDTH] * 6 + [NSA_Q_HEADS * 3, GM_WIDTH, GM_WIDTH, D_MODEL, D_MODEL])
    q, kc, vc, ks, vs, kw, vw, g, u, v, ga, gb = [w[:, o[i]:o[i + 1]] for i in range(12)]
    per = NSA_GROUP * 3
    pad = jnp.zeros((w.shape[0], LANES - per), w.dtype)
    return jnp.concatenate([q, ks, kw, kc, vc, vs, vw, g[:, :per], pad, g[:, per:], pad, u, v, ga, gb], axis=1)


def _block_plan(counts, n_rows):
    rows = MOE_ROWS
    counts = counts.astype(jnp.int32)
    padded = (counts + rows - 1) // rows * rows
    pad_end = jnp.cumsum(padded)
    pad_start = pad_end - padded
    blk_start = jnp.arange(n_rows // rows, dtype=jnp.int32) * rows
    blk_e = jnp.minimum(jnp.sum((pad_end[None, :] <= blk_start[:, None]).astype(jnp.int32), axis=1), N_EXPERTS - 1)
    own = blk_e[:, None] == jnp.arange(N_EXPERTS, dtype=jnp.int32)[None, :]
    e_count = jnp.sum(jnp.where(own, counts[None, :], 0), axis=1)
    e_start = jnp.sum(jnp.where(own, pad_start[None, :], 0), axis=1)
    blk_valid = jnp.clip(e_count - (blk_start - e_start), 0, rows).astype(jnp.int32)
    n_used = (pad_end[-1:] // rows).astype(jnp.int32)
    return pad_start, blk_e.astype(jnp.int32), blk_valid, n_used


def kernel(x, c, positions, w_mod, b_mod, w_in, q_gain, k_gain, cmp_pos_k, cmp_pos_v, cmp_w1_k, cmp_w2_k, cmp_w1_v, cmp_w2_v, gm_ln_g, gm_ln_b, gm_ws, gm_bs, w_proj_nsa, w_proj_gm, w_out, w_router, b_router, w_gate_e, w_up_e, w_down_e, w_gate_sh, w_up_sh, w_down_sh):
    B, S, D = x.shape
    L = w_mod.shape[0]
    T = B * S
    tm = 256
    scale = HEAD_DIM ** -0.5
    cos_t, sin_lo, sin_hi = _rope_tables(positions)
    mod = _mod_call(c, w_mod, b_mod)
    n_rows = T * TOP_K + N_EXPERTS * MOE_ROWS

    for l in range(L):
        sh_a, sc_a, gt_a, sh_f, sc_f, gt_f = [mod[l, :, i * D:(i + 1) * D].reshape(B, 1, D) for i in range(6)]
        qg = (jnp.tile(q_gain[l], NSA_Q_HEADS) * scale).reshape(1, Q_WIDTH)
        kg = jnp.tile(k_gain[l], 2 * NSA_KV_HEADS).reshape(1, 2 * KV_WIDTH)
        (qn, qr, ks, kw, vs, vw, kc_raw, vc_raw, gates, u, v, ga, gb) = _inproj_call(
            x, sc_a, sh_a, _reorder_w_in(w_in[l]).astype(BF16), qg, kg, cos_t, sin_lo, sin_hi,
            gm_ln_g[l].reshape(1, GM_WIDTH), gm_ln_b[l].reshape(1, GM_WIDTH), tm)
        kc, vc = _compress_call(
            kc_raw, vc_raw, cmp_w1_k[l].astype(BF16), cmp_w2_k[l].astype(BF16), cmp_pos_k[l].reshape(1, -1),
            cmp_w1_v[l].astype(BF16), cmp_w2_v[l].astype(BF16), cmp_pos_v[l].reshape(1, -1),
            k_gain[l].reshape(1, HEAD_DIM))
        o_cmp, q_aug = _nsa_cmp_call(qn, qr, kc, vc, WINDOW // 2)
        o_nsa = _nsa_flash_call(q_aug, ks, vs, kw, vw, o_cmp, gates)
        bs_full = jnp.repeat(gm_bs[l].T, GM_GROUP_DIM, axis=1)
        o_gm = _gmlp_call(u, v, gm_ws[l], bs_full, 512)
        x1, h2, logits, shared = _mixout_call(
            o_nsa, o_gm, ga, gb, x, gt_a, sc_f, sh_f,
            w_proj_nsa[l].astype(BF16), w_proj_gm[l].astype(BF16), w_out[l].astype(BF16), w_router[l],
            w_gate_sh[l].astype(BF16), w_up_sh[l].astype(BF16), w_down_sh[l].astype(BF16), tm)
        idx, wts, rank, counts = _route_call(logits.reshape(T, N_EXPERTS), b_router[l], tm)
        pad_start, blk_e, blk_valid, n_used = _block_plan(counts[0], n_rows)
        slots = _slot_call(idx, rank, pad_start, tm).reshape(T * TOP_K)
        xs = _dispatch_call(blk_valid, slots, h2.reshape(T, D // 2), n_rows, tm)
        y = _expert_call(l, blk_e, n_used, xs, w_gate_e, w_up_e, w_down_e)
        x = _combine_call(slots, wts, shared.reshape(T, D), x1.reshape(T, D), gt_f, y, tm).reshape(B, S, D)
    return x
```

```python
import functools

import jax
import jax.numpy as jnp
import numpy as np
from jax import lax
from jax.experimental import pallas as pl
from jax.experimental.pallas import tpu as pltpu

D_MODEL = 1024
NSA_Q_HEADS = 8
NSA_KV_HEADS = 2
HEAD_DIM = 64
NSA_GROUP = NSA_Q_HEADS // NSA_KV_HEADS
CMP_LEN = 32
CMP_STRIDE = 16
CMP_HIDDEN = 256
SEL_LEN = 64
SEL_TOPN = 16
WINDOW = 512
ROPE_THETA = 10000.0
Q_WIDTH = NSA_Q_HEADS * HEAD_DIM
KV_WIDTH = NSA_KV_HEADS * HEAD_DIM
GM_GROUPS = 8
GM_GROUP_DIM = 64
GM_WIDTH = GM_GROUPS * GM_GROUP_DIM
GM_CHUNK = 128
N_EXPERTS = 256
TOP_K = 8
D_EXPERT = 256
D_SHARED = 256
ROUTE_SCALE = 2.5
EPS = 1e-6

LANES = 128
SEL_BIAS_WIDTH = 64
MASK_NEG = -1e30
SEL_NEG = -30000.0
MOE_ROWS = 256
VMEM_LIMIT = 56 * 1024 * 1024

F32 = jnp.float32
BF16 = jnp.bfloat16
HI = lax.Precision.HIGHEST


def _cparams(*sem):
    return pltpu.CompilerParams(dimension_semantics=sem, vmem_limit_bytes=VMEM_LIMIT)


def _dot(a, b, **kw):
    return jnp.dot(a, b, preferred_element_type=F32, **kw)


def _dot_nt(a, b, **kw):
    return lax.dot_general(a, b, (((1,), (1,)), ((), ())), preferred_element_type=F32, **kw)


def _gelu(x):
    return 0.5 * x * (1.0 + jnp.tanh(0.7978845608028654 * (x + 0.044715 * (x * x * x))))


def _sigmoid(x):
    return 1.0 / (1.0 + jnp.exp(-x))


def _silu(x):
    return x * _sigmoid(x)


_HI_MASK = np.uint32(0xFFFF0000)


def _pack_halves(a):
    w = a.shape[1] // 2
    bits = lax.bitcast_convert_type(a.astype(BF16).astype(F32), jnp.uint32)
    return (bits[:, w:] & _HI_MASK) | (bits[:, :w] >> 16)


def _unpack_halves(words):
    lo = lax.bitcast_convert_type(words << 16, F32)
    hi = lax.bitcast_convert_type(words & _HI_MASK, F32)
    return lo, hi


def _mod_kernel(c_ref, w_ref, b_ref, o_ref):
    c = c_ref[...]
    o_ref[0] = _dot(_silu(c), w_ref[0], precision=HI) + b_ref[0]


def _mod_call(c, w_mod, b_mod):
    L, D, N = w_mod.shape
    B = c.shape[0]
    tn = 1536
    return pl.pallas_call(
        _mod_kernel,
        grid=(L, N // tn),
        in_specs=[pl.BlockSpec((B, D), lambda l, j: (0, 0)),
                  pl.BlockSpec((1, D, tn), lambda l, j: (l, 0, j)),
                  pl.BlockSpec((1, 1, tn), lambda l, j: (l, 0, j))],
        out_specs=pl.BlockSpec((1, B, tn), lambda l, j: (l, 0, j)),
        out_shape=jax.ShapeDtypeStruct((L, B, N), F32),
        compiler_params=_cparams("arbitrary", "arbitrary"),
        name="mod",
    )(c, w_mod, b_mod.reshape(L, 1, N))


_C_Q = 0
_C_K = 512
_C_KC = 768
_C_VC = 896
_C_VS = 1024
_C_VW = 1152
_C_G = 1280
_C_U = 1536
_C_V = 2048
_C_GA = 2560
_C_GB = 3584
IN_COLS_P = 4608


def _head_norm(z, bd):
    ms = _dot(z * z, bd, precision=HI)
    return z * lax.rsqrt(ms + EPS)


def _rope(z, cos, sin_lo, sin_hi):
    w = z.shape[-1]
    half = HEAD_DIM // 2
    return z * cos + pltpu.roll(z, w - half, 1) * sin_lo + pltpu.roll(z, half, 1) * sin_hi


def _tile_lanes(t, n):
    return t if n == 1 else jnp.concatenate([t] * n, axis=-1)


def _inproj_kernel(x_ref, sc_ref, sh_ref, w_ref, bdq_ref, bdk_ref, qg_ref, kg_ref,
                   cos_ref, sl_ref, shi_ref, lng_ref, lnb_ref,
                   qn_ref, qr_ref, ks_ref, kw_ref, vs_ref, vw_ref, kc_ref, vc_ref,
                   g_ref, u_ref, v_ref, ga_ref, gb_ref):
    tm = x_ref.shape[1]
    x = x_ref[0]
    ms = jnp.mean(x * x, axis=-1, keepdims=True)
    h = (x * lax.rsqrt(ms + EPS)) * (1.0 + sc_ref[0]) + sh_ref[0]
    hb = h.astype(BF16)

    def mm(lo, width):
        return _dot(hb, w_ref[:, lo:lo + width])

    cos, sl, shi = cos_ref[0], sl_ref[0], shi_ref[0]

    zq = mm(_C_Q, Q_WIDTH)
    qn = _head_norm(zq, bdq_ref[...]) * qg_ref[...]
    qr = _rope(qn, _tile_lanes(cos, 4), _tile_lanes(sl, 4), _tile_lanes(shi, 4))
    qn_ref[0] = qn.astype(BF16)
    qr_ref[0] = qr.astype(BF16)

    zk = mm(_C_K, 2 * KV_WIDTH)
    kn = _head_norm(zk, bdk_ref[...]) * kg_ref[...]
    kr = _rope(kn, _tile_lanes(cos, 2), _tile_lanes(sl, 2), _tile_lanes(shi, 2))
    lane = lax.broadcasted_iota(jnp.int32, (tm, LANES), 1)
    tok = pl.program_id(1) * tm + lax.broadcasted_iota(jnp.int32, (tm, LANES), 0)
    onehot = jnp.where(lane - HEAD_DIM == tok // SEL_LEN, 1.0, 0.0)
    ones_col = jnp.where(lane == HEAD_DIM, 1.0, 0.0)
    low = lane < HEAD_DIM
    zvs = mm(_C_VS, KV_WIDTH)
    zvw = mm(_C_VW, KV_WIDTH)
    zkc = mm(_C_KC, KV_WIDTH)
    zvc = mm(_C_VC, KV_WIDTH)
    for kv in range(NSA_KV_HEADS):
        def head(a):
            return a if kv == 0 else pltpu.roll(a, HEAD_DIM, 1)
        ks_ref[0, kv] = jnp.where(low, head(kr[:, :KV_WIDTH]), onehot).astype(BF16)
        kw_ref[0, kv] = jnp.where(low, head(kr[:, KV_WIDTH:]), 0.0).astype(BF16)
        vs_ref[0, kv] = jnp.where(low, head(zvs), ones_col).astype(BF16)
        vw_ref[0, kv] = jnp.where(low, head(zvw), ones_col).astype(BF16)
        kc_ref[0, kv] = head(zkc)[:, :HEAD_DIM]
        vc_ref[0, kv] = head(zvc)[:, :HEAD_DIM]

    zg = mm(_C_G, 2 * LANES)
    sg = _sigmoid(zg)
    g_ref[0, 0] = sg[:, :LANES]
    g_ref[0, 1] = sg[:, LANES:]

    u_ref[0] = _gelu(mm(_C_U, GM_WIDTH)).astype(BF16)
    gv = _gelu(mm(_C_V, GM_WIDTH))
    mu = jnp.mean(gv, axis=-1, keepdims=True)
    cen = gv - mu
    var = jnp.mean(cen * cen, axis=-1, keepdims=True)
    v_ref[0] = ((cen * lax.rsqrt(var + EPS)) * lng_ref[...] + lnb_ref[...]).astype(BF16)

    ga_ref[0] = _sigmoid(mm(_C_GA, D_MODEL)).astype(BF16)
    gb_ref[0] = _sigmoid(mm(_C_GB, D_MODEL)).astype(BF16)


def _block_diag_mean(width):
    idx = np.arange(width) // HEAD_DIM
    return jnp.asarray((idx[:, None] == idx[None, :]).astype(np.float32) / HEAD_DIM)


def _inproj_call(x, sc, sh, w_p, qg, kg, cos, sl, shi, lng, lnb, tm):
    B, S, D = x.shape
    H = NSA_KV_HEADS
    full = lambda *shape: pl.BlockSpec(shape, lambda b, i: (0,) * len(shape))
    tok3 = lambda w: pl.BlockSpec((1, tm, w), lambda b, i: (b, i, 0))
    per_b = pl.BlockSpec((1, 1, D), lambda b, i: (b, 0, 0))
    kv4 = lambda w: pl.BlockSpec((1, H, tm, w), lambda b, i: (b, 0, i, 0))
    sds = jax.ShapeDtypeStruct
    out_shape = [
        sds((B, S, Q_WIDTH), BF16), sds((B, S, Q_WIDTH), BF16),
        sds((B, H, S, LANES), BF16), sds((B, H, S, LANES), BF16),
        sds((B, H, S, LANES), BF16), sds((B, H, S, LANES), BF16),
        sds((B, H, S, HEAD_DIM), F32), sds((B, H, S, HEAD_DIM), F32),
        sds((B, H, S, LANES), F32),
        sds((B, S, GM_WIDTH), BF16), sds((B, S, GM_WIDTH), BF16),
        sds((B, S, D), BF16), sds((B, S, D), BF16),
    ]
    out_specs = [
        tok3(Q_WIDTH), tok3(Q_WIDTH), kv4(LANES), kv4(LANES), kv4(LANES), kv4(LANES),
        kv4(HEAD_DIM), kv4(HEAD_DIM), kv4(LANES),
        tok3(GM_WIDTH), tok3(GM_WIDTH), tok3(D), tok3(D),
    ]
    return pl.pallas_call(
        _inproj_kernel,
        grid=(B, S // tm),
        in_specs=[tok3(D), per_b, per_b, full(D, IN_COLS_P),
                  full(Q_WIDTH, Q_WIDTH), full(2 * KV_WIDTH, 2 * KV_WIDTH),
                  full(1, Q_WIDTH), full(1, 2 * KV_WIDTH),
                  tok3(LANES), tok3(LANES), tok3(LANES),
                  full(1, GM_WIDTH), full(1, GM_WIDTH)],
        out_specs=out_specs,
        out_shape=out_shape,
        compiler_params=_cparams("arbitrary", "arbitrary"),
        name="in_proj",
    )(x, sc, sh, w_p, _block_diag_mean(Q_WIDTH), _block_diag_mean(2 * KV_WIDTH), qg, kg,
      cos, sl, shi, lng, lnb)


def _compress_kernel(kr_ref, vr_ref, w1k_ref, w2k_ref, pek_ref, w1v_ref, w2v_ref, pev_ref,
                     kg_ref, kc_ref, vc_ref):
    nc = kr_ref.shape[2]
    half = CMP_STRIDE * HEAD_DIM

    def mlp(raw, w1_ref, w2_ref, pe_ref):
        a = raw.astype(BF16)
        top = _dot(a, w1_ref[:half, :])
        bot = _dot(a, w1_ref[half:, :])
        pe = jnp.broadcast_to(pe_ref[...], (8, 2 * half)).astype(BF16)
        pe_row = _dot(pe, w1_ref[...])[0:1, :]
        hid = top + pltpu.roll(bot, nc - 1, 0) + pe_row
        return _dot(_gelu(hid).astype(BF16), w2_ref[...])

    kc = mlp(kr_ref[0, 0], w1k_ref, w2k_ref, pek_ref)
    ms = jnp.mean(kc * kc, axis=-1, keepdims=True)
    kc_ref[0, 0] = (kc * lax.rsqrt(ms + EPS) * kg_ref[...]).astype(BF16)
    vc_ref[0, 0] = mlp(vr_ref[0, 0], w1v_ref, w2v_ref, pev_ref).astype(BF16)


def _compress_call(kc_raw, vc_raw, w1k, w2k, pek, w1v, w2v, pev, kg):
    B, H, S, hd = kc_raw.shape
    nc = S // CMP_STRIDE
    feat = CMP_STRIDE * hd
    raw = pl.BlockSpec((1, 1, nc, feat), lambda b, h: (b, h, 0, 0))
    full = lambda *shape: pl.BlockSpec(shape, lambda b, h: (0,) * len(shape))
    out = pl.BlockSpec((1, 1, nc, hd), lambda b, h: (b, h, 0, 0))
    return pl.pallas_call(
        _compress_kernel,
        grid=(B, H),
        in_specs=[raw, raw, full(2 * feat, CMP_HIDDEN), full(CMP_HIDDEN, hd), full(1, 2 * feat),
                  full(2 * feat, CMP_HIDDEN), full(CMP_HIDDEN, hd), full(1, 2 * feat),
                  full(1, hd)],
        out_specs=[out, out],
        out_shape=[jax.ShapeDtypeStruct((B, H, nc, hd), BF16)] * 2,
        compiler_params=_cparams("arbitrary", "arbitrary"),
        name="compress",
    )(kc_raw.reshape(B, H, nc, feat), vc_raw.reshape(B, H, nc, feat),
      w1k, w2k, pek, w1v, w2v, pev, kg)


def _group_rows(a):
    return jnp.concatenate([a[:, g * HEAD_DIM:(g + 1) * HEAD_DIM] for g in range(NSA_GROUP)], axis=0)


def _nsa_cmp_kernel(qn_ref, qr_ref, kc_ref, vc_ref, ovl_ref, oc_ref, qa_ref, *, n_sel):
    tq = qn_ref.shape[1]
    nc = kc_ref.shape[2]
    G = NSA_GROUP
    q0 = pl.program_id(2) * tq
    q4 = _group_rows(qn_ref[0])
    s = _dot_nt(q4, kc_ref[0, 0])
    row = lax.broadcasted_iota(jnp.int32, (G, tq, nc), 1).reshape(G * tq, nc)
    col = lax.broadcasted_iota(jnp.int32, (G * tq, nc), 1)
    vis = col * CMP_STRIDE + (CMP_LEN - 1) <= q0 + row
    s = jnp.where(vis, s, MASK_NEG)
    m = jnp.max(s, axis=-1, keepdims=True)
    e = jnp.where(vis, jnp.exp(s - m), 0.0)
    p = e / jnp.maximum(jnp.sum(e, axis=-1, keepdims=True), 1e-30)
    oc = _dot(p.astype(BF16), vc_ref[0, 0])
    oc_ref[0, 0] = oc.reshape(G, tq, HEAD_DIM)

    psum = p[0:tq] + p[tq:2 * tq] + p[2 * tq:3 * tq] + p[3 * tq:4 * tq]
    imp = _dot_nt(ovl_ref[...], psum, precision=HI)[:SEL_BIAS_WIDTH]
    blk = lax.broadcasted_iota(jnp.int32, (SEL_BIAS_WIDTH, tq), 0)
    cur = (q0 + lax.broadcasted_iota(jnp.int32, (SEL_BIAS_WIDTH, tq), 1)) // SEL_LEN
    valid = blk <= cur
    forced = (blk == 0) | (blk == cur) | (blk == cur - 1)
    cand = valid & jnp.logical_not(forced)
    n_forced = jnp.minimum(cur, 2) + 1
    val = jnp.where(cand, imp, -1.0)
    cnt = jnp.zeros((SEL_BIAS_WIDTH, tq), F32)
    for j in range(n_sel):
        vj = jnp.broadcast_to(val[j:j + 1, :], (SEL_BIAS_WIDTH, tq))
        cnt = cnt + jnp.where(blk > j, jnp.where(vj >= val, 1.0, 0.0), jnp.where(vj > val, 1.0, 0.0))
    free = (min(SEL_TOPN, n_sel) - n_forced).astype(F32)
    sel = (forced & valid) | (cand & (cnt < free))
    sel_t = jnp.concatenate([jnp.where(sel, 1.0, 0.0), jnp.zeros((LANES - SEL_BIAS_WIDTH, tq), F32)], axis=0)
    sel_q = sel_t.T
    bias = jnp.where(sel_q > 0.5, 0.0, SEL_NEG)
    lane = lax.broadcasted_iota(jnp.int32, (tq, LANES), 1)
    bias_hi = pltpu.roll(bias, SEL_BIAS_WIDTH, 1)
    qr = qr_ref[0]
    for g in range(G):
        qg = qr[:, g * HEAD_DIM:(g + 1) * HEAD_DIM].astype(F32)
        qg = jnp.concatenate([qg, qg], axis=-1)
        qa_ref[0, 0, g] = jnp.where(lane < HEAD_DIM, qg, bias_hi).astype(BF16)


def _overlap_t(S):
    n_cmp = (S - CMP_LEN) // CMP_STRIDE + 1
    nc = S // CMP_STRIDE
    n_sel = S // SEL_LEN
    start = np.arange(nc) * CMP_STRIDE
    end = start + CMP_LEN - 1
    sel_start = np.arange(n_sel) * SEL_LEN
    ov = (start[None, :] <= sel_start[:, None] + SEL_LEN - 1) & (end[None, :] >= sel_start[:, None])
    ov = ov & (np.arange(nc) < n_cmp)[None, :]
    out = np.zeros((LANES, nc), np.float32)
    out[:n_sel] = ov.astype(np.float32)
    return jnp.asarray(out)


def _nsa_cmp_call(qn, qr, kc, vc, tq):
    B, S, _ = qn.shape
    H, G = NSA_KV_HEADS, NSA_GROUP
    nc = kc.shape[2]
    n_sel = S // SEL_LEN
    assert n_sel <= SEL_BIAS_WIDTH and nc % LANES == 0
    qspec = pl.BlockSpec((1, tq, G * HEAD_DIM), lambda b, h, i: (b, i, h))
    cspec = pl.BlockSpec((1, 1, nc, HEAD_DIM), lambda b, h, i: (b, h, 0, 0))
    return pl.pallas_call(
        functools.partial(_nsa_cmp_kernel, n_sel=n_sel),
        grid=(B, H, S // tq),
        in_specs=[qspec, qspec, cspec, cspec, pl.BlockSpec((LANES, nc), lambda b, h, i: (0, 0))],
        out_specs=[pl.BlockSpec((1, 1, G, tq, HEAD_DIM), lambda b, h, i: (b, h, 0, i, 0)),
                   pl.BlockSpec((1, 1, G, tq, LANES), lambda b, h, i: (b, h, 0, i, 0))],
        out_shape=[jax.ShapeDtypeStruct((B, H, G, S, HEAD_DIM), F32),
                   jax.ShapeDtypeStruct((B, H, G, S, LANES), BF16)],
        compiler_params=_cparams("arbitrary", "arbitrary", "arbitrary"),
        name="nsa_cmp",
    )(qn, qr, kc, vc, _overlap_t(S))


SEL_CHUNK = 4


def _nsa_flash_kernel(qa_ref, ks_ref, vs_ref, kw_ref, vw_ref, oc_ref, g_ref, o_ref, m_scr, acc_scr):
    G = NSA_GROUP
    tq = qa_ref.shape[3]
    R = G * tq
    tk = tq
    i = pl.program_id(2)
    q0 = i * tq

    def rows_of(ref, j0, nt):
        return ref[0, 0, pl.ds(pl.multiple_of(j0 * tk, tk), nt * tk), :]

    def attend(k_ref, v_ref, j0, nt, visible, state):
        k, v = rows_of(k_ref, j0, nt), rows_of(v_ref, j0, nt)
        ss = [_dot_nt(qa_ref[0, 0, g], k) for g in range(G)]
        if visible is not None:
            qpos = q0 + lax.broadcasted_iota(jnp.int32, (tq, nt * tk), 0)
            kpos = j0 * tk + lax.broadcasted_iota(jnp.int32, (tq, nt * tk), 1)
            mask = visible(qpos, kpos)
        ps, alphas = [], []
        for g in range(G):
            s = ss[g] if visible is None else jnp.where(mask, ss[g], MASK_NEG)
            cols = [s[:, c * LANES:(c + 1) * LANES] for c in range(nt * tk // LANES)]
            m_new = jnp.max(functools.reduce(jnp.maximum, cols), axis=-1, keepdims=True)
            if state:
                m_prev = m_scr[g]
                m_new = jnp.maximum(m_prev, m_new)
                alphas.append(jnp.exp(m_prev - m_new))
                m_scr[g] = m_new
            ps.append(jnp.concatenate([jnp.exp(c - m_new) for c in cols], axis=-1).astype(BF16))
        if not state:
            return [_dot(ps[g], v) for g in range(G)]
        for g in range(G):
            acc_scr[g] = alphas[g] * acc_scr[g] + _dot(ps[g], v)

    def normalise(acc):
        return acc[:, :HEAD_DIM] / acc[:, HEAD_DIM:HEAD_DIM + 1]

    m_scr[...] = jnp.full((G, tq, LANES), MASK_NEG, F32)
    acc_scr[...] = jnp.zeros((G, tq, LANES), F32)
    causal = lambda qpos, kpos: kpos <= qpos

    def sel_body(c, carry):
        attend(ks_ref, vs_ref, c * SEL_CHUNK, SEL_CHUNK, None, True)
        return carry

    n_full = i // SEL_CHUNK
    lax.fori_loop(0, n_full, sel_body, 0)
    attend(ks_ref, vs_ref, n_full * SEL_CHUNK, 2, causal, True)

    @pl.when(i % SEL_CHUNK >= 2)
    def _():
        attend(ks_ref, vs_ref, n_full * SEL_CHUNK + 2, 2, causal, True)

    o_sel = normalise(acc_scr[...].reshape(R, LANES))

    assert WINDOW == 2 * tk
    band = lambda qpos, kpos: (kpos <= qpos) & (kpos > qpos - WINDOW)
    o_win = normalise(jnp.concatenate(attend(kw_ref, vw_ref, jnp.maximum(i - 2, 0), 3, band, False), axis=0))

    o_cmp = oc_ref[0, 0].reshape(R, HEAD_DIM)
    gates = g_ref[0, 0]
    outs = []
    for g in range(G):
        r = slice(g * tq, (g + 1) * tq)
        outs.append(gates[:, 3 * g:3 * g + 1] * o_cmp[r]
                    + gates[:, 3 * g + 1:3 * g + 2] * o_sel[r]
                    + gates[:, 3 * g + 2:3 * g + 3] * o_win[r])
    o_ref[0] = jnp.concatenate(outs, axis=-1).astype(BF16)


def _nsa_flash_call(qa, ks, vs, kw, vw, oc, gates):
    B, H, G, S, _ = qa.shape
    tq = WINDOW // 2
    assert (S // tq) % SEL_CHUNK == 0
    kv = pl.BlockSpec((1, 1, S, LANES), lambda b, h, i: (b, h, 0, 0))
    return pl.pallas_call(
        _nsa_flash_kernel,
        grid=(B, H, S // tq),
        in_specs=[pl.BlockSpec((1, 1, G, tq, LANES), lambda b, h, i: (b, h, 0, i, 0)),
                  kv, kv, kv, kv,
                  pl.BlockSpec((1, 1, G, tq, HEAD_DIM), lambda b, h, i: (b, h, 0, i, 0)),
                  pl.BlockSpec((1, 1, tq, LANES), lambda b, h, i: (b, h, i, 0))],
        out_specs=pl.BlockSpec((1, tq, G * HEAD_DIM), lambda b, h, i: (b, i, h)),
        out_shape=jax.ShapeDtypeStruct((B, S, Q_WIDTH), BF16),
        scratch_shapes=[pltpu.VMEM((G, tq, LANES), F32), pltpu.VMEM((G, tq, LANES), F32)],
        compiler_params=_cparams("arbitrary", "arbitrary", "arbitrary"),
        name="nsa_flash",
    )(qa, ks, vs, kw, vw, oc, gates)


def _gmlp_kernel(u_ref, v_ref, ws_ref, bs_ref, o_ref):
    tm = u_ref.shape[1]
    C = GM_CHUNK
    r = lax.broadcasted_iota(jnp.int32, (C, C), 0)
    c = lax.broadcasted_iota(jnp.int32, (C, C), 1)
    causal = c <= r
    ws = [jnp.where(causal, ws_ref[g], 0.0).astype(BF16) for g in range(GM_GROUPS)]
    for n in range(tm // C):
        rows = slice(n * C, (n + 1) * C)
        vn = v_ref[0, rows, :]
        mixed = jnp.concatenate(
            [_dot(ws[g], vn[:, g * GM_GROUP_DIM:(g + 1) * GM_GROUP_DIM]) for g in range(GM_GROUPS)],
            axis=-1)
        o_ref[0, rows, :] = (u_ref[0, rows, :].astype(F32) * (mixed + bs_ref[...])).astype(BF16)


def _gmlp_call(u, v, ws, bs_full, tm):
    B, S, W = u.shape
    tok = pl.BlockSpec((1, tm, W), lambda b, i: (b, i, 0))
    return pl.pallas_call(
        _gmlp_kernel,
        grid=(B, S // tm),
        in_specs=[tok, tok,
                  pl.BlockSpec((GM_GROUPS, GM_CHUNK, GM_CHUNK), lambda b, i: (0, 0, 0)),
                  pl.BlockSpec((GM_CHUNK, W), lambda b, i: (0, 0))],
        out_specs=tok,
        out_shape=jax.ShapeDtypeStruct((B, S, W), BF16),
        compiler_params=_cparams("arbitrary", "arbitrary"),
        name="gmlp",
    )(u, v, ws, bs_full)


def _mixout_kernel(on_ref, og_ref, ga_ref, gb_ref, x_ref, gt_ref, sc_ref, sh_ref,
                   wpn_ref, wpg_ref, wo_ref, wr_ref, wgs_ref, wus_ref, wds_ref,
                   x1_ref, h2_ref, lg_ref, shd_ref):
    ya = _dot(on_ref[0], wpn_ref[...])
    yb = _dot(og_ref[0], wpg_ref[...])
    merged = ga_ref[0].astype(F32) * ya + gb_ref[0].astype(F32) * yb
    x1 = x_ref[0] + gt_ref[0] * _dot(merged.astype(BF16), wo_ref[...])
    x1_ref[0] = x1
    ms = jnp.mean(x1 * x1, axis=-1, keepdims=True)
    h2 = (x1 * lax.rsqrt(ms + EPS)) * (1.0 + sc_ref[0]) + sh_ref[0]
    lg_ref[0] = _dot(h2, wr_ref[...], precision=HI)
    hb = h2.astype(BF16)
    h2_ref[0] = _pack_halves(hb)
    act = _silu(_dot(hb, wgs_ref[...])) * _dot(hb, wus_ref[...])
    shd_ref[0] = _dot(act.astype(BF16), wds_ref[...])


def _mixout_call(o_nsa, o_gm, ga, gb, x, gt, sc, sh, wpn, wpg, wo, wr, wgs, wus, wds, tm):
    B, S, D = x.shape
    tok = lambda w: pl.BlockSpec((1, tm, w), lambda b, i: (b, i, 0))
    per_b = pl.BlockSpec((1, 1, D), lambda b, i: (b, 0, 0))
    full = lambda a: pl.BlockSpec(a.shape, lambda b, i: (0,) * a.ndim)
    sds = jax.ShapeDtypeStruct
    return pl.pallas_call(
        _mixout_kernel,
        grid=(B, S // tm),
        in_specs=[tok(Q_WIDTH), tok(GM_WIDTH), tok(D), tok(D), tok(D), per_b, per_b, per_b,
                  full(wpn), full(wpg), full(wo), full(wr), full(wgs), full(wus), full(wds)],
        out_specs=[tok(D), tok(D // 2), tok(N_EXPERTS), tok(D)],
        out_shape=[sds((B, S, D), F32), sds((B, S, D // 2), jnp.uint32), sds((B, S, N_EXPERTS), F32),
                   sds((B, S, D), F32)],
        compiler_params=_cparams("arbitrary", "arbitrary"),
        name="mix_out",
    )(o_nsa, o_gm, ga, gb, x, gt, sc, sh, wpn, wpg, wo, wr, wgs, wus, wds)


def _route_kernel(lg_ref, br_ref, idx_ref, wt_ref, rank_ref, cnt_ref, run_scr):
    tm, E = lg_ref.shape

    @pl.when(pl.program_id(0) == 0)
    def _():
        run_scr[...] = jnp.zeros_like(run_scr)

    aff = _sigmoid(lg_ref[...])
    work = aff + br_ref[...]
    lane = lax.broadcasted_iota(jnp.int32, (tm, E), 1).astype(F32)
    picked = jnp.zeros((tm, E), F32)
    idxs, tops = [], []
    for _ in range(TOP_K):
        m = jnp.max(work, axis=-1, keepdims=True)
        idx = jnp.min(jnp.where(work == m, lane, float(E)), axis=-1, keepdims=True)
        hit = lane == idx
        tops.append(jnp.sum(jnp.where(hit, aff, 0.0), axis=-1, keepdims=True))
        idxs.append(idx)
        picked = jnp.where(hit, 1.0, picked)
        work = jnp.where(hit, -jnp.inf, work)
    total = tops[0]
    for t in tops[1:]:
        total = total + t
    r = lax.broadcasted_iota(jnp.int32, (tm, tm), 0)
    c = lax.broadcasted_iota(jnp.int32, (tm, tm), 1)
    before = _dot(jnp.where(c < r, 1.0, 0.0).astype(BF16), picked.astype(BF16)) + run_scr[...]
    ranks = [jnp.sum(jnp.where(lane == idx, before, 0.0), axis=-1, keepdims=True) for idx in idxs]
    run_scr[...] = run_scr[...] + jnp.sum(picked, axis=0, keepdims=True)
    cnt_ref[...] = run_scr[...]
    lane_k = lax.broadcasted_iota(jnp.int32, (tm, LANES), 1)
    idx_o = jnp.zeros((tm, LANES), jnp.int32)
    wt_o = jnp.zeros((tm, LANES), F32)
    rank_o = jnp.zeros((tm, LANES), jnp.int32)
    for k in range(TOP_K):
        idx_o = jnp.where(lane_k == k, idxs[k].astype(jnp.int32), idx_o)
        wt_o = jnp.where(lane_k == k, tops[k] / total * ROUTE_SCALE, wt_o)
        rank_o = jnp.where(lane_k == k, ranks[k].astype(jnp.int32), rank_o)
    idx_ref[...] = idx_o
    wt_ref[...] = wt_o
    rank_ref[...] = rank_o


def _route_call(logits, b_router, tm):
    T, E = logits.shape
    tok = lambda w: pl.BlockSpec((tm, w), lambda i: (i, 0))
    one = pl.BlockSpec((1, E), lambda i: (0, 0))
    sds = jax.ShapeDtypeStruct
    return pl.pallas_call(
        _route_kernel,
        grid=(T // tm,),
        in_specs=[tok(E), one],
        out_specs=[tok(LANES), tok(LANES), tok(LANES), one],
        out_shape=[sds((T, LANES), jnp.int32), sds((T, LANES), F32), sds((T, LANES), jnp.int32),
                   sds((1, E), F32)],
        scratch_shapes=[pltpu.VMEM((1, E), F32)],
        compiler_params=_cparams("arbitrary"),
        name="route",
    )(logits, b_router.reshape(1, E))


def _slot_kernel(idx_ref, rank_ref, ps_ref, o_ref):
    tm = idx_ref.shape[0]
    E = ps_ref.shape[1]
    lane = lax.broadcasted_iota(jnp.int32, (tm, E), 1)
    lane_k = lax.broadcasted_iota(jnp.int32, (tm, LANES), 1)
    idx, rank, ps = idx_ref[...], rank_ref[...], ps_ref[...]
    out = jnp.zeros((tm, LANES), jnp.int32)
    for k in range(TOP_K):
        base = jnp.sum(jnp.where(lane == idx[:, k:k + 1], ps, 0.0), axis=-1, keepdims=True)
        out = jnp.where(lane_k == k, base.astype(jnp.int32) + rank[:, k:k + 1], out)
    o_ref[...] = out[:, :TOP_K]


def _slot_call(idx, rank, pad_start, tm):
    T = idx.shape[0]
    E = pad_start.shape[0]
    tok = pl.BlockSpec((tm, LANES), lambda i: (i, 0))
    return pl.pallas_call(
        _slot_kernel,
        grid=(T // tm,),
        in_specs=[tok, tok, pl.BlockSpec((1, E), lambda i: (0, 0))],
        out_specs=pl.BlockSpec((tm, TOP_K), lambda i: (i, 0)),
        out_shape=jax.ShapeDtypeStruct((T, TOP_K), jnp.int32),
        compiler_params=_cparams("arbitrary"),
        name="slot",
    )(idx, rank, pad_start.astype(F32).reshape(1, E))


def _dispatch_kernel(bv_ref, slot_ref, h_ref, xs_ref, zbuf, sem, zsem):
    tm = h_ref.shape[0]
    rows = zbuf.shape[0]

    @pl.when(pl.program_id(0) == 0)
    def _():
        zbuf[...] = jnp.zeros_like(zbuf)

        def zero_copy(i):
            return pltpu.make_async_copy(zbuf, xs_ref.at[pl.ds(pl.multiple_of(i * rows, rows), rows), :], zsem)

        def start(i, carry):
            @pl.when(bv_ref[i] < rows)
            def _():
                zero_copy(i).start()
            return carry

        def wait(i, carry):
            @pl.when(bv_ref[i] < rows)
            def _():
                zero_copy(i).wait()
            return carry

        lax.fori_loop(0, bv_ref.shape[0], start, 0)
        lax.fori_loop(0, bv_ref.shape[0], wait, 0)

    def body(r, carry):
        for k in range(TOP_K):
            d = slot_ref[r * TOP_K + k]
            pltpu.make_async_copy(h_ref.at[pl.ds(r, 1), :], xs_ref.at[pl.ds(d, 1), :], sem).start(priority=k % 2)
        return carry

    lax.fori_loop(0, tm, body, 0)
    for k in range(TOP_K):
        pltpu.make_async_copy(h_ref, xs_ref.at[pl.ds(0, tm), :], sem).wait()


def _dispatch_call(blk_valid, slots, h2p, n_rows, tm):
    T, W = h2p.shape
    grid_spec = pltpu.PrefetchScalarGridSpec(
        num_scalar_prefetch=1,
        grid=(T // tm,),
        in_specs=[pl.BlockSpec((tm * TOP_K,), lambda i, bv: (i,), memory_space=pltpu.SMEM),
                  pl.BlockSpec((tm, W), lambda i, bv: (i, 0))],
        out_specs=pl.BlockSpec(memory_space=pl.ANY),
        scratch_shapes=[pltpu.VMEM((MOE_ROWS, W), jnp.uint32), pltpu.SemaphoreType.DMA, pltpu.SemaphoreType.DMA],
    )
    return pl.pallas_call(
        _dispatch_kernel,
        grid_spec=grid_spec,
        out_shape=jax.ShapeDtypeStruct((n_rows, W), jnp.uint32),
        compiler_params=pltpu.CompilerParams(dimension_semantics=("arbitrary",), vmem_limit_bytes=VMEM_LIMIT,
                                             has_side_effects=True),
        name="dispatch",
    )(blk_valid, slots, h2p)


def _expert_kernel(be_ref, nb_ref, x_ref, wg_ref, wu_ref, wd_ref, y_ref, act_scr):
    i = pl.program_id(0)
    nb = nb_ref[0]

    def gate_up():
        lo, hi = _unpack_halves(x_ref[...])
        x = jnp.concatenate([lo, hi], axis=1).astype(BF16)
        a = _dot(x, wg_ref[0, 0].astype(BF16))
        b = _dot(x, wu_ref[0, 0].astype(BF16))
        return (_silu(a) * b).astype(BF16)

    def down(act):
        return _pack_halves(_dot(act, wd_ref[0, 0].astype(BF16)))

    @pl.when(i == 0)
    def _():
        act_scr[...] = gate_up()
        y_ref[...] = jnp.zeros_like(y_ref)

    @pl.when((i >= 1) & (i < nb))
    def _():
        prev = act_scr[...]
        y_ref[...] = down(prev)
        act_scr[...] = gate_up()

    @pl.when(i == nb)
    def _():
        y_ref[...] = down(act_scr[...])

    @pl.when(i > nb)
    def _():
        y_ref[...] = jnp.zeros_like(y_ref)


def _expert_call(layer, blk_e, n_used, xs, wg, wu, wd):
    n_rows, W = xs.shape
    rows = MOE_ROWS
    nblk = n_rows // rows
    D, F = wg.shape[2], wg.shape[3]
    cur = lambda i, be, nb: (layer, be[jnp.minimum(i, nblk - 1)], 0, 0)
    prev = lambda i, be, nb: (layer, be[jnp.maximum(i - 1, 0)], 0, 0)
    grid_spec = pltpu.PrefetchScalarGridSpec(
        num_scalar_prefetch=2,
        grid=(nblk + 1,),
        in_specs=[pl.BlockSpec((rows, W), lambda i, be, nb: (jnp.minimum(i, nb[0] - 1), 0)),
                  pl.BlockSpec((1, 1, D, F), cur), pl.BlockSpec((1, 1, D, F), cur),
                  pl.BlockSpec((1, 1, F, D), prev)],
        out_specs=pl.BlockSpec((rows, W), lambda i, be, nb: (jnp.maximum(i - 1, 0), 0)),
        scratch_shapes=[pltpu.VMEM((rows, F), BF16)],
    )
    return pl.pallas_call(
        _expert_kernel,
        grid_spec=grid_spec,
        out_shape=jax.ShapeDtypeStruct((n_rows, W), jnp.uint32),
        compiler_params=_cparams("arbitrary"),
        name="expert",
    )(blk_e, n_used, xs, wg, wu, wd)


def _combine_kernel(slot_ref, wt_ref, shd_ref, x_ref, gt_ref, y_ref, o_ref, ybuf, sem):
    tm = x_ref.shape[0]

    def body(r, carry):
        for k in range(TOP_K):
            d = slot_ref[r * TOP_K + k]
            pltpu.make_async_copy(y_ref.at[pl.ds(d, 1), :], ybuf.at[k, pl.ds(r, 1), :], sem).start(priority=k % 2)
        return carry

    lax.fori_loop(0, tm, body, 0)
    for k in range(TOP_K):
        pltpu.make_async_copy(y_ref.at[pl.ds(0, tm), :], ybuf.at[k], sem).wait()
    wt = wt_ref[...]
    acc_lo = acc_hi = None
    for k in range(TOP_K):
        lo, hi = _unpack_halves(ybuf[k])
        w = wt[:, k:k + 1]
        acc_lo = w * lo if k == 0 else acc_lo + w * lo
        acc_hi = w * hi if k == 0 else acc_hi + w * hi
    routed = jnp.concatenate([acc_lo, acc_hi], axis=1)
    o_ref[...] = x_ref[...] + gt_ref[0] * (routed + shd_ref[...])


def _combine_call(slots, wts, shared, x1, gt, y, tm):
    T, D = x1.shape
    B = gt.shape[0]
    per_b = T // B // tm
    tok = lambda w: pl.BlockSpec((tm, w), lambda i: (i, 0))
    return pl.pallas_call(
        _combine_kernel,
        grid=(T // tm,),
        in_specs=[pl.BlockSpec((tm * TOP_K,), lambda i: (i,), memory_space=pltpu.SMEM),
                  tok(LANES), tok(D), tok(D),
                  pl.BlockSpec((1, 1, D), lambda i: (i // per_b, 0, 0)),
                  pl.BlockSpec(memory_space=pl.ANY)],
        out_specs=tok(D),
        out_shape=jax.ShapeDtypeStruct((T, D), F32),
        scratch_shapes=[pltpu.VMEM((TOP_K, tm, y.shape[1]), jnp.uint32), pltpu.SemaphoreType.DMA],
        compiler_params=_cparams("arbitrary"),
        name="combine",
    )(slots, wts, shared, x1, gt, y)


def _rope_tables(positions):
    half = HEAD_DIM // 2
    inv = ROPE_THETA ** (-jnp.arange(half, dtype=F32) / half)
    ang = positions.astype(F32)[..., None] * inv
    cos, sin, zero = jnp.cos(ang), jnp.sin(ang), jnp.zeros_like(ang)
    cos_t = jnp.concatenate([cos, cos] * 2, axis=-1)
    sin_lo = jnp.concatenate([-sin, zero] * 2, axis=-1)
    sin_hi = jnp.concatenate([zero, sin] * 2, axis=-1)
    return cos_t, sin_lo, sin_hi


def _reorder_w_in(w):
    o = np.cumsum([0, Q_WIDTH] + [KV_WIDTH] * 6 + [NSA_Q_HEADS * 3, GM_WIDTH, GM_WIDTH, D_MODEL, D_MODEL])
    q, kc, vc, ks, vs, kw, vw, g, u, v, ga, gb = [w[:, o[i]:o[i + 1]] for i in range(12)]
    per = NSA_GROUP * 3
    pad = jnp.zeros((w.shape[0], LANES - per), w.dtype)
    return jnp.concatenate([q, ks, kw, kc, vc, vs, vw, g[:, :per], pad, g[:, per:], pad, u, v, ga, gb], axis=1)


def _block_plan(counts, n_rows):
    rows = MOE_ROWS
    counts = counts.astype(jnp.int32)
    padded = (counts + rows - 1) // rows * rows
    pad_end = jnp.cumsum(padded)
    pad_start = pad_end - padded
    blk_start = jnp.arange(n_rows // rows, dtype=jnp.int32) * rows
    blk_e = jnp.minimum(jnp.sum((pad_end[None, :] <= blk_start[:, None]).astype(jnp.int32), axis=1), N_EXPERTS - 1)
    own = blk_e[:, None] == jnp.arange(N_EXPERTS, dtype=jnp.int32)[None, :]
    e_count = jnp.sum(jnp.where(own, counts[None, :], 0), axis=1)
    e_start = jnp.sum(jnp.where(own, pad_start[None, :], 0), axis=1)
    blk_valid = jnp.clip(e_count - (blk_start - e_start), 0, rows).astype(jnp.int32)
    n_used = (pad_end[-1:] // rows).astype(jnp.int32)
    return pad_start, blk_e.astype(jnp.int32), blk_valid, n_used


def kernel(x, c, positions, w_mod, b_mod, w_in, q_gain, k_gain, cmp_pos_k, cmp_pos_v, cmp_w1_k, cmp_w2_k, cmp_w1_v, cmp_w2_v, gm_ln_g, gm_ln_b, gm_ws, gm_bs, w_proj_nsa, w_proj_gm, w_out, w_router, b_router, w_gate_e, w_up_e, w_down_e, w_gate_sh, w_up_sh, w_down_sh):
    B, S, D = x.shape
    L = w_mod.shape[0]
    T = B * S
    tm = 256
    scale = HEAD_DIM ** -0.5
    cos_t, sin_lo, sin_hi = _rope_tables(positions)
    mod = _mod_call(c, w_mod, b_mod)
    n_rows = T * TOP_K + N_EXPERTS * MOE_ROWS

    for l in range(L):
        sh_a, sc_a, gt_a, sh_f, sc_f, gt_f = [mod[l, :, i * D:(i + 1) * D].reshape(B, 1, D) for i in range(6)]
        qg = (jnp.tile(q_gain[l], NSA_Q_HEADS) * scale).reshape(1, Q_WIDTH)
        kg = jnp.tile(k_gain[l], 2 * NSA_KV_HEADS).reshape(1, 2 * KV_WIDTH)
        (qn, qr, ks, kw, vs, vw, kc_raw, vc_raw, gates, u, v, ga, gb) = _inproj_call(
            x, sc_a, sh_a, _reorder_w_in(w_in[l]).astype(BF16), qg, kg, cos_t, sin_lo, sin_hi,
            gm_ln_g[l].reshape(1, GM_WIDTH), gm_ln_b[l].reshape(1, GM_WIDTH), tm)
        kc, vc = _compress_call(
            kc_raw, vc_raw, cmp_w1_k[l].astype(BF16), cmp_w2_k[l].astype(BF16), cmp_pos_k[l].reshape(1, -1),
            cmp_w1_v[l].astype(BF16), cmp_w2_v[l].astype(BF16), cmp_pos_v[l].reshape(1, -1),
            k_gain[l].reshape(1, HEAD_DIM))
        o_cmp, q_aug = _nsa_cmp_call(qn, qr, kc, vc, WINDOW // 2)
        o_nsa = _nsa_flash_call(q_aug, ks, vs, kw, vw, o_cmp, gates)
        bs_full = jnp.repeat(gm_bs[l].T, GM_GROUP_DIM, axis=1)
        o_gm = _gmlp_call(u, v, gm_ws[l], bs_full, 512)
        x1, h2, logits, shared = _mixout_call(
            o_nsa, o_gm, ga, gb, x, gt_a, sc_f, sh_f,
            w_proj_nsa[l].astype(BF16), w_proj_gm[l].astype(BF16), w_out[l].astype(BF16), w_router[l],
            w_gate_sh[l].astype(BF16), w_up_sh[l].astype(BF16), w_down_sh[l].astype(BF16), tm)
        idx, wts, rank, counts = _route_call(logits.reshape(T, N_EXPERTS), b_router[l], tm)
        pad_start, blk_e, blk_valid, n_used = _block_plan(counts[0], n_rows)
        slots = _slot_call(idx, rank, pad_start, tm).reshape(T * TOP_K)
        xs = _dispatch_call(blk_valid, slots, h2.reshape(T, D // 2), n_rows, tm)
        y = _expert_call(l, blk_e, n_used, xs, w_gate_e, w_up_e, w_down_e)
        x = _combine_call(slots, wts, shared.reshape(T, D), x1.reshape(T, D), gt_f, y, tm).reshape(B, S, D)
    return x
```

```python
import functools

import jax
import jax.numpy as jnp
import numpy as np
from jax import lax
from jax.experimental import pallas as pl
from jax.experimental.pallas import tpu as pltpu

D_MODEL = 1024
NSA_Q_HEADS = 8
NSA_KV_HEADS = 2
HEAD_DIM = 64
NSA_GROUP = NSA_Q_HEADS // NSA_KV_HEADS
CMP_LEN = 32
CMP_STRIDE = 16
CMP_HIDDEN = 256
SEL_LEN = 64
SEL_TOPN = 16
WINDOW = 512
ROPE_THETA = 10000.0
Q_WIDTH = NSA_Q_HEADS * HEAD_DIM
KV_WIDTH = NSA_KV_HEADS * HEAD_DIM
GM_GROUPS = 8
GM_GROUP_DIM = 64
GM_WIDTH = GM_GROUPS * GM_GROUP_DIM
GM_CHUNK = 128
N_EXPERTS = 256
TOP_K = 8
D_EXPERT = 256
D_SHARED = 256
ROUTE_SCALE = 2.5
EPS = 1e-6

LANES = 128
SEL_BIAS_WIDTH = 64
MASK_NEG = -1e30
SEL_NEG = -30000.0
MOE_GRAN = 128
MOE_ROWS = 9 * MOE_GRAN
VMEM_LIMIT = 56 * 1024 * 1024

F32 = jnp.float32
BF16 = jnp.bfloat16
HI = lax.Precision.HIGHEST


def _cparams(*sem):
    return pltpu.CompilerParams(dimension_semantics=sem, vmem_limit_bytes=VMEM_LIMIT)


def _dot(a, b, **kw):
    return jnp.dot(a, b, preferred_element_type=F32, **kw)


def _dot_nt(a, b, **kw):
    return lax.dot_general(a, b, (((1,), (1,)), ((), ())), preferred_element_type=F32, **kw)


def _gelu(x):
    return 0.5 * x * (1.0 + jnp.tanh(0.7978845608028654 * (x + 0.044715 * (x * x * x))))


def _sigmoid(x):
    return 1.0 / (1.0 + jnp.exp(-x))


def _silu(x):
    return x * _sigmoid(x)


_HI_MASK = np.uint32(0xFFFF0000)


def _pack_halves(a):
    w = a.shape[1] // 2
    bits = lax.bitcast_convert_type(a.astype(BF16).astype(F32), jnp.uint32)
    return (bits[:, w:] & _HI_MASK) | (bits[:, :w] >> 16)


def _unpack_halves(words):
    lo = lax.bitcast_convert_type(words << 16, F32)
    hi = lax.bitcast_convert_type(words & _HI_MASK, F32)
    return lo, hi


def _mod_kernel(c_ref, w_ref, b_ref, o_ref):
    c = c_ref[...]
    o_ref[0] = _dot(_silu(c), w_ref[0], precision=HI) + b_ref[0]


def _mod_call(c, w_mod, b_mod):
    L, D, N = w_mod.shape
    B = c.shape[0]
    tn = 1536
    return pl.pallas_call(
        _mod_kernel,
        grid=(L, N // tn),
        in_specs=[pl.BlockSpec((B, D), lambda l, j: (0, 0)),
                  pl.BlockSpec((1, D, tn), lambda l, j: (l, 0, j)),
                  pl.BlockSpec((1, 1, tn), lambda l, j: (l, 0, j))],
        out_specs=pl.BlockSpec((1, B, tn), lambda l, j: (l, 0, j)),
        out_shape=jax.ShapeDtypeStruct((L, B, N), F32),
        compiler_params=_cparams("arbitrary", "arbitrary"),
        name="mod",
    )(c, w_mod, b_mod.reshape(L, 1, N))


_C_Q = 0
_C_K = 512
_C_KC = 768
_C_VC = 896
_C_VS = 1024
_C_VW = 1152
_C_G = 1280
_C_U = 1536
_C_V = 2048
_C_GA = 2560
_C_GB = 3584
IN_COLS_P = 4608


def _head_norm(z, bd):
    ms = _dot(z * z, bd, precision=HI)
    return z * lax.rsqrt(ms + EPS)


def _rope(z, cos, sin_lo, sin_hi):
    w = z.shape[-1]
    half = HEAD_DIM // 2
    return z * cos + pltpu.roll(z, w - half, 1) * sin_lo + pltpu.roll(z, half, 1) * sin_hi


def _tile_lanes(t, n):
    return t if n == 1 else jnp.concatenate([t] * n, axis=-1)


def _inproj_kernel(x_ref, sc_ref, sh_ref, w_ref, bdq_ref, bdk_ref, qg_ref, kg_ref,
                   cos_ref, sl_ref, shi_ref, lng_ref, lnb_ref,
                   qn_ref, qr_ref, ks_ref, kw_ref, vs_ref, vw_ref, kc_ref, vc_ref,
                   g_ref, u_ref, v_ref, ga_ref, gb_ref):
    tm = x_ref.shape[1]
    x = x_ref[0]
    ms = jnp.mean(x * x, axis=-1, keepdims=True)
    h = (x * lax.rsqrt(ms + EPS)) * (1.0 + sc_ref[0]) + sh_ref[0]
    hb = h.astype(BF16)

    def mm(lo, width):
        return _dot(hb, w_ref[:, lo:lo + width])

    cos, sl, shi = cos_ref[0], sl_ref[0], shi_ref[0]

    zq = mm(_C_Q, Q_WIDTH)
    qn = _head_norm(zq, bdq_ref[...]) * qg_ref[...]
    qr = _rope(qn, _tile_lanes(cos, 4), _tile_lanes(sl, 4), _tile_lanes(shi, 4))
    qn_ref[0] = qn.astype(BF16)
    qr_ref[0] = qr.astype(BF16)

    zk = mm(_C_K, 2 * KV_WIDTH)
    kn = _head_norm(zk, bdk_ref[...]) * kg_ref[...]
    kr = _rope(kn, _tile_lanes(cos, 2), _tile_lanes(sl, 2), _tile_lanes(shi, 2))
    lane = lax.broadcasted_iota(jnp.int32, (tm, LANES), 1)
    tok = pl.program_id(1) * tm + lax.broadcasted_iota(jnp.int32, (tm, LANES), 0)
    onehot = jnp.where(lane - HEAD_DIM == tok // SEL_LEN, 1.0, 0.0)
    ones_col = jnp.where(lane == HEAD_DIM, 1.0, 0.0)
    low = lane < HEAD_DIM
    zvs = mm(_C_VS, KV_WIDTH)
    zvw = mm(_C_VW, KV_WIDTH)
    zkc = mm(_C_KC, KV_WIDTH)
    zvc = mm(_C_VC, KV_WIDTH)
    for kv in range(NSA_KV_HEADS):
        def head(a):
            return a if kv == 0 else pltpu.roll(a, HEAD_DIM, 1)
        ks_ref[0, kv] = jnp.where(low, head(kr[:, :KV_WIDTH]), onehot).astype(BF16)
        kw_ref[0, kv] = jnp.where(low, head(kr[:, KV_WIDTH:]), 0.0).astype(BF16)
        vs_ref[0, kv] = jnp.where(low, head(zvs), ones_col).astype(BF16)
        vw_ref[0, kv] = jnp.where(low, head(zvw), ones_col).astype(BF16)
        kc_ref[0, kv] = head(zkc)[:, :HEAD_DIM]
        vc_ref[0, kv] = head(zvc)[:, :HEAD_DIM]

    zg = mm(_C_G, 2 * LANES)
    sg = _sigmoid(zg)
    g_ref[0, 0] = sg[:, :LANES]
    g_ref[0, 1] = sg[:, LANES:]

    u_ref[0] = _gelu(mm(_C_U, GM_WIDTH)).astype(BF16)
    gv = _gelu(mm(_C_V, GM_WIDTH))
    mu = jnp.mean(gv, axis=-1, keepdims=True)
    cen = gv - mu
    var = jnp.mean(cen * cen, axis=-1, keepdims=True)
    v_ref[0] = ((cen * lax.rsqrt(var + EPS)) * lng_ref[...] + lnb_ref[...]).astype(BF16)

    ga_ref[0] = _sigmoid(mm(_C_GA, D_MODEL)).astype(BF16)
    gb_ref[0] = _sigmoid(mm(_C_GB, D_MODEL)).astype(BF16)


def _block_diag_mean(width):
    idx = np.arange(width) // HEAD_DIM
    return jnp.asarray((idx[:, None] == idx[None, :]).astype(np.float32) / HEAD_DIM)


def _inproj_call(x, sc, sh, w_p, qg, kg, cos, sl, shi, lng, lnb, tm):
    B, S, D = x.shape
    H = NSA_KV_HEADS
    full = lambda *shape: pl.BlockSpec(shape, lambda b, i: (0,) * len(shape))
    tok3 = lambda w: pl.BlockSpec((1, tm, w), lambda b, i: (b, i, 0))
    per_b = pl.BlockSpec((1, 1, D), lambda b, i: (b, 0, 0))
    kv4 = lambda w: pl.BlockSpec((1, H, tm, w), lambda b, i: (b, 0, i, 0))
    sds = jax.ShapeDtypeStruct
    out_shape = [
        sds((B, S, Q_WIDTH), BF16), sds((B, S, Q_WIDTH), BF16),
        sds((B, H, S, LANES), BF16), sds((B, H, S, LANES), BF16),
        sds((B, H, S, LANES), BF16), sds((B, H, S, LANES), BF16),
        sds((B, H, S, HEAD_DIM), F32), sds((B, H, S, HEAD_DIM), F32),
        sds((B, H, S, LANES), F32),
        sds((B, S, GM_WIDTH), BF16), sds((B, S, GM_WIDTH), BF16),
        sds((B, S, D), BF16), sds((B, S, D), BF16),
    ]
    out_specs = [
        tok3(Q_WIDTH), tok3(Q_WIDTH), kv4(LANES), kv4(LANES), kv4(LANES), kv4(LANES),
        kv4(HEAD_DIM), kv4(HEAD_DIM), kv4(LANES),
        tok3(GM_WIDTH), tok3(GM_WIDTH), tok3(D), tok3(D),
    ]
    return pl.pallas_call(
        _inproj_kernel,
        grid=(B, S // tm),
        in_specs=[tok3(D), per_b, per_b, full(D, IN_COLS_P),
                  full(Q_WIDTH, Q_WIDTH), full(2 * KV_WIDTH, 2 * KV_WIDTH),
                  full(1, Q_WIDTH), full(1, 2 * KV_WIDTH),
                  tok3(LANES), tok3(LANES), tok3(LANES),
                  full(1, GM_WIDTH), full(1, GM_WIDTH)],
        out_specs=out_specs,
        out_shape=out_shape,
        compiler_params=_cparams("arbitrary", "arbitrary"),
        name="in_proj",
    )(x, sc, sh, w_p, _block_diag_mean(Q_WIDTH), _block_diag_mean(2 * KV_WIDTH), qg, kg,
      cos, sl, shi, lng, lnb)


def _compress_kernel(kr_ref, vr_ref, w1k_ref, w2k_ref, pek_ref, w1v_ref, w2v_ref, pev_ref,
                     kg_ref, kc_ref, vc_ref):
    nc = kr_ref.shape[2]
    half = CMP_STRIDE * HEAD_DIM

    def mlp(raw, w1_ref, w2_ref, pe_ref):
        a = raw.astype(BF16)
        top = _dot(a, w1_ref[:half, :])
        bot = _dot(a, w1_ref[half:, :])
        pe = jnp.broadcast_to(pe_ref[...], (8, 2 * half)).astype(BF16)
        pe_row = _dot(pe, w1_ref[...])[0:1, :]
        hid = top + pltpu.roll(bot, nc - 1, 0) + pe_row
        return _dot(_gelu(hid).astype(BF16), w2_ref[...])

    kc = mlp(kr_ref[0, 0], w1k_ref, w2k_ref, pek_ref)
    ms = jnp.mean(kc * kc, axis=-1, keepdims=True)
    kc_ref[0, 0] = (kc * lax.rsqrt(ms + EPS) * kg_ref[...]).astype(BF16)
    vc_ref[0, 0] = mlp(vr_ref[0, 0], w1v_ref, w2v_ref, pev_ref).astype(BF16)


def _compress_call(kc_raw, vc_raw, w1k, w2k, pek, w1v, w2v, pev, kg):
    B, H, S, hd = kc_raw.shape
    nc = S // CMP_STRIDE
    feat = CMP_STRIDE * hd
    raw = pl.BlockSpec((1, 1, nc, feat), lambda b, h: (b, h, 0, 0))
    full = lambda *shape: pl.BlockSpec(shape, lambda b, h: (0,) * len(shape))
    out = pl.BlockSpec((1, 1, nc, hd), lambda b, h: (b, h, 0, 0))
    return pl.pallas_call(
        _compress_kernel,
        grid=(B, H),
        in_specs=[raw, raw, full(2 * feat, CMP_HIDDEN), full(CMP_HIDDEN, hd), full(1, 2 * feat),
                  full(2 * feat, CMP_HIDDEN), full(CMP_HIDDEN, hd), full(1, 2 * feat),
                  full(1, hd)],
        out_specs=[out, out],
        out_shape=[jax.ShapeDtypeStruct((B, H, nc, hd), BF16)] * 2,
        compiler_params=_cparams("arbitrary", "arbitrary"),
        name="compress",
    )(kc_raw.reshape(B, H, nc, feat), vc_raw.reshape(B, H, nc, feat),
      w1k, w2k, pek, w1v, w2v, pev, kg)


def _group_rows(a):
    return jnp.concatenate([a[:, g * HEAD_DIM:(g + 1) * HEAD_DIM] for g in range(NSA_GROUP)], axis=0)


def _nsa_cmp_kernel(qn_ref, qr_ref, kc_ref, vc_ref, ovl_ref, oc_ref, qa_ref, *, n_sel):
    tq = qn_ref.shape[1]
    nc = kc_ref.shape[2]
    G = NSA_GROUP
    q0 = pl.program_id(2) * tq
    q4 = _group_rows(qn_ref[0])
    s = _dot_nt(q4, kc_ref[0, 0])
    row = lax.broadcasted_iota(jnp.int32, (G, tq, nc), 1).reshape(G * tq, nc)
    col = lax.broadcasted_iota(jnp.int32, (G * tq, nc), 1)
    vis = col * CMP_STRIDE + (CMP_LEN - 1) <= q0 + row
    s = jnp.where(vis, s, MASK_NEG)
    m = jnp.max(s, axis=-1, keepdims=True)
    e = jnp.where(vis, jnp.exp(s - m), 0.0)
    p = e / jnp.maximum(jnp.sum(e, axis=-1, keepdims=True), 1e-30)
    oc = _dot(p.astype(BF16), vc_ref[0, 0])
    oc_ref[0, 0] = oc.reshape(G, tq, HEAD_DIM)

    psum = p[0:tq] + p[tq:2 * tq] + p[2 * tq:3 * tq] + p[3 * tq:4 * tq]
    imp = _dot_nt(ovl_ref[...], psum, precision=HI)[:SEL_BIAS_WIDTH]
    blk = lax.broadcasted_iota(jnp.int32, (SEL_BIAS_WIDTH, tq), 0)
    cur = (q0 + lax.broadcasted_iota(jnp.int32, (SEL_BIAS_WIDTH, tq), 1)) // SEL_LEN
    valid = blk <= cur
    forced = (blk == 0) | (blk == cur) | (blk == cur - 1)
    cand = valid & jnp.logical_not(forced)
    n_forced = jnp.minimum(cur, 2) + 1
    val = jnp.where(cand, imp, -1.0)
    cnt = jnp.zeros((SEL_BIAS_WIDTH, tq), F32)
    for j in range(n_sel):
        vj = jnp.broadcast_to(val[j:j + 1, :], (SEL_BIAS_WIDTH, tq))
        cnt = cnt + jnp.where(blk > j, jnp.where(vj >= val, 1.0, 0.0), jnp.where(vj > val, 1.0, 0.0))
    free = (min(SEL_TOPN, n_sel) - n_forced).astype(F32)
    sel = (forced & valid) | (cand & (cnt < free))
    sel_t = jnp.concatenate([jnp.where(sel, 1.0, 0.0), jnp.zeros((LANES - SEL_BIAS_WIDTH, tq), F32)], axis=0)
    sel_q = sel_t.T
    bias = jnp.where(sel_q > 0.5, 0.0, SEL_NEG)
    lane = lax.broadcasted_iota(jnp.int32, (tq, LANES), 1)
    bias_hi = pltpu.roll(bias, SEL_BIAS_WIDTH, 1)
    qr = qr_ref[0]
    for g in range(G):
        qg = qr[:, g * HEAD_DIM:(g + 1) * HEAD_DIM].astype(F32)
        qg = jnp.concatenate([qg, qg], axis=-1)
        qa_ref[0, 0, g] = jnp.where(lane < HEAD_DIM, qg, bias_hi).astype(BF16)


def _overlap_t(S):
    n_cmp = (S - CMP_LEN) // CMP_STRIDE + 1
    nc = S // CMP_STRIDE
    n_sel = S // SEL_LEN
    start = np.arange(nc) * CMP_STRIDE
    end = start + CMP_LEN - 1
    sel_start = np.arange(n_sel) * SEL_LEN
    ov = (start[None, :] <= sel_start[:, None] + SEL_LEN - 1) & (end[None, :] >= sel_start[:, None])
    ov = ov & (np.arange(nc) < n_cmp)[None, :]
    out = np.zeros((LANES, nc), np.float32)
    out[:n_sel] = ov.astype(np.float32)
    return jnp.asarray(out)


def _nsa_cmp_call(qn, qr, kc, vc, tq):
    B, S, _ = qn.shape
    H, G = NSA_KV_HEADS, NSA_GROUP
    nc = kc.shape[2]
    n_sel = S // SEL_LEN
    assert n_sel <= SEL_BIAS_WIDTH and nc % LANES == 0
    qspec = pl.BlockSpec((1, tq, G * HEAD_DIM), lambda b, h, i: (b, i, h))
    cspec = pl.BlockSpec((1, 1, nc, HEAD_DIM), lambda b, h, i: (b, h, 0, 0))
    return pl.pallas_call(
        functools.partial(_nsa_cmp_kernel, n_sel=n_sel),
        grid=(B, H, S // tq),
        in_specs=[qspec, qspec, cspec, cspec, pl.BlockSpec((LANES, nc), lambda b, h, i: (0, 0))],
        out_specs=[pl.BlockSpec((1, 1, G, tq, HEAD_DIM), lambda b, h, i: (b, h, 0, i, 0)),
                   pl.BlockSpec((1, 1, G, tq, LANES), lambda b, h, i: (b, h, 0, i, 0))],
        out_shape=[jax.ShapeDtypeStruct((B, H, G, S, HEAD_DIM), F32),
                   jax.ShapeDtypeStruct((B, H, G, S, LANES), BF16)],
        compiler_params=_cparams("arbitrary", "arbitrary", "arbitrary"),
        name="nsa_cmp",
    )(qn, qr, kc, vc, _overlap_t(S))


SEL_CHUNK = 4


def _nsa_flash_kernel(qa_ref, ks_ref, vs_ref, kw_ref, vw_ref, oc_ref, g_ref, o_ref, m_scr, acc_scr):
    G = NSA_GROUP
    tq = qa_ref.shape[3]
    R = G * tq
    tk = tq
    i = pl.program_id(2)
    q0 = i * tq

    def rows_of(ref, j0, nt):
        return ref[0, 0, pl.ds(pl.multiple_of(j0 * tk, tk), nt * tk), :]

    def attend(k_ref, v_ref, j0, nt, visible, state):
        k, v = rows_of(k_ref, j0, nt), rows_of(v_ref, j0, nt)
        ss = [_dot_nt(qa_ref[0, 0, g], k) for g in range(G)]
        if visible is not None:
            qpos = q0 + lax.broadcasted_iota(jnp.int32, (tq, nt * tk), 0)
            kpos = j0 * tk + lax.broadcasted_iota(jnp.int32, (tq, nt * tk), 1)
            mask = visible(qpos, kpos)
        ps, alphas = [], []
        for g in range(G):
            s = ss[g] if visible is None else jnp.where(mask, ss[g], MASK_NEG)
            cols = [s[:, c * LANES:(c + 1) * LANES] for c in range(nt * tk // LANES)]
            m_new = jnp.max(functools.reduce(jnp.maximum, cols), axis=-1, keepdims=True)
            if state:
                m_prev = m_scr[g]
                m_new = jnp.maximum(m_prev, m_new)
                alphas.append(jnp.exp(m_prev - m_new))
                m_scr[g] = m_new
            ps.append(jnp.concatenate([jnp.exp(c - m_new) for c in cols], axis=-1).astype(BF16))
        if not state:
            return [_dot(ps[g], v) for g in range(G)]
        for g in range(G):
            acc_scr[g] = alphas[g] * acc_scr[g] + _dot(ps[g], v)

    def normalise(acc):
        return acc[:, :HEAD_DIM] / acc[:, HEAD_DIM:HEAD_DIM + 1]

    m_scr[...] = jnp.full((G, tq, LANES), MASK_NEG, F32)
    acc_scr[...] = jnp.zeros((G, tq, LANES), F32)
    causal = lambda qpos, kpos: kpos <= qpos

    def sel_body(c, carry):
        attend(ks_ref, vs_ref, c * SEL_CHUNK, SEL_CHUNK, None, True)
        return carry

    n_full = i // SEL_CHUNK
    lax.fori_loop(0, n_full, sel_body, 0)
    attend(ks_ref, vs_ref, n_full * SEL_CHUNK, 2, causal, True)

    @pl.when(i % SEL_CHUNK >= 2)
    def _():
        attend(ks_ref, vs_ref, n_full * SEL_CHUNK + 2, 2, causal, True)

    o_sel = normalise(acc_scr[...].reshape(R, LANES))

    assert WINDOW == 2 * tk
    band = lambda qpos, kpos: (kpos <= qpos) & (kpos > qpos - WINDOW)
    o_win = normalise(jnp.concatenate(attend(kw_ref, vw_ref, jnp.maximum(i - 2, 0), 3, band, False), axis=0))

    o_cmp = oc_ref[0, 0].reshape(R, HEAD_DIM)
    gates = g_ref[0, 0]
    outs = []
    for g in range(G):
        r = slice(g * tq, (g + 1) * tq)
        outs.append(gates[:, 3 * g:3 * g + 1] * o_cmp[r]
                    + gates[:, 3 * g + 1:3 * g + 2] * o_sel[r]
                    + gates[:, 3 * g + 2:3 * g + 3] * o_win[r])
    o_ref[0] = jnp.concatenate(outs, axis=-1).astype(BF16)


def _nsa_flash_call(qa, ks, vs, kw, vw, oc, gates):
    B, H, G, S, _ = qa.shape
    tq = WINDOW // 2
    assert (S // tq) % SEL_CHUNK == 0
    kv = pl.BlockSpec((1, 1, S, LANES), lambda b, h, i: (b, h, 0, 0))
    return pl.pallas_call(
        _nsa_flash_kernel,
        grid=(B, H, S // tq),
        in_specs=[pl.BlockSpec((1, 1, G, tq, LANES), lambda b, h, i: (b, h, 0, i, 0)),
                  kv, kv, kv, kv,
                  pl.BlockSpec((1, 1, G, tq, HEAD_DIM), lambda b, h, i: (b, h, 0, i, 0)),
                  pl.BlockSpec((1, 1, tq, LANES), lambda b, h, i: (b, h, i, 0))],
        out_specs=pl.BlockSpec((1, tq, G * HEAD_DIM), lambda b, h, i: (b, i, h)),
        out_shape=jax.ShapeDtypeStruct((B, S, Q_WIDTH), BF16),
        scratch_shapes=[pltpu.VMEM((G, tq, LANES), F32), pltpu.VMEM((G, tq, LANES), F32)],
        compiler_params=_cparams("arbitrary", "arbitrary", "arbitrary"),
        name="nsa_flash",
    )(qa, ks, vs, kw, vw, oc, gates)


def _gmlp_kernel(u_ref, v_ref, ws_ref, bs_ref, o_ref):
    tm = u_ref.shape[1]
    C = GM_CHUNK
    r = lax.broadcasted_iota(jnp.int32, (C, C), 0)
    c = lax.broadcasted_iota(jnp.int32, (C, C), 1)
    causal = c <= r
    ws = [jnp.where(causal, ws_ref[g], 0.0).astype(BF16) for g in range(GM_GROUPS)]
    for n in range(tm // C):
        rows = slice(n * C, (n + 1) * C)
        vn = v_ref[0, rows, :]
        mixed = jnp.concatenate(
            [_dot(ws[g], vn[:, g * GM_GROUP_DIM:(g + 1) * GM_GROUP_DIM]) for g in range(GM_GROUPS)],
            axis=-1)
        o_ref[0, rows, :] = (u_ref[0, rows, :].astype(F32) * (mixed + bs_ref[...])).astype(BF16)


def _gmlp_call(u, v, ws, bs_full, tm):
    B, S, W = u.shape
    tok = pl.BlockSpec((1, tm, W), lambda b, i: (b, i, 0))
    return pl.pallas_call(
        _gmlp_kernel,
        grid=(B, S // tm),
        in_specs=[tok, tok,
                  pl.BlockSpec((GM_GROUPS, GM_CHUNK, GM_CHUNK), lambda b, i: (0, 0, 0)),
                  pl.BlockSpec((GM_CHUNK, W), lambda b, i: (0, 0))],
        out_specs=tok,
        out_shape=jax.ShapeDtypeStruct((B, S, W), BF16),
        compiler_params=_cparams("arbitrary", "arbitrary"),
        name="gmlp",
    )(u, v, ws, bs_full)


def _mixout_kernel(on_ref, og_ref, ga_ref, gb_ref, x_ref, gt_ref, sc_ref, sh_ref,
                   wpn_ref, wpg_ref, wo_ref, wr_ref, wgs_ref, wus_ref, wds_ref,
                   x1_ref, h2_ref, lg_ref, shd_ref):
    ya = _dot(on_ref[0], wpn_ref[...])
    yb = _dot(og_ref[0], wpg_ref[...])
    merged = ga_ref[0].astype(F32) * ya + gb_ref[0].astype(F32) * yb
    x1 = x_ref[0] + gt_ref[0] * _dot(merged.astype(BF16), wo_ref[...])
    x1_ref[0] = x1
    ms = jnp.mean(x1 * x1, axis=-1, keepdims=True)
    h2 = (x1 * lax.rsqrt(ms + EPS)) * (1.0 + sc_ref[0]) + sh_ref[0]
    lg_ref[0] = _dot(h2, wr_ref[...], precision=HI)
    hb = h2.astype(BF16)
    h2_ref[0] = _pack_halves(hb)
    act = _silu(_dot(hb, wgs_ref[...])) * _dot(hb, wus_ref[...])
    shd_ref[0] = _dot(act.astype(BF16), wds_ref[...])


def _mixout_call(o_nsa, o_gm, ga, gb, x, gt, sc, sh, wpn, wpg, wo, wr, wgs, wus, wds, tm):
    B, S, D = x.shape
    tok = lambda w: pl.BlockSpec((1, tm, w), lambda b, i: (b, i, 0))
    per_b = pl.BlockSpec((1, 1, D), lambda b, i: (b, 0, 0))
    full = lambda a: pl.BlockSpec(a.shape, lambda b, i: (0,) * a.ndim)
    sds = jax.ShapeDtypeStruct
    return pl.pallas_call(
        _mixout_kernel,
        grid=(B, S // tm),
        in_specs=[tok(Q_WIDTH), tok(GM_WIDTH), tok(D), tok(D), tok(D), per_b, per_b, per_b,
                  full(wpn), full(wpg), full(wo), full(wr), full(wgs), full(wus), full(wds)],
        out_specs=[tok(D), tok(D // 2), tok(N_EXPERTS), tok(D)],
        out_shape=[sds((B, S, D), F32), sds((B, S, D // 2), jnp.uint32), sds((B, S, N_EXPERTS), F32),
                   sds((B, S, D), F32)],
        compiler_params=_cparams("arbitrary", "arbitrary"),
        name="mix_out",
    )(o_nsa, o_gm, ga, gb, x, gt, sc, sh, wpn, wpg, wo, wr, wgs, wus, wds)


def _route_kernel(lg_ref, br_ref, idx_ref, wt_ref, rank_ref, cnt_ref, run_scr):
    tm, E = lg_ref.shape

    @pl.when(pl.program_id(0) == 0)
    def _():
        run_scr[...] = jnp.zeros_like(run_scr)

    aff = _sigmoid(lg_ref[...])
    work = aff + br_ref[...]
    lane = lax.broadcasted_iota(jnp.int32, (tm, E), 1).astype(F32)
    picked = jnp.zeros((tm, E), F32)
    idxs, tops = [], []
    for _ in range(TOP_K):
        m = jnp.max(work, axis=-1, keepdims=True)
        idx = jnp.min(jnp.where(work == m, lane, float(E)), axis=-1, keepdims=True)
        hit = lane == idx
        tops.append(jnp.sum(jnp.where(hit, aff, 0.0), axis=-1, keepdims=True))
        idxs.append(idx)
        picked = jnp.where(hit, 1.0, picked)
        work = jnp.where(hit, -jnp.inf, work)
    total = tops[0]
    for t in tops[1:]:
        total = total + t
    r = lax.broadcasted_iota(jnp.int32, (tm, tm), 0)
    c = lax.broadcasted_iota(jnp.int32, (tm, tm), 1)
    before = _dot(jnp.where(c < r, 1.0, 0.0).astype(BF16), picked.astype(BF16)) + run_scr[...]
    ranks = [jnp.sum(jnp.where(lane == idx, before, 0.0), axis=-1, keepdims=True) for idx in idxs]
    run_scr[...] = run_scr[...] + jnp.sum(picked, axis=0, keepdims=True)
    cnt_ref[...] = run_scr[...]
    lane_k = lax.broadcasted_iota(jnp.int32, (tm, LANES), 1)
    idx_o = jnp.zeros((tm, LANES), jnp.int32)
    wt_o = jnp.zeros((tm, LANES), F32)
    rank_o = jnp.zeros((tm, LANES), jnp.int32)
    for k in range(TOP_K):
        idx_o = jnp.where(lane_k == k, idxs[k].astype(jnp.int32), idx_o)
        wt_o = jnp.where(lane_k == k, tops[k] / total * ROUTE_SCALE, wt_o)
        rank_o = jnp.where(lane_k == k, ranks[k].astype(jnp.int32), rank_o)
    idx_ref[...] = idx_o
    wt_ref[...] = wt_o
    rank_ref[...] = rank_o


def _route_call(logits, b_router, tm):
    T, E = logits.shape
    tok = lambda w: pl.BlockSpec((tm, w), lambda i: (i, 0))
    one = pl.BlockSpec((1, E), lambda i: (0, 0))
    sds = jax.ShapeDtypeStruct
    return pl.pallas_call(
        _route_kernel,
        grid=(T // tm,),
        in_specs=[tok(E), one],
        out_specs=[tok(LANES), tok(LANES), tok(LANES), one],
        out_shape=[sds((T, LANES), jnp.int32), sds((T, LANES), F32), sds((T, LANES), jnp.int32),
                   sds((1, E), F32)],
        scratch_shapes=[pltpu.VMEM((1, E), F32)],
        compiler_params=_cparams("arbitrary"),
        name="route",
    )(logits, b_router.reshape(1, E))


def _slot_kernel(idx_ref, rank_ref, ps_ref, o_ref):
    tm = idx_ref.shape[0]
    E = ps_ref.shape[1]
    lane = lax.broadcasted_iota(jnp.int32, (tm, E), 1)
    lane_k = lax.broadcasted_iota(jnp.int32, (tm, LANES), 1)
    idx, rank, ps = idx_ref[...], rank_ref[...], ps_ref[...]
    out = jnp.zeros((tm, LANES), jnp.int32)
    for k in range(TOP_K):
        base = jnp.sum(jnp.where(lane == idx[:, k:k + 1], ps, 0.0), axis=-1, keepdims=True)
        out = jnp.where(lane_k == k, base.astype(jnp.int32) + rank[:, k:k + 1], out)
    o_ref[...] = out[:, :TOP_K]


def _slot_call(idx, rank, pad_start, tm):
    T = idx.shape[0]
    E = pad_start.shape[0]
    tok = pl.BlockSpec((tm, LANES), lambda i: (i, 0))
    return pl.pallas_call(
        _slot_kernel,
        grid=(T // tm,),
        in_specs=[tok, tok, pl.BlockSpec((1, E), lambda i: (0, 0))],
        out_specs=pl.BlockSpec((tm, TOP_K), lambda i: (i, 0)),
        out_shape=jax.ShapeDtypeStruct((T, TOP_K), jnp.int32),
        compiler_params=_cparams("arbitrary"),
        name="slot",
    )(idx, rank, pad_start.astype(F32).reshape(1, E))


def _dispatch_kernel(bv_ref, slot_ref, h_ref, xs_ref, zbuf, sem, zsem):
    tm = h_ref.shape[0]
    rows = zbuf.shape[0]

    @pl.when(pl.program_id(0) == 0)
    def _():
        zbuf[...] = jnp.zeros_like(zbuf)

        def zero_copy(i):
            return pltpu.make_async_copy(zbuf, xs_ref.at[pl.ds(pl.multiple_of(i * rows, rows), rows), :], zsem)

        def start(i, carry):
            @pl.when(bv_ref[i] < rows)
            def _():
                zero_copy(i).start()
            return carry

        def wait(i, carry):
            @pl.when(bv_ref[i] < rows)
            def _():
                zero_copy(i).wait()
            return carry

        lax.fori_loop(0, bv_ref.shape[0], start, 0)
        lax.fori_loop(0, bv_ref.shape[0], wait, 0)

    def body(r, carry):
        for k in range(TOP_K):
            d = slot_ref[r * TOP_K + k]
            pltpu.make_async_copy(h_ref.at[pl.ds(r, 1), :], xs_ref.at[pl.ds(d, 1), :], sem).start(priority=k % 2)
        return carry

    lax.fori_loop(0, tm, body, 0)
    for k in range(TOP_K):
        pltpu.make_async_copy(h_ref, xs_ref.at[pl.ds(0, tm), :], sem).wait()


def _dispatch_call(blk_valid, slots, h2p, n_rows, tm):
    T, W = h2p.shape
    grid_spec = pltpu.PrefetchScalarGridSpec(
        num_scalar_prefetch=1,
        grid=(T // tm,),
        in_specs=[pl.BlockSpec((tm * TOP_K,), lambda i, bv: (i,), memory_space=pltpu.SMEM),
                  pl.BlockSpec((tm, W), lambda i, bv: (i, 0))],
        out_specs=pl.BlockSpec(memory_space=pl.ANY),
        scratch_shapes=[pltpu.VMEM((MOE_GRAN, W), jnp.uint32), pltpu.SemaphoreType.DMA, pltpu.SemaphoreType.DMA],
    )
    return pl.pallas_call(
        _dispatch_kernel,
        grid_spec=grid_spec,
        out_shape=jax.ShapeDtypeStruct((n_rows, W), jnp.uint32),
        compiler_params=pltpu.CompilerParams(dimension_semantics=("arbitrary",), vmem_limit_bytes=VMEM_LIMIT,
                                             has_side_effects=True),
        name="dispatch",
    )(blk_valid, slots, h2p)


def _expert_kernel(ie_ref, ir_ref, ig_ref, n_ref, x_hbm, wg_ref, wu_ref, wd_ref, y_hbm,
                   xbuf, ybuf, act_scr, xsem, ysem):
    w = pl.program_id(0)
    n = n_ref[0]
    R = xbuf.shape[1]
    last = ie_ref.shape[0] - 1

    def x_copy(item, slot):
        row = pl.multiple_of(ir_ref[jnp.minimum(item, last)], MOE_GRAN)
        return pltpu.make_async_copy(x_hbm.at[pl.ds(row, R), :], xbuf.at[slot], xsem.at[slot])

    def y_copy(item, slot, gi):
        row = pl.multiple_of(ir_ref[jnp.minimum(item, last)] + gi * MOE_GRAN, MOE_GRAN)
        src = ybuf.at[slot, pl.ds(pl.multiple_of(gi * MOE_GRAN, MOE_GRAN), MOE_GRAN), :]
        return pltpu.make_async_copy(src, y_hbm.at[pl.ds(row, MOE_GRAN), :], ysem.at[slot])

    def for_granules(item, fn):
        def body(gi, carry):
            fn(gi)
            return carry
        lax.fori_loop(0, ig_ref[jnp.clip(item, 0, last)], body, 0)

    def gate_up(slot):
        lo, hi = _unpack_halves(xbuf[slot])
        x = jnp.concatenate([lo, hi], axis=1).astype(BF16)
        a = _dot(x, wg_ref[0, 0].astype(BF16))
        b = _dot(x, wu_ref[0, 0].astype(BF16))
        return (_silu(a) * b).astype(BF16)

    def down(act):
        return _pack_halves(_dot(act, wd_ref[0, 0].astype(BF16)))

    cur, prv = w % 2, (w + 1) % 2

    @pl.when(w == 0)
    def _():
        x_copy(0, 0).start()

    @pl.when(w < n)
    def _():
        x_copy(w, cur).wait()

    @pl.when(w + 1 < n)
    def _():
        x_copy(w + 1, prv).start()

    @pl.when((w >= 3) & (w - 3 < n))
    def _():
        for_granules(w - 3, lambda gi: y_copy(w - 3, prv, gi).wait())

    @pl.when(w == 0)
    def _():
        act_scr[...] = gate_up(0)

    @pl.when((w >= 1) & (w < n))
    def _():
        prev = act_scr[...]
        ybuf[prv] = down(prev)
        act_scr[...] = gate_up(cur)

    @pl.when((w >= 1) & (w == n))
    def _():
        ybuf[prv] = down(act_scr[...])

    @pl.when((w >= 1) & (w <= n))
    def _():
        for_granules(w - 1, lambda gi: y_copy(w - 1, prv, gi).start())


def _expert_call(layer, item_e, item_row, item_ng, n_items, xs, wg, wu, wd):
    n_alloc, W = xs.shape
    R = MOE_ROWS
    nw = item_e.shape[0]
    D, F = wg.shape[2], wg.shape[3]
    cur = lambda w, ie, ir, ig, n: (layer, ie[jnp.minimum(w, nw - 1)], 0, 0)
    prev = lambda w, ie, ir, ig, n: (layer, ie[jnp.clip(w - 1, 0, nw - 1)], 0, 0)
    grid_spec = pltpu.PrefetchScalarGridSpec(
        num_scalar_prefetch=4,
        grid=(nw + 3,),
        in_specs=[pl.BlockSpec(memory_space=pl.ANY),
                  pl.BlockSpec((1, 1, D, F), cur), pl.BlockSpec((1, 1, D, F), cur),
                  pl.BlockSpec((1, 1, F, D), prev)],
        out_specs=pl.BlockSpec(memory_space=pl.ANY),
        scratch_shapes=[pltpu.VMEM((2, R, W), jnp.uint32), pltpu.VMEM((2, R, W), jnp.uint32),
                        pltpu.VMEM((R, F), BF16),
                        pltpu.SemaphoreType.DMA((2,)), pltpu.SemaphoreType.DMA((2,))],
    )
    return pl.pallas_call(
        _expert_kernel,
        grid_spec=grid_spec,
        out_shape=jax.ShapeDtypeStruct((n_alloc, W), jnp.uint32),
        input_output_aliases={4: 0},
        compiler_params=pltpu.CompilerParams(dimension_semantics=("arbitrary",), vmem_limit_bytes=VMEM_LIMIT,
                                             has_side_effects=True),
        name="expert",
    )(item_e, item_row, item_ng, n_items, xs, wg, wu, wd)


def _combine_kernel(slot_ref, wt_ref, shd_ref, x_ref, gt_ref, y_ref, o_ref, ybuf, sem):
    tm = x_ref.shape[0]

    def body(r, carry):
        for k in range(TOP_K):
            d = slot_ref[r * TOP_K + k]
            pltpu.make_async_copy(y_ref.at[pl.ds(d, 1), :], ybuf.at[k, pl.ds(r, 1), :], sem).start(priority=k % 2)
        return carry

    lax.fori_loop(0, tm, body, 0)
    for k in range(TOP_K):
        pltpu.make_async_copy(y_ref.at[pl.ds(0, tm), :], ybuf.at[k], sem).wait()
    wt = wt_ref[...]
    acc_lo = acc_hi = None
    for k in range(TOP_K):
        lo, hi = _unpack_halves(ybuf[k])
        w = wt[:, k:k + 1]
        acc_lo = w * lo if k == 0 else acc_lo + w * lo
        acc_hi = w * hi if k == 0 else acc_hi + w * hi
    routed = jnp.concatenate([acc_lo, acc_hi], axis=1)
    o_ref[...] = x_ref[...] + gt_ref[0] * (routed + shd_ref[...])


def _combine_call(slots, wts, shared, x1, gt, y, tm):
    T, D = x1.shape
    B = gt.shape[0]
    per_b = T // B // tm
    tok = lambda w: pl.BlockSpec((tm, w), lambda i: (i, 0))
    return pl.pallas_call(
        _combine_kernel,
        grid=(T // tm,),
        in_specs=[pl.BlockSpec((tm * TOP_K,), lambda i: (i,), memory_space=pltpu.SMEM),
                  tok(LANES), tok(D), tok(D),
                  pl.BlockSpec((1, 1, D), lambda i: (i // per_b, 0, 0)),
                  pl.BlockSpec(memory_space=pl.ANY)],
        out_specs=tok(D),
        out_shape=jax.ShapeDtypeStruct((T, D), F32),
        scratch_shapes=[pltpu.VMEM((TOP_K, tm, y.shape[1]), jnp.uint32), pltpu.SemaphoreType.DMA],
        compiler_params=_cparams("arbitrary"),
        name="combine",
    )(slots, wts, shared, x1, gt, y)


def _rope_tables(positions):
    half = HEAD_DIM // 2
    inv = ROPE_THETA ** (-jnp.arange(half, dtype=F32) / half)
    ang = positions.astype(F32)[..., None] * inv
    cos, sin, zero = jnp.cos(ang), jnp.sin(ang), jnp.zeros_like(ang)
    cos_t = jnp.concatenate([cos, cos] * 2, axis=-1)
    sin_lo = jnp.concatenate([-sin, zero] * 2, axis=-1)
    sin_hi = jnp.concatenate([zero, sin] * 2, axis=-1)
    return cos_t, sin_lo, sin_hi


def _reorder_w_in(w):
    o = np.cumsum([0, Q_WIDTH] + [KV_WIDTH] * 6 + [NSA_Q_HEADS * 3, GM_WIDTH, GM_WIDTH, D_MODEL, D_MODEL])
    q, kc, vc, ks, vs, kw, vw, g, u, v, ga, gb = [w[:, o[i]:o[i + 1]] for i in range(12)]
    per = NSA_GROUP * 3
    pad = jnp.zeros((w.shape[0], LANES - per), w.dtype)
    return jnp.concatenate([q, ks, kw, kc, vc, vs, vw, g[:, :per], pad, g[:, per:], pad, u, v, ga, gb], axis=1)


def _owner(ends, pos):
    return jnp.minimum(jnp.sum((ends[None, :] <= pos[:, None]).astype(jnp.int32), axis=1), ends.shape[0] - 1)


def _lookup(table, idx):
    hit = idx[:, None] == jnp.arange(table.shape[0], dtype=jnp.int32)[None, :]
    return jnp.sum(jnp.where(hit, table[None, :], 0), axis=1)


def _expert_plan(counts, n_alloc, n_items_max):
    per_item = MOE_ROWS // MOE_GRAN
    counts = counts.astype(jnp.int32)
    gran = (counts + MOE_GRAN - 1) // MOE_GRAN
    gran_end = jnp.cumsum(gran)
    gran_start = gran_end - gran
    g = jnp.arange(n_alloc // MOE_GRAN, dtype=jnp.int32)
    ge = _owner(gran_end, g)
    gran_valid = jnp.clip(_lookup(counts, ge) - (g - _lookup(gran_start, ge)) * MOE_GRAN, 0, MOE_GRAN)
    gran_valid = jnp.where(g < gran_end[-1], gran_valid, 0).astype(jnp.int32)
    items = (gran + per_item - 1) // per_item
    item_end = jnp.cumsum(items)
    w = jnp.arange(n_items_max, dtype=jnp.int32)
    ie = _owner(item_end, w)
    part = w - (_lookup(item_end, ie) - _lookup(items, ie))
    live = w < item_end[-1]
    item_ng = jnp.where(live, jnp.clip(_lookup(gran, ie) - part * per_item, 0, per_item), 0).astype(jnp.int32)
    item_row = jnp.where(live, (_lookup(gran_start, ie) + part * per_item) * MOE_GRAN, 0).astype(jnp.int32)
    return gran_start * MOE_GRAN, gran_valid, ie.astype(jnp.int32), item_row, item_ng, item_end[-1:].astype(jnp.int32)


def kernel(x, c, positions, w_mod, b_mod, w_in, q_gain, k_gain, cmp_pos_k, cmp_pos_v, cmp_w1_k, cmp_w2_k, cmp_w1_v, cmp_w2_v, gm_ln_g, gm_ln_b, gm_ws, gm_bs, w_proj_nsa, w_proj_gm, w_out, w_router, b_router, w_gate_e, w_up_e, w_down_e, w_gate_sh, w_up_sh, w_down_sh):
    B, S, D = x.shape
    L = w_mod.shape[0]
    T = B * S
    tm = 256
    scale = HEAD_DIM ** -0.5
    cos_t, sin_lo, sin_hi = _rope_tables(positions)
    mod = _mod_call(c, w_mod, b_mod)
    n_alloc = T * TOP_K + N_EXPERTS * MOE_GRAN + MOE_ROWS
    n_items_max = N_EXPERTS + T * TOP_K // MOE_ROWS + 1

    for l in range(L):
        sh_a, sc_a, gt_a, sh_f, sc_f, gt_f = [mod[l, :, i * D:(i + 1) * D].reshape(B, 1, D) for i in range(6)]
        qg = (jnp.tile(q_gain[l], NSA_Q_HEADS) * scale).reshape(1, Q_WIDTH)
        kg = jnp.tile(k_gain[l], 2 * NSA_KV_HEADS).reshape(1, 2 * KV_WIDTH)
        (qn, qr, ks, kw, vs, vw, kc_raw, vc_raw, gates, u, v, ga, gb) = _inproj_call(
            x, sc_a, sh_a, _reorder_w_in(w_in[l]).astype(BF16), qg, kg, cos_t, sin_lo, sin_hi,
            gm_ln_g[l].reshape(1, GM_WIDTH), gm_ln_b[l].reshape(1, GM_WIDTH), tm)
        kc, vc = _compress_call(
            kc_raw, vc_raw, cmp_w1_k[l].astype(BF16), cmp_w2_k[l].astype(BF16), cmp_pos_k[l].reshape(1, -1),
            cmp_w1_v[l].astype(BF16), cmp_w2_v[l].astype(BF16), cmp_pos_v[l].reshape(1, -1),
            k_gain[l].reshape(1, HEAD_DIM))
        o_cmp, q_aug = _nsa_cmp_call(qn, qr, kc, vc, WINDOW // 2)
        o_nsa = _nsa_flash_call(q_aug, ks, vs, kw, vw, o_cmp, gates)
        bs_full = jnp.repeat(gm_bs[l].T, GM_GROUP_DIM, axis=1)
        o_gm = _gmlp_call(u, v, gm_ws[l], bs_full, 512)
        x1, h2, logits, shared = _mixout_call(
            o_nsa, o_gm, ga, gb, x, gt_a, sc_f, sh_f,
            w_proj_nsa[l].astype(BF16), w_proj_gm[l].astype(BF16), w_out[l].astype(BF16), w_router[l],
            w_gate_sh[l].astype(BF16), w_up_sh[l].astype(BF16), w_down_sh[l].astype(BF16), tm)
        idx, wts, rank, counts = _route_call(logits.reshape(T, N_EXPERTS), b_router[l], tm)
        row_start, gran_valid, item_e, item_row, item_ng, n_items = _expert_plan(counts[0], n_alloc, n_items_max)
        slots = _slot_call(idx, rank, row_start, tm).reshape(T * TOP_K)
        xs = _dispatch_call(gran_valid, slots, h2.reshape(T, D // 2), n_alloc, tm)
        y = _expert_call(l, item_e, item_row, item_ng, n_items, xs, w_gate_e, w_up_e, w_down_e)
        x = _combine_call(slots, wts, shared.reshape(T, D), x1.reshape(T, D), gt_f, y, tm).reshape(B, S, D)
    return x
```

```python
import functools

import jax
import jax.numpy as jnp
import numpy as np
from jax import lax
from jax.experimental import pallas as pl
from jax.experimental.pallas import tpu as pltpu

D_MODEL = 1024
NSA_Q_HEADS = 8
NSA_KV_HEADS = 2
HEAD_DIM = 64
NSA_GROUP = NSA_Q_HEADS // NSA_KV_HEADS
CMP_LEN = 32
CMP_STRIDE = 16
CMP_HIDDEN = 256
SEL_LEN = 64
SEL_TOPN = 16
WINDOW = 512
ROPE_THETA = 10000.0
Q_WIDTH = NSA_Q_HEADS * HEAD_DIM
KV_WIDTH = NSA_KV_HEADS * HEAD_DIM
GM_GROUPS = 8
GM_GROUP_DIM = 64
GM_WIDTH = GM_GROUPS * GM_GROUP_DIM
GM_CHUNK = 128
N_EXPERTS = 256
TOP_K = 8
D_EXPERT = 256
D_SHARED = 256
ROUTE_SCALE = 2.5
EPS = 1e-6

LANES = 128
SEL_BIAS_WIDTH = 64
MASK_NEG = -1e30
SEL_NEG = -30000.0
MOE_GRAN = 128
MOE_ROWS = 9 * MOE_GRAN
VMEM_LIMIT = 56 * 1024 * 1024

F32 = jnp.float32
BF16 = jnp.bfloat16
HI = lax.Precision.HIGHEST


def _cparams(*sem):
    return pltpu.CompilerParams(dimension_semantics=sem, vmem_limit_bytes=VMEM_LIMIT)


def _dot(a, b, **kw):
    return jnp.dot(a, b, preferred_element_type=F32, **kw)


def _dot_nt(a, b, **kw):
    return lax.dot_general(a, b, (((1,), (1,)), ((), ())), preferred_element_type=F32, **kw)


def _gelu(x):
    return 0.5 * x * (1.0 + jnp.tanh(0.7978845608028654 * (x + 0.044715 * (x * x * x))))


def _sigmoid(x):
    return 1.0 / (1.0 + jnp.exp(-x))


def _silu(x):
    return x * _sigmoid(x)


_HI_MASK = np.uint32(0xFFFF0000)


def _pack_halves(a):
    w = a.shape[1] // 2
    bits = lax.bitcast_convert_type(a.astype(BF16).astype(F32), jnp.uint32)
    return (bits[:, w:] & _HI_MASK) | (bits[:, :w] >> 16)


def _unpack_halves(words):
    lo = lax.bitcast_convert_type(words << 16, F32)
    hi = lax.bitcast_convert_type(words & _HI_MASK, F32)
    return lo, hi


def _mod_kernel(c_ref, w_ref, b_ref, o_ref):
    c = c_ref[...]
    o_ref[0] = _dot(_silu(c), w_ref[0], precision=HI) + b_ref[0]


def _mod_call(c, w_mod, b_mod):
    L, D, N = w_mod.shape
    B = c.shape[0]
    tn = 1536
    return pl.pallas_call(
        _mod_kernel,
        grid=(L, N // tn),
        in_specs=[pl.BlockSpec((B, D), lambda l, j: (0, 0)),
                  pl.BlockSpec((1, D, tn), lambda l, j: (l, 0, j)),
                  pl.BlockSpec((1, 1, tn), lambda l, j: (l, 0, j))],
        out_specs=pl.BlockSpec((1, B, tn), lambda l, j: (l, 0, j)),
        out_shape=jax.ShapeDtypeStruct((L, B, N), F32),
        compiler_params=_cparams("arbitrary", "arbitrary"),
        name="mod",
    )(c, w_mod, b_mod.reshape(L, 1, N))


_C_Q = 0
_C_K = 512
_C_KC = 768
_C_VC = 896
_C_VS = 1024
_C_VW = 1152
_C_G = 1280
_C_U = 1536
_C_V = 2048
_C_GA = 2560
_C_GB = 3584
IN_COLS_P = 4608


def _dot_split(a, b):
    hi = a.astype(BF16)
    lo = (a - hi.astype(F32)).astype(BF16)
    return _dot(hi, b) + _dot(lo, b)


def _head_norm(z, bd):
    ms = _dot_split(z * z, bd)
    return z * lax.rsqrt(ms + EPS)


def _rope(z, cos, sin_lo, sin_hi):
    w = z.shape[-1]
    half = HEAD_DIM // 2
    return z * cos + pltpu.roll(z, w - half, 1) * sin_lo + pltpu.roll(z, half, 1) * sin_hi


def _tile_lanes(t, n):
    return t if n == 1 else jnp.concatenate([t] * n, axis=-1)


def _inproj_kernel(x_ref, sc_ref, sh_ref, w_ref, bdq_ref, bdk_ref, qg_ref, kg_ref,
                   cos_ref, sl_ref, shi_ref, lng_ref, lnb_ref,
                   qn_ref, qr_ref, ks_ref, kw_ref, vs_ref, vw_ref, kc_ref, vc_ref,
                   g_ref, u_ref, v_ref, ga_ref, gb_ref):
    tm = x_ref.shape[1]
    x = x_ref[0]
    ms = jnp.mean(x * x, axis=-1, keepdims=True)
    h = (x * lax.rsqrt(ms + EPS)) * (1.0 + sc_ref[0]) + sh_ref[0]
    hb = h.astype(BF16)

    def mm(lo, width):
        return _dot(hb, w_ref[:, lo:lo + width])

    cos, sl, shi = cos_ref[0], sl_ref[0], shi_ref[0]

    zq = mm(_C_Q, Q_WIDTH)
    qn = _head_norm(zq, bdq_ref[...]) * qg_ref[...]
    qr = _rope(qn, _tile_lanes(cos, 4), _tile_lanes(sl, 4), _tile_lanes(shi, 4))
    qn_ref[0] = qn.astype(BF16)
    qr_ref[0] = qr.astype(BF16)

    zk = mm(_C_K, 2 * KV_WIDTH)
    kn = _head_norm(zk, bdk_ref[...]) * kg_ref[...]
    kr = _rope(kn, _tile_lanes(cos, 2), _tile_lanes(sl, 2), _tile_lanes(shi, 2))
    lane = lax.broadcasted_iota(jnp.int32, (tm, LANES), 1)
    tok = pl.program_id(1) * tm + lax.broadcasted_iota(jnp.int32, (tm, LANES), 0)
    onehot = jnp.where(lane - HEAD_DIM == tok // SEL_LEN, 1.0, 0.0)
    ones_col = jnp.where(lane == HEAD_DIM, 1.0, 0.0)
    low = lane < HEAD_DIM
    zvs = mm(_C_VS, KV_WIDTH)
    zvw = mm(_C_VW, KV_WIDTH)
    zkc = mm(_C_KC, KV_WIDTH)
    zvc = mm(_C_VC, KV_WIDTH)
    for kv in range(NSA_KV_HEADS):
        def head(a):
            return a if kv == 0 else pltpu.roll(a, HEAD_DIM, 1)
        ks_ref[0, kv] = jnp.where(low, head(kr[:, :KV_WIDTH]), onehot).astype(BF16)
        kw_ref[0, kv] = jnp.where(low, head(kr[:, KV_WIDTH:]), 0.0).astype(BF16)
        vs_ref[0, kv] = jnp.where(low, head(zvs), ones_col).astype(BF16)
        vw_ref[0, kv] = jnp.where(low, head(zvw), ones_col).astype(BF16)
        kc_ref[0, kv] = head(zkc)[:, :HEAD_DIM]
        vc_ref[0, kv] = head(zvc)[:, :HEAD_DIM]

    zg = mm(_C_G, 2 * LANES)
    sg = _sigmoid(zg)
    g_ref[0, 0] = sg[:, :LANES]
    g_ref[0, 1] = sg[:, LANES:]

    u_ref[0] = _gelu(mm(_C_U, GM_WIDTH)).astype(BF16)
    gv = _gelu(mm(_C_V, GM_WIDTH))
    mu = jnp.mean(gv, axis=-1, keepdims=True)
    cen = gv - mu
    var = jnp.mean(cen * cen, axis=-1, keepdims=True)
    v_ref[0] = ((cen * lax.rsqrt(var + EPS)) * lng_ref[...] + lnb_ref[...]).astype(BF16)

    ga_ref[0] = _sigmoid(mm(_C_GA, D_MODEL)).astype(BF16)
    gb_ref[0] = _sigmoid(mm(_C_GB, D_MODEL)).astype(BF16)


def _block_diag_mean(width):
    idx = np.arange(width) // HEAD_DIM
    return jnp.asarray((idx[:, None] == idx[None, :]).astype(np.float32) / HEAD_DIM).astype(BF16)


def _inproj_call(x, sc, sh, w_p, qg, kg, cos, sl, shi, lng, lnb, tm):
    B, S, D = x.shape
    H = NSA_KV_HEADS
    full = lambda *shape: pl.BlockSpec(shape, lambda b, i: (0,) * len(shape))
    tok3 = lambda w: pl.BlockSpec((1, tm, w), lambda b, i: (b, i, 0))
    per_b = pl.BlockSpec((1, 1, D), lambda b, i: (b, 0, 0))
    kv4 = lambda w: pl.BlockSpec((1, H, tm, w), lambda b, i: (b, 0, i, 0))
    sds = jax.ShapeDtypeStruct
    out_shape = [
        sds((B, S, Q_WIDTH), BF16), sds((B, S, Q_WIDTH), BF16),
        sds((B, H, S, LANES), BF16), sds((B, H, S, LANES), BF16),
        sds((B, H, S, LANES), BF16), sds((B, H, S, LANES), BF16),
        sds((B, H, S, HEAD_DIM), F32), sds((B, H, S, HEAD_DIM), F32),
        sds((B, H, S, LANES), F32),
        sds((B, S, GM_WIDTH), BF16), sds((B, S, GM_WIDTH), BF16),
        sds((B, S, D), BF16), sds((B, S, D), BF16),
    ]
    out_specs = [
        tok3(Q_WIDTH), tok3(Q_WIDTH), kv4(LANES), kv4(LANES), kv4(LANES), kv4(LANES),
        kv4(HEAD_DIM), kv4(HEAD_DIM), kv4(LANES),
        tok3(GM_WIDTH), tok3(GM_WIDTH), tok3(D), tok3(D),
    ]
    return pl.pallas_call(
        _inproj_kernel,
        grid=(B, S // tm),
        in_specs=[tok3(D), per_b, per_b, full(D, IN_COLS_P),
                  full(Q_WIDTH, Q_WIDTH), full(2 * KV_WIDTH, 2 * KV_WIDTH),
                  full(1, Q_WIDTH), full(1, 2 * KV_WIDTH),
                  tok3(LANES), tok3(LANES), tok3(LANES),
                  full(1, GM_WIDTH), full(1, GM_WIDTH)],
        out_specs=out_specs,
        out_shape=out_shape,
        compiler_params=_cparams("arbitrary", "arbitrary"),
        name="in_proj",
    )(x, sc, sh, w_p, _block_diag_mean(Q_WIDTH), _block_diag_mean(2 * KV_WIDTH), qg, kg,
      cos, sl, shi, lng, lnb)


def _compress_kernel(kr_ref, vr_ref, w1k_ref, w2k_ref, pek_ref, w1v_ref, w2v_ref, pev_ref,
                     kg_ref, kc_ref, vc_ref):
    nc = kr_ref.shape[2]
    half = CMP_STRIDE * HEAD_DIM

    def mlp(raw, w1_ref, w2_ref, pe_ref):
        a = raw.astype(BF16)
        top = _dot(a, w1_ref[:half, :])
        bot = _dot(a, w1_ref[half:, :])
        pe = jnp.broadcast_to(pe_ref[...], (8, 2 * half)).astype(BF16)
        pe_row = _dot(pe, w1_ref[...])[0:1, :]
        hid = top + pltpu.roll(bot, nc - 1, 0) + pe_row
        return _dot(_gelu(hid).astype(BF16), w2_ref[...])

    kc = mlp(kr_ref[0, 0], w1k_ref, w2k_ref, pek_ref)
    ms = jnp.mean(kc * kc, axis=-1, keepdims=True)
    kc_ref[0, 0] = (kc * lax.rsqrt(ms + EPS) * kg_ref[...]).astype(BF16)
    vc_ref[0, 0] = mlp(vr_ref[0, 0], w1v_ref, w2v_ref, pev_ref).astype(BF16)


def _compress_call(kc_raw, vc_raw, w1k, w2k, pek, w1v, w2v, pev, kg):
    B, H, S, hd = kc_raw.shape
    nc = S // CMP_STRIDE
    feat = CMP_STRIDE * hd
    raw = pl.BlockSpec((1, 1, nc, feat), lambda b, h: (b, h, 0, 0))
    full = lambda *shape: pl.BlockSpec(shape, lambda b, h: (0,) * len(shape))
    out = pl.BlockSpec((1, 1, nc, hd), lambda b, h: (b, h, 0, 0))
    return pl.pallas_call(
        _compress_kernel,
        grid=(B, H),
        in_specs=[raw, raw, full(2 * feat, CMP_HIDDEN), full(CMP_HIDDEN, hd), full(1, 2 * feat),
                  full(2 * feat, CMP_HIDDEN), full(CMP_HIDDEN, hd), full(1, 2 * feat),
                  full(1, hd)],
        out_specs=[out, out],
        out_shape=[jax.ShapeDtypeStruct((B, H, nc, hd), BF16)] * 2,
        compiler_params=_cparams("arbitrary", "arbitrary"),
        name="compress",
    )(kc_raw.reshape(B, H, nc, feat), vc_raw.reshape(B, H, nc, feat),
      w1k, w2k, pek, w1v, w2v, pev, kg)


def _group_rows(a):
    return jnp.concatenate([a[:, g * HEAD_DIM:(g + 1) * HEAD_DIM] for g in range(NSA_GROUP)], axis=0)


def _nsa_cmp_kernel(qn_ref, qr_ref, kc_ref, vc_ref, ovl_ref, oc_ref, qa_ref, *, n_sel):
    tq = qn_ref.shape[1]
    nc = kc_ref.shape[2]
    G = NSA_GROUP
    q0 = pl.program_id(2) * tq
    q4 = _group_rows(qn_ref[0])
    s = _dot_nt(q4, kc_ref[0, 0])
    row = lax.broadcasted_iota(jnp.int32, (G, tq, nc), 1).reshape(G * tq, nc)
    col = lax.broadcasted_iota(jnp.int32, (G * tq, nc), 1)
    vis = col * CMP_STRIDE + (CMP_LEN - 1) <= q0 + row
    s = jnp.where(vis, s, MASK_NEG)
    m = jnp.max(s, axis=-1, keepdims=True)
    e = jnp.where(vis, jnp.exp(s - m), 0.0)
    p = e / jnp.maximum(jnp.sum(e, axis=-1, keepdims=True), 1e-30)
    oc = _dot(p.astype(BF16), vc_ref[0, 0])
    oc_ref[0, 0] = oc.reshape(G, tq, HEAD_DIM)

    psum = p[0:tq] + p[tq:2 * tq] + p[2 * tq:3 * tq] + p[3 * tq:4 * tq]
    p_hi = psum.astype(BF16)
    p_lo = (psum - p_hi.astype(F32)).astype(BF16)
    imp = (_dot_nt(ovl_ref[...], p_hi) + _dot_nt(ovl_ref[...], p_lo))[:SEL_BIAS_WIDTH]
    blk = lax.broadcasted_iota(jnp.int32, (SEL_BIAS_WIDTH, tq), 0)
    cur = (q0 + lax.broadcasted_iota(jnp.int32, (SEL_BIAS_WIDTH, tq), 1)) // SEL_LEN
    valid = blk <= cur
    forced = (blk == 0) | (blk == cur) | (blk == cur - 1)
    cand = valid & jnp.logical_not(forced)
    n_forced = jnp.minimum(cur, 2) + 1
    val = jnp.where(cand, imp, -1.0)
    cnt = jnp.zeros((SEL_BIAS_WIDTH, tq), F32)
    for j in range(n_sel):
        vj = jnp.broadcast_to(val[j:j + 1, :], (SEL_BIAS_WIDTH, tq))
        cnt = cnt + jnp.where(blk > j, jnp.where(vj >= val, 1.0, 0.0), jnp.where(vj > val, 1.0, 0.0))
    free = (min(SEL_TOPN, n_sel) - n_forced).astype(F32)
    sel = (forced & valid) | (cand & (cnt < free))
    sel_t = jnp.concatenate([jnp.where(sel, 1.0, 0.0), jnp.zeros((LANES - SEL_BIAS_WIDTH, tq), F32)], axis=0)
    sel_q = sel_t.T
    bias = jnp.where(sel_q > 0.5, 0.0, SEL_NEG)
    lane = lax.broadcasted_iota(jnp.int32, (tq, LANES), 1)
    bias_hi = pltpu.roll(bias, SEL_BIAS_WIDTH, 1)
    qr = qr_ref[0]
    for g in range(G):
        qg = qr[:, g * HEAD_DIM:(g + 1) * HEAD_DIM].astype(F32)
        qg = jnp.concatenate([qg, qg], axis=-1)
        qa_ref[0, 0, g] = jnp.where(lane < HEAD_DIM, qg, bias_hi).astype(BF16)


def _overlap_t(S):
    n_cmp = (S - CMP_LEN) // CMP_STRIDE + 1
    nc = S // CMP_STRIDE
    n_sel = S // SEL_LEN
    start = np.arange(nc) * CMP_STRIDE
    end = start + CMP_LEN - 1
    sel_start = np.arange(n_sel) * SEL_LEN
    ov = (start[None, :] <= sel_start[:, None] + SEL_LEN - 1) & (end[None, :] >= sel_start[:, None])
    ov = ov & (np.arange(nc) < n_cmp)[None, :]
    out = np.zeros((LANES, nc), np.float32)
    out[:n_sel] = ov.astype(np.float32)
    return jnp.asarray(out).astype(BF16)


def _nsa_cmp_call(qn, qr, kc, vc, tq):
    B, S, _ = qn.shape
    H, G = NSA_KV_HEADS, NSA_GROUP
    nc = kc.shape[2]
    n_sel = S // SEL_LEN
    assert n_sel <= SEL_BIAS_WIDTH and nc % LANES == 0
    qspec = pl.BlockSpec((1, tq, G * HEAD_DIM), lambda b, h, i: (b, i, h))
    cspec = pl.BlockSpec((1, 1, nc, HEAD_DIM), lambda b, h, i: (b, h, 0, 0))
    return pl.pallas_call(
        functools.partial(_nsa_cmp_kernel, n_sel=n_sel),
        grid=(B, H, S // tq),
        in_specs=[qspec, qspec, cspec, cspec, pl.BlockSpec((LANES, nc), lambda b, h, i: (0, 0))],
        out_specs=[pl.BlockSpec((1, 1, G, tq, HEAD_DIM), lambda b, h, i: (b, h, 0, i, 0)),
                   pl.BlockSpec((1, 1, G, tq, LANES), lambda b, h, i: (b, h, 0, i, 0))],
        out_shape=[jax.ShapeDtypeStruct((B, H, G, S, HEAD_DIM), F32),
                   jax.ShapeDtypeStruct((B, H, G, S, LANES), BF16)],
        compiler_params=_cparams("arbitrary", "arbitrary", "arbitrary"),
        name="nsa_cmp",
    )(qn, qr, kc, vc, _overlap_t(S))


SEL_CHUNK = 4


def _nsa_flash_kernel(qa_ref, ks_ref, vs_ref, kw_ref, vw_ref, oc_ref, g_ref, o_ref, m_scr, acc_scr):
    G = NSA_GROUP
    tq = qa_ref.shape[3]
    R = G * tq
    tk = tq
    i = pl.program_id(2)
    q0 = i * tq

    def rows_of(ref, j0, nt):
        return ref[0, 0, pl.ds(pl.multiple_of(j0 * tk, tk), nt * tk), :]

    def attend(k_ref, v_ref, j0, nt, visible, state):
        k, v = rows_of(k_ref, j0, nt), rows_of(v_ref, j0, nt)
        ss = [_dot_nt(qa_ref[0, 0, g], k) for g in range(G)]
        if visible is not None:
            qpos = q0 + lax.broadcasted_iota(jnp.int32, (tq, nt * tk), 0)
            kpos = j0 * tk + lax.broadcasted_iota(jnp.int32, (tq, nt * tk), 1)
            mask = visible(qpos, kpos)
        ps, alphas = [], []
        for g in range(G):
            s = ss[g] if visible is None else jnp.where(mask, ss[g], MASK_NEG)
            cols = [s[:, c * LANES:(c + 1) * LANES] for c in range(nt * tk // LANES)]
            m_new = jnp.max(functools.reduce(jnp.maximum, cols), axis=-1, keepdims=True)
            if state:
                m_prev = m_scr[g]
                m_new = jnp.maximum(m_prev, m_new)
                alphas.append(jnp.exp(m_prev - m_new))
                m_scr[g] = m_new
            ps.append(jnp.concatenate([jnp.exp(c - m_new) for c in cols], axis=-1).astype(BF16))
        if not state:
            return [_dot(ps[g], v) for g in range(G)]
        for g in range(G):
            acc_scr[g] = alphas[g] * acc_scr[g] + _dot(ps[g], v)

    def normalise(acc):
        return acc[:, :HEAD_DIM] / acc[:, HEAD_DIM:HEAD_DIM + 1]

    m_scr[...] = jnp.full((G, tq, LANES), MASK_NEG, F32)
    acc_scr[...] = jnp.zeros((G, tq, LANES), F32)
    causal = lambda qpos, kpos: kpos <= qpos

    def sel_body(c, carry):
        attend(ks_ref, vs_ref, c * SEL_CHUNK, SEL_CHUNK, None, True)
        return carry

    n_full = i // SEL_CHUNK
    lax.fori_loop(0, n_full, sel_body, 0)
    attend(ks_ref, vs_ref, n_full * SEL_CHUNK, 2, causal, True)

    @pl.when(i % SEL_CHUNK >= 2)
    def _():
        attend(ks_ref, vs_ref, n_full * SEL_CHUNK + 2, 2, causal, True)

    o_sel = normalise(acc_scr[...].reshape(R, LANES))

    assert WINDOW == 2 * tk
    band = lambda qpos, kpos: (kpos <= qpos) & (kpos > qpos - WINDOW)
    o_win = normalise(jnp.concatenate(attend(kw_ref, vw_ref, jnp.maximum(i - 2, 0), 3, band, False), axis=0))

    o_cmp = oc_ref[0, 0].reshape(R, HEAD_DIM)
    gates = g_ref[0, 0]
    outs = []
    for g in range(G):
        r = slice(g * tq, (g + 1) * tq)
        outs.append(gates[:, 3 * g:3 * g + 1] * o_cmp[r]
                    + gates[:, 3 * g + 1:3 * g + 2] * o_sel[r]
                    + gates[:, 3 * g + 2:3 * g + 3] * o_win[r])
    o_ref[0] = jnp.concatenate(outs, axis=-1).astype(BF16)


def _nsa_flash_call(qa, ks, vs, kw, vw, oc, gates):
    B, H, G, S, _ = qa.shape
    tq = WINDOW // 2
    assert (S // tq) % SEL_CHUNK == 0
    kv = pl.BlockSpec((1, 1, S, LANES), lambda b, h, i: (b, h, 0, 0))
    return pl.pallas_call(
        _nsa_flash_kernel,
        grid=(B, H, S // tq),
        in_specs=[pl.BlockSpec((1, 1, G, tq, LANES), lambda b, h, i: (b, h, 0, i, 0)),
                  kv, kv, kv, kv,
                  pl.BlockSpec((1, 1, G, tq, HEAD_DIM), lambda b, h, i: (b, h, 0, i, 0)),
                  pl.BlockSpec((1, 1, tq, LANES), lambda b, h, i: (b, h, i, 0))],
        out_specs=pl.BlockSpec((1, tq, G * HEAD_DIM), lambda b, h, i: (b, i, h)),
        out_shape=jax.ShapeDtypeStruct((B, S, Q_WIDTH), BF16),
        scratch_shapes=[pltpu.VMEM((G, tq, LANES), F32), pltpu.VMEM((G, tq, LANES), F32)],
        compiler_params=_cparams("arbitrary", "arbitrary", "arbitrary"),
        name="nsa_flash",
    )(qa, ks, vs, kw, vw, oc, gates)


def _gmlp_kernel(u_ref, v_ref, ws_ref, bs_ref, o_ref):
    tm = u_ref.shape[1]
    C = GM_CHUNK
    r = lax.broadcasted_iota(jnp.int32, (C, C), 0)
    c = lax.broadcasted_iota(jnp.int32, (C, C), 1)
    causal = c <= r
    ws = [jnp.where(causal, ws_ref[g], 0.0).astype(BF16) for g in range(GM_GROUPS)]
    for n in range(tm // C):
        rows = slice(n * C, (n + 1) * C)
        vn = v_ref[0, rows, :]
        mixed = jnp.concatenate(
            [_dot(ws[g], vn[:, g * GM_GROUP_DIM:(g + 1) * GM_GROUP_DIM]) for g in range(GM_GROUPS)],
            axis=-1)
        o_ref[0, rows, :] = (u_ref[0, rows, :].astype(F32) * (mixed + bs_ref[...])).astype(BF16)


def _gmlp_call(u, v, ws, bs_full, tm):
    B, S, W = u.shape
    tok = pl.BlockSpec((1, tm, W), lambda b, i: (b, i, 0))
    return pl.pallas_call(
        _gmlp_kernel,
        grid=(B, S // tm),
        in_specs=[tok, tok,
                  pl.BlockSpec((GM_GROUPS, GM_CHUNK, GM_CHUNK), lambda b, i: (0, 0, 0)),
                  pl.BlockSpec((GM_CHUNK, W), lambda b, i: (0, 0))],
        out_specs=tok,
        out_shape=jax.ShapeDtypeStruct((B, S, W), BF16),
        compiler_params=_cparams("arbitrary", "arbitrary"),
        name="gmlp",
    )(u, v, ws, bs_full)


def _mixout_kernel(on_ref, og_ref, ga_ref, gb_ref, x_ref, gt_ref, sc_ref, sh_ref,
                   wpn_ref, wpg_ref, wo_ref, wr_ref, wgs_ref, wus_ref, wds_ref,
                   x1_ref, h2_ref, lg_ref, shd_ref):
    ya = _dot(on_ref[0], wpn_ref[...])
    yb = _dot(og_ref[0], wpg_ref[...])
    merged = ga_ref[0].astype(F32) * ya + gb_ref[0].astype(F32) * yb
    x1 = x_ref[0] + gt_ref[0] * _dot(merged.astype(BF16), wo_ref[...])
    x1_ref[0] = x1
    ms = jnp.mean(x1 * x1, axis=-1, keepdims=True)
    h2 = (x1 * lax.rsqrt(ms + EPS)) * (1.0 + sc_ref[0]) + sh_ref[0]
    hb = h2.astype(BF16)
    lg_ref[0] = _dot(hb, wr_ref[...])
    h2_ref[0] = _pack_halves(hb)
    act = _silu(_dot(hb, wgs_ref[...])) * _dot(hb, wus_ref[...])
    shd_ref[0] = _dot(act.astype(BF16), wds_ref[...])


def _mixout_call(o_nsa, o_gm, ga, gb, x, gt, sc, sh, wpn, wpg, wo, wr, wgs, wus, wds, tm):
    B, S, D = x.shape
    tok = lambda w: pl.BlockSpec((1, tm, w), lambda b, i: (b, i, 0))
    per_b = pl.BlockSpec((1, 1, D), lambda b, i: (b, 0, 0))
    full = lambda a: pl.BlockSpec(a.shape, lambda b, i: (0,) * a.ndim)
    sds = jax.ShapeDtypeStruct
    return pl.pallas_call(
        _mixout_kernel,
        grid=(B, S // tm),
        in_specs=[tok(Q_WIDTH), tok(GM_WIDTH), tok(D), tok(D), tok(D), per_b, per_b, per_b,
                  full(wpn), full(wpg), full(wo), full(wr), full(wgs), full(wus), full(wds)],
        out_specs=[tok(D), tok(D // 2), tok(N_EXPERTS), tok(D)],
        out_shape=[sds((B, S, D), F32), sds((B, S, D // 2), jnp.uint32), sds((B, S, N_EXPERTS), F32),
                   sds((B, S, D), F32)],
        compiler_params=_cparams("arbitrary", "arbitrary"),
        name="mix_out",
    )(o_nsa, o_gm, ga, gb, x, gt, sc, sh, wpn, wpg, wo, wr, wgs, wus, wds)


def _route_kernel(lg_ref, br_ref, idx_ref, wt_ref, rank_ref, cnt_ref, run_scr):
    tm, E = lg_ref.shape

    @pl.when(pl.program_id(0) == 0)
    def _():
        run_scr[...] = jnp.zeros_like(run_scr)

    aff = _sigmoid(lg_ref[...])
    work = aff + br_ref[...]
    lane = lax.broadcasted_iota(jnp.int32, (tm, E), 1).astype(F32)
    picked = jnp.zeros((tm, E), F32)
    idxs, tops = [], []
    for _ in range(TOP_K):
        m = jnp.max(work, axis=-1, keepdims=True)
        idx = jnp.min(jnp.where(work == m, lane, float(E)), axis=-1, keepdims=True)
        hit = lane == idx
        tops.append(jnp.sum(jnp.where(hit, aff, 0.0), axis=-1, keepdims=True))
        idxs.append(idx)
        picked = jnp.where(hit, 1.0, picked)
        work = jnp.where(hit, -jnp.inf, work)
    total = tops[0]
    for t in tops[1:]:
        total = total + t
    r = lax.broadcasted_iota(jnp.int32, (tm, tm), 0)
    c = lax.broadcasted_iota(jnp.int32, (tm, tm), 1)
    before = _dot(jnp.where(c < r, 1.0, 0.0).astype(BF16), picked.astype(BF16)) + run_scr[...]
    ranks = [jnp.sum(jnp.where(lane == idx, before, 0.0), axis=-1, keepdims=True) for idx in idxs]
    run_scr[...] = run_scr[...] + jnp.sum(picked, axis=0, keepdims=True)
    cnt_ref[...] = run_scr[...]
    lane_k = lax.broadcasted_iota(jnp.int32, (tm, LANES), 1)
    idx_o = jnp.zeros((tm, LANES), jnp.int32)
    wt_o = jnp.zeros((tm, LANES), F32)
    rank_o = jnp.zeros((tm, LANES), jnp.int32)
    for k in range(TOP_K):
        idx_o = jnp.where(lane_k == k, idxs[k].astype(jnp.int32), idx_o)
        wt_o = jnp.where(lane_k == k, tops[k] / total * ROUTE_SCALE, wt_o)
        rank_o = jnp.where(lane_k == k, ranks[k].astype(jnp.int32), rank_o)
    idx_ref[...] = idx_o
    wt_ref[...] = wt_o
    rank_ref[...] = rank_o


def _route_call(logits, b_router, tm):
    T, E = logits.shape
    tok = lambda w: pl.BlockSpec((tm, w), lambda i: (i, 0))
    one = pl.BlockSpec((1, E), lambda i: (0, 0))
    sds = jax.ShapeDtypeStruct
    return pl.pallas_call(
        _route_kernel,
        grid=(T // tm,),
        in_specs=[tok(E), one],
        out_specs=[tok(LANES), tok(LANES), tok(LANES), one],
        out_shape=[sds((T, LANES), jnp.int32), sds((T, LANES), F32), sds((T, LANES), jnp.int32),
                   sds((1, E), F32)],
        scratch_shapes=[pltpu.VMEM((1, E), F32)],
        compiler_params=_cparams("arbitrary"),
        name="route",
    )(logits, b_router.reshape(1, E))


def _slot_kernel(idx_ref, rank_ref, ps_ref, o_ref):
    tm = idx_ref.shape[0]
    E = ps_ref.shape[1]
    lane = lax.broadcasted_iota(jnp.int32, (tm, E), 1)
    lane_k = lax.broadcasted_iota(jnp.int32, (tm, LANES), 1)
    idx, rank, ps = idx_ref[...], rank_ref[...], ps_ref[...]
    out = jnp.zeros((tm, LANES), jnp.int32)
    for k in range(TOP_K):
        base = jnp.sum(jnp.where(lane == idx[:, k:k + 1], ps, 0.0), axis=-1, keepdims=True)
        out = jnp.where(lane_k == k, base.astype(jnp.int32) + rank[:, k:k + 1], out)
    o_ref[...] = out[:, :TOP_K]


def _slot_call(idx, rank, pad_start, tm):
    T = idx.shape[0]
    E = pad_start.shape[0]
    tok = pl.BlockSpec((tm, LANES), lambda i: (i, 0))
    return pl.pallas_call(
        _slot_kernel,
        grid=(T // tm,),
        in_specs=[tok, tok, pl.BlockSpec((1, E), lambda i: (0, 0))],
        out_specs=pl.BlockSpec((tm, TOP_K), lambda i: (i, 0)),
        out_shape=jax.ShapeDtypeStruct((T, TOP_K), jnp.int32),
        compiler_params=_cparams("arbitrary"),
        name="slot",
    )(idx, rank, pad_start.astype(F32).reshape(1, E))


def _dispatch_kernel(bv_ref, slot_ref, h_ref, xs_ref, zbuf, sem, zsem):
    tm = h_ref.shape[0]
    rows = zbuf.shape[0]

    @pl.when(pl.program_id(0) == 0)
    def _():
        zbuf[...] = jnp.zeros_like(zbuf)

        def zero_copy(i):
            return pltpu.make_async_copy(zbuf, xs_ref.at[pl.ds(pl.multiple_of(i * rows, rows), rows), :], zsem)

        def start(i, carry):
            @pl.when(bv_ref[i] < rows)
            def _():
                zero_copy(i).start()
            return carry

        def wait(i, carry):
            @pl.when(bv_ref[i] < rows)
            def _():
                zero_copy(i).wait()
            return carry

        lax.fori_loop(0, bv_ref.shape[0], start, 0)
        lax.fori_loop(0, bv_ref.shape[0], wait, 0)

    def body(r, carry):
        for k in range(TOP_K):
            d = slot_ref[r * TOP_K + k]
            pltpu.make_async_copy(h_ref.at[pl.ds(r, 1), :], xs_ref.at[pl.ds(d, 1), :], sem).start(priority=k % 2)
        return carry

    lax.fori_loop(0, tm, body, 0)
    for k in range(TOP_K):
        pltpu.make_async_copy(h_ref, xs_ref.at[pl.ds(0, tm), :], sem).wait()


def _dispatch_call(blk_valid, slots, h2p, n_rows, tm):
    T, W = h2p.shape
    grid_spec = pltpu.PrefetchScalarGridSpec(
        num_scalar_prefetch=1,
        grid=(T // tm,),
        in_specs=[pl.BlockSpec((tm * TOP_K,), lambda i, bv: (i,), memory_space=pltpu.SMEM),
                  pl.BlockSpec((tm, W), lambda i, bv: (i, 0))],
        out_specs=pl.BlockSpec(memory_space=pl.ANY),
        scratch_shapes=[pltpu.VMEM((MOE_GRAN, W), jnp.uint32), pltpu.SemaphoreType.DMA, pltpu.SemaphoreType.DMA],
    )
    return pl.pallas_call(
        _dispatch_kernel,
        grid_spec=grid_spec,
        out_shape=jax.ShapeDtypeStruct((n_rows, W), jnp.uint32),
        compiler_params=pltpu.CompilerParams(dimension_semantics=("arbitrary",), vmem_limit_bytes=VMEM_LIMIT,
                                             has_side_effects=True),
        name="dispatch",
    )(blk_valid, slots, h2p)


def _expert_kernel(ie_ref, ir_ref, ig_ref, n_ref, x_hbm, wg_ref, wu_ref, wd_ref, y_hbm,
                   xbuf, ybuf, act_scr, xsem, ysem):
    w = pl.program_id(0)
    n = n_ref[0]
    R = xbuf.shape[1]
    last = ie_ref.shape[0] - 1

    def x_copy(item, slot):
        row = pl.multiple_of(ir_ref[jnp.minimum(item, last)], MOE_GRAN)
        return pltpu.make_async_copy(x_hbm.at[pl.ds(row, R), :], xbuf.at[slot], xsem.at[slot])

    def y_copy(item, slot, gi):
        row = pl.multiple_of(ir_ref[jnp.minimum(item, last)] + gi * MOE_GRAN, MOE_GRAN)
        src = ybuf.at[slot, pl.ds(pl.multiple_of(gi * MOE_GRAN, MOE_GRAN), MOE_GRAN), :]
        return pltpu.make_async_copy(src, y_hbm.at[pl.ds(row, MOE_GRAN), :], ysem.at[slot])

    def for_granules(item, fn):
        def body(gi, carry):
            fn(gi)
            return carry
        lax.fori_loop(0, ig_ref[jnp.clip(item, 0, last)], body, 0)

    def gate_up(slot):
        lo, hi = _unpack_halves(xbuf[slot])
        x = jnp.concatenate([lo, hi], axis=1).astype(BF16)
        a = _dot(x, wg_ref[0, 0].astype(BF16))
        b = _dot(x, wu_ref[0, 0].astype(BF16))
        return (_silu(a) * b).astype(BF16)

    def down(act):
        return _pack_halves(_dot(act, wd_ref[0, 0].astype(BF16)))

    cur, prv = w % 2, (w + 1) % 2

    @pl.when(w == 0)
    def _():
        x_copy(0, 0).start()

    @pl.when(w < n)
    def _():
        x_copy(w, cur).wait()

    @pl.when(w + 1 < n)
    def _():
        x_copy(w + 1, prv).start()

    @pl.when((w >= 3) & (w - 3 < n))
    def _():
        for_granules(w - 3, lambda gi: y_copy(w - 3, prv, gi).wait())

    @pl.when(w == 0)
    def _():
        act_scr[...] = gate_up(0)

    @pl.when((w >= 1) & (w < n))
    def _():
        prev = act_scr[...]
        ybuf[prv] = down(prev)
        act_scr[...] = gate_up(cur)

    @pl.when((w >= 1) & (w == n))
    def _():
        ybuf[prv] = down(act_scr[...])

    @pl.when((w >= 1) & (w <= n))
    def _():
        for_granules(w - 1, lambda gi: y_copy(w - 1, prv, gi).start())


def _expert_call(layer, item_e, item_row, item_ng, n_items, xs, wg, wu, wd):
    n_alloc, W = xs.shape
    R = MOE_ROWS
    nw = item_e.shape[0]
    D, F = wg.shape[2], wg.shape[3]
    cur = lambda w, ie, ir, ig, n: (layer, ie[jnp.minimum(w, nw - 1)], 0, 0)
    prev = lambda w, ie, ir, ig, n: (layer, ie[jnp.clip(w - 1, 0, nw - 1)], 0, 0)
    grid_spec = pltpu.PrefetchScalarGridSpec(
        num_scalar_prefetch=4,
        grid=(nw + 3,),
        in_specs=[pl.BlockSpec(memory_space=pl.ANY),
                  pl.BlockSpec((1, 1, D, F), cur), pl.BlockSpec((1, 1, D, F), cur),
                  pl.BlockSpec((1, 1, F, D), prev)],
        out_specs=pl.BlockSpec(memory_space=pl.ANY),
        scratch_shapes=[pltpu.VMEM((2, R, W), jnp.uint32), pltpu.VMEM((2, R, W), jnp.uint32),
                        pltpu.VMEM((R, F), BF16),
                        pltpu.SemaphoreType.DMA((2,)), pltpu.SemaphoreType.DMA((2,))],
    )
    return pl.pallas_call(
        _expert_kernel,
        grid_spec=grid_spec,
        out_shape=jax.ShapeDtypeStruct((n_alloc, W), jnp.uint32),
        input_output_aliases={4: 0},
        compiler_params=pltpu.CompilerParams(dimension_semantics=("arbitrary",), vmem_limit_bytes=VMEM_LIMIT,
                                             has_side_effects=True),
        name="expert",
    )(item_e, item_row, item_ng, n_items, xs, wg, wu, wd)


def _combine_kernel(slot_ref, wt_ref, shd_ref, x_ref, gt_ref, y_ref, o_ref, ybuf, sem):
    tm = x_ref.shape[0]

    def body(r, carry):
        for k in range(TOP_K):
            d = slot_ref[r * TOP_K + k]
            pltpu.make_async_copy(y_ref.at[pl.ds(d, 1), :], ybuf.at[k, pl.ds(r, 1), :], sem).start(priority=k % 2)
        return carry

    lax.fori_loop(0, tm, body, 0)
    for k in range(TOP_K):
        pltpu.make_async_copy(y_ref.at[pl.ds(0, tm), :], ybuf.at[k], sem).wait()
    wt = wt_ref[...]
    acc_lo = acc_hi = None
    for k in range(TOP_K):
        lo, hi = _unpack_halves(ybuf[k])
        w = wt[:, k:k + 1]
        acc_lo = w * lo if k == 0 else acc_lo + w * lo
        acc_hi = w * hi if k == 0 else acc_hi + w * hi
    routed = jnp.concatenate([acc_lo, acc_hi], axis=1)
    o_ref[...] = x_ref[...] + gt_ref[0] * (routed + shd_ref[...])


def _combine_call(slots, wts, shared, x1, gt, y, tm):
    T, D = x1.shape
    B = gt.shape[0]
    per_b = T // B // tm
    tok = lambda w: pl.BlockSpec((tm, w), lambda i: (i, 0))
    return pl.pallas_call(
        _combine_kernel,
        grid=(T // tm,),
        in_specs=[pl.BlockSpec((tm * TOP_K,), lambda i: (i,), memory_space=pltpu.SMEM),
                  tok(LANES), tok(D), tok(D),
                  pl.BlockSpec((1, 1, D), lambda i: (i // per_b, 0, 0)),
                  pl.BlockSpec(memory_space=pl.ANY)],
        out_specs=tok(D),
        out_shape=jax.ShapeDtypeStruct((T, D), F32),
        scratch_shapes=[pltpu.VMEM((TOP_K, tm, y.shape[1]), jnp.uint32), pltpu.SemaphoreType.DMA],
        compiler_params=_cparams("arbitrary"),
        name="combine",
    )(slots, wts, shared, x1, gt, y)


def _rope_tables(positions):
    half = HEAD_DIM // 2
    inv = ROPE_THETA ** (-jnp.arange(half, dtype=F32) / half)
    ang = positions.astype(F32)[..., None] * inv
    cos, sin, zero = jnp.cos(ang), jnp.sin(ang), jnp.zeros_like(ang)
    cos_t = jnp.concatenate([cos, cos] * 2, axis=-1)
    sin_lo = jnp.concatenate([-sin, zero] * 2, axis=-1)
    sin_hi = jnp.concatenate([zero, sin] * 2, axis=-1)
    return cos_t, sin_lo, sin_hi


def _reorder_w_in(w):
    o = np.cumsum([0, Q_WIDTH] + [KV_WIDTH] * 6 + [NSA_Q_HEADS * 3, GM_WIDTH, GM_WIDTH, D_MODEL, D_MODEL])
    q, kc, vc, ks, vs, kw, vw, g, u, v, ga, gb = [w[:, o[i]:o[i + 1]] for i in range(12)]
    per = NSA_GROUP * 3
    pad = jnp.zeros((w.shape[0], LANES - per), w.dtype)
    return jnp.concatenate([q, ks, kw, kc, vc, vs, vw, g[:, :per], pad, g[:, per:], pad, u, v, ga, gb], axis=1)


def _owner(ends, pos):
    return jnp.minimum(jnp.sum((ends[None, :] <= pos[:, None]).astype(jnp.int32), axis=1), ends.shape[0] - 1)


def _lookup(table, idx):
    hit = idx[:, None] == jnp.arange(table.shape[0], dtype=jnp.int32)[None, :]
    return jnp.sum(jnp.where(hit, table[None, :], 0), axis=1)


def _expert_plan(counts, n_alloc, n_items_max):
    per_item = MOE_ROWS // MOE_GRAN
    counts = counts.astype(jnp.int32)
    gran = (counts + MOE_GRAN - 1) // MOE_GRAN
    gran_end = jnp.cumsum(gran)
    gran_start = gran_end - gran
    g = jnp.arange(n_alloc // MOE_GRAN, dtype=jnp.int32)
    ge = _owner(gran_end, g)
    gran_valid = jnp.clip(_lookup(counts, ge) - (g - _lookup(gran_start, ge)) * MOE_GRAN, 0, MOE_GRAN)
    gran_valid = jnp.where(g < gran_end[-1], gran_valid, 0).astype(jnp.int32)
    items = (gran + per_item - 1) // per_item
    item_end = jnp.cumsum(items)
    w = jnp.arange(n_items_max, dtype=jnp.int32)
    ie = _owner(item_end, w)
    part = w - (_lookup(item_end, ie) - _lookup(items, ie))
    live = w < item_end[-1]
    item_ng = jnp.where(live, jnp.clip(_lookup(gran, ie) - part * per_item, 0, per_item), 0).astype(jnp.int32)
    item_row = jnp.where(live, (_lookup(gran_start, ie) + part * per_item) * MOE_GRAN, 0).astype(jnp.int32)
    return gran_start * MOE_GRAN, gran_valid, ie.astype(jnp.int32), item_row, item_ng, item_end[-1:].astype(jnp.int32)


def kernel(x, c, positions, w_mod, b_mod, w_in, q_gain, k_gain, cmp_pos_k, cmp_pos_v, cmp_w1_k, cmp_w2_k, cmp_w1_v, cmp_w2_v, gm_ln_g, gm_ln_b, gm_ws, gm_bs, w_proj_nsa, w_proj_gm, w_out, w_router, b_router, w_gate_e, w_up_e, w_down_e, w_gate_sh, w_up_sh, w_down_sh):
    B, S, D = x.shape
    L = w_mod.shape[0]
    T = B * S
    tm = 256
    tm_dense = 512
    scale = HEAD_DIM ** -0.5
    cos_t, sin_lo, sin_hi = _rope_tables(positions)
    mod = _mod_call(c, w_mod, b_mod)
    n_alloc = T * TOP_K + N_EXPERTS * MOE_GRAN + MOE_ROWS
    n_items_max = N_EXPERTS + T * TOP_K // MOE_ROWS + 1

    for l in range(L):
        sh_a, sc_a, gt_a, sh_f, sc_f, gt_f = [mod[l, :, i * D:(i + 1) * D].reshape(B, 1, D) for i in range(6)]
        qg = (jnp.tile(q_gain[l], NSA_Q_HEADS) * scale).reshape(1, Q_WIDTH)
        kg = jnp.tile(k_gain[l], 2 * NSA_KV_HEADS).reshape(1, 2 * KV_WIDTH)
        (qn, qr, ks, kw, vs, vw, kc_raw, vc_raw, gates, u, v, ga, gb) = _inproj_call(
            x, sc_a, sh_a, _reorder_w_in(w_in[l]).astype(BF16), qg, kg, cos_t, sin_lo, sin_hi,
            gm_ln_g[l].reshape(1, GM_WIDTH), gm_ln_b[l].reshape(1, GM_WIDTH), tm_dense)
        kc, vc = _compress_call(
            kc_raw, vc_raw, cmp_w1_k[l].astype(BF16), cmp_w2_k[l].astype(BF16), cmp_pos_k[l].reshape(1, -1),
            cmp_w1_v[l].astype(BF16), cmp_w2_v[l].astype(BF16), cmp_pos_v[l].reshape(1, -1),
            k_gain[l].reshape(1, HEAD_DIM))
        o_cmp, q_aug = _nsa_cmp_call(qn, qr, kc, vc, WINDOW // 2)
        o_nsa = _nsa_flash_call(q_aug, ks, vs, kw, vw, o_cmp, gates)
        bs_full = jnp.repeat(gm_bs[l].T, GM_GROUP_DIM, axis=1)
        o_gm = _gmlp_call(u, v, gm_ws[l], bs_full, 512)
        x1, h2, logits, shared = _mixout_call(
            o_nsa, o_gm, ga, gb, x, gt_a, sc_f, sh_f,
            w_proj_nsa[l].astype(BF16), w_proj_gm[l].astype(BF16), w_out[l].astype(BF16), w_router[l].astype(BF16),
            w_gate_sh[l].astype(BF16), w_up_sh[l].astype(BF16), w_down_sh[l].astype(BF16), tm_dense)
        idx, wts, rank, counts = _route_call(logits.reshape(T, N_EXPERTS), b_router[l], tm)
        row_start, gran_valid, item_e, item_row, item_ng, n_items = _expert_plan(counts[0], n_alloc, n_items_max)
        slots = _slot_call(idx, rank, row_start, tm).reshape(T * TOP_K)
        xs = _dispatch_call(gran_valid, slots, h2.reshape(T, D // 2), n_alloc, tm)
        y = _expert_call(l, item_e, item_row, item_ng, n_items, xs, w_gate_e, w_up_e, w_down_e)
        x = _combine_call(slots, wts, shared.reshape(T, D), x1.reshape(T, D), gt_f, y, tm).reshape(B, S, D)
    return x
```

```python
import functools

import jax
import jax.numpy as jnp
import numpy as np
from jax import lax
from jax.experimental import pallas as pl
from jax.experimental.pallas import tpu as pltpu
from jax.experimental.pallas import tpu_sc as plsc

D_MODEL = 1024
NSA_Q_HEADS = 8
NSA_KV_HEADS = 2
HEAD_DIM = 64
NSA_GROUP = NSA_Q_HEADS // NSA_KV_HEADS
CMP_LEN = 32
CMP_STRIDE = 16
CMP_HIDDEN = 256
SEL_LEN = 64
SEL_TOPN = 16
WINDOW = 512
ROPE_THETA = 10000.0
Q_WIDTH = NSA_Q_HEADS * HEAD_DIM
KV_WIDTH = NSA_KV_HEADS * HEAD_DIM
GM_GROUPS = 8
GM_GROUP_DIM = 64
GM_WIDTH = GM_GROUPS * GM_GROUP_DIM
GM_CHUNK = 128
N_EXPERTS = 256
TOP_K = 8
D_EXPERT = 256
D_SHARED = 256
ROUTE_SCALE = 2.5
EPS = 1e-6

LANES = 128
SEL_BIAS_WIDTH = 64
MASK_NEG = -1e30
SEL_NEG = -30000.0
SC_WINDOW = 128
MOE_GRAN = 128
MOE_ROWS = 9 * MOE_GRAN
VMEM_LIMIT = 56 * 1024 * 1024

F32 = jnp.float32
BF16 = jnp.bfloat16
HI = lax.Precision.HIGHEST


def _cparams(*sem):
    return pltpu.CompilerParams(dimension_semantics=sem, vmem_limit_bytes=VMEM_LIMIT)


def _dot(a, b, **kw):
    return jnp.dot(a, b, preferred_element_type=F32, **kw)


def _dot_nt(a, b, **kw):
    return lax.dot_general(a, b, (((1,), (1,)), ((), ())), preferred_element_type=F32, **kw)


def _gelu(x):
    return 0.5 * x * (1.0 + jnp.tanh(0.7978845608028654 * (x + 0.044715 * (x * x * x))))


def _sigmoid(x):
    return 1.0 / (1.0 + jnp.exp(-x))


def _silu(x):
    return x * _sigmoid(x)


_HI_MASK = np.uint32(0xFFFF0000)


def _pack_halves(a):
    w = a.shape[1] // 2
    bits = lax.bitcast_convert_type(a.astype(BF16).astype(F32), jnp.uint32)
    return (bits[:, w:] & _HI_MASK) | (bits[:, :w] >> 16)


def _unpack_halves(words):
    lo = lax.bitcast_convert_type(words << 16, F32)
    hi = lax.bitcast_convert_type(words & _HI_MASK, F32)
    return lo, hi


def _mod_kernel(c_ref, w_ref, b_ref, o_ref):
    c = c_ref[...]
    o_ref[0] = _dot(_silu(c), w_ref[0], precision=HI) + b_ref[0]


def _mod_call(c, w_mod, b_mod):
    L, D, N = w_mod.shape
    B = c.shape[0]
    tn = 1536
    return pl.pallas_call(
        _mod_kernel,
        grid=(L, N // tn),
        in_specs=[pl.BlockSpec((B, D), lambda l, j: (0, 0)),
                  pl.BlockSpec((1, D, tn), lambda l, j: (l, 0, j)),
                  pl.BlockSpec((1, 1, tn), lambda l, j: (l, 0, j))],
        out_specs=pl.BlockSpec((1, B, tn), lambda l, j: (l, 0, j)),
        out_shape=jax.ShapeDtypeStruct((L, B, N), F32),
        compiler_params=_cparams("arbitrary", "arbitrary"),
        name="mod",
    )(c, w_mod, b_mod.reshape(L, 1, N))


_C_Q = 0
_C_K = 512
_C_KC = 768
_C_VC = 896
_C_VS = 1024
_C_VW = 1152
_C_G = 1280
_C_U = 1536
_C_V = 2048
_C_GA = 2560
_C_GB = 3584
IN_COLS_P = 4608


def _dot_split(a, b):
    hi = a.astype(BF16)
    lo = (a - hi.astype(F32)).astype(BF16)
    return _dot(hi, b) + _dot(lo, b)


def _head_norm(z, bd):
    ms = _dot_split(z * z, bd)
    return z * lax.rsqrt(ms + EPS)


def _rope(z, cos, sin_lo, sin_hi):
    w = z.shape[-1]
    half = HEAD_DIM // 2
    return z * cos + pltpu.roll(z, w - half, 1) * sin_lo + pltpu.roll(z, half, 1) * sin_hi


def _tile_lanes(t, n):
    return t if n == 1 else jnp.concatenate([t] * n, axis=-1)


def _inproj_kernel(x_ref, sc_ref, sh_ref, w_ref, bdq_ref, bdk_ref, qg_ref, kg_ref,
                   cos_ref, sl_ref, shi_ref, lng_ref, lnb_ref,
                   qn_ref, qr_ref, ks_ref, kw_ref, vs_ref, vw_ref, kc_ref, vc_ref,
                   g_ref, u_ref, v_ref, ga_ref, gb_ref):
    tm = x_ref.shape[1]
    x = x_ref[0]
    ms = jnp.mean(x * x, axis=-1, keepdims=True)
    h = (x * lax.rsqrt(ms + EPS)) * (1.0 + sc_ref[0]) + sh_ref[0]
    hb = h.astype(BF16)

    def mm(lo, width):
        return _dot(hb, w_ref[:, lo:lo + width])

    cos, sl, shi = cos_ref[0], sl_ref[0], shi_ref[0]

    zq = mm(_C_Q, Q_WIDTH)
    qn = _head_norm(zq, bdq_ref[...]) * qg_ref[...]
    qr = _rope(qn, _tile_lanes(cos, 4), _tile_lanes(sl, 4), _tile_lanes(shi, 4))
    qn_ref[0] = qn.astype(BF16)
    qr_ref[0] = qr.astype(BF16)

    zk = mm(_C_K, 2 * KV_WIDTH)
    kn = _head_norm(zk, bdk_ref[...]) * kg_ref[...]
    kr = _rope(kn, _tile_lanes(cos, 2), _tile_lanes(sl, 2), _tile_lanes(shi, 2))
    lane = lax.broadcasted_iota(jnp.int32, (tm, LANES), 1)
    tok = pl.program_id(1) * tm + lax.broadcasted_iota(jnp.int32, (tm, LANES), 0)
    onehot = jnp.where(lane - HEAD_DIM == tok // SEL_LEN, 1.0, 0.0)
    ones_col = jnp.where(lane == HEAD_DIM, 1.0, 0.0)
    low = lane < HEAD_DIM
    zvs = mm(_C_VS, KV_WIDTH)
    zvw = mm(_C_VW, KV_WIDTH)
    zkc = mm(_C_KC, KV_WIDTH)
    zvc = mm(_C_VC, KV_WIDTH)
    for kv in range(NSA_KV_HEADS):
        def head(a):
            return a if kv == 0 else pltpu.roll(a, HEAD_DIM, 1)
        ks_ref[0, kv] = jnp.where(low, head(kr[:, :KV_WIDTH]), onehot).astype(BF16)
        kw_ref[0, kv] = jnp.where(low, head(kr[:, KV_WIDTH:]), 0.0).astype(BF16)
        vs_ref[0, kv] = jnp.where(low, head(zvs), ones_col).astype(BF16)
        vw_ref[0, kv] = jnp.where(low, head(zvw), ones_col).astype(BF16)
        kc_ref[0, kv] = head(zkc)[:, :HEAD_DIM]
        vc_ref[0, kv] = head(zvc)[:, :HEAD_DIM]

    zg = mm(_C_G, 2 * LANES)
    sg = _sigmoid(zg)
    g_ref[0, 0] = sg[:, :LANES]
    g_ref[0, 1] = sg[:, LANES:]

    u_ref[0] = _gelu(mm(_C_U, GM_WIDTH)).astype(BF16)
    gv = _gelu(mm(_C_V, GM_WIDTH))
    mu = jnp.mean(gv, axis=-1, keepdims=True)
    cen = gv - mu
    var = jnp.mean(cen * cen, axis=-1, keepdims=True)
    v_ref[0] = ((cen * lax.rsqrt(var + EPS)) * lng_ref[...] + lnb_ref[...]).astype(BF16)

    ga_ref[0] = _sigmoid(mm(_C_GA, D_MODEL)).astype(BF16)
    gb_ref[0] = _sigmoid(mm(_C_GB, D_MODEL)).astype(BF16)


def _block_diag_mean(width):
    idx = np.arange(width) // HEAD_DIM
    return jnp.asarray((idx[:, None] == idx[None, :]).astype(np.float32) / HEAD_DIM).astype(BF16)


def _inproj_call(x, sc, sh, w_p, qg, kg, cos, sl, shi, lng, lnb, tm):
    B, S, D = x.shape
    H = NSA_KV_HEADS
    full = lambda *shape: pl.BlockSpec(shape, lambda b, i: (0,) * len(shape))
    tok3 = lambda w: pl.BlockSpec((1, tm, w), lambda b, i: (b, i, 0))
    per_b = pl.BlockSpec((1, 1, D), lambda b, i: (b, 0, 0))
    kv4 = lambda w: pl.BlockSpec((1, H, tm, w), lambda b, i: (b, 0, i, 0))
    sds = jax.ShapeDtypeStruct
    out_shape = [
        sds((B, S, Q_WIDTH), BF16), sds((B, S, Q_WIDTH), BF16),
        sds((B, H, S, LANES), BF16), sds((B, H, S, LANES), BF16),
        sds((B, H, S, LANES), BF16), sds((B, H, S, LANES), BF16),
        sds((B, H, S, HEAD_DIM), F32), sds((B, H, S, HEAD_DIM), F32),
        sds((B, H, S, LANES), F32),
        sds((B, S, GM_WIDTH), BF16), sds((B, S, GM_WIDTH), BF16),
        sds((B, S, D), BF16), sds((B, S, D), BF16),
    ]
    out_specs = [
        tok3(Q_WIDTH), tok3(Q_WIDTH), kv4(LANES), kv4(LANES), kv4(LANES), kv4(LANES),
        kv4(HEAD_DIM), kv4(HEAD_DIM), kv4(LANES),
        tok3(GM_WIDTH), tok3(GM_WIDTH), tok3(D), tok3(D),
    ]
    return pl.pallas_call(
        _inproj_kernel,
        grid=(B, S // tm),
        in_specs=[tok3(D), per_b, per_b, full(D, IN_COLS_P),
                  full(Q_WIDTH, Q_WIDTH), full(2 * KV_WIDTH, 2 * KV_WIDTH),
                  full(1, Q_WIDTH), full(1, 2 * KV_WIDTH),
                  tok3(LANES), tok3(LANES), tok3(LANES),
                  full(1, GM_WIDTH), full(1, GM_WIDTH)],
        out_specs=out_specs,
        out_shape=out_shape,
        compiler_params=_cparams("arbitrary", "arbitrary"),
        name="in_proj",
    )(x, sc, sh, w_p, _block_diag_mean(Q_WIDTH), _block_diag_mean(2 * KV_WIDTH), qg, kg,
      cos, sl, shi, lng, lnb)


def _compress_kernel(kr_ref, vr_ref, w1k_ref, w2k_ref, pek_ref, w1v_ref, w2v_ref, pev_ref,
                     kg_ref, kc_ref, vc_ref):
    nc = kr_ref.shape[2]
    half = CMP_STRIDE * HEAD_DIM

    def mlp(raw, w1_ref, w2_ref, pe_ref):
        a = raw.astype(BF16)
        top = _dot(a, w1_ref[:half, :])
        bot = _dot(a, w1_ref[half:, :])
        pe = jnp.broadcast_to(pe_ref[...], (8, 2 * half)).astype(BF16)
        pe_row = _dot(pe, w1_ref[...])[0:1, :]
        hid = top + pltpu.roll(bot, nc - 1, 0) + pe_row
        return _dot(_gelu(hid).astype(BF16), w2_ref[...])

    kc = mlp(kr_ref[0, 0], w1k_ref, w2k_ref, pek_ref)
    ms = jnp.mean(kc * kc, axis=-1, keepdims=True)
    kc_ref[0, 0] = (kc * lax.rsqrt(ms + EPS) * kg_ref[...]).astype(BF16)
    vc_ref[0, 0] = mlp(vr_ref[0, 0], w1v_ref, w2v_ref, pev_ref).astype(BF16)


def _compress_call(kc_raw, vc_raw, w1k, w2k, pek, w1v, w2v, pev, kg):
    B, H, S, hd = kc_raw.shape
    nc = S // CMP_STRIDE
    feat = CMP_STRIDE * hd
    raw = pl.BlockSpec((1, 1, nc, feat), lambda b, h: (b, h, 0, 0))
    full = lambda *shape: pl.BlockSpec(shape, lambda b, h: (0,) * len(shape))
    out = pl.BlockSpec((1, 1, nc, hd), lambda b, h: (b, h, 0, 0))
    return pl.pallas_call(
        _compress_kernel,
        grid=(B, H),
        in_specs=[raw, raw, full(2 * feat, CMP_HIDDEN), full(CMP_HIDDEN, hd), full(1, 2 * feat),
                  full(2 * feat, CMP_HIDDEN), full(CMP_HIDDEN, hd), full(1, 2 * feat),
                  full(1, hd)],
        out_specs=[out, out],
        out_shape=[jax.ShapeDtypeStruct((B, H, nc, hd), BF16)] * 2,
        compiler_params=_cparams("arbitrary", "arbitrary"),
        name="compress",
    )(kc_raw.reshape(B, H, nc, feat), vc_raw.reshape(B, H, nc, feat),
      w1k, w2k, pek, w1v, w2v, pev, kg)


def _group_rows(a):
    return jnp.concatenate([a[:, g * HEAD_DIM:(g + 1) * HEAD_DIM] for g in range(NSA_GROUP)], axis=0)


def _nsa_cmp_kernel(qn_ref, qr_ref, kc_ref, vc_ref, ovl_ref, oc_ref, qa_ref, *, n_sel):
    tq = qn_ref.shape[1]
    nc = kc_ref.shape[2]
    G = NSA_GROUP
    q0 = pl.program_id(2) * tq
    q4 = _group_rows(qn_ref[0])
    s = _dot_nt(q4, kc_ref[0, 0])
    row = lax.broadcasted_iota(jnp.int32, (G, tq, nc), 1).reshape(G * tq, nc)
    col = lax.broadcasted_iota(jnp.int32, (G * tq, nc), 1)
    vis = col * CMP_STRIDE + (CMP_LEN - 1) <= q0 + row
    s = jnp.where(vis, s, MASK_NEG)
    m = jnp.max(s, axis=-1, keepdims=True)
    e = jnp.where(vis, jnp.exp(s - m), 0.0)
    p = e / jnp.maximum(jnp.sum(e, axis=-1, keepdims=True), 1e-30)
    oc = _dot(p.astype(BF16), vc_ref[0, 0])
    oc_ref[0, 0] = oc.reshape(G, tq, HEAD_DIM)

    psum = p[0:tq] + p[tq:2 * tq] + p[2 * tq:3 * tq] + p[3 * tq:4 * tq]
    p_hi = psum.astype(BF16)
    p_lo = (psum - p_hi.astype(F32)).astype(BF16)
    imp = (_dot_nt(ovl_ref[...], p_hi) + _dot_nt(ovl_ref[...], p_lo))[:SEL_BIAS_WIDTH]
    blk = lax.broadcasted_iota(jnp.int32, (SEL_BIAS_WIDTH, tq), 0)
    cur = (q0 + lax.broadcasted_iota(jnp.int32, (SEL_BIAS_WIDTH, tq), 1)) // SEL_LEN
    valid = blk <= cur
    forced = (blk == 0) | (blk == cur) | (blk == cur - 1)
    cand = valid & jnp.logical_not(forced)
    n_forced = jnp.minimum(cur, 2) + 1
    val = jnp.where(cand, imp, -1.0)
    cnt = jnp.zeros((SEL_BIAS_WIDTH, tq), F32)
    for j in range(n_sel):
        vj = jnp.broadcast_to(val[j:j + 1, :], (SEL_BIAS_WIDTH, tq))
        cnt = cnt + jnp.where(blk > j, jnp.where(vj >= val, 1.0, 0.0), jnp.where(vj > val, 1.0, 0.0))
    free = (min(SEL_TOPN, n_sel) - n_forced).astype(F32)
    sel = (forced & valid) | (cand & (cnt < free))
    sel_t = jnp.concatenate([jnp.where(sel, 1.0, 0.0), jnp.zeros((LANES - SEL_BIAS_WIDTH, tq), F32)], axis=0)
    sel_q = sel_t.T
    bias = jnp.where(sel_q > 0.5, 0.0, SEL_NEG)
    lane = lax.broadcasted_iota(jnp.int32, (tq, LANES), 1)
    bias_hi = pltpu.roll(bias, SEL_BIAS_WIDTH, 1)
    qr = qr_ref[0]
    for g in range(G):
        qg = qr[:, g * HEAD_DIM:(g + 1) * HEAD_DIM].astype(F32)
        qg = jnp.concatenate([qg, qg], axis=-1)
        qa_ref[0, 0, g] = jnp.where(lane < HEAD_DIM, qg, bias_hi).astype(BF16)


def _overlap_t(S):
    n_cmp = (S - CMP_LEN) // CMP_STRIDE + 1
    nc = S // CMP_STRIDE
    n_sel = S // SEL_LEN
    start = np.arange(nc) * CMP_STRIDE
    end = start + CMP_LEN - 1
    sel_start = np.arange(n_sel) * SEL_LEN
    ov = (start[None, :] <= sel_start[:, None] + SEL_LEN - 1) & (end[None, :] >= sel_start[:, None])
    ov = ov & (np.arange(nc) < n_cmp)[None, :]
    out = np.zeros((LANES, nc), np.float32)
    out[:n_sel] = ov.astype(np.float32)
    return jnp.asarray(out).astype(BF16)


def _nsa_cmp_call(qn, qr, kc, vc, tq):
    B, S, _ = qn.shape
    H, G = NSA_KV_HEADS, NSA_GROUP
    nc = kc.shape[2]
    n_sel = S // SEL_LEN
    assert n_sel <= SEL_BIAS_WIDTH and nc % LANES == 0
    qspec = pl.BlockSpec((1, tq, G * HEAD_DIM), lambda b, h, i: (b, i, h))
    cspec = pl.BlockSpec((1, 1, nc, HEAD_DIM), lambda b, h, i: (b, h, 0, 0))
    return pl.pallas_call(
        functools.partial(_nsa_cmp_kernel, n_sel=n_sel),
        grid=(B, H, S // tq),
        in_specs=[qspec, qspec, cspec, cspec, pl.BlockSpec((LANES, nc), lambda b, h, i: (0, 0))],
        out_specs=[pl.BlockSpec((1, 1, G, tq, HEAD_DIM), lambda b, h, i: (b, h, 0, i, 0)),
                   pl.BlockSpec((1, 1, G, tq, LANES), lambda b, h, i: (b, h, 0, i, 0))],
        out_shape=[jax.ShapeDtypeStruct((B, H, G, S, HEAD_DIM), F32),
                   jax.ShapeDtypeStruct((B, H, G, S, LANES), BF16)],
        compiler_params=_cparams("arbitrary", "arbitrary", "arbitrary"),
        name="nsa_cmp",
    )(qn, qr, kc, vc, _overlap_t(S))


SEL_CHUNK = 4


def _nsa_flash_kernel(qa_ref, ks_ref, vs_ref, kw_ref, vw_ref, oc_ref, g_ref, o_ref, m_scr, acc_scr):
    G = NSA_GROUP
    tq = qa_ref.shape[3]
    R = G * tq
    tk = tq
    i = pl.program_id(2)
    q0 = i * tq

    def rows_of(ref, j0, nt):
        return ref[0, 0, pl.ds(pl.multiple_of(j0 * tk, tk), nt * tk), :]

    def attend(k_ref, v_ref, j0, nt, visible, state):
        k, v = rows_of(k_ref, j0, nt), rows_of(v_ref, j0, nt)
        ss = [_dot_nt(qa_ref[0, 0, g], k) for g in range(G)]
        if visible is not None:
            qpos = q0 + lax.broadcasted_iota(jnp.int32, (tq, nt * tk), 0)
            kpos = j0 * tk + lax.broadcasted_iota(jnp.int32, (tq, nt * tk), 1)
            mask = visible(qpos, kpos)
        ps, alphas = [], []
        for g in range(G):
            s = ss[g] if visible is None else jnp.where(mask, ss[g], MASK_NEG)
            cols = [s[:, c * LANES:(c + 1) * LANES] for c in range(nt * tk // LANES)]
            m_new = jnp.max(functools.reduce(jnp.maximum, cols), axis=-1, keepdims=True)
            if state:
                m_prev = m_scr[g]
                m_new = jnp.maximum(m_prev, m_new)
                alphas.append(jnp.exp(m_prev - m_new))
                m_scr[g] = m_new
            ps.append(jnp.concatenate([jnp.exp(c - m_new) for c in cols], axis=-1).astype(BF16))
        if not state:
            return [_dot(ps[g], v) for g in range(G)]
        for g in range(G):
            acc_scr[g] = alphas[g] * acc_scr[g] + _dot(ps[g], v)

    def normalise(acc):
        return acc[:, :HEAD_DIM] / acc[:, HEAD_DIM:HEAD_DIM + 1]

    m_scr[...] = jnp.full((G, tq, LANES), MASK_NEG, F32)
    acc_scr[...] = jnp.zeros((G, tq, LANES), F32)
    causal = lambda qpos, kpos: kpos <= qpos

    def sel_body(c, carry):
        attend(ks_ref, vs_ref, c * SEL_CHUNK, SEL_CHUNK, None, True)
        return carry

    n_full = i // SEL_CHUNK
    lax.fori_loop(0, n_full, sel_body, 0)
    attend(ks_ref, vs_ref, n_full * SEL_CHUNK, 2, causal, True)

    @pl.when(i % SEL_CHUNK >= 2)
    def _():
        attend(ks_ref, vs_ref, n_full * SEL_CHUNK + 2, 2, causal, True)

    o_sel = normalise(acc_scr[...].reshape(R, LANES))

    assert WINDOW == 2 * tk
    band = lambda qpos, kpos: (kpos <= qpos) & (kpos > qpos - WINDOW)
    o_win = normalise(jnp.concatenate(attend(kw_ref, vw_ref, jnp.maximum(i - 2, 0), 3, band, False), axis=0))

    o_cmp = oc_ref[0, 0].reshape(R, HEAD_DIM)
    gates = g_ref[0, 0]
    outs = []
    for g in range(G):
        r = slice(g * tq, (g + 1) * tq)
        outs.append(gates[:, 3 * g:3 * g + 1] * o_cmp[r]
                    + gates[:, 3 * g + 1:3 * g + 2] * o_sel[r]
                    + gates[:, 3 * g + 2:3 * g + 3] * o_win[r])
    o_ref[0] = jnp.concatenate(outs, axis=-1).astype(BF16)


def _nsa_flash_call(qa, ks, vs, kw, vw, oc, gates):
    B, H, G, S, _ = qa.shape
    tq = WINDOW // 2
    assert (S // tq) % SEL_CHUNK == 0
    kv = pl.BlockSpec((1, 1, S, LANES), lambda b, h, i: (b, h, 0, 0))
    return pl.pallas_call(
        _nsa_flash_kernel,
        grid=(B, H, S // tq),
        in_specs=[pl.BlockSpec((1, 1, G, tq, LANES), lambda b, h, i: (b, h, 0, i, 0)),
                  kv, kv, kv, kv,
                  pl.BlockSpec((1, 1, G, tq, HEAD_DIM), lambda b, h, i: (b, h, 0, i, 0)),
                  pl.BlockSpec((1, 1, tq, LANES), lambda b, h, i: (b, h, i, 0))],
        out_specs=pl.BlockSpec((1, tq, G * HEAD_DIM), lambda b, h, i: (b, i, h)),
        out_shape=jax.ShapeDtypeStruct((B, S, Q_WIDTH), BF16),
        scratch_shapes=[pltpu.VMEM((G, tq, LANES), F32), pltpu.VMEM((G, tq, LANES), F32)],
        compiler_params=_cparams("arbitrary", "arbitrary", "arbitrary"),
        name="nsa_flash",
    )(qa, ks, vs, kw, vw, oc, gates)


def _gmlp_kernel(u_ref, v_ref, ws_ref, bs_ref, o_ref):
    tm = u_ref.shape[1]
    C = GM_CHUNK
    r = lax.broadcasted_iota(jnp.int32, (C, C), 0)
    c = lax.broadcasted_iota(jnp.int32, (C, C), 1)
    causal = c <= r
    ws = [jnp.where(causal, ws_ref[g], 0.0).astype(BF16) for g in range(GM_GROUPS)]
    for n in range(tm // C):
        rows = slice(n * C, (n + 1) * C)
        vn = v_ref[0, rows, :]
        mixed = jnp.concatenate(
            [_dot(ws[g], vn[:, g * GM_GROUP_DIM:(g + 1) * GM_GROUP_DIM]) for g in range(GM_GROUPS)],
            axis=-1)
        o_ref[0, rows, :] = (u_ref[0, rows, :].astype(F32) * (mixed + bs_ref[...])).astype(BF16)


def _gmlp_call(u, v, ws, bs_full, tm):
    B, S, W = u.shape
    tok = pl.BlockSpec((1, tm, W), lambda b, i: (b, i, 0))
    return pl.pallas_call(
        _gmlp_kernel,
        grid=(B, S // tm),
        in_specs=[tok, tok,
                  pl.BlockSpec((GM_GROUPS, GM_CHUNK, GM_CHUNK), lambda b, i: (0, 0, 0)),
                  pl.BlockSpec((GM_CHUNK, W), lambda b, i: (0, 0))],
        out_specs=tok,
        out_shape=jax.ShapeDtypeStruct((B, S, W), BF16),
        compiler_params=_cparams("arbitrary", "arbitrary"),
        name="gmlp",
    )(u, v, ws, bs_full)


def _mixout_kernel(on_ref, og_ref, ga_ref, gb_ref, x_ref, gt_ref, sc_ref, sh_ref,
                   wpn_ref, wpg_ref, wo_ref, wr_ref, wgs_ref, wus_ref, wds_ref,
                   x1_ref, h2_ref, lg_ref, shd_ref):
    ya = _dot(on_ref[0], wpn_ref[...])
    yb = _dot(og_ref[0], wpg_ref[...])
    merged = ga_ref[0].astype(F32) * ya + gb_ref[0].astype(F32) * yb
    x1 = x_ref[0] + gt_ref[0] * _dot(merged.astype(BF16), wo_ref[...])
    x1_ref[0] = x1
    ms = jnp.mean(x1 * x1, axis=-1, keepdims=True)
    h2 = (x1 * lax.rsqrt(ms + EPS)) * (1.0 + sc_ref[0]) + sh_ref[0]
    hb = h2.astype(BF16)
    lg_ref[0] = _dot(hb, wr_ref[...])
    h2_ref[0] = _pack_halves(hb)
    act = _silu(_dot(hb, wgs_ref[...])) * _dot(hb, wus_ref[...])
    shd_ref[0] = _dot(act.astype(BF16), wds_ref[...])


def _mixout_call(o_nsa, o_gm, ga, gb, x, gt, sc, sh, wpn, wpg, wo, wr, wgs, wus, wds, tm):
    B, S, D = x.shape
    tok = lambda w: pl.BlockSpec((1, tm, w), lambda b, i: (b, i, 0))
    per_b = pl.BlockSpec((1, 1, D), lambda b, i: (b, 0, 0))
    full = lambda a: pl.BlockSpec(a.shape, lambda b, i: (0,) * a.ndim)
    sds = jax.ShapeDtypeStruct
    return pl.pallas_call(
        _mixout_kernel,
        grid=(B, S // tm),
        in_specs=[tok(Q_WIDTH), tok(GM_WIDTH), tok(D), tok(D), tok(D), per_b, per_b, per_b,
                  full(wpn), full(wpg), full(wo), full(wr), full(wgs), full(wus), full(wds)],
        out_specs=[tok(D), tok(D // 2), tok(N_EXPERTS), tok(D)],
        out_shape=[sds((B, S, D), F32), sds((B, S, D // 2), jnp.uint32), sds((B, S, N_EXPERTS), F32),
                   sds((B, S, D), F32)],
        compiler_params=_cparams("arbitrary", "arbitrary"),
        name="mix_out",
    )(o_nsa, o_gm, ga, gb, x, gt, sc, sh, wpn, wpg, wo, wr, wgs, wus, wds)


def _route_kernel(lg_ref, br_ref, idx_ref, wt_ref, rank_ref, cnt_ref, run_scr):
    tm, E = lg_ref.shape

    @pl.when(pl.program_id(0) == 0)
    def _():
        run_scr[...] = jnp.zeros_like(run_scr)

    aff = _sigmoid(lg_ref[...])
    work = aff + br_ref[...]
    lane = lax.broadcasted_iota(jnp.int32, (tm, E), 1).astype(F32)
    picked = jnp.zeros((tm, E), F32)
    idxs, tops = [], []
    for _ in range(TOP_K):
        m = jnp.max(work, axis=-1, keepdims=True)
        idx = jnp.min(jnp.where(work == m, lane, float(E)), axis=-1, keepdims=True)
        hit = lane == idx
        tops.append(jnp.sum(jnp.where(hit, aff, 0.0), axis=-1, keepdims=True))
        idxs.append(idx)
        picked = jnp.where(hit, 1.0, picked)
        work = jnp.where(hit, -jnp.inf, work)
    total = tops[0]
    for t in tops[1:]:
        total = total + t
    r = lax.broadcasted_iota(jnp.int32, (tm, tm), 0)
    c = lax.broadcasted_iota(jnp.int32, (tm, tm), 1)
    before = _dot(jnp.where(c < r, 1.0, 0.0).astype(BF16), picked.astype(BF16)) + run_scr[...]
    ranks = [jnp.sum(jnp.where(lane == idx, before, 0.0), axis=-1, keepdims=True) for idx in idxs]
    run_scr[...] = run_scr[...] + jnp.sum(picked, axis=0, keepdims=True)
    cnt_ref[...] = run_scr[...]
    lane_k = lax.broadcasted_iota(jnp.int32, (tm, LANES), 1)
    idx_o = jnp.zeros((tm, LANES), jnp.int32)
    wt_o = jnp.zeros((tm, LANES), F32)
    rank_o = jnp.zeros((tm, LANES), jnp.int32)
    for k in range(TOP_K):
        idx_o = jnp.where(lane_k == k, idxs[k].astype(jnp.int32), idx_o)
        wt_o = jnp.where(lane_k == k, tops[k] / total * ROUTE_SCALE, wt_o)
        rank_o = jnp.where(lane_k == k, ranks[k].astype(jnp.int32), rank_o)
    idx_ref[...] = idx_o
    wt_ref[...] = wt_o
    rank_ref[...] = rank_o


def _route_call(logits, b_router, tm):
    T, E = logits.shape
    tok = lambda w: pl.BlockSpec((tm, w), lambda i: (i, 0))
    one = pl.BlockSpec((1, E), lambda i: (0, 0))
    sds = jax.ShapeDtypeStruct
    return pl.pallas_call(
        _route_kernel,
        grid=(T // tm,),
        in_specs=[tok(E), one],
        out_specs=[tok(LANES), tok(LANES), tok(LANES), one],
        out_shape=[sds((T, LANES), jnp.int32), sds((T, LANES), F32), sds((T, LANES), jnp.int32),
                   sds((1, E), F32)],
        scratch_shapes=[pltpu.VMEM((1, E), F32)],
        compiler_params=_cparams("arbitrary"),
        name="route",
    )(logits, b_router.reshape(1, E))


def _slot_kernel(idx_ref, rank_ref, ps_ref, o_ref):
    tm = idx_ref.shape[0]
    E = ps_ref.shape[1]
    lane = lax.broadcasted_iota(jnp.int32, (tm, E), 1)
    lane_k = lax.broadcasted_iota(jnp.int32, (tm, LANES), 1)
    idx, rank, ps = idx_ref[...], rank_ref[...], ps_ref[...]
    out = jnp.zeros((tm, LANES), jnp.int32)
    for k in range(TOP_K):
        base = jnp.sum(jnp.where(lane == idx[:, k:k + 1], ps, 0.0), axis=-1, keepdims=True)
        out = jnp.where(lane_k == k, base.astype(jnp.int32) + rank[:, k:k + 1], out)
    o_ref[...] = out[:, :TOP_K]


def _slot_call(idx, rank, pad_start, tm):
    T = idx.shape[0]
    E = pad_start.shape[0]
    tok = pl.BlockSpec((tm, LANES), lambda i: (i, 0))
    return pl.pallas_call(
        _slot_kernel,
        grid=(T // tm,),
        in_specs=[tok, tok, pl.BlockSpec((1, E), lambda i: (0, 0))],
        out_specs=pl.BlockSpec((tm, TOP_K), lambda i: (i, 0)),
        out_shape=jax.ShapeDtypeStruct((T, TOP_K), jnp.int32),
        compiler_params=_cparams("arbitrary"),
        name="slot",
    )(idx, rank, pad_start.astype(F32).reshape(1, E))


def _dispatch_kernel(bv_ref, slot_ref, h_ref, xs_ref, zbuf, sem, zsem):
    tm = h_ref.shape[0]
    rows = zbuf.shape[0]

    @pl.when(pl.program_id(0) == 0)
    def _():
        zbuf[...] = jnp.zeros_like(zbuf)

        def zero_copy(i):
            return pltpu.make_async_copy(zbuf, xs_ref.at[pl.ds(pl.multiple_of(i * rows, rows), rows), :], zsem)

        def start(i, carry):
            @pl.when(bv_ref[i] < rows)
            def _():
                zero_copy(i).start()
            return carry

        def wait(i, carry):
            @pl.when(bv_ref[i] < rows)
            def _():
                zero_copy(i).wait()
            return carry

        lax.fori_loop(0, bv_ref.shape[0], start, 0)
        lax.fori_loop(0, bv_ref.shape[0], wait, 0)

    def body(r, carry):
        for k in range(TOP_K):
            d = slot_ref[r * TOP_K + k]
            pltpu.make_async_copy(h_ref.at[pl.ds(r, 1), :], xs_ref.at[pl.ds(d, 1), :], sem).start(priority=k % 2)
        return carry

    lax.fori_loop(0, tm, body, 0)
    for k in range(TOP_K):
        pltpu.make_async_copy(h_ref, xs_ref.at[pl.ds(0, tm), :], sem).wait()


def _dispatch_call(blk_valid, slots, h2p, n_rows, tm):
    T, W = h2p.shape
    grid_spec = pltpu.PrefetchScalarGridSpec(
        num_scalar_prefetch=1,
        grid=(T // tm,),
        in_specs=[pl.BlockSpec((tm * TOP_K,), lambda i, bv: (i,), memory_space=pltpu.SMEM),
                  pl.BlockSpec((tm, W), lambda i, bv: (i, 0))],
        out_specs=pl.BlockSpec(memory_space=pl.ANY),
        scratch_shapes=[pltpu.VMEM((MOE_GRAN, W), jnp.uint32), pltpu.SemaphoreType.DMA, pltpu.SemaphoreType.DMA],
    )
    return pl.pallas_call(
        _dispatch_kernel,
        grid_spec=grid_spec,
        out_shape=jax.ShapeDtypeStruct((n_rows, W), jnp.uint32),
        compiler_params=pltpu.CompilerParams(dimension_semantics=("arbitrary",), vmem_limit_bytes=VMEM_LIMIT,
                                             has_side_effects=True),
        name="dispatch",
    )(blk_valid, slots, h2p)


def _expert_kernel(ie_ref, ir_ref, ig_ref, n_ref, x_hbm, wg_ref, wu_ref, wd_ref, y_hbm,
                   xbuf, ybuf, act_scr, xsem, ysem):
    w = pl.program_id(0)
    n = n_ref[0]
    R = xbuf.shape[1]
    last = ie_ref.shape[0] - 1

    def x_copy(item, slot):
        row = pl.multiple_of(ir_ref[jnp.minimum(item, last)], MOE_GRAN)
        return pltpu.make_async_copy(x_hbm.at[pl.ds(row, R), :], xbuf.at[slot], xsem.at[slot])

    def y_copy(item, slot, gi):
        row = pl.multiple_of(ir_ref[jnp.minimum(item, last)] + gi * MOE_GRAN, MOE_GRAN)
        src = ybuf.at[slot, pl.ds(pl.multiple_of(gi * MOE_GRAN, MOE_GRAN), MOE_GRAN), :]
        return pltpu.make_async_copy(src, y_hbm.at[pl.ds(row, MOE_GRAN), :], ysem.at[slot])

    def for_granules(item, fn):
        def body(gi, carry):
            fn(gi)
            return carry
        lax.fori_loop(0, ig_ref[jnp.clip(item, 0, last)], body, 0)

    def gate_up(slot):
        lo, hi = _unpack_halves(xbuf[slot])
        x = jnp.concatenate([lo, hi], axis=1).astype(BF16)
        a = _dot(x, wg_ref[0, 0].astype(BF16))
        b = _dot(x, wu_ref[0, 0].astype(BF16))
        return (_silu(a) * b).astype(BF16)

    def down(act):
        return _pack_halves(_dot(act, wd_ref[0, 0].astype(BF16)))

    cur, prv = w % 2, (w + 1) % 2

    @pl.when(w == 0)
    def _():
        x_copy(0, 0).start()

    @pl.when(w < n)
    def _():
        x_copy(w, cur).wait()

    @pl.when(w + 1 < n)
    def _():
        x_copy(w + 1, prv).start()

    @pl.when((w >= 3) & (w - 3 < n))
    def _():
        for_granules(w - 3, lambda gi: y_copy(w - 3, prv, gi).wait())

    @pl.when(w == 0)
    def _():
        act_scr[...] = gate_up(0)

    @pl.when((w >= 1) & (w < n))
    def _():
        prev = act_scr[...]
        ybuf[prv] = down(prev)
        act_scr[...] = gate_up(cur)

    @pl.when((w >= 1) & (w == n))
    def _():
        ybuf[prv] = down(act_scr[...])

    @pl.when((w >= 1) & (w <= n))
    def _():
        for_granules(w - 1, lambda gi: y_copy(w - 1, prv, gi).start())


def _expert_call(layer, item_e, item_row, item_ng, n_items, xs, wg, wu, wd):
    n_alloc, W = xs.shape
    R = MOE_ROWS
    nw = item_e.shape[0]
    D, F = wg.shape[2], wg.shape[3]
    cur = lambda w, ie, ir, ig, n: (layer, ie[jnp.minimum(w, nw - 1)], 0, 0)
    prev = lambda w, ie, ir, ig, n: (layer, ie[jnp.clip(w - 1, 0, nw - 1)], 0, 0)
    grid_spec = pltpu.PrefetchScalarGridSpec(
        num_scalar_prefetch=4,
        grid=(nw + 3,),
        in_specs=[pl.BlockSpec(memory_space=pl.ANY),
                  pl.BlockSpec((1, 1, D, F), cur), pl.BlockSpec((1, 1, D, F), cur),
                  pl.BlockSpec((1, 1, F, D), prev)],
        out_specs=pl.BlockSpec(memory_space=pl.ANY),
        scratch_shapes=[pltpu.VMEM((2, R, W), jnp.uint32), pltpu.VMEM((2, R, W), jnp.uint32),
                        pltpu.VMEM((R, F), BF16),
                        pltpu.SemaphoreType.DMA((2,)), pltpu.SemaphoreType.DMA((2,))],
    )
    return pl.pallas_call(
        _expert_kernel,
        grid_spec=grid_spec,
        out_shape=jax.ShapeDtypeStruct((n_alloc, W), jnp.uint32),
        input_output_aliases={4: 0},
        compiler_params=pltpu.CompilerParams(dimension_semantics=("arbitrary",), vmem_limit_bytes=VMEM_LIMIT,
                                             has_side_effects=True),
        name="expert",
    )(item_e, item_row, item_ng, n_items, xs, wg, wu, wd)


def _gather_rows(table, indices):
    n = indices.shape[0]
    width = table.shape[1]
    mesh = plsc.VectorSubcoreMesh(core_axis_name="core", subcore_axis_name="subcore")
    workers = mesh.num_cores * mesh.num_subcores
    per_worker = n // workers
    assert per_worker * workers == n and per_worker % SC_WINDOW == 0

    @pl.kernel(out_type=jax.ShapeDtypeStruct((n, width), table.dtype), mesh=mesh, name="gather_rows",
               scratch_types=[pltpu.VMEM((SC_WINDOW,), jnp.int32), pltpu.VMEM((SC_WINDOW, width), table.dtype),
                              pltpu.SemaphoreType.DMA])
    def gather(table_hbm, idx_hbm, out_hbm, idx_v, rows_v, sem):
        worker = lax.axis_index("subcore") * mesh.num_cores + lax.axis_index("core")

        @pl.loop(0, per_worker // SC_WINDOW)
        def _(j):
            base = pl.multiple_of(worker * per_worker + j * SC_WINDOW, SC_WINDOW)
            pltpu.sync_copy(idx_hbm.at[pl.ds(base, SC_WINDOW)], idx_v)
            pltpu.async_copy(table_hbm.at[idx_v], rows_v, sem).wait()
            pltpu.sync_copy(rows_v, out_hbm.at[pl.ds(base, SC_WINDOW)])

    return gather(table, indices)


def _combine_kernel(y_ref, wt_ref, shd_ref, x_ref, gt_ref, o_ref):
    wt = wt_ref[...]
    acc_lo = acc_hi = None
    for k in range(TOP_K):
        lo, hi = _unpack_halves(y_ref[k])
        w = wt[:, k:k + 1]
        acc_lo = w * lo if k == 0 else acc_lo + w * lo
        acc_hi = w * hi if k == 0 else acc_hi + w * hi
    routed = jnp.concatenate([acc_lo, acc_hi], axis=1)
    o_ref[...] = x_ref[...] + gt_ref[0] * (routed + shd_ref[...])


def _combine_call(yk, wts, shared, x1, gt, tm):
    T, D = x1.shape
    B = gt.shape[0]
    per_b = T // B // tm
    tok = lambda w: pl.BlockSpec((tm, w), lambda i: (i, 0))
    return pl.pallas_call(
        _combine_kernel,
        grid=(T // tm,),
        in_specs=[pl.BlockSpec((TOP_K, tm, yk.shape[2]), lambda i: (0, i, 0)),
                  tok(LANES), tok(D), tok(D),
                  pl.BlockSpec((1, 1, D), lambda i: (i // per_b, 0, 0))],
        out_specs=tok(D),
        out_shape=jax.ShapeDtypeStruct((T, D), F32),
        compiler_params=_cparams("arbitrary"),
        name="combine",
    )(yk, wts, shared, x1, gt)


def _rope_tables(positions):
    half = HEAD_DIM // 2
    inv = ROPE_THETA ** (-jnp.arange(half, dtype=F32) / half)
    ang = positions.astype(F32)[..., None] * inv
    cos, sin, zero = jnp.cos(ang), jnp.sin(ang), jnp.zeros_like(ang)
    cos_t = jnp.concatenate([cos, cos] * 2, axis=-1)
    sin_lo = jnp.concatenate([-sin, zero] * 2, axis=-1)
    sin_hi = jnp.concatenate([zero, sin] * 2, axis=-1)
    return cos_t, sin_lo, sin_hi


def _reorder_w_in(w):
    o = np.cumsum([0, Q_WIDTH] + [KV_WIDTH] * 6 + [NSA_Q_HEADS * 3, GM_WIDTH, GM_WIDTH, D_MODEL, D_MODEL])
    q, kc, vc, ks, vs, kw, vw, g, u, v, ga, gb = [w[:, o[i]:o[i + 1]] for i in range(12)]
    per = NSA_GROUP * 3
    pad = jnp.zeros((w.shape[0], LANES - per), w.dtype)
    return jnp.concatenate([q, ks, kw, kc, vc, vs, vw, g[:, :per], pad, g[:, per:], pad, u, v, ga, gb], axis=1)


def _owner(ends, pos):
    return jnp.minimum(jnp.sum((ends[None, :] <= pos[:, None]).astype(jnp.int32), axis=1), ends.shape[0] - 1)


def _lookup(table, idx):
    hit = idx[:, None] == jnp.arange(table.shape[0], dtype=jnp.int32)[None, :]
    return jnp.sum(jnp.where(hit, table[None, :], 0), axis=1)


def _expert_plan(counts, n_alloc, n_items_max):
    per_item = MOE_ROWS // MOE_GRAN
    counts = counts.astype(jnp.int32)
    gran = (counts + MOE_GRAN - 1) // MOE_GRAN
    gran_end = jnp.cumsum(gran)
    gran_start = gran_end - gran
    g = jnp.arange(n_alloc // MOE_GRAN, dtype=jnp.int32)
    ge = _owner(gran_end, g)
    gran_valid = jnp.clip(_lookup(counts, ge) - (g - _lookup(gran_start, ge)) * MOE_GRAN, 0, MOE_GRAN)
    gran_valid = jnp.where(g < gran_end[-1], gran_valid, 0).astype(jnp.int32)
    items = (gran + per_item - 1) // per_item
    item_end = jnp.cumsum(items)
    w = jnp.arange(n_items_max, dtype=jnp.int32)
    ie = _owner(item_end, w)
    part = w - (_lookup(item_end, ie) - _lookup(items, ie))
    live = w < item_end[-1]
    item_ng = jnp.where(live, jnp.clip(_lookup(gran, ie) - part * per_item, 0, per_item), 0).astype(jnp.int32)
    item_row = jnp.where(live, (_lookup(gran_start, ie) + part * per_item) * MOE_GRAN, 0).astype(jnp.int32)
    return gran_start * MOE_GRAN, gran_valid, ie.astype(jnp.int32), item_row, item_ng, item_end[-1:].astype(jnp.int32)


def kernel(x, c, positions, w_mod, b_mod, w_in, q_gain, k_gain, cmp_pos_k, cmp_pos_v, cmp_w1_k, cmp_w2_k, cmp_w1_v, cmp_w2_v, gm_ln_g, gm_ln_b, gm_ws, gm_bs, w_proj_nsa, w_proj_gm, w_out, w_router, b_router, w_gate_e, w_up_e, w_down_e, w_gate_sh, w_up_sh, w_down_sh):
    B, S, D = x.shape
    L = w_mod.shape[0]
    T = B * S
    tm = 256
    tm_dense = 512
    scale = HEAD_DIM ** -0.5
    cos_t, sin_lo, sin_hi = _rope_tables(positions)
    mod = _mod_call(c, w_mod, b_mod)
    n_alloc = T * TOP_K + N_EXPERTS * MOE_GRAN + MOE_ROWS
    n_items_max = N_EXPERTS + T * TOP_K // MOE_ROWS + 1

    for l in range(L):
        sh_a, sc_a, gt_a, sh_f, sc_f, gt_f = [mod[l, :, i * D:(i + 1) * D].reshape(B, 1, D) for i in range(6)]
        qg = (jnp.tile(q_gain[l], NSA_Q_HEADS) * scale).reshape(1, Q_WIDTH)
        kg = jnp.tile(k_gain[l], 2 * NSA_KV_HEADS).reshape(1, 2 * KV_WIDTH)
        (qn, qr, ks, kw, vs, vw, kc_raw, vc_raw, gates, u, v, ga, gb) = _inproj_call(
            x, sc_a, sh_a, _reorder_w_in(w_in[l]).astype(BF16), qg, kg, cos_t, sin_lo, sin_hi,
            gm_ln_g[l].reshape(1, GM_WIDTH), gm_ln_b[l].reshape(1, GM_WIDTH), tm_dense)
        kc, vc = _compress_call(
            kc_raw, vc_raw, cmp_w1_k[l].astype(BF16), cmp_w2_k[l].astype(BF16), cmp_pos_k[l].reshape(1, -1),
            cmp_w1_v[l].astype(BF16), cmp_w2_v[l].astype(BF16), cmp_pos_v[l].reshape(1, -1),
            k_gain[l].reshape(1, HEAD_DIM))
        o_cmp, q_aug = _nsa_cmp_call(qn, qr, kc, vc, WINDOW // 2)
        o_nsa = _nsa_flash_call(q_aug, ks, vs, kw, vw, o_cmp, gates)
        bs_full = jnp.repeat(gm_bs[l].T, GM_GROUP_DIM, axis=1)
        o_gm = _gmlp_call(u, v, gm_ws[l], bs_full, 512)
        x1, h2, logits, shared = _mixout_call(
            o_nsa, o_gm, ga, gb, x, gt_a, sc_f, sh_f,
            w_proj_nsa[l].astype(BF16), w_proj_gm[l].astype(BF16), w_out[l].astype(BF16), w_router[l].astype(BF16),
            w_gate_sh[l].astype(BF16), w_up_sh[l].astype(BF16), w_down_sh[l].astype(BF16), tm_dense)
        idx, wts, rank, counts = _route_call(logits.reshape(T, N_EXPERTS), b_router[l], tm)
        row_start, gran_valid, item_e, item_row, item_ng, n_items = _expert_plan(counts[0], n_alloc, n_items_max)
        slots = _slot_call(idx, rank, row_start, tm).reshape(T * TOP_K)
        xs = _dispatch_call(gran_valid, slots, h2.reshape(T, D // 2), n_alloc, tm)
        y = _expert_call(l, item_e, item_row, item_ng, n_items, xs, w_gate_e, w_up_e, w_down_e)
        slots_by_k = slots.reshape(T, TOP_K).T.reshape(T * TOP_K)
        yk = _gather_rows(y, slots_by_k).reshape(TOP_K, T, D // 2)
        x = _combine_call(yk, wts, shared.reshape(T, D), x1.reshape(T, D), gt_f, tm).reshape(B, S, D)
    return x
```

```python
import functools

import jax
import jax.numpy as jnp
import numpy as np
from jax import lax
from jax.experimental import pallas as pl
from jax.experimental.pallas import tpu as pltpu
from jax.experimental.pallas import tpu_sc as plsc

D_MODEL = 1024
NSA_Q_HEADS = 8
NSA_KV_HEADS = 2
HEAD_DIM = 64
NSA_GROUP = NSA_Q_HEADS // NSA_KV_HEADS
CMP_LEN = 32
CMP_STRIDE = 16
CMP_HIDDEN = 256
SEL_LEN = 64
SEL_TOPN = 16
WINDOW = 512
ROPE_THETA = 10000.0
Q_WIDTH = NSA_Q_HEADS * HEAD_DIM
KV_WIDTH = NSA_KV_HEADS * HEAD_DIM
GM_GROUPS = 8
GM_GROUP_DIM = 64
GM_WIDTH = GM_GROUPS * GM_GROUP_DIM
GM_CHUNK = 128
N_EXPERTS = 256
TOP_K = 8
D_EXPERT = 256
D_SHARED = 256
ROUTE_SCALE = 2.5
EPS = 1e-6

LANES = 128
SEL_BIAS_WIDTH = 64
MASK_NEG = -1e30
SEL_NEG = -30000.0
SC_WINDOW = 128
MOE_GRAN = 128
MOE_ROWS = 9 * MOE_GRAN
VMEM_LIMIT = 56 * 1024 * 1024

F32 = jnp.float32
BF16 = jnp.bfloat16
HI = lax.Precision.HIGHEST


def _cparams(*sem):
    return pltpu.CompilerParams(dimension_semantics=sem, vmem_limit_bytes=VMEM_LIMIT)


def _dot(a, b, **kw):
    return jnp.dot(a, b, preferred_element_type=F32, **kw)


def _dot_nt(a, b, **kw):
    return lax.dot_general(a, b, (((1,), (1,)), ((), ())), preferred_element_type=F32, **kw)


def _gelu(x):
    return 0.5 * x * (1.0 + jnp.tanh(0.7978845608028654 * (x + 0.044715 * (x * x * x))))


def _sigmoid(x):
    return 1.0 / (1.0 + jnp.exp(-x))


def _silu(x):
    return x * _sigmoid(x)


_HI_MASK = np.uint32(0xFFFF0000)


def _pack_halves(a):
    w = a.shape[1] // 2
    bits = lax.bitcast_convert_type(a.astype(BF16).astype(F32), jnp.uint32)
    return (bits[:, w:] & _HI_MASK) | (bits[:, :w] >> 16)


def _unpack_halves(words):
    lo = lax.bitcast_convert_type(words << 16, F32)
    hi = lax.bitcast_convert_type(words & _HI_MASK, F32)
    return lo, hi


def _mod_kernel(c_ref, w_ref, b_ref, o_ref):
    c = c_ref[...]
    o_ref[0] = _dot(_silu(c), w_ref[0], precision=HI) + b_ref[0]


def _mod_call(c, w_mod, b_mod):
    L, D, N = w_mod.shape
    B = c.shape[0]
    tn = 1536
    return pl.pallas_call(
        _mod_kernel,
        grid=(L, N // tn),
        in_specs=[pl.BlockSpec((B, D), lambda l, j: (0, 0)),
                  pl.BlockSpec((1, D, tn), lambda l, j: (l, 0, j)),
                  pl.BlockSpec((1, 1, tn), lambda l, j: (l, 0, j))],
        out_specs=pl.BlockSpec((1, B, tn), lambda l, j: (l, 0, j)),
        out_shape=jax.ShapeDtypeStruct((L, B, N), F32),
        compiler_params=_cparams("arbitrary", "arbitrary"),
        name="mod",
    )(c, w_mod, b_mod.reshape(L, 1, N))


_C_Q = 0
_C_K = 512
_C_KC = 768
_C_VC = 896
_C_VS = 1024
_C_VW = 1152
_C_G = 1280
_C_U = 1536
_C_V = 2048
_C_GA = 2560
_C_GB = 3584
IN_COLS_P = 4608


def _dot_split(a, b):
    hi = a.astype(BF16)
    lo = (a - hi.astype(F32)).astype(BF16)
    return _dot(hi, b) + _dot(lo, b)


def _head_norm(z, bd):
    ms = _dot_split(z * z, bd)
    return z * lax.rsqrt(ms + EPS)


def _rope(z, cos, sin_lo, sin_hi):
    w = z.shape[-1]
    half = HEAD_DIM // 2
    return z * cos + pltpu.roll(z, w - half, 1) * sin_lo + pltpu.roll(z, half, 1) * sin_hi


def _tile_lanes(t, n):
    return t if n == 1 else jnp.concatenate([t] * n, axis=-1)


def _inproj_kernel(x_ref, sc_ref, sh_ref, w_ref, bdq_ref, bdk_ref, qg_ref, kg_ref,
                   cos_ref, sl_ref, shi_ref, lng_ref, lnb_ref,
                   qn_ref, qr_ref, ks_ref, kw_ref, vs_ref, vw_ref, kc_ref, vc_ref,
                   g_ref, u_ref, v_ref, ga_ref, gb_ref):
    tm = x_ref.shape[1]
    x = x_ref[0]
    ms = jnp.mean(x * x, axis=-1, keepdims=True)
    h = (x * lax.rsqrt(ms + EPS)) * (1.0 + sc_ref[0]) + sh_ref[0]
    hb = h.astype(BF16)

    def mm(lo, width):
        return _dot(hb, w_ref[:, lo:lo + width])

    cos, sl, shi = cos_ref[0], sl_ref[0], shi_ref[0]

    zq = mm(_C_Q, Q_WIDTH)
    qn = _head_norm(zq, bdq_ref[...]) * qg_ref[...]
    qr = _rope(qn, _tile_lanes(cos, 4), _tile_lanes(sl, 4), _tile_lanes(shi, 4))
    qn_ref[0] = qn.astype(BF16)
    qr_ref[0] = qr.astype(BF16)

    zk = mm(_C_K, 2 * KV_WIDTH)
    kn = _head_norm(zk, bdk_ref[...]) * kg_ref[...]
    kr = _rope(kn, _tile_lanes(cos, 2), _tile_lanes(sl, 2), _tile_lanes(shi, 2))
    lane = lax.broadcasted_iota(jnp.int32, (tm, LANES), 1)
    tok = pl.program_id(1) * tm + lax.broadcasted_iota(jnp.int32, (tm, LANES), 0)
    onehot = jnp.where(lane - HEAD_DIM == tok // SEL_LEN, 1.0, 0.0)
    ones_col = jnp.where(lane == HEAD_DIM, 1.0, 0.0)
    low = lane < HEAD_DIM
    zvs = mm(_C_VS, KV_WIDTH)
    zvw = mm(_C_VW, KV_WIDTH)
    zkc = mm(_C_KC, KV_WIDTH)
    zvc = mm(_C_VC, KV_WIDTH)
    for kv in range(NSA_KV_HEADS):
        def head(a):
            return a if kv == 0 else pltpu.roll(a, HEAD_DIM, 1)
        ks_ref[0, kv] = jnp.where(low, head(kr[:, :KV_WIDTH]), onehot).astype(BF16)
        kw_ref[0, kv] = jnp.where(low, head(kr[:, KV_WIDTH:]), 0.0).astype(BF16)
        vs_ref[0, kv] = jnp.where(low, head(zvs), ones_col).astype(BF16)
        vw_ref[0, kv] = jnp.where(low, head(zvw), ones_col).astype(BF16)
        kc_ref[0, kv] = head(zkc)[:, :HEAD_DIM]
        vc_ref[0, kv] = head(zvc)[:, :HEAD_DIM]

    zg = mm(_C_G, 2 * LANES)
    sg = _sigmoid(zg)
    g_ref[0, 0] = sg[:, :LANES]
    g_ref[0, 1] = sg[:, LANES:]

    u_ref[0] = _gelu(mm(_C_U, GM_WIDTH)).astype(BF16)
    gv = _gelu(mm(_C_V, GM_WIDTH))
    mu = jnp.mean(gv, axis=-1, keepdims=True)
    cen = gv - mu
    var = jnp.mean(cen * cen, axis=-1, keepdims=True)
    v_ref[0] = ((cen * lax.rsqrt(var + EPS)) * lng_ref[...] + lnb_ref[...]).astype(BF16)

    ga_ref[0] = _sigmoid(mm(_C_GA, D_MODEL)).astype(BF16)
    gb_ref[0] = _sigmoid(mm(_C_GB, D_MODEL)).astype(BF16)


def _block_diag_mean(width):
    idx = np.arange(width) // HEAD_DIM
    return jnp.asarray((idx[:, None] == idx[None, :]).astype(np.float32) / HEAD_DIM).astype(BF16)


def _inproj_call(x, sc, sh, w_p, qg, kg, cos, sl, shi, lng, lnb, tm):
    B, S, D = x.shape
    H = NSA_KV_HEADS
    full = lambda *shape: pl.BlockSpec(shape, lambda b, i: (0,) * len(shape))
    tok3 = lambda w: pl.BlockSpec((1, tm, w), lambda b, i: (b, i, 0))
    per_b = pl.BlockSpec((1, 1, D), lambda b, i: (b, 0, 0))
    kv4 = lambda w: pl.BlockSpec((1, H, tm, w), lambda b, i: (b, 0, i, 0))
    sds = jax.ShapeDtypeStruct
    out_shape = [
        sds((B, S, Q_WIDTH), BF16), sds((B, S, Q_WIDTH), BF16),
        sds((B, H, S, LANES), BF16), sds((B, H, S, LANES), BF16),
        sds((B, H, S, LANES), BF16), sds((B, H, S, LANES), BF16),
        sds((B, H, S, HEAD_DIM), F32), sds((B, H, S, HEAD_DIM), F32),
        sds((B, H, S, LANES), F32),
        sds((B, S, GM_WIDTH), BF16), sds((B, S, GM_WIDTH), BF16),
        sds((B, S, D), BF16), sds((B, S, D), BF16),
    ]
    out_specs = [
        tok3(Q_WIDTH), tok3(Q_WIDTH), kv4(LANES), kv4(LANES), kv4(LANES), kv4(LANES),
        kv4(HEAD_DIM), kv4(HEAD_DIM), kv4(LANES),
        tok3(GM_WIDTH), tok3(GM_WIDTH), tok3(D), tok3(D),
    ]
    return pl.pallas_call(
        _inproj_kernel,
        grid=(B, S // tm),
        in_specs=[tok3(D), per_b, per_b, full(D, IN_COLS_P),
                  full(Q_WIDTH, Q_WIDTH), full(2 * KV_WIDTH, 2 * KV_WIDTH),
                  full(1, Q_WIDTH), full(1, 2 * KV_WIDTH),
                  tok3(LANES), tok3(LANES), tok3(LANES),
                  full(1, GM_WIDTH), full(1, GM_WIDTH)],
        out_specs=out_specs,
        out_shape=out_shape,
        compiler_params=_cparams("arbitrary", "arbitrary"),
        name="in_proj",
    )(x, sc, sh, w_p, _block_diag_mean(Q_WIDTH), _block_diag_mean(2 * KV_WIDTH), qg, kg,
      cos, sl, shi, lng, lnb)


def _compress_kernel(kr_ref, vr_ref, w1k_ref, w2k_ref, pek_ref, w1v_ref, w2v_ref, pev_ref,
                     kg_ref, kc_ref, vc_ref):
    nc = kr_ref.shape[2]
    half = CMP_STRIDE * HEAD_DIM

    def mlp(raw, w1_ref, w2_ref, pe_ref):
        a = raw.astype(BF16)
        top = _dot(a, w1_ref[:half, :])
        bot = _dot(a, w1_ref[half:, :])
        pe = jnp.broadcast_to(pe_ref[...], (8, 2 * half)).astype(BF16)
        pe_row = _dot(pe, w1_ref[...])[0:1, :]
        hid = top + pltpu.roll(bot, nc - 1, 0) + pe_row
        return _dot(_gelu(hid).astype(BF16), w2_ref[...])

    kc = mlp(kr_ref[0, 0], w1k_ref, w2k_ref, pek_ref)
    ms = jnp.mean(kc * kc, axis=-1, keepdims=True)
    kc_ref[0, 0] = (kc * lax.rsqrt(ms + EPS) * kg_ref[...]).astype(BF16)
    vc_ref[0, 0] = mlp(vr_ref[0, 0], w1v_ref, w2v_ref, pev_ref).astype(BF16)


def _compress_call(kc_raw, vc_raw, w1k, w2k, pek, w1v, w2v, pev, kg):
    B, H, S, hd = kc_raw.shape
    nc = S // CMP_STRIDE
    feat = CMP_STRIDE * hd
    raw = pl.BlockSpec((1, 1, nc, feat), lambda b, h: (b, h, 0, 0))
    full = lambda *shape: pl.BlockSpec(shape, lambda b, h: (0,) * len(shape))
    out = pl.BlockSpec((1, 1, nc, hd), lambda b, h: (b, h, 0, 0))
    return pl.pallas_call(
        _compress_kernel,
        grid=(B, H),
        in_specs=[raw, raw, full(2 * feat, CMP_HIDDEN), full(CMP_HIDDEN, hd), full(1, 2 * feat),
                  full(2 * feat, CMP_HIDDEN), full(CMP_HIDDEN, hd), full(1, 2 * feat),
                  full(1, hd)],
        out_specs=[out, out],
        out_shape=[jax.ShapeDtypeStruct((B, H, nc, hd), BF16)] * 2,
        compiler_params=_cparams("arbitrary", "arbitrary"),
        name="compress",
    )(kc_raw.reshape(B, H, nc, feat), vc_raw.reshape(B, H, nc, feat),
      w1k, w2k, pek, w1v, w2v, pev, kg)


def _group_rows(a):
    return jnp.concatenate([a[:, g * HEAD_DIM:(g + 1) * HEAD_DIM] for g in range(NSA_GROUP)], axis=0)


def _nsa_cmp_kernel(qn_ref, qr_ref, kc_ref, vc_ref, ovl_ref, oc_ref, qa_ref, *, n_sel):
    tq = qn_ref.shape[1]
    nc = kc_ref.shape[2]
    G = NSA_GROUP
    q0 = pl.program_id(2) * tq
    q4 = _group_rows(qn_ref[0])
    s = _dot_nt(q4, kc_ref[0, 0])
    row = lax.broadcasted_iota(jnp.int32, (G, tq, nc), 1).reshape(G * tq, nc)
    col = lax.broadcasted_iota(jnp.int32, (G * tq, nc), 1)
    vis = col * CMP_STRIDE + (CMP_LEN - 1) <= q0 + row
    s = jnp.where(vis, s, MASK_NEG)
    m = jnp.max(s, axis=-1, keepdims=True)
    e = jnp.where(vis, jnp.exp(s - m), 0.0)
    p = e / jnp.maximum(jnp.sum(e, axis=-1, keepdims=True), 1e-30)
    oc = _dot(p.astype(BF16), vc_ref[0, 0])
    oc_ref[0, 0] = oc.reshape(G, tq, HEAD_DIM)

    psum = p[0:tq] + p[tq:2 * tq] + p[2 * tq:3 * tq] + p[3 * tq:4 * tq]
    p_hi = psum.astype(BF16)
    p_lo = (psum - p_hi.astype(F32)).astype(BF16)
    imp = (_dot_nt(ovl_ref[...], p_hi) + _dot_nt(ovl_ref[...], p_lo))[:SEL_BIAS_WIDTH]
    blk = lax.broadcasted_iota(jnp.int32, (SEL_BIAS_WIDTH, tq), 0)
    cur = (q0 + lax.broadcasted_iota(jnp.int32, (SEL_BIAS_WIDTH, tq), 1)) // SEL_LEN
    valid = blk <= cur
    forced = (blk == 0) | (blk == cur) | (blk == cur - 1)
    cand = valid & jnp.logical_not(forced)
    n_forced = jnp.minimum(cur, 2) + 1
    val = jnp.where(cand, imp, -1.0)
    cnt = jnp.zeros((SEL_BIAS_WIDTH, tq), F32)
    for j in range(n_sel):
        vj = jnp.broadcast_to(val[j:j + 1, :], (SEL_BIAS_WIDTH, tq))
        cnt = cnt + jnp.where(blk > j, jnp.where(vj >= val, 1.0, 0.0), jnp.where(vj > val, 1.0, 0.0))
    free = (min(SEL_TOPN, n_sel) - n_forced).astype(F32)
    sel = (forced & valid) | (cand & (cnt < free))
    sel_t = jnp.concatenate([jnp.where(sel, 1.0, 0.0), jnp.zeros((LANES - SEL_BIAS_WIDTH, tq), F32)], axis=0)
    sel_q = sel_t.T
    bias = jnp.where(sel_q > 0.5, 0.0, SEL_NEG)
    lane = lax.broadcasted_iota(jnp.int32, (tq, LANES), 1)
    bias_hi = pltpu.roll(bias, SEL_BIAS_WIDTH, 1)
    qr = qr_ref[0]
    for g in range(G):
        qg = qr[:, g * HEAD_DIM:(g + 1) * HEAD_DIM].astype(F32)
        qg = jnp.concatenate([qg, qg], axis=-1)
        qa_ref[0, 0, g] = jnp.where(lane < HEAD_DIM, qg, bias_hi).astype(BF16)


def _overlap_t(S):
    n_cmp = (S - CMP_LEN) // CMP_STRIDE + 1
    nc = S // CMP_STRIDE
    n_sel = S // SEL_LEN
    start = np.arange(nc) * CMP_STRIDE
    end = start + CMP_LEN - 1
    sel_start = np.arange(n_sel) * SEL_LEN
    ov = (start[None, :] <= sel_start[:, None] + SEL_LEN - 1) & (end[None, :] >= sel_start[:, None])
    ov = ov & (np.arange(nc) < n_cmp)[None, :]
    out = np.zeros((LANES, nc), np.float32)
    out[:n_sel] = ov.astype(np.float32)
    return jnp.asarray(out).astype(BF16)


def _nsa_cmp_call(qn, qr, kc, vc, tq):
    B, S, _ = qn.shape
    H, G = NSA_KV_HEADS, NSA_GROUP
    nc = kc.shape[2]
    n_sel = S // SEL_LEN
    assert n_sel <= SEL_BIAS_WIDTH and nc % LANES == 0
    qspec = pl.BlockSpec((1, tq, G * HEAD_DIM), lambda b, h, i: (b, i, h))
    cspec = pl.BlockSpec((1, 1, nc, HEAD_DIM), lambda b, h, i: (b, h, 0, 0))
    return pl.pallas_call(
        functools.partial(_nsa_cmp_kernel, n_sel=n_sel),
        grid=(B, H, S // tq),
        in_specs=[qspec, qspec, cspec, cspec, pl.BlockSpec((LANES, nc), lambda b, h, i: (0, 0))],
        out_specs=[pl.BlockSpec((1, 1, G, tq, HEAD_DIM), lambda b, h, i: (b, h, 0, i, 0)),
                   pl.BlockSpec((1, 1, G, tq, LANES), lambda b, h, i: (b, h, 0, i, 0))],
        out_shape=[jax.ShapeDtypeStruct((B, H, G, S, HEAD_DIM), F32),
                   jax.ShapeDtypeStruct((B, H, G, S, LANES), BF16)],
        compiler_params=_cparams("arbitrary", "arbitrary", "arbitrary"),
        name="nsa_cmp",
    )(qn, qr, kc, vc, _overlap_t(S))


SEL_CHUNK = 4


def _nsa_flash_kernel(qa_ref, ks_ref, vs_ref, kw_ref, vw_ref, oc_ref, g_ref, o_ref, m_scr, acc_scr):
    G = NSA_GROUP
    tq = qa_ref.shape[3]
    R = G * tq
    tk = tq
    i = pl.program_id(2)
    q0 = i * tq

    def rows_of(ref, j0, nt):
        return ref[0, 0, pl.ds(pl.multiple_of(j0 * tk, tk), nt * tk), :]

    def attend(k_ref, v_ref, j0, nt, visible, state):
        k, v = rows_of(k_ref, j0, nt), rows_of(v_ref, j0, nt)
        ss = [_dot_nt(qa_ref[0, 0, g], k) for g in range(G)]
        if visible is not None:
            qpos = q0 + lax.broadcasted_iota(jnp.int32, (tq, nt * tk), 0)
            kpos = j0 * tk + lax.broadcasted_iota(jnp.int32, (tq, nt * tk), 1)
            mask = visible(qpos, kpos)
        ps, alphas = [], []
        for g in range(G):
            s = ss[g] if visible is None else jnp.where(mask, ss[g], MASK_NEG)
            cols = [s[:, c * LANES:(c + 1) * LANES] for c in range(nt * tk // LANES)]
            m_new = jnp.max(functools.reduce(jnp.maximum, cols), axis=-1, keepdims=True)
            if state:
                m_prev = m_scr[g]
                m_new = jnp.maximum(m_prev, m_new)
                alphas.append(jnp.exp(m_prev - m_new))
                m_scr[g] = m_new
            ps.append(jnp.concatenate([jnp.exp(c - m_new) for c in cols], axis=-1).astype(BF16))
        if not state:
            return [_dot(ps[g], v) for g in range(G)]
        for g in range(G):
            acc_scr[g] = alphas[g] * acc_scr[g] + _dot(ps[g], v)

    def normalise(acc):
        return acc[:, :HEAD_DIM] / acc[:, HEAD_DIM:HEAD_DIM + 1]

    m_scr[...] = jnp.full((G, tq, LANES), MASK_NEG, F32)
    acc_scr[...] = jnp.zeros((G, tq, LANES), F32)
    causal = lambda qpos, kpos: kpos <= qpos

    def sel_body(c, carry):
        attend(ks_ref, vs_ref, c * SEL_CHUNK, SEL_CHUNK, None, True)
        return carry

    n_full = i // SEL_CHUNK
    lax.fori_loop(0, n_full, sel_body, 0)
    attend(ks_ref, vs_ref, n_full * SEL_CHUNK, 2, causal, True)

    @pl.when(i % SEL_CHUNK >= 2)
    def _():
        attend(ks_ref, vs_ref, n_full * SEL_CHUNK + 2, 2, causal, True)

    o_sel = normalise(acc_scr[...].reshape(R, LANES))

    assert WINDOW == 2 * tk
    band = lambda qpos, kpos: (kpos <= qpos) & (kpos > qpos - WINDOW)
    o_win = normalise(jnp.concatenate(attend(kw_ref, vw_ref, jnp.maximum(i - 2, 0), 3, band, False), axis=0))

    o_cmp = oc_ref[0, 0].reshape(R, HEAD_DIM)
    gates = g_ref[0, 0]
    outs = []
    for g in range(G):
        r = slice(g * tq, (g + 1) * tq)
        outs.append(gates[:, 3 * g:3 * g + 1] * o_cmp[r]
                    + gates[:, 3 * g + 1:3 * g + 2] * o_sel[r]
                    + gates[:, 3 * g + 2:3 * g + 3] * o_win[r])
    o_ref[0] = jnp.concatenate(outs, axis=-1).astype(BF16)


def _nsa_flash_call(qa, ks, vs, kw, vw, oc, gates):
    B, H, G, S, _ = qa.shape
    tq = WINDOW // 2
    assert (S // tq) % SEL_CHUNK == 0
    kv = pl.BlockSpec((1, 1, S, LANES), lambda b, h, i: (b, h, 0, 0))
    return pl.pallas_call(
        _nsa_flash_kernel,
        grid=(B, H, S // tq),
        in_specs=[pl.BlockSpec((1, 1, G, tq, LANES), lambda b, h, i: (b, h, 0, i, 0)),
                  kv, kv, kv, kv,
                  pl.BlockSpec((1, 1, G, tq, HEAD_DIM), lambda b, h, i: (b, h, 0, i, 0)),
                  pl.BlockSpec((1, 1, tq, LANES), lambda b, h, i: (b, h, i, 0))],
        out_specs=pl.BlockSpec((1, tq, G * HEAD_DIM), lambda b, h, i: (b, i, h)),
        out_shape=jax.ShapeDtypeStruct((B, S, Q_WIDTH), BF16),
        scratch_shapes=[pltpu.VMEM((G, tq, LANES), F32), pltpu.VMEM((G, tq, LANES), F32)],
        compiler_params=_cparams("arbitrary", "arbitrary", "arbitrary"),
        name="nsa_flash",
    )(qa, ks, vs, kw, vw, oc, gates)


def _gmlp_kernel(u_ref, v_ref, ws_ref, bs_ref, o_ref):
    tm = u_ref.shape[1]
    C = GM_CHUNK
    r = lax.broadcasted_iota(jnp.int32, (C, C), 0)
    c = lax.broadcasted_iota(jnp.int32, (C, C), 1)
    causal = c <= r
    ws = [jnp.where(causal, ws_ref[g], 0.0).astype(BF16) for g in range(GM_GROUPS)]
    for n in range(tm // C):
        rows = slice(n * C, (n + 1) * C)
        vn = v_ref[0, rows, :]
        mixed = jnp.concatenate(
            [_dot(ws[g], vn[:, g * GM_GROUP_DIM:(g + 1) * GM_GROUP_DIM]) for g in range(GM_GROUPS)],
            axis=-1)
        o_ref[0, rows, :] = (u_ref[0, rows, :].astype(F32) * (mixed + bs_ref[...])).astype(BF16)


def _gmlp_call(u, v, ws, bs_full, tm):
    B, S, W = u.shape
    tok = pl.BlockSpec((1, tm, W), lambda b, i: (b, i, 0))
    return pl.pallas_call(
        _gmlp_kernel,
        grid=(B, S // tm),
        in_specs=[tok, tok,
                  pl.BlockSpec((GM_GROUPS, GM_CHUNK, GM_CHUNK), lambda b, i: (0, 0, 0)),
                  pl.BlockSpec((GM_CHUNK, W), lambda b, i: (0, 0))],
        out_specs=tok,
        out_shape=jax.ShapeDtypeStruct((B, S, W), BF16),
        compiler_params=_cparams("arbitrary", "arbitrary"),
        name="gmlp",
    )(u, v, ws, bs_full)


def _mixout_kernel(on_ref, og_ref, ga_ref, gb_ref, x_ref, gt_ref, sc_ref, sh_ref,
                   wpn_ref, wpg_ref, wo_ref, wr_ref, wgs_ref, wus_ref, wds_ref,
                   x1_ref, h2_ref, lg_ref, shd_ref):
    ya = _dot(on_ref[0], wpn_ref[...])
    yb = _dot(og_ref[0], wpg_ref[...])
    merged = ga_ref[0].astype(F32) * ya + gb_ref[0].astype(F32) * yb
    x1 = x_ref[0] + gt_ref[0] * _dot(merged.astype(BF16), wo_ref[...])
    x1_ref[0] = x1
    ms = jnp.mean(x1 * x1, axis=-1, keepdims=True)
    h2 = (x1 * lax.rsqrt(ms + EPS)) * (1.0 + sc_ref[0]) + sh_ref[0]
    hb = h2.astype(BF16)
    lg_ref[0] = _dot(hb, wr_ref[...])
    h2_ref[0] = _pack_halves(hb)
    act = _silu(_dot(hb, wgs_ref[...])) * _dot(hb, wus_ref[...])
    shd_ref[0] = _dot(act.astype(BF16), wds_ref[...])


def _mixout_call(o_nsa, o_gm, ga, gb, x, gt, sc, sh, wpn, wpg, wo, wr, wgs, wus, wds, tm):
    B, S, D = x.shape
    tok = lambda w: pl.BlockSpec((1, tm, w), lambda b, i: (b, i, 0))
    per_b = pl.BlockSpec((1, 1, D), lambda b, i: (b, 0, 0))
    full = lambda a: pl.BlockSpec(a.shape, lambda b, i: (0,) * a.ndim)
    sds = jax.ShapeDtypeStruct
    return pl.pallas_call(
        _mixout_kernel,
        grid=(B, S // tm),
        in_specs=[tok(Q_WIDTH), tok(GM_WIDTH), tok(D), tok(D), tok(D), per_b, per_b, per_b,
                  full(wpn), full(wpg), full(wo), full(wr), full(wgs), full(wus), full(wds)],
        out_specs=[tok(D), tok(D // 2), tok(N_EXPERTS), tok(D)],
        out_shape=[sds((B, S, D), F32), sds((B, S, D // 2), jnp.uint32), sds((B, S, N_EXPERTS), F32),
                   sds((B, S, D), F32)],
        compiler_params=_cparams("arbitrary", "arbitrary"),
        name="mix_out",
    )(o_nsa, o_gm, ga, gb, x, gt, sc, sh, wpn, wpg, wo, wr, wgs, wus, wds)


def _route_kernel(lg_ref, br_ref, idx_ref, wt_ref, rank_ref, cnt_ref, run_scr):
    tm, E = lg_ref.shape

    @pl.when(pl.program_id(0) == 0)
    def _():
        run_scr[...] = jnp.zeros_like(run_scr)

    aff = _sigmoid(lg_ref[...])
    work = aff + br_ref[...]
    lane = lax.broadcasted_iota(jnp.int32, (tm, E), 1).astype(F32)
    picked = jnp.zeros((tm, E), F32)
    idxs, tops = [], []
    for _ in range(TOP_K):
        m = jnp.max(work, axis=-1, keepdims=True)
        idx = jnp.min(jnp.where(work == m, lane, float(E)), axis=-1, keepdims=True)
        hit = lane == idx
        tops.append(jnp.sum(jnp.where(hit, aff, 0.0), axis=-1, keepdims=True))
        idxs.append(idx)
        picked = jnp.where(hit, 1.0, picked)
        work = jnp.where(hit, -jnp.inf, work)
    total = tops[0]
    for t in tops[1:]:
        total = total + t
    r = lax.broadcasted_iota(jnp.int32, (tm, tm), 0)
    c = lax.broadcasted_iota(jnp.int32, (tm, tm), 1)
    before = _dot(jnp.where(c < r, 1.0, 0.0).astype(BF16), picked.astype(BF16)) + run_scr[...]
    ranks = [jnp.sum(jnp.where(lane == idx, before, 0.0), axis=-1, keepdims=True) for idx in idxs]
    run_scr[...] = run_scr[...] + jnp.sum(picked, axis=0, keepdims=True)
    cnt_ref[...] = run_scr[...]
    lane_k = lax.broadcasted_iota(jnp.int32, (tm, LANES), 1)
    idx_o = jnp.zeros((tm, LANES), jnp.int32)
    wt_o = jnp.zeros((tm, LANES), F32)
    rank_o = jnp.zeros((tm, LANES), jnp.int32)
    for k in range(TOP_K):
        idx_o = jnp.where(lane_k == k, idxs[k].astype(jnp.int32), idx_o)
        wt_o = jnp.where(lane_k == k, tops[k] / total * ROUTE_SCALE, wt_o)
        rank_o = jnp.where(lane_k == k, ranks[k].astype(jnp.int32), rank_o)
    idx_ref[...] = idx_o
    wt_ref[...] = wt_o
    rank_ref[...] = rank_o


def _route_call(logits, b_router, tm):
    T, E = logits.shape
    tok = lambda w: pl.BlockSpec((tm, w), lambda i: (i, 0))
    one = pl.BlockSpec((1, E), lambda i: (0, 0))
    sds = jax.ShapeDtypeStruct
    return pl.pallas_call(
        _route_kernel,
        grid=(T // tm,),
        in_specs=[tok(E), one],
        out_specs=[tok(LANES), tok(LANES), tok(LANES), one],
        out_shape=[sds((T, LANES), jnp.int32), sds((T, LANES), F32), sds((T, LANES), jnp.int32),
                   sds((1, E), F32)],
        scratch_shapes=[pltpu.VMEM((1, E), F32)],
        compiler_params=_cparams("arbitrary"),
        name="route",
    )(logits, b_router.reshape(1, E))


def _slot_kernel(idx_ref, rank_ref, ps_ref, o_ref):
    tm = idx_ref.shape[0]
    E = ps_ref.shape[1]
    lane = lax.broadcasted_iota(jnp.int32, (tm, E), 1)
    lane_k = lax.broadcasted_iota(jnp.int32, (tm, LANES), 1)
    idx, rank, ps = idx_ref[...], rank_ref[...], ps_ref[...]
    out = jnp.zeros((tm, LANES), jnp.int32)
    for k in range(TOP_K):
        base = jnp.sum(jnp.where(lane == idx[:, k:k + 1], ps, 0.0), axis=-1, keepdims=True)
        out = jnp.where(lane_k == k, base.astype(jnp.int32) + rank[:, k:k + 1], out)
    o_ref[...] = out[:, :TOP_K]


def _slot_call(idx, rank, pad_start, tm):
    T = idx.shape[0]
    E = pad_start.shape[0]
    tok = pl.BlockSpec((tm, LANES), lambda i: (i, 0))
    return pl.pallas_call(
        _slot_kernel,
        grid=(T // tm,),
        in_specs=[tok, tok, pl.BlockSpec((1, E), lambda i: (0, 0))],
        out_specs=pl.BlockSpec((tm, TOP_K), lambda i: (i, 0)),
        out_shape=jax.ShapeDtypeStruct((T, TOP_K), jnp.int32),
        compiler_params=_cparams("arbitrary"),
        name="slot",
    )(idx, rank, pad_start.astype(F32).reshape(1, E))


def _sc_mesh():
    mesh = plsc.VectorSubcoreMesh(core_axis_name="core", subcore_axis_name="subcore")
    return mesh, mesh.num_cores * mesh.num_subcores


def _sc_worker(mesh):
    return lax.axis_index("subcore") * mesh.num_cores + lax.axis_index("core")


def _scatter_rows(rows, indices_by_k, out_ref):
    n, width = rows.shape
    mesh, workers = _sc_mesh()
    per_worker = n // workers
    assert per_worker * workers == n and per_worker % SC_WINDOW == 0

    @pl.kernel(out_type=(), mesh=mesh, name="scatter_rows",
               scratch_types=[pltpu.VMEM((SC_WINDOW,), jnp.int32), pltpu.VMEM((SC_WINDOW, width), rows.dtype)])
    def scatter(rows_hbm, idx_hbm, out_hbm, idx_v, rows_v):
        worker = _sc_worker(mesh)

        @pl.loop(0, per_worker // SC_WINDOW)
        def _(j):
            base = pl.multiple_of(worker * per_worker + j * SC_WINDOW, SC_WINDOW)
            pltpu.sync_copy(rows_hbm.at[pl.ds(base, SC_WINDOW)], rows_v)
            for k in range(TOP_K):
                pltpu.sync_copy(idx_hbm.at[pl.ds(pl.multiple_of(k * n + base, SC_WINDOW), SC_WINDOW)], idx_v)
                pltpu.sync_copy(rows_v, out_hbm.at[idx_v])

    scatter(rows, indices_by_k, out_ref)


def _expert_kernel(ie_ref, ir_ref, ig_ref, n_ref, x_hbm, wg_ref, wu_ref, wd_ref, y_hbm,
                   xbuf, ybuf, act_scr, xsem, ysem):
    w = pl.program_id(0)
    n = n_ref[0]
    R = xbuf.shape[1]
    last = ie_ref.shape[0] - 1

    def x_copy(item, slot):
        row = pl.multiple_of(ir_ref[jnp.minimum(item, last)], MOE_GRAN)
        return pltpu.make_async_copy(x_hbm.at[pl.ds(row, R), :], xbuf.at[slot], xsem.at[slot])

    def y_copy(item, slot, gi):
        row = pl.multiple_of(ir_ref[jnp.minimum(item, last)] + gi * MOE_GRAN, MOE_GRAN)
        src = ybuf.at[slot, pl.ds(pl.multiple_of(gi * MOE_GRAN, MOE_GRAN), MOE_GRAN), :]
        return pltpu.make_async_copy(src, y_hbm.at[pl.ds(row, MOE_GRAN), :], ysem.at[slot])

    def for_granules(item, fn):
        def body(gi, carry):
            fn(gi)
            return carry
        lax.fori_loop(0, ig_ref[jnp.clip(item, 0, last)], body, 0)

    def gate_up(slot):
        lo, hi = _unpack_halves(xbuf[slot])
        x = jnp.concatenate([lo, hi], axis=1).astype(BF16)
        a = _dot(x, wg_ref[0, 0].astype(BF16))
        b = _dot(x, wu_ref[0, 0].astype(BF16))
        return (_silu(a) * b).astype(BF16)

    def down(act):
        return _pack_halves(_dot(act, wd_ref[0, 0].astype(BF16)))

    cur, prv = w % 2, (w + 1) % 2

    @pl.when(w == 0)
    def _():
        x_copy(0, 0).start()

    @pl.when(w < n)
    def _():
        x_copy(w, cur).wait()

    @pl.when(w + 1 < n)
    def _():
        x_copy(w + 1, prv).start()

    @pl.when((w >= 3) & (w - 3 < n))
    def _():
        for_granules(w - 3, lambda gi: y_copy(w - 3, prv, gi).wait())

    @pl.when(w == 0)
    def _():
        act_scr[...] = gate_up(0)

    @pl.when((w >= 1) & (w < n))
    def _():
        prev = act_scr[...]
        ybuf[prv] = down(prev)
        act_scr[...] = gate_up(cur)

    @pl.when((w >= 1) & (w == n))
    def _():
        ybuf[prv] = down(act_scr[...])

    @pl.when((w >= 1) & (w <= n))
    def _():
        for_granules(w - 1, lambda gi: y_copy(w - 1, prv, gi).start())


def _expert_call(layer, item_e, item_row, item_ng, n_items, xs, wg, wu, wd):
    n_alloc, W = xs.shape
    R = MOE_ROWS
    nw = item_e.shape[0]
    D, F = wg.shape[2], wg.shape[3]
    cur = lambda w, ie, ir, ig, n: (layer, ie[jnp.minimum(w, nw - 1)], 0, 0)
    prev = lambda w, ie, ir, ig, n: (layer, ie[jnp.clip(w - 1, 0, nw - 1)], 0, 0)
    grid_spec = pltpu.PrefetchScalarGridSpec(
        num_scalar_prefetch=4,
        grid=(nw + 3,),
        in_specs=[pl.BlockSpec(memory_space=pl.ANY),
                  pl.BlockSpec((1, 1, D, F), cur), pl.BlockSpec((1, 1, D, F), cur),
                  pl.BlockSpec((1, 1, F, D), prev)],
        out_specs=pl.BlockSpec(memory_space=pl.ANY),
        scratch_shapes=[pltpu.VMEM((2, R, W), jnp.uint32), pltpu.VMEM((2, R, W), jnp.uint32),
                        pltpu.VMEM((R, F), BF16),
                        pltpu.SemaphoreType.DMA((2,)), pltpu.SemaphoreType.DMA((2,))],
    )
    return pl.pallas_call(
        _expert_kernel,
        grid_spec=grid_spec,
        out_shape=jax.ShapeDtypeStruct((n_alloc, W), jnp.uint32),
        input_output_aliases={4: 0},
        compiler_params=pltpu.CompilerParams(dimension_semantics=("arbitrary",), vmem_limit_bytes=VMEM_LIMIT,
                                             has_side_effects=True),
        name="expert",
    )(item_e, item_row, item_ng, n_items, xs, wg, wu, wd)


def _gather_rows(table, indices):
    n = indices.shape[0]
    width = table.shape[1]
    mesh, workers = _sc_mesh()
    per_worker = n // workers
    assert per_worker * workers == n and per_worker % SC_WINDOW == 0

    @pl.kernel(out_type=jax.ShapeDtypeStruct((n, width), table.dtype), mesh=mesh, name="gather_rows",
               scratch_types=[pltpu.VMEM((SC_WINDOW,), jnp.int32), pltpu.VMEM((SC_WINDOW, width), table.dtype),
                              pltpu.SemaphoreType.DMA])
    def gather(table_hbm, idx_hbm, out_hbm, idx_v, rows_v, sem):
        worker = _sc_worker(mesh)

        @pl.loop(0, per_worker // SC_WINDOW)
        def _(j):
            base = pl.multiple_of(worker * per_worker + j * SC_WINDOW, SC_WINDOW)
            pltpu.sync_copy(idx_hbm.at[pl.ds(base, SC_WINDOW)], idx_v)
            pltpu.async_copy(table_hbm.at[idx_v], rows_v, sem).wait()
            pltpu.sync_copy(rows_v, out_hbm.at[pl.ds(base, SC_WINDOW)])

    return gather(table, indices)


def _combine_kernel(y_ref, wt_ref, shd_ref, x_ref, gt_ref, o_ref):
    wt = wt_ref[...]
    acc_lo = acc_hi = None
    for k in range(TOP_K):
        lo, hi = _unpack_halves(y_ref[k])
        w = wt[:, k:k + 1]
        acc_lo = w * lo if k == 0 else acc_lo + w * lo
        acc_hi = w * hi if k == 0 else acc_hi + w * hi
    routed = jnp.concatenate([acc_lo, acc_hi], axis=1)
    o_ref[...] = x_ref[...] + gt_ref[0] * (routed + shd_ref[...])


def _combine_call(yk, wts, shared, x1, gt, tm):
    T, D = x1.shape
    B = gt.shape[0]
    per_b = T // B // tm
    tok = lambda w: pl.BlockSpec((tm, w), lambda i: (i, 0))
    return pl.pallas_call(
        _combine_kernel,
        grid=(T // tm,),
        in_specs=[pl.BlockSpec((TOP_K, tm, yk.shape[2]), lambda i: (0, i, 0)),
                  tok(LANES), tok(D), tok(D),
                  pl.BlockSpec((1, 1, D), lambda i: (i // per_b, 0, 0))],
        out_specs=tok(D),
        out_shape=jax.ShapeDtypeStruct((T, D), F32),
        compiler_params=_cparams("arbitrary"),
        name="combine",
    )(yk, wts, shared, x1, gt)


def _rope_tables(positions):
    half = HEAD_DIM // 2
    inv = ROPE_THETA ** (-jnp.arange(half, dtype=F32) / half)
    ang = positions.astype(F32)[..., None] * inv
    cos, sin, zero = jnp.cos(ang), jnp.sin(ang), jnp.zeros_like(ang)
    cos_t = jnp.concatenate([cos, cos] * 2, axis=-1)
    sin_lo = jnp.concatenate([-sin, zero] * 2, axis=-1)
    sin_hi = jnp.concatenate([zero, sin] * 2, axis=-1)
    return cos_t, sin_lo, sin_hi


def _reorder_w_in(w):
    o = np.cumsum([0, Q_WIDTH] + [KV_WIDTH] * 6 + [NSA_Q_HEADS * 3, GM_WIDTH, GM_WIDTH, D_MODEL, D_MODEL])
    q, kc, vc, ks, vs, kw, vw, g, u, v, ga, gb = [w[:, o[i]:o[i + 1]] for i in range(12)]
    per = NSA_GROUP * 3
    pad = jnp.zeros((w.shape[0], LANES - per), w.dtype)
    return jnp.concatenate([q, ks, kw, kc, vc, vs, vw, g[:, :per], pad, g[:, per:], pad, u, v, ga, gb], axis=1)


def _owner(ends, pos):
    return jnp.minimum(jnp.sum((ends[None, :] <= pos[:, None]).astype(jnp.int32), axis=1), ends.shape[0] - 1)


def _lookup(table, idx):
    hit = idx[:, None] == jnp.arange(table.shape[0], dtype=jnp.int32)[None, :]
    return jnp.sum(jnp.where(hit, table[None, :], 0), axis=1)


def _expert_plan(counts, n_items_max):
    per_item = MOE_ROWS // MOE_GRAN
    counts = counts.astype(jnp.int32)
    gran = (counts + MOE_GRAN - 1) // MOE_GRAN
    gran_start = jnp.cumsum(gran) - gran
    items = (gran + per_item - 1) // per_item
    item_end = jnp.cumsum(items)
    w = jnp.arange(n_items_max, dtype=jnp.int32)
    ie = _owner(item_end, w)
    part = w - (_lookup(item_end, ie) - _lookup(items, ie))
    live = w < item_end[-1]
    item_ng = jnp.where(live, jnp.clip(_lookup(gran, ie) - part * per_item, 0, per_item), 0).astype(jnp.int32)
    item_row = jnp.where(live, (_lookup(gran_start, ie) + part * per_item) * MOE_GRAN, 0).astype(jnp.int32)
    return gran_start * MOE_GRAN, ie.astype(jnp.int32), item_row, item_ng, item_end[-1:].astype(jnp.int32)


def kernel(x, c, positions, w_mod, b_mod, w_in, q_gain, k_gain, cmp_pos_k, cmp_pos_v, cmp_w1_k, cmp_w2_k, cmp_w1_v, cmp_w2_v, gm_ln_g, gm_ln_b, gm_ws, gm_bs, w_proj_nsa, w_proj_gm, w_out, w_router, b_router, w_gate_e, w_up_e, w_down_e, w_gate_sh, w_up_sh, w_down_sh):
    B, S, D = x.shape
    L = w_mod.shape[0]
    T = B * S
    tm = 256
    tm_dense = 512
    scale = HEAD_DIM ** -0.5
    cos_t, sin_lo, sin_hi = _rope_tables(positions)
    mod = _mod_call(c, w_mod, b_mod)
    n_alloc = T * TOP_K + N_EXPERTS * MOE_GRAN + MOE_ROWS
    n_items_max = N_EXPERTS + T * TOP_K // MOE_ROWS + 1
    rows_buf = jnp.zeros((n_alloc, D // 2), jnp.uint32)

    for l in range(L):
        sh_a, sc_a, gt_a, sh_f, sc_f, gt_f = [mod[l, :, i * D:(i + 1) * D].reshape(B, 1, D) for i in range(6)]
        qg = (jnp.tile(q_gain[l], NSA_Q_HEADS) * scale).reshape(1, Q_WIDTH)
        kg = jnp.tile(k_gain[l], 2 * NSA_KV_HEADS).reshape(1, 2 * KV_WIDTH)
        (qn, qr, ks, kw, vs, vw, kc_raw, vc_raw, gates, u, v, ga, gb) = _inproj_call(
            x, sc_a, sh_a, _reorder_w_in(w_in[l]).astype(BF16), qg, kg, cos_t, sin_lo, sin_hi,
            gm_ln_g[l].reshape(1, GM_WIDTH), gm_ln_b[l].reshape(1, GM_WIDTH), tm_dense)
        kc, vc = _compress_call(
            kc_raw, vc_raw, cmp_w1_k[l].astype(BF16), cmp_w2_k[l].astype(BF16), cmp_pos_k[l].reshape(1, -1),
            cmp_w1_v[l].astype(BF16), cmp_w2_v[l].astype(BF16), cmp_pos_v[l].reshape(1, -1),
            k_gain[l].reshape(1, HEAD_DIM))
        o_cmp, q_aug = _nsa_cmp_call(qn, qr, kc, vc, WINDOW // 2)
        o_nsa = _nsa_flash_call(q_aug, ks, vs, kw, vw, o_cmp, gates)
        bs_full = jnp.repeat(gm_bs[l].T, GM_GROUP_DIM, axis=1)
        o_gm = _gmlp_call(u, v, gm_ws[l], bs_full, 512)
        x1, h2, logits, shared = _mixout_call(
            o_nsa, o_gm, ga, gb, x, gt_a, sc_f, sh_f,
            w_proj_nsa[l].astype(BF16), w_proj_gm[l].astype(BF16), w_out[l].astype(BF16), w_router[l].astype(BF16),
            w_gate_sh[l].astype(BF16), w_up_sh[l].astype(BF16), w_down_sh[l].astype(BF16), tm_dense)
        idx, wts, rank, counts = _route_call(logits.reshape(T, N_EXPERTS), b_router[l], tm)
        row_start, item_e, item_row, item_ng, n_items = _expert_plan(counts[0], n_items_max)
        slots_by_k = _slot_call(idx, rank, row_start, tm).T.reshape(T * TOP_K)
        xs_ref = jax.new_ref(rows_buf)
        _scatter_rows(h2.reshape(T, D // 2), slots_by_k, xs_ref)
        y = _expert_call(l, item_e, item_row, item_ng, n_items, jax.freeze(xs_ref), w_gate_e, w_up_e, w_down_e)
        yk = _gather_rows(y, slots_by_k).reshape(TOP_K, T, D // 2)
        rows_buf = y
        x = _combine_call(yk, wts, shared.reshape(T, D), x1.reshape(T, D), gt_f, tm).reshape(B, S, D)
    return x
```

```python
import functools

import jax
import jax.numpy as jnp
import numpy as np
from jax import lax
from jax.experimental import pallas as pl
from jax.experimental.pallas import tpu as pltpu
from jax.experimental.pallas import tpu_sc as plsc

D_MODEL = 1024
NSA_Q_HEADS = 8
NSA_KV_HEADS = 2
HEAD_DIM = 64
NSA_GROUP = NSA_Q_HEADS // NSA_KV_HEADS
CMP_LEN = 32
CMP_STRIDE = 16
CMP_HIDDEN = 256
SEL_LEN = 64
SEL_TOPN = 16
WINDOW = 512
ROPE_THETA = 10000.0
Q_WIDTH = NSA_Q_HEADS * HEAD_DIM
KV_WIDTH = NSA_KV_HEADS * HEAD_DIM
GM_GROUPS = 8
GM_GROUP_DIM = 64
GM_WIDTH = GM_GROUPS * GM_GROUP_DIM
GM_CHUNK = 128
N_EXPERTS = 256
TOP_K = 8
D_EXPERT = 256
D_SHARED = 256
ROUTE_SCALE = 2.5
EPS = 1e-6

LANES = 128
SEL_BIAS_WIDTH = 64
MASK_NEG = -1e30
SEL_NEG = -30000.0
SC_WINDOW = 128
MOE_GRAN = 128
MOE_ROWS = 9 * MOE_GRAN
VMEM_LIMIT = 56 * 1024 * 1024

F32 = jnp.float32
BF16 = jnp.bfloat16
HI = lax.Precision.HIGHEST


def _cparams(*sem):
    return pltpu.CompilerParams(dimension_semantics=sem, vmem_limit_bytes=VMEM_LIMIT)


def _dot(a, b, **kw):
    return jnp.dot(a, b, preferred_element_type=F32, **kw)


def _dot_nt(a, b, **kw):
    return lax.dot_general(a, b, (((1,), (1,)), ((), ())), preferred_element_type=F32, **kw)


def _gelu(x):
    return 0.5 * x * (1.0 + jnp.tanh(0.7978845608028654 * (x + 0.044715 * (x * x * x))))


def _sigmoid(x):
    return 1.0 / (1.0 + jnp.exp(-x))


def _silu(x):
    return x * _sigmoid(x)


_HI_MASK = np.uint32(0xFFFF0000)


def _pack_halves(a):
    w = a.shape[1] // 2
    bits = lax.bitcast_convert_type(a.astype(BF16).astype(F32), jnp.uint32)
    return (bits[:, w:] & _HI_MASK) | (bits[:, :w] >> 16)


def _unpack_halves(words):
    lo = lax.bitcast_convert_type(words << 16, F32)
    hi = lax.bitcast_convert_type(words & _HI_MASK, F32)
    return lo, hi


def _mod_kernel(c_ref, w_ref, b_ref, o_ref):
    c = c_ref[...]
    o_ref[0] = _dot(_silu(c), w_ref[0], precision=HI) + b_ref[0]


def _mod_call(c, w_mod, b_mod):
    L, D, N = w_mod.shape
    B = c.shape[0]
    tn = 1536
    return pl.pallas_call(
        _mod_kernel,
        grid=(L, N // tn),
        in_specs=[pl.BlockSpec((B, D), lambda l, j: (0, 0)),
                  pl.BlockSpec((1, D, tn), lambda l, j: (l, 0, j)),
                  pl.BlockSpec((1, 1, tn), lambda l, j: (l, 0, j))],
        out_specs=pl.BlockSpec((1, B, tn), lambda l, j: (l, 0, j)),
        out_shape=jax.ShapeDtypeStruct((L, B, N), F32),
        compiler_params=_cparams("arbitrary", "arbitrary"),
        name="mod",
    )(c, w_mod, b_mod.reshape(L, 1, N))


_C_Q = 0
_C_K = 512
_C_KC = 768
_C_VC = 896
_C_VS = 1024
_C_VW = 1152
_C_G = 1280
_C_U = 1536
_C_V = 2048
_C_GA = 2560
_C_GB = 3584
IN_COLS_P = 4608


def _dot_split(a, b):
    hi = a.astype(BF16)
    lo = (a - hi.astype(F32)).astype(BF16)
    return _dot(hi, b) + _dot(lo, b)


def _head_norm(z, bd):
    ms = _dot_split(z * z, bd)
    return z * lax.rsqrt(ms + EPS)


def _rope(z, cos, sin_lo, sin_hi):
    w = z.shape[-1]
    half = HEAD_DIM // 2
    return z * cos + pltpu.roll(z, w - half, 1) * sin_lo + pltpu.roll(z, half, 1) * sin_hi


def _tile_lanes(t, n):
    return t if n == 1 else jnp.concatenate([t] * n, axis=-1)


def _inproj_kernel(x_ref, sc_ref, sh_ref, w_ref, bdq_ref, bdk_ref, qg_ref, kg_ref,
                   cos_ref, sl_ref, shi_ref, lng_ref, lnb_ref,
                   qn_ref, qr_ref, ks_ref, kw_ref, vs_ref, vw_ref, kc_ref, vc_ref,
                   g_ref, u_ref, v_ref, ga_ref, gb_ref):
    tm = x_ref.shape[1]
    x = x_ref[0]
    ms = jnp.mean(x * x, axis=-1, keepdims=True)
    h = (x * lax.rsqrt(ms + EPS)) * (1.0 + sc_ref[0]) + sh_ref[0]
    hb = h.astype(BF16)

    def mm(lo, width):
        return _dot(hb, w_ref[:, lo:lo + width])

    cos, sl, shi = cos_ref[0], sl_ref[0], shi_ref[0]

    zq = mm(_C_Q, Q_WIDTH)
    qn = _head_norm(zq, bdq_ref[...]) * qg_ref[...]
    qr = _rope(qn, _tile_lanes(cos, 4), _tile_lanes(sl, 4), _tile_lanes(shi, 4))
    qn_ref[0] = qn.astype(BF16)
    qr_ref[0] = qr.astype(BF16)

    zk = mm(_C_K, 2 * KV_WIDTH)
    kn = _head_norm(zk, bdk_ref[...]) * kg_ref[...]
    kr = _rope(kn, _tile_lanes(cos, 2), _tile_lanes(sl, 2), _tile_lanes(shi, 2))
    lane = lax.broadcasted_iota(jnp.int32, (tm, LANES), 1)
    tok = pl.program_id(1) * tm + lax.broadcasted_iota(jnp.int32, (tm, LANES), 0)
    onehot = jnp.where(lane - HEAD_DIM == tok // SEL_LEN, 1.0, 0.0)
    ones_col = jnp.where(lane == HEAD_DIM, 1.0, 0.0)
    low = lane < HEAD_DIM
    zvs = mm(_C_VS, KV_WIDTH)
    zvw = mm(_C_VW, KV_WIDTH)
    zkc = mm(_C_KC, KV_WIDTH)
    zvc = mm(_C_VC, KV_WIDTH)
    for kv in range(NSA_KV_HEADS):
        def head(a):
            return a if kv == 0 else pltpu.roll(a, HEAD_DIM, 1)
        ks_ref[0, kv] = jnp.where(low, head(kr[:, :KV_WIDTH]), onehot).astype(BF16)
        kw_ref[0, kv] = jnp.where(low, head(kr[:, KV_WIDTH:]), 0.0).astype(BF16)
        vs_ref[0, kv] = jnp.where(low, head(zvs), ones_col).astype(BF16)
        vw_ref[0, kv] = jnp.where(low, head(zvw), ones_col).astype(BF16)
        kc_ref[0, kv] = head(zkc)[:, :HEAD_DIM]
        vc_ref[0, kv] = head(zvc)[:, :HEAD_DIM]

    zg = mm(_C_G, 2 * LANES)
    sg = _sigmoid(zg)
    g_ref[0, 0] = sg[:, :LANES]
    g_ref[0, 1] = sg[:, LANES:]

    u_ref[0] = _gelu(mm(_C_U, GM_WIDTH)).astype(BF16)
    gv = _gelu(mm(_C_V, GM_WIDTH))
    mu = jnp.mean(gv, axis=-1, keepdims=True)
    cen = gv - mu
    var = jnp.mean(cen * cen, axis=-1, keepdims=True)
    v_ref[0] = ((cen * lax.rsqrt(var + EPS)) * lng_ref[...] + lnb_ref[...]).astype(BF16)

    ga_ref[0] = _sigmoid(mm(_C_GA, D_MODEL)).astype(BF16)
    gb_ref[0] = _sigmoid(mm(_C_GB, D_MODEL)).astype(BF16)


def _block_diag_mean(width):
    idx = np.arange(width) // HEAD_DIM
    return jnp.asarray((idx[:, None] == idx[None, :]).astype(np.float32) / HEAD_DIM).astype(BF16)


def _inproj_call(x, sc, sh, w_p, qg, kg, cos, sl, shi, lng, lnb, tm):
    B, S, D = x.shape
    H = NSA_KV_HEADS
    full = lambda *shape: pl.BlockSpec(shape, lambda b, i: (0,) * len(shape))
    tok3 = lambda w: pl.BlockSpec((1, tm, w), lambda b, i: (b, i, 0))
    per_b = pl.BlockSpec((1, 1, D), lambda b, i: (b, 0, 0))
    kv4 = lambda w: pl.BlockSpec((1, H, tm, w), lambda b, i: (b, 0, i, 0))
    sds = jax.ShapeDtypeStruct
    out_shape = [
        sds((B, S, Q_WIDTH), BF16), sds((B, S, Q_WIDTH), BF16),
        sds((B, H, S, LANES), BF16), sds((B, H, S, LANES), BF16),
        sds((B, H, S, LANES), BF16), sds((B, H, S, LANES), BF16),
        sds((B, H, S, HEAD_DIM), F32), sds((B, H, S, HEAD_DIM), F32),
        sds((B, H, S, LANES), F32),
        sds((B, S, GM_WIDTH), BF16), sds((B, S, GM_WIDTH), BF16),
        sds((B, S, D), BF16), sds((B, S, D), BF16),
    ]
    out_specs = [
        tok3(Q_WIDTH), tok3(Q_WIDTH), kv4(LANES), kv4(LANES), kv4(LANES), kv4(LANES),
        kv4(HEAD_DIM), kv4(HEAD_DIM), kv4(LANES),
        tok3(GM_WIDTH), tok3(GM_WIDTH), tok3(D), tok3(D),
    ]
    return pl.pallas_call(
        _inproj_kernel,
        grid=(B, S // tm),
        in_specs=[tok3(D), per_b, per_b, full(D, IN_COLS_P),
                  full(Q_WIDTH, Q_WIDTH), full(2 * KV_WIDTH, 2 * KV_WIDTH),
                  full(1, Q_WIDTH), full(1, 2 * KV_WIDTH),
                  tok3(LANES), tok3(LANES), tok3(LANES),
                  full(1, GM_WIDTH), full(1, GM_WIDTH)],
        out_specs=out_specs,
        out_shape=out_shape,
        compiler_params=_cparams("arbitrary", "arbitrary"),
        name="in_proj",
    )(x, sc, sh, w_p, _block_diag_mean(Q_WIDTH), _block_diag_mean(2 * KV_WIDTH), qg, kg,
      cos, sl, shi, lng, lnb)


def _compress_kernel(kr_ref, vr_ref, w1k_ref, w2k_ref, pek_ref, w1v_ref, w2v_ref, pev_ref,
                     kg_ref, kc_ref, vc_ref):
    nc = kr_ref.shape[2]
    half = CMP_STRIDE * HEAD_DIM

    def mlp(raw, w1_ref, w2_ref, pe_ref):
        a = raw.astype(BF16)
        top = _dot(a, w1_ref[:half, :])
        bot = _dot(a, w1_ref[half:, :])
        pe = jnp.broadcast_to(pe_ref[...], (8, 2 * half)).astype(BF16)
        pe_row = _dot(pe, w1_ref[...])[0:1, :]
        hid = top + pltpu.roll(bot, nc - 1, 0) + pe_row
        return _dot(_gelu(hid).astype(BF16), w2_ref[...])

    kc = mlp(kr_ref[0, 0], w1k_ref, w2k_ref, pek_ref)
    ms = jnp.mean(kc * kc, axis=-1, keepdims=True)
    kc_ref[0, 0] = (kc * lax.rsqrt(ms + EPS) * kg_ref[...]).astype(BF16)
    vc_ref[0, 0] = mlp(vr_ref[0, 0], w1v_ref, w2v_ref, pev_ref).astype(BF16)


def _compress_call(kc_raw, vc_raw, w1k, w2k, pek, w1v, w2v, pev, kg):
    B, H, S, hd = kc_raw.shape
    nc = S // CMP_STRIDE
    feat = CMP_STRIDE * hd
    raw = pl.BlockSpec((1, 1, nc, feat), lambda b, h: (b, h, 0, 0))
    full = lambda *shape: pl.BlockSpec(shape, lambda b, h: (0,) * len(shape))
    out = pl.BlockSpec((1, 1, nc, hd), lambda b, h: (b, h, 0, 0))
    return pl.pallas_call(
        _compress_kernel,
        grid=(B, H),
        in_specs=[raw, raw, full(2 * feat, CMP_HIDDEN), full(CMP_HIDDEN, hd), full(1, 2 * feat),
                  full(2 * feat, CMP_HIDDEN), full(CMP_HIDDEN, hd), full(1, 2 * feat),
                  full(1, hd)],
        out_specs=[out, out],
        out_shape=[jax.ShapeDtypeStruct((B, H, nc, hd), BF16)] * 2,
        compiler_params=_cparams("arbitrary", "arbitrary"),
        name="compress",
    )(kc_raw.reshape(B, H, nc, feat), vc_raw.reshape(B, H, nc, feat),
      w1k, w2k, pek, w1v, w2v, pev, kg)


def _group_rows(a):
    return jnp.concatenate([a[:, g * HEAD_DIM:(g + 1) * HEAD_DIM] for g in range(NSA_GROUP)], axis=0)


def _nsa_cmp_kernel(qn_ref, qr_ref, kc_ref, vc_ref, ovl_ref, oc_ref, qa_ref, *, n_sel):
    tq = qn_ref.shape[1]
    nc = kc_ref.shape[2]
    G = NSA_GROUP
    q0 = pl.program_id(2) * tq
    q4 = _group_rows(qn_ref[0])
    s = _dot_nt(q4, kc_ref[0, 0])
    row = lax.broadcasted_iota(jnp.int32, (G, tq, nc), 1).reshape(G * tq, nc)
    col = lax.broadcasted_iota(jnp.int32, (G * tq, nc), 1)
    vis = col * CMP_STRIDE + (CMP_LEN - 1) <= q0 + row
    s = jnp.where(vis, s, MASK_NEG)
    m = jnp.max(s, axis=-1, keepdims=True)
    e = jnp.where(vis, jnp.exp(s - m), 0.0)
    p = e / jnp.maximum(jnp.sum(e, axis=-1, keepdims=True), 1e-30)
    oc = _dot(p.astype(BF16), vc_ref[0, 0])
    oc_ref[0, 0] = oc.reshape(G, tq, HEAD_DIM)

    psum = p[0:tq] + p[tq:2 * tq] + p[2 * tq:3 * tq] + p[3 * tq:4 * tq]
    p_hi = psum.astype(BF16)
    p_lo = (psum - p_hi.astype(F32)).astype(BF16)
    imp = (_dot_nt(ovl_ref[...], p_hi) + _dot_nt(ovl_ref[...], p_lo))[:SEL_BIAS_WIDTH]
    blk = lax.broadcasted_iota(jnp.int32, (SEL_BIAS_WIDTH, tq), 0)
    cur = (q0 + lax.broadcasted_iota(jnp.int32, (SEL_BIAS_WIDTH, tq), 1)) // SEL_LEN
    valid = blk <= cur
    forced = (blk == 0) | (blk == cur) | (blk == cur - 1)
    cand = valid & jnp.logical_not(forced)
    n_forced = jnp.minimum(cur, 2) + 1
    val = jnp.where(cand, imp, -1.0)
    cnt = jnp.zeros((SEL_BIAS_WIDTH, tq), F32)
    for j in range(n_sel):
        vj = jnp.broadcast_to(val[j:j + 1, :], (SEL_BIAS_WIDTH, tq))
        cnt = cnt + jnp.where(blk > j, jnp.where(vj >= val, 1.0, 0.0), jnp.where(vj > val, 1.0, 0.0))
    free = (min(SEL_TOPN, n_sel) - n_forced).astype(F32)
    sel = (forced & valid) | (cand & (cnt < free))
    sel_t = jnp.concatenate([jnp.where(sel, 1.0, 0.0), jnp.zeros((LANES - SEL_BIAS_WIDTH, tq), F32)], axis=0)
    sel_q = sel_t.T
    bias = jnp.where(sel_q > 0.5, 0.0, SEL_NEG)
    lane = lax.broadcasted_iota(jnp.int32, (tq, LANES), 1)
    bias_hi = pltpu.roll(bias, SEL_BIAS_WIDTH, 1)
    qr = qr_ref[0]
    for g in range(G):
        qg = qr[:, g * HEAD_DIM:(g + 1) * HEAD_DIM].astype(F32)
        qg = jnp.concatenate([qg, qg], axis=-1)
        qa_ref[0, 0, g] = jnp.where(lane < HEAD_DIM, qg, bias_hi).astype(BF16)


def _overlap_t(S):
    n_cmp = (S - CMP_LEN) // CMP_STRIDE + 1
    nc = S // CMP_STRIDE
    n_sel = S // SEL_LEN
    start = np.arange(nc) * CMP_STRIDE
    end = start + CMP_LEN - 1
    sel_start = np.arange(n_sel) * SEL_LEN
    ov = (start[None, :] <= sel_start[:, None] + SEL_LEN - 1) & (end[None, :] >= sel_start[:, None])
    ov = ov & (np.arange(nc) < n_cmp)[None, :]
    out = np.zeros((LANES, nc), np.float32)
    out[:n_sel] = ov.astype(np.float32)
    return jnp.asarray(out).astype(BF16)


def _nsa_cmp_call(qn, qr, kc, vc, tq):
    B, S, _ = qn.shape
    H, G = NSA_KV_HEADS, NSA_GROUP
    nc = kc.shape[2]
    n_sel = S // SEL_LEN
    assert n_sel <= SEL_BIAS_WIDTH and nc % LANES == 0
    qspec = pl.BlockSpec((1, tq, G * HEAD_DIM), lambda b, h, i: (b, i, h))
    cspec = pl.BlockSpec((1, 1, nc, HEAD_DIM), lambda b, h, i: (b, h, 0, 0))
    return pl.pallas_call(
        functools.partial(_nsa_cmp_kernel, n_sel=n_sel),
        grid=(B, H, S // tq),
        in_specs=[qspec, qspec, cspec, cspec, pl.BlockSpec((LANES, nc), lambda b, h, i: (0, 0))],
        out_specs=[pl.BlockSpec((1, 1, G, tq, HEAD_DIM), lambda b, h, i: (b, h, 0, i, 0)),
                   pl.BlockSpec((1, 1, G, tq, LANES), lambda b, h, i: (b, h, 0, i, 0))],
        out_shape=[jax.ShapeDtypeStruct((B, H, G, S, HEAD_DIM), F32),
                   jax.ShapeDtypeStruct((B, H, G, S, LANES), BF16)],
        compiler_params=_cparams("arbitrary", "arbitrary", "arbitrary"),
        name="nsa_cmp",
    )(qn, qr, kc, vc, _overlap_t(S))


SEL_CHUNK = 4


def _nsa_flash_kernel(qa_ref, ks_ref, vs_ref, kw_ref, vw_ref, oc_ref, g_ref, o_ref, m_scr, acc_scr):
    G = NSA_GROUP
    tq = qa_ref.shape[3]
    R = G * tq
    tk = tq
    i = pl.program_id(2)
    q0 = i * tq

    def rows_of(ref, j0, nt):
        return ref[0, 0, pl.ds(pl.multiple_of(j0 * tk, tk), nt * tk), :]

    def attend(k_ref, v_ref, j0, nt, visible, state):
        k, v = rows_of(k_ref, j0, nt), rows_of(v_ref, j0, nt)
        ss = [_dot_nt(qa_ref[0, 0, g], k) for g in range(G)]
        if visible is not None:
            qpos = q0 + lax.broadcasted_iota(jnp.int32, (tq, nt * tk), 0)
            kpos = j0 * tk + lax.broadcasted_iota(jnp.int32, (tq, nt * tk), 1)
            mask = visible(qpos, kpos)
        ps, alphas = [], []
        for g in range(G):
            s = ss[g] if visible is None else jnp.where(mask, ss[g], MASK_NEG)
            cols = [s[:, c * LANES:(c + 1) * LANES] for c in range(nt * tk // LANES)]
            m_new = jnp.max(functools.reduce(jnp.maximum, cols), axis=-1, keepdims=True)
            if state:
                m_prev = m_scr[g]
                m_new = jnp.maximum(m_prev, m_new)
                alphas.append(jnp.exp(m_prev - m_new))
                m_scr[g] = m_new
            ps.append(jnp.concatenate([jnp.exp(c - m_new) for c in cols], axis=-1).astype(BF16))
        if not state:
            return [_dot(ps[g], v) for g in range(G)]
        for g in range(G):
            acc_scr[g] = alphas[g] * acc_scr[g] + _dot(ps[g], v)

    def normalise(acc):
        return acc[:, :HEAD_DIM] / acc[:, HEAD_DIM:HEAD_DIM + 1]

    m_scr[...] = jnp.full((G, tq, LANES), MASK_NEG, F32)
    acc_scr[...] = jnp.zeros((G, tq, LANES), F32)
    causal = lambda qpos, kpos: kpos <= qpos

    def sel_body(c, carry):
        attend(ks_ref, vs_ref, c * SEL_CHUNK, SEL_CHUNK, None, True)
        return carry

    n_full = i // SEL_CHUNK
    lax.fori_loop(0, n_full, sel_body, 0)
    attend(ks_ref, vs_ref, n_full * SEL_CHUNK, 2, causal, True)

    @pl.when(i % SEL_CHUNK >= 2)
    def _():
        attend(ks_ref, vs_ref, n_full * SEL_CHUNK + 2, 2, causal, True)

    o_sel = normalise(acc_scr[...].reshape(R, LANES))

    assert WINDOW == 2 * tk
    band = lambda qpos, kpos: (kpos <= qpos) & (kpos > qpos - WINDOW)
    o_win = normalise(jnp.concatenate(attend(kw_ref, vw_ref, jnp.maximum(i - 2, 0), 3, band, False), axis=0))

    o_cmp = oc_ref[0, 0].reshape(R, HEAD_DIM)
    gates = g_ref[0, 0]
    outs = []
    for g in range(G):
        r = slice(g * tq, (g + 1) * tq)
        outs.append(gates[:, 3 * g:3 * g + 1] * o_cmp[r]
                    + gates[:, 3 * g + 1:3 * g + 2] * o_sel[r]
                    + gates[:, 3 * g + 2:3 * g + 3] * o_win[r])
    o_ref[0] = jnp.concatenate(outs, axis=-1).astype(BF16)


def _nsa_flash_call(qa, ks, vs, kw, vw, oc, gates):
    B, H, G, S, _ = qa.shape
    tq = WINDOW // 2
    assert (S // tq) % SEL_CHUNK == 0
    kv = pl.BlockSpec((1, 1, S, LANES), lambda b, h, i: (b, h, 0, 0))
    return pl.pallas_call(
        _nsa_flash_kernel,
        grid=(B, H, S // tq),
        in_specs=[pl.BlockSpec((1, 1, G, tq, LANES), lambda b, h, i: (b, h, 0, i, 0)),
                  kv, kv, kv, kv,
                  pl.BlockSpec((1, 1, G, tq, HEAD_DIM), lambda b, h, i: (b, h, 0, i, 0)),
                  pl.BlockSpec((1, 1, tq, LANES), lambda b, h, i: (b, h, i, 0))],
        out_specs=pl.BlockSpec((1, tq, G * HEAD_DIM), lambda b, h, i: (b, i, h)),
        out_shape=jax.ShapeDtypeStruct((B, S, Q_WIDTH), BF16),
        scratch_shapes=[pltpu.VMEM((G, tq, LANES), F32), pltpu.VMEM((G, tq, LANES), F32)],
        compiler_params=_cparams("arbitrary", "arbitrary", "arbitrary"),
        name="nsa_flash",
    )(qa, ks, vs, kw, vw, oc, gates)


def _gmlp_kernel(u_ref, v_ref, ws_ref, bs_ref, o_ref):
    tm = u_ref.shape[1]
    C = GM_CHUNK
    r = lax.broadcasted_iota(jnp.int32, (C, C), 0)
    c = lax.broadcasted_iota(jnp.int32, (C, C), 1)
    causal = c <= r
    ws = [jnp.where(causal, ws_ref[g], 0.0).astype(BF16) for g in range(GM_GROUPS)]
    for n in range(tm // C):
        rows = slice(n * C, (n + 1) * C)
        vn = v_ref[0, rows, :]
        mixed = jnp.concatenate(
            [_dot(ws[g], vn[:, g * GM_GROUP_DIM:(g + 1) * GM_GROUP_DIM]) for g in range(GM_GROUPS)],
            axis=-1)
        o_ref[0, rows, :] = (u_ref[0, rows, :].astype(F32) * (mixed + bs_ref[...])).astype(BF16)


def _gmlp_call(u, v, ws, bs_full, tm):
    B, S, W = u.shape
    tok = pl.BlockSpec((1, tm, W), lambda b, i: (b, i, 0))
    return pl.pallas_call(
        _gmlp_kernel,
        grid=(B, S // tm),
        in_specs=[tok, tok,
                  pl.BlockSpec((GM_GROUPS, GM_CHUNK, GM_CHUNK), lambda b, i: (0, 0, 0)),
                  pl.BlockSpec((GM_CHUNK, W), lambda b, i: (0, 0))],
        out_specs=tok,
        out_shape=jax.ShapeDtypeStruct((B, S, W), BF16),
        compiler_params=_cparams("arbitrary", "arbitrary"),
        name="gmlp",
    )(u, v, ws, bs_full)


def _mixout_kernel(on_ref, og_ref, ga_ref, gb_ref, x_ref, gt_ref, sc_ref, sh_ref,
                   wpn_ref, wpg_ref, wo_ref, wr_ref, wgs_ref, wus_ref, wds_ref,
                   x1_ref, h2_ref, lg_ref, shd_ref):
    ya = _dot(on_ref[0], wpn_ref[...])
    yb = _dot(og_ref[0], wpg_ref[...])
    merged = ga_ref[0].astype(F32) * ya + gb_ref[0].astype(F32) * yb
    x1 = x_ref[0] + gt_ref[0] * _dot(merged.astype(BF16), wo_ref[...])
    x1_ref[0] = x1
    ms = jnp.mean(x1 * x1, axis=-1, keepdims=True)
    h2 = (x1 * lax.rsqrt(ms + EPS)) * (1.0 + sc_ref[0]) + sh_ref[0]
    hb = h2.astype(BF16)
    lg_ref[0] = _dot_nt(wr_ref[...], hb)
    h2_ref[0] = _pack_halves(hb)
    act = _silu(_dot(hb, wgs_ref[...])) * _dot(hb, wus_ref[...])
    shd_ref[0] = _dot(act.astype(BF16), wds_ref[...])


def _mixout_call(o_nsa, o_gm, ga, gb, x, gt, sc, sh, wpn, wpg, wo, wr, wgs, wus, wds, tm):
    B, S, D = x.shape
    tok = lambda w: pl.BlockSpec((1, tm, w), lambda b, i: (b, i, 0))
    per_b = pl.BlockSpec((1, 1, D), lambda b, i: (b, 0, 0))
    full = lambda a: pl.BlockSpec(a.shape, lambda b, i: (0,) * a.ndim)
    sds = jax.ShapeDtypeStruct
    return pl.pallas_call(
        _mixout_kernel,
        grid=(B, S // tm),
        in_specs=[tok(Q_WIDTH), tok(GM_WIDTH), tok(D), tok(D), tok(D), per_b, per_b, per_b,
                  full(wpn), full(wpg), full(wo), full(wr), full(wgs), full(wus), full(wds)],
        out_specs=[tok(D), tok(D // 2), pl.BlockSpec((1, N_EXPERTS, tm), lambda b, i: (b, 0, i)), tok(D)],
        out_shape=[sds((B, S, D), F32), sds((B, S, D // 2), jnp.uint32), sds((B, N_EXPERTS, S), F32),
                   sds((B, S, D), F32)],
        compiler_params=_cparams("arbitrary", "arbitrary"),
        name="mix_out",
    )(o_nsa, o_gm, ga, gb, x, gt, sc, sh, wpn, wpg, wo, wr, wgs, wus, wds)


def _route_kernel(lg_ref, br_ref, idx_ref, wt_ref, rank_ref, cnt_ref, run_scr):
    E, tm = lg_ref.shape[1], lg_ref.shape[2]

    @pl.when(pl.program_id(0) == 0)
    def _():
        run_scr[...] = jnp.zeros_like(run_scr)

    aff = _sigmoid(lg_ref[0])
    work = aff + br_ref[...]
    row = lax.broadcasted_iota(jnp.int32, (E, tm), 0).astype(F32)
    picked = jnp.zeros((E, tm), F32)
    idxs, tops = [], []
    for _ in range(TOP_K):
        m = jnp.max(work, axis=0, keepdims=True)
        idx = jnp.min(jnp.where(work == m, row, float(E)), axis=0, keepdims=True)
        hit = row == idx
        tops.append(jnp.sum(jnp.where(hit, aff, 0.0), axis=0, keepdims=True))
        idxs.append(idx)
        picked = jnp.where(hit, 1.0, picked)
        work = jnp.where(hit, -jnp.inf, work)
    total = functools.reduce(jnp.add, tops)
    r = lax.broadcasted_iota(jnp.int32, (tm, tm), 0)
    c = lax.broadcasted_iota(jnp.int32, (tm, tm), 1)
    before = _dot(picked.astype(BF16), jnp.where(r < c, 1.0, 0.0).astype(BF16)) + run_scr[...]
    ranks = [jnp.sum(jnp.where(row == idx, before, 0.0), axis=0, keepdims=True) for idx in idxs]
    run_scr[...] = run_scr[...] + jnp.sum(picked, axis=1, keepdims=True)
    cnt_ref[...] = run_scr[...]
    idx_ref[...] = jnp.concatenate(idxs, axis=0).astype(jnp.int32)
    rank_ref[...] = jnp.concatenate(ranks, axis=0).astype(jnp.int32)
    wt = jnp.concatenate([t / total * ROUTE_SCALE for t in tops]
                         + [jnp.zeros((LANES - TOP_K, tm), F32)], axis=0)
    eye = jnp.where(r == c, 1.0, 0.0).astype(BF16)
    cols = jnp.zeros((tm, LANES), F32)
    rest = wt
    for _ in range(3):
        part = rest.astype(BF16)
        cols = cols + _dot_nt(eye, part)
        rest = rest - part.astype(F32)
    wt_ref[...] = cols


def _route_call(logits_t, b_router, tm):
    B, E, S = logits_t.shape
    per_b = S // tm
    T = B * S
    kt = pl.BlockSpec((TOP_K, tm), lambda i: (0, i))
    col = pl.BlockSpec((E, 1), lambda i: (0, 0))
    sds = jax.ShapeDtypeStruct
    return pl.pallas_call(
        _route_kernel,
        grid=(T // tm,),
        in_specs=[pl.BlockSpec((1, E, tm), lambda i: (i // per_b, 0, i % per_b)), col],
        out_specs=[kt, pl.BlockSpec((tm, LANES), lambda i: (i, 0)), kt, col],
        out_shape=[sds((TOP_K, T), jnp.int32), sds((T, LANES), F32), sds((TOP_K, T), jnp.int32),
                   sds((E, 1), F32)],
        scratch_shapes=[pltpu.VMEM((E, 1), F32)],
        compiler_params=_cparams("arbitrary"),
        name="route",
    )(logits_t, b_router.reshape(E, 1))


def _slot_kernel(idx_ref, rank_ref, ps_ref, o_ref):
    tm = idx_ref.shape[1]
    E = ps_ref.shape[0]
    row = lax.broadcasted_iota(jnp.int32, (E, tm), 0)
    idx, rank, ps = idx_ref[...], rank_ref[...], ps_ref[...]
    base = [jnp.sum(jnp.where(row == idx[k:k + 1, :], ps, 0.0), axis=0, keepdims=True) for k in range(TOP_K)]
    o_ref[...] = jnp.concatenate(base, axis=0).astype(jnp.int32) + rank


def _slot_call(idx, rank, pad_start, tm):
    T = idx.shape[1]
    E = pad_start.shape[0]
    kt = pl.BlockSpec((TOP_K, tm), lambda i: (0, i))
    return pl.pallas_call(
        _slot_kernel,
        grid=(T // tm,),
        in_specs=[kt, kt, pl.BlockSpec((E, 1), lambda i: (0, 0))],
        out_specs=kt,
        out_shape=jax.ShapeDtypeStruct((TOP_K, T), jnp.int32),
        compiler_params=_cparams("arbitrary"),
        name="slot",
    )(idx, rank, pad_start.astype(F32).reshape(E, 1))


def _sc_mesh():
    mesh = plsc.VectorSubcoreMesh(core_axis_name="core", subcore_axis_name="subcore")
    return mesh, mesh.num_cores * mesh.num_subcores


def _sc_worker(mesh):
    return lax.axis_index("subcore") * mesh.num_cores + lax.axis_index("core")


def _scatter_rows(rows, indices_by_k, out_ref):
    n, width = rows.shape
    mesh, workers = _sc_mesh()
    per_worker = n // workers
    assert per_worker * workers == n and per_worker % SC_WINDOW == 0

    @pl.kernel(out_type=(), mesh=mesh, name="scatter_rows",
               scratch_types=[pltpu.VMEM((SC_WINDOW,), jnp.int32), pltpu.VMEM((SC_WINDOW, width), rows.dtype)])
    def scatter(rows_hbm, idx_hbm, out_hbm, idx_v, rows_v):
        worker = _sc_worker(mesh)

        @pl.loop(0, per_worker // SC_WINDOW)
        def _(j):
            base = pl.multiple_of(worker * per_worker + j * SC_WINDOW, SC_WINDOW)
            pltpu.sync_copy(rows_hbm.at[pl.ds(base, SC_WINDOW)], rows_v)
            for k in range(TOP_K):
                pltpu.sync_copy(idx_hbm.at[pl.ds(pl.multiple_of(k * n + base, SC_WINDOW), SC_WINDOW)], idx_v)
                pltpu.sync_copy(rows_v, out_hbm.at[idx_v])

    scatter(rows, indices_by_k, out_ref)


def _expert_kernel(ie_ref, ir_ref, ig_ref, n_ref, x_hbm, wg_ref, wu_ref, wd_ref, y_hbm,
                   xbuf, ybuf, act_scr, xsem, ysem):
    w = pl.program_id(0)
    n = n_ref[0]
    R = xbuf.shape[1]
    last = ie_ref.shape[0] - 1

    def x_copy(item, slot):
        row = pl.multiple_of(ir_ref[jnp.minimum(item, last)], MOE_GRAN)
        return pltpu.make_async_copy(x_hbm.at[pl.ds(row, R), :], xbuf.at[slot], xsem.at[slot])

    def y_copy(item, slot, gi):
        row = pl.multiple_of(ir_ref[jnp.minimum(item, last)] + gi * MOE_GRAN, MOE_GRAN)
        src = ybuf.at[slot, pl.ds(pl.multiple_of(gi * MOE_GRAN, MOE_GRAN), MOE_GRAN), :]
        return pltpu.make_async_copy(src, y_hbm.at[pl.ds(row, MOE_GRAN), :], ysem.at[slot])

    def for_granules(item, fn):
        def body(gi, carry):
            fn(gi)
            return carry
        lax.fori_loop(0, ig_ref[jnp.clip(item, 0, last)], body, 0)

    def gate_up(slot):
        lo, hi = _unpack_halves(xbuf[slot])
        x = jnp.concatenate([lo, hi], axis=1).astype(BF16)
        a = _dot(x, wg_ref[0, 0].astype(BF16))
        b = _dot(x, wu_ref[0, 0].astype(BF16))
        return (_silu(a) * b).astype(BF16)

    def down(act):
        return _pack_halves(_dot(act, wd_ref[0, 0].astype(BF16)))

    cur, prv = w % 2, (w + 1) % 2

    @pl.when(w == 0)
    def _():
        x_copy(0, 0).start()

    @pl.when(w < n)
    def _():
        x_copy(w, cur).wait()

    @pl.when(w + 1 < n)
    def _():
        x_copy(w + 1, prv).start()

    @pl.when((w >= 3) & (w - 3 < n))
    def _():
        for_granules(w - 3, lambda gi: y_copy(w - 3, prv, gi).wait())

    @pl.when(w == 0)
    def _():
        act_scr[...] = gate_up(0)

    @pl.when((w >= 1) & (w < n))
    def _():
        prev = act_scr[...]
        ybuf[prv] = down(prev)
        act_scr[...] = gate_up(cur)

    @pl.when((w >= 1) & (w == n))
    def _():
        ybuf[prv] = down(act_scr[...])

    @pl.when((w >= 1) & (w <= n))
    def _():
        for_granules(w - 1, lambda gi: y_copy(w - 1, prv, gi).start())


def _expert_call(layer, item_e, item_row, item_ng, n_items, xs, wg, wu, wd):
    n_alloc, W = xs.shape
    R = MOE_ROWS
    nw = item_e.shape[0]
    D, F = wg.shape[2], wg.shape[3]
    cur = lambda w, ie, ir, ig, n: (layer, ie[jnp.minimum(w, nw - 1)], 0, 0)
    prev = lambda w, ie, ir, ig, n: (layer, ie[jnp.clip(w - 1, 0, nw - 1)], 0, 0)
    grid_spec = pltpu.PrefetchScalarGridSpec(
        num_scalar_prefetch=4,
        grid=(nw + 3,),
        in_specs=[pl.BlockSpec(memory_space=pl.ANY),
                  pl.BlockSpec((1, 1, D, F), cur), pl.BlockSpec((1, 1, D, F), cur),
                  pl.BlockSpec((1, 1, F, D), prev)],
        out_specs=pl.BlockSpec(memory_space=pl.ANY),
        scratch_shapes=[pltpu.VMEM((2, R, W), jnp.uint32), pltpu.VMEM((2, R, W), jnp.uint32),
                        pltpu.VMEM((R, F), BF16),
                        pltpu.SemaphoreType.DMA((2,)), pltpu.SemaphoreType.DMA((2,))],
    )
    return pl.pallas_call(
        _expert_kernel,
        grid_spec=grid_spec,
        out_shape=jax.ShapeDtypeStruct((n_alloc, W), jnp.uint32),
        input_output_aliases={4: 0},
        compiler_params=pltpu.CompilerParams(dimension_semantics=("arbitrary",), vmem_limit_bytes=VMEM_LIMIT,
                                             has_side_effects=True),
        name="expert",
    )(item_e, item_row, item_ng, n_items, xs, wg, wu, wd)


def _gather_rows(table, indices):
    n = indices.shape[0]
    width = table.shape[1]
    mesh, workers = _sc_mesh()
    per_worker = n // workers
    assert per_worker * workers == n and per_worker % SC_WINDOW == 0

    @pl.kernel(out_type=jax.ShapeDtypeStruct((n, width), table.dtype), mesh=mesh, name="gather_rows",
               scratch_types=[pltpu.VMEM((SC_WINDOW,), jnp.int32), pltpu.VMEM((SC_WINDOW, width), table.dtype),
                              pltpu.SemaphoreType.DMA])
    def gather(table_hbm, idx_hbm, out_hbm, idx_v, rows_v, sem):
        worker = _sc_worker(mesh)

        @pl.loop(0, per_worker // SC_WINDOW)
        def _(j):
            base = pl.multiple_of(worker * per_worker + j * SC_WINDOW, SC_WINDOW)
            pltpu.sync_copy(idx_hbm.at[pl.ds(base, SC_WINDOW)], idx_v)
            pltpu.async_copy(table_hbm.at[idx_v], rows_v, sem).wait()
            pltpu.sync_copy(rows_v, out_hbm.at[pl.ds(base, SC_WINDOW)])

    return gather(table, indices)


def _combine_kernel(y_ref, wt_ref, shd_ref, x_ref, gt_ref, o_ref):
    wt = wt_ref[...]
    acc_lo = acc_hi = None
    for k in range(TOP_K):
        lo, hi = _unpack_halves(y_ref[k])
        w = wt[:, k:k + 1]
        acc_lo = w * lo if k == 0 else acc_lo + w * lo
        acc_hi = w * hi if k == 0 else acc_hi + w * hi
    routed = jnp.concatenate([acc_lo, acc_hi], axis=1)
    o_ref[...] = x_ref[...] + gt_ref[0] * (routed + shd_ref[...])


def _combine_call(yk, wts, shared, x1, gt, tm):
    T, D = x1.shape
    B = gt.shape[0]
    per_b = T // B // tm
    tok = lambda w: pl.BlockSpec((tm, w), lambda i: (i, 0))
    return pl.pallas_call(
        _combine_kernel,
        grid=(T // tm,),
        in_specs=[pl.BlockSpec((TOP_K, tm, yk.shape[2]), lambda i: (0, i, 0)),
                  tok(LANES), tok(D), tok(D),
                  pl.BlockSpec((1, 1, D), lambda i: (i // per_b, 0, 0))],
        out_specs=tok(D),
        out_shape=jax.ShapeDtypeStruct((T, D), F32),
        compiler_params=_cparams("arbitrary"),
        name="combine",
    )(yk, wts, shared, x1, gt)


def _rope_tables(positions):
    half = HEAD_DIM // 2
    inv = ROPE_THETA ** (-jnp.arange(half, dtype=F32) / half)
    ang = positions.astype(F32)[..., None] * inv
    cos, sin, zero = jnp.cos(ang), jnp.sin(ang), jnp.zeros_like(ang)
    cos_t = jnp.concatenate([cos, cos] * 2, axis=-1)
    sin_lo = jnp.concatenate([-sin, zero] * 2, axis=-1)
    sin_hi = jnp.concatenate([zero, sin] * 2, axis=-1)
    return cos_t, sin_lo, sin_hi


def _reorder_w_in(w):
    o = np.cumsum([0, Q_WIDTH] + [KV_WIDTH] * 6 + [NSA_Q_HEADS * 3, GM_WIDTH, GM_WIDTH, D_MODEL, D_MODEL])
    q, kc, vc, ks, vs, kw, vw, g, u, v, ga, gb = [w[:, o[i]:o[i + 1]] for i in range(12)]
    per = NSA_GROUP * 3
    pad = jnp.zeros((w.shape[0], LANES - per), w.dtype)
    return jnp.concatenate([q, ks, kw, kc, vc, vs, vw, g[:, :per], pad, g[:, per:], pad, u, v, ga, gb], axis=1)


def _owner(ends, pos):
    return jnp.minimum(jnp.sum((ends[None, :] <= pos[:, None]).astype(jnp.int32), axis=1), ends.shape[0] - 1)


def _lookup(table, idx):
    hit = idx[:, None] == jnp.arange(table.shape[0], dtype=jnp.int32)[None, :]
    return jnp.sum(jnp.where(hit, table[None, :], 0), axis=1)


def _expert_plan(counts, n_items_max):
    per_item = MOE_ROWS // MOE_GRAN
    counts = counts.astype(jnp.int32)
    gran = (counts + MOE_GRAN - 1) // MOE_GRAN
    gran_start = jnp.cumsum(gran) - gran
    items = (gran + per_item - 1) // per_item
    item_end = jnp.cumsum(items)
    w = jnp.arange(n_items_max, dtype=jnp.int32)
    ie = _owner(item_end, w)
    part = w - (_lookup(item_end, ie) - _lookup(items, ie))
    live = w < item_end[-1]
    item_ng = jnp.where(live, jnp.clip(_lookup(gran, ie) - part * per_item, 0, per_item), 0).astype(jnp.int32)
    item_row = jnp.where(live, (_lookup(gran_start, ie) + part * per_item) * MOE_GRAN, 0).astype(jnp.int32)
    return gran_start * MOE_GRAN, ie.astype(jnp.int32), item_row, item_ng, item_end[-1:].astype(jnp.int32)


def kernel(x, c, positions, w_mod, b_mod, w_in, q_gain, k_gain, cmp_pos_k, cmp_pos_v, cmp_w1_k, cmp_w2_k, cmp_w1_v, cmp_w2_v, gm_ln_g, gm_ln_b, gm_ws, gm_bs, w_proj_nsa, w_proj_gm, w_out, w_router, b_router, w_gate_e, w_up_e, w_down_e, w_gate_sh, w_up_sh, w_down_sh):
    B, S, D = x.shape
    L = w_mod.shape[0]
    T = B * S
    tm = 256
    tm_dense = 512
    scale = HEAD_DIM ** -0.5
    cos_t, sin_lo, sin_hi = _rope_tables(positions)
    mod = _mod_call(c, w_mod, b_mod)
    n_alloc = T * TOP_K + N_EXPERTS * MOE_GRAN + MOE_ROWS
    n_items_max = N_EXPERTS + T * TOP_K // MOE_ROWS + 1
    rows_buf = jnp.zeros((n_alloc, D // 2), jnp.uint32)

    for l in range(L):
        sh_a, sc_a, gt_a, sh_f, sc_f, gt_f = [mod[l, :, i * D:(i + 1) * D].reshape(B, 1, D) for i in range(6)]
        qg = (jnp.tile(q_gain[l], NSA_Q_HEADS) * scale).reshape(1, Q_WIDTH)
        kg = jnp.tile(k_gain[l], 2 * NSA_KV_HEADS).reshape(1, 2 * KV_WIDTH)
        (qn, qr, ks, kw, vs, vw, kc_raw, vc_raw, gates, u, v, ga, gb) = _inproj_call(
            x, sc_a, sh_a, _reorder_w_in(w_in[l]).astype(BF16), qg, kg, cos_t, sin_lo, sin_hi,
            gm_ln_g[l].reshape(1, GM_WIDTH), gm_ln_b[l].reshape(1, GM_WIDTH), tm_dense)
        kc, vc = _compress_call(
            kc_raw, vc_raw, cmp_w1_k[l].astype(BF16), cmp_w2_k[l].astype(BF16), cmp_pos_k[l].reshape(1, -1),
            cmp_w1_v[l].astype(BF16), cmp_w2_v[l].astype(BF16), cmp_pos_v[l].reshape(1, -1),
            k_gain[l].reshape(1, HEAD_DIM))
        o_cmp, q_aug = _nsa_cmp_call(qn, qr, kc, vc, WINDOW // 2)
        o_nsa = _nsa_flash_call(q_aug, ks, vs, kw, vw, o_cmp, gates)
        bs_full = jnp.repeat(gm_bs[l].T, GM_GROUP_DIM, axis=1)
        o_gm = _gmlp_call(u, v, gm_ws[l], bs_full, 512)
        x1, h2, logits, shared = _mixout_call(
            o_nsa, o_gm, ga, gb, x, gt_a, sc_f, sh_f,
            w_proj_nsa[l].astype(BF16), w_proj_gm[l].astype(BF16), w_out[l].astype(BF16), w_router[l].T.astype(BF16),
            w_gate_sh[l].astype(BF16), w_up_sh[l].astype(BF16), w_down_sh[l].astype(BF16), tm_dense)
        idx, wts, rank, counts = _route_call(logits, b_router[l], tm)
        row_start, item_e, item_row, item_ng, n_items = _expert_plan(counts[:, 0], n_items_max)
        slots_by_k = _slot_call(idx, rank, row_start, tm).reshape(TOP_K * T)
        xs_ref = jax.new_ref(rows_buf)
        _scatter_rows(h2.reshape(T, D // 2), slots_by_k, xs_ref)
        y = _expert_call(l, item_e, item_row, item_ng, n_items, jax.freeze(xs_ref), w_gate_e, w_up_e, w_down_e)
        yk = _gather_rows(y, slots_by_k).reshape(TOP_K, T, D // 2)
        rows_buf = y
        x = _combine_call(yk, wts, shared.reshape(T, D), x1.reshape(T, D), gt_f, tm).reshape(B, S, D)
    return x
```

```python
import functools

import jax
import jax.numpy as jnp
import numpy as np
from jax import lax
from jax.experimental import pallas as pl
from jax.experimental.pallas import tpu as pltpu
from jax.experimental.pallas import tpu_sc as plsc

D_MODEL = 1024
NSA_Q_HEADS = 8
NSA_KV_HEADS = 2
HEAD_DIM = 64
NSA_GROUP = NSA_Q_HEADS // NSA_KV_HEADS
CMP_LEN = 32
CMP_STRIDE = 16
CMP_HIDDEN = 256
SEL_LEN = 64
SEL_TOPN = 16
WINDOW = 512
ROPE_THETA = 10000.0
Q_WIDTH = NSA_Q_HEADS * HEAD_DIM
KV_WIDTH = NSA_KV_HEADS * HEAD_DIM
GM_GROUPS = 8
GM_GROUP_DIM = 64
GM_WIDTH = GM_GROUPS * GM_GROUP_DIM
GM_CHUNK = 128
N_EXPERTS = 256
TOP_K = 8
D_EXPERT = 256
D_SHARED = 256
ROUTE_SCALE = 2.5
EPS = 1e-6

LANES = 128
SEL_BIAS_WIDTH = 64
MASK_NEG = -1e30
SEL_NEG = -30000.0
SC_WINDOW = 128
MOE_GRAN = 128
MOE_ROWS = 9 * MOE_GRAN
VMEM_LIMIT = 56 * 1024 * 1024

F32 = jnp.float32
BF16 = jnp.bfloat16
HI = lax.Precision.HIGHEST


def _cparams(*sem):
    return pltpu.CompilerParams(dimension_semantics=sem, vmem_limit_bytes=VMEM_LIMIT)


def _dot(a, b, **kw):
    return jnp.dot(a, b, preferred_element_type=F32, **kw)


def _dot_nt(a, b, **kw):
    return lax.dot_general(a, b, (((1,), (1,)), ((), ())), preferred_element_type=F32, **kw)


def _gelu(x):
    return 0.5 * x * (1.0 + jnp.tanh(0.7978845608028654 * (x + 0.044715 * (x * x * x))))


def _sigmoid(x):
    return 1.0 / (1.0 + jnp.exp(-x))


def _silu(x):
    return x * _sigmoid(x)


_HI_MASK = np.uint32(0xFFFF0000)


def _pack_halves(a):
    w = a.shape[1] // 2
    bits = lax.bitcast_convert_type(a.astype(BF16).astype(F32), jnp.uint32)
    return (bits[:, w:] & _HI_MASK) | (bits[:, :w] >> 16)


def _unpack_halves(words):
    lo = lax.bitcast_convert_type(words << 16, F32)
    hi = lax.bitcast_convert_type(words & _HI_MASK, F32)
    return lo, hi


def _mod_kernel(c_ref, w_ref, b_ref, o_ref):
    c = c_ref[...]
    o_ref[0] = _dot(_silu(c), w_ref[0], precision=HI) + b_ref[0]


def _mod_call(c, w_mod, b_mod):
    L, D, N = w_mod.shape
    B = c.shape[0]
    tn = 1536
    return pl.pallas_call(
        _mod_kernel,
        grid=(L, N // tn),
        in_specs=[pl.BlockSpec((B, D), lambda l, j: (0, 0)),
                  pl.BlockSpec((1, D, tn), lambda l, j: (l, 0, j)),
                  pl.BlockSpec((1, 1, tn), lambda l, j: (l, 0, j))],
        out_specs=pl.BlockSpec((1, B, tn), lambda l, j: (l, 0, j)),
        out_shape=jax.ShapeDtypeStruct((L, B, N), F32),
        compiler_params=_cparams("arbitrary", "arbitrary"),
        name="mod",
    )(c, w_mod, b_mod.reshape(L, 1, N))


_C_Q = 0
_C_K = 512
_C_KC = 768
_C_VC = 896
_C_VS = 1024
_C_VW = 1152
_C_G = 1280
_C_U = 1536
_C_V = 2048
_C_GA = 2560
_C_GB = 3584
IN_COLS_P = 4608


def _dot_split(a, b):
    hi = a.astype(BF16)
    lo = (a - hi.astype(F32)).astype(BF16)
    return _dot(hi, b) + _dot(lo, b)


def _head_norm(z, bd):
    ms = _dot_split(z * z, bd)
    return z * lax.rsqrt(ms + EPS)


def _rope(z, cos, sin_lo, sin_hi):
    w = z.shape[-1]
    half = HEAD_DIM // 2
    return z * cos + pltpu.roll(z, w - half, 1) * sin_lo + pltpu.roll(z, half, 1) * sin_hi


def _tile_lanes(t, n):
    return t if n == 1 else jnp.concatenate([t] * n, axis=-1)


def _inproj_kernel(x_ref, sc_ref, sh_ref, w_ref, bdq_ref, bdk_ref, qg_ref, kg_ref,
                   cos_ref, sl_ref, shi_ref, lng_ref, lnb_ref,
                   qn_ref, qr_ref, ks_ref, kw_ref, vs_ref, vw_ref, kc_ref, vc_ref,
                   g_ref, u_ref, v_ref, ga_ref, gb_ref):
    tm = x_ref.shape[1]
    x = x_ref[0]
    ms = jnp.mean(x * x, axis=-1, keepdims=True)
    h = (x * lax.rsqrt(ms + EPS)) * (1.0 + sc_ref[0]) + sh_ref[0]
    hb = h.astype(BF16)

    def mm(lo, width):
        return _dot(hb, w_ref[:, lo:lo + width])

    cos, sl, shi = cos_ref[0], sl_ref[0], shi_ref[0]

    zq = mm(_C_Q, Q_WIDTH)
    qn = _head_norm(zq, bdq_ref[...]) * qg_ref[...]
    qr = _rope(qn, _tile_lanes(cos, 4), _tile_lanes(sl, 4), _tile_lanes(shi, 4))
    qn_ref[0] = qn.astype(BF16)
    qr_ref[0] = qr.astype(BF16)

    zk = mm(_C_K, 2 * KV_WIDTH)
    kn = _head_norm(zk, bdk_ref[...]) * kg_ref[...]
    kr = _rope(kn, _tile_lanes(cos, 2), _tile_lanes(sl, 2), _tile_lanes(shi, 2))
    lane = lax.broadcasted_iota(jnp.int32, (tm, LANES), 1)
    tok = pl.program_id(1) * tm + lax.broadcasted_iota(jnp.int32, (tm, LANES), 0)
    onehot = jnp.where(lane - HEAD_DIM == tok // SEL_LEN, 1.0, 0.0)
    ones_col = jnp.where(lane == HEAD_DIM, 1.0, 0.0)
    low = lane < HEAD_DIM
    zvs = mm(_C_VS, KV_WIDTH)
    zvw = mm(_C_VW, KV_WIDTH)
    zkc = mm(_C_KC, KV_WIDTH)
    zvc = mm(_C_VC, KV_WIDTH)
    for kv in range(NSA_KV_HEADS):
        def head(a):
            return a if kv == 0 else pltpu.roll(a, HEAD_DIM, 1)
        ks_ref[0, kv] = jnp.where(low, head(kr[:, :KV_WIDTH]), onehot).T.astype(BF16)
        kw_ref[0, kv] = jnp.where(low, head(kr[:, KV_WIDTH:]), 0.0).T.astype(BF16)
        vs_ref[0, kv] = jnp.where(low, head(zvs), ones_col).astype(BF16)
        vw_ref[0, kv] = jnp.where(low, head(zvw), ones_col).astype(BF16)
        kc_ref[0, kv] = head(zkc)[:, :HEAD_DIM]
        vc_ref[0, kv] = head(zvc)[:, :HEAD_DIM]

    zg = mm(_C_G, 2 * LANES)
    sg = _sigmoid(zg)
    g_ref[0, 0] = sg[:, :LANES]
    g_ref[0, 1] = sg[:, LANES:]

    u_ref[0] = _gelu(mm(_C_U, GM_WIDTH)).astype(BF16)
    gv = _gelu(mm(_C_V, GM_WIDTH))
    mu = jnp.mean(gv, axis=-1, keepdims=True)
    cen = gv - mu
    var = jnp.mean(cen * cen, axis=-1, keepdims=True)
    v_ref[0] = ((cen * lax.rsqrt(var + EPS)) * lng_ref[...] + lnb_ref[...]).astype(BF16)

    ga_ref[0] = _sigmoid(mm(_C_GA, D_MODEL)).astype(BF16)
    gb_ref[0] = _sigmoid(mm(_C_GB, D_MODEL)).astype(BF16)


def _block_diag_mean(width):
    idx = np.arange(width) // HEAD_DIM
    return jnp.asarray((idx[:, None] == idx[None, :]).astype(np.float32) / HEAD_DIM).astype(BF16)


def _inproj_call(x, sc, sh, w_p, qg, kg, cos, sl, shi, lng, lnb, tm):
    B, S, D = x.shape
    H = NSA_KV_HEADS
    full = lambda *shape: pl.BlockSpec(shape, lambda b, i: (0,) * len(shape))
    tok3 = lambda w: pl.BlockSpec((1, tm, w), lambda b, i: (b, i, 0))
    per_b = pl.BlockSpec((1, 1, D), lambda b, i: (b, 0, 0))
    kv4 = lambda w: pl.BlockSpec((1, H, tm, w), lambda b, i: (b, 0, i, 0))
    kt4 = pl.BlockSpec((1, H, LANES, tm), lambda b, i: (b, 0, 0, i))
    sds = jax.ShapeDtypeStruct
    out_shape = [
        sds((B, S, Q_WIDTH), BF16), sds((B, S, Q_WIDTH), BF16),
        sds((B, H, LANES, S), BF16), sds((B, H, LANES, S), BF16),
        sds((B, H, S, LANES), BF16), sds((B, H, S, LANES), BF16),
        sds((B, H, S, HEAD_DIM), F32), sds((B, H, S, HEAD_DIM), F32),
        sds((B, H, S, LANES), F32),
        sds((B, S, GM_WIDTH), BF16), sds((B, S, GM_WIDTH), BF16),
        sds((B, S, D), BF16), sds((B, S, D), BF16),
    ]
    out_specs = [
        tok3(Q_WIDTH), tok3(Q_WIDTH), kt4, kt4, kv4(LANES), kv4(LANES),
        kv4(HEAD_DIM), kv4(HEAD_DIM), kv4(LANES),
        tok3(GM_WIDTH), tok3(GM_WIDTH), tok3(D), tok3(D),
    ]
    return pl.pallas_call(
        _inproj_kernel,
        grid=(B, S // tm),
        in_specs=[tok3(D), per_b, per_b, full(D, IN_COLS_P),
                  full(Q_WIDTH, Q_WIDTH), full(2 * KV_WIDTH, 2 * KV_WIDTH),
                  full(1, Q_WIDTH), full(1, 2 * KV_WIDTH),
                  tok3(LANES), tok3(LANES), tok3(LANES),
                  full(1, GM_WIDTH), full(1, GM_WIDTH)],
        out_specs=out_specs,
        out_shape=out_shape,
        compiler_params=_cparams("arbitrary", "arbitrary"),
        name="in_proj",
    )(x, sc, sh, w_p, _block_diag_mean(Q_WIDTH), _block_diag_mean(2 * KV_WIDTH), qg, kg,
      cos, sl, shi, lng, lnb)


def _compress_kernel(kr_ref, vr_ref, w1k_ref, w2k_ref, pek_ref, w1v_ref, w2v_ref, pev_ref,
                     kg_ref, kc_ref, vc_ref):
    nc = kr_ref.shape[2]
    half = CMP_STRIDE * HEAD_DIM

    def mlp(raw, w1_ref, w2_ref, pe_ref):
        a = raw.astype(BF16)
        top = _dot(a, w1_ref[:half, :])
        bot = _dot(a, w1_ref[half:, :])
        pe = jnp.broadcast_to(pe_ref[...], (8, 2 * half)).astype(BF16)
        pe_row = _dot(pe, w1_ref[...])[0:1, :]
        hid = top + pltpu.roll(bot, nc - 1, 0) + pe_row
        return _dot(_gelu(hid).astype(BF16), w2_ref[...])

    kc = mlp(kr_ref[0, 0], w1k_ref, w2k_ref, pek_ref)
    ms = jnp.mean(kc * kc, axis=-1, keepdims=True)
    kc_ref[0, 0] = (kc * lax.rsqrt(ms + EPS) * kg_ref[...]).astype(BF16)
    vc_ref[0, 0] = mlp(vr_ref[0, 0], w1v_ref, w2v_ref, pev_ref).astype(BF16)


def _compress_call(kc_raw, vc_raw, w1k, w2k, pek, w1v, w2v, pev, kg):
    B, H, S, hd = kc_raw.shape
    nc = S // CMP_STRIDE
    feat = CMP_STRIDE * hd
    raw = pl.BlockSpec((1, 1, nc, feat), lambda b, h: (b, h, 0, 0))
    full = lambda *shape: pl.BlockSpec(shape, lambda b, h: (0,) * len(shape))
    out = pl.BlockSpec((1, 1, nc, hd), lambda b, h: (b, h, 0, 0))
    return pl.pallas_call(
        _compress_kernel,
        grid=(B, H),
        in_specs=[raw, raw, full(2 * feat, CMP_HIDDEN), full(CMP_HIDDEN, hd), full(1, 2 * feat),
                  full(2 * feat, CMP_HIDDEN), full(CMP_HIDDEN, hd), full(1, 2 * feat),
                  full(1, hd)],
        out_specs=[out, out],
        out_shape=[jax.ShapeDtypeStruct((B, H, nc, hd), BF16)] * 2,
        compiler_params=_cparams("arbitrary", "arbitrary"),
        name="compress",
    )(kc_raw.reshape(B, H, nc, feat), vc_raw.reshape(B, H, nc, feat),
      w1k, w2k, pek, w1v, w2v, pev, kg)


def _group_rows(a):
    return jnp.concatenate([a[:, g * HEAD_DIM:(g + 1) * HEAD_DIM] for g in range(NSA_GROUP)], axis=0)


def _nsa_cmp_kernel(qn_ref, qr_ref, kc_ref, vc_ref, ovl_ref, oc_ref, qa_ref, *, n_sel):
    tq = qn_ref.shape[1]
    nc = kc_ref.shape[2]
    G = NSA_GROUP
    q0 = pl.program_id(2) * tq
    q4 = _group_rows(qn_ref[0])
    s = _dot_nt(q4, kc_ref[0, 0])
    row = lax.broadcasted_iota(jnp.int32, (G, tq, nc), 1).reshape(G * tq, nc)
    col = lax.broadcasted_iota(jnp.int32, (G * tq, nc), 1)
    vis = col * CMP_STRIDE + (CMP_LEN - 1) <= q0 + row
    s = jnp.where(vis, s, MASK_NEG)
    m = jnp.max(s, axis=-1, keepdims=True)
    e = jnp.where(vis, jnp.exp(s - m), 0.0)
    p = e / jnp.maximum(jnp.sum(e, axis=-1, keepdims=True), 1e-30)
    oc = _dot(p.astype(BF16), vc_ref[0, 0])
    oc_ref[0, 0] = oc.reshape(G, tq, HEAD_DIM)

    psum = p[0:tq] + p[tq:2 * tq] + p[2 * tq:3 * tq] + p[3 * tq:4 * tq]
    p_hi = psum.astype(BF16)
    p_lo = (psum - p_hi.astype(F32)).astype(BF16)
    imp = (_dot_nt(ovl_ref[...], p_hi) + _dot_nt(ovl_ref[...], p_lo))[:SEL_BIAS_WIDTH]
    blk = lax.broadcasted_iota(jnp.int32, (SEL_BIAS_WIDTH, tq), 0)
    cur = (q0 + lax.broadcasted_iota(jnp.int32, (SEL_BIAS_WIDTH, tq), 1)) // SEL_LEN
    valid = blk <= cur
    forced = (blk == 0) | (blk == cur) | (blk == cur - 1)
    cand = valid & jnp.logical_not(forced)
    n_forced = jnp.minimum(cur, 2) + 1
    val = jnp.where(cand, imp, -1.0)
    cnt = jnp.zeros((SEL_BIAS_WIDTH, tq), F32)
    for j in range(n_sel):
        vj = jnp.broadcast_to(val[j:j + 1, :], (SEL_BIAS_WIDTH, tq))
        cnt = cnt + jnp.where(blk > j, jnp.where(vj >= val, 1.0, 0.0), jnp.where(vj > val, 1.0, 0.0))
    free = (min(SEL_TOPN, n_sel) - n_forced).astype(F32)
    sel = (forced & valid) | (cand & (cnt < free))
    sel_t = jnp.concatenate([jnp.where(sel, 1.0, 0.0), jnp.zeros((LANES - SEL_BIAS_WIDTH, tq), F32)], axis=0)
    sel_q = sel_t.T
    bias = jnp.where(sel_q > 0.5, 0.0, SEL_NEG)
    lane = lax.broadcasted_iota(jnp.int32, (tq, LANES), 1)
    bias_hi = pltpu.roll(bias, SEL_BIAS_WIDTH, 1)
    qr = qr_ref[0]
    for g in range(G):
        qg = qr[:, g * HEAD_DIM:(g + 1) * HEAD_DIM].astype(F32)
        qg = jnp.concatenate([qg, qg], axis=-1)
        qa_ref[0, 0, g] = jnp.where(lane < HEAD_DIM, qg, bias_hi).astype(BF16)


def _overlap_t(S):
    n_cmp = (S - CMP_LEN) // CMP_STRIDE + 1
    nc = S // CMP_STRIDE
    n_sel = S // SEL_LEN
    start = np.arange(nc) * CMP_STRIDE
    end = start + CMP_LEN - 1
    sel_start = np.arange(n_sel) * SEL_LEN
    ov = (start[None, :] <= sel_start[:, None] + SEL_LEN - 1) & (end[None, :] >= sel_start[:, None])
    ov = ov & (np.arange(nc) < n_cmp)[None, :]
    out = np.zeros((LANES, nc), np.float32)
    out[:n_sel] = ov.astype(np.float32)
    return jnp.asarray(out).astype(BF16)


def _nsa_cmp_call(qn, qr, kc, vc, tq):
    B, S, _ = qn.shape
    H, G = NSA_KV_HEADS, NSA_GROUP
    nc = kc.shape[2]
    n_sel = S // SEL_LEN
    assert n_sel <= SEL_BIAS_WIDTH and nc % LANES == 0
    qspec = pl.BlockSpec((1, tq, G * HEAD_DIM), lambda b, h, i: (b, i, h))
    cspec = pl.BlockSpec((1, 1, nc, HEAD_DIM), lambda b, h, i: (b, h, 0, 0))
    return pl.pallas_call(
        functools.partial(_nsa_cmp_kernel, n_sel=n_sel),
        grid=(B, H, S // tq),
        in_specs=[qspec, qspec, cspec, cspec, pl.BlockSpec((LANES, nc), lambda b, h, i: (0, 0))],
        out_specs=[pl.BlockSpec((1, 1, G, tq, HEAD_DIM), lambda b, h, i: (b, h, 0, i, 0)),
                   pl.BlockSpec((1, 1, G, tq, LANES), lambda b, h, i: (b, h, 0, i, 0))],
        out_shape=[jax.ShapeDtypeStruct((B, H, G, S, HEAD_DIM), F32),
                   jax.ShapeDtypeStruct((B, H, G, S, LANES), BF16)],
        compiler_params=_cparams("arbitrary", "arbitrary", "arbitrary"),
        name="nsa_cmp",
    )(qn, qr, kc, vc, _overlap_t(S))


SEL_CHUNK = 4


def _nsa_flash_kernel(qa_ref, ks_ref, vs_ref, kw_ref, vw_ref, oc_ref, g_ref, o_ref, m_scr, acc_scr):
    G = NSA_GROUP
    tq = qa_ref.shape[3]
    R = G * tq
    tk = tq
    i = pl.program_id(2)
    q0 = i * tq

    def rows_of(ref, j0, nt):
        return ref[0, 0, pl.ds(pl.multiple_of(j0 * tk, tk), nt * tk), :]

    def attend(k_ref, v_ref, j0, nt, visible, state):
        v = rows_of(v_ref, j0, nt)
        kt = k_ref[0, 0, :, pl.ds(pl.multiple_of(j0 * tk, tk), nt * tk)]
        s_all = _dot(qa_ref[0, 0].reshape(R, LANES), kt)
        ss = [s_all[g * tq:(g + 1) * tq] for g in range(G)]
        if visible is not None:
            qpos = q0 + lax.broadcasted_iota(jnp.int32, (tq, nt * tk), 0)
            kpos = j0 * tk + lax.broadcasted_iota(jnp.int32, (tq, nt * tk), 1)
            mask = visible(qpos, kpos)
        ps, alphas = [], []
        for g in range(G):
            s = ss[g] if visible is None else jnp.where(mask, ss[g], MASK_NEG)
            cols = [s[:, c * LANES:(c + 1) * LANES] for c in range(nt * tk // LANES)]
            m_new = jnp.max(functools.reduce(jnp.maximum, cols), axis=-1, keepdims=True)
            if state:
                m_prev = m_scr[g]
                m_new = jnp.maximum(m_prev, m_new)
                alphas.append(jnp.exp(m_prev - m_new))
                m_scr[g] = m_new
            ps.append(jnp.concatenate([jnp.exp(c - m_new) for c in cols], axis=-1).astype(BF16))
        if not state:
            return [_dot(ps[g], v) for g in range(G)]
        for g in range(G):
            acc_scr[g] = alphas[g] * acc_scr[g] + _dot(ps[g], v)

    def normalise(acc):
        return acc[:, :HEAD_DIM] / acc[:, HEAD_DIM:HEAD_DIM + 1]

    m_scr[...] = jnp.full((G, tq, LANES), MASK_NEG, F32)
    acc_scr[...] = jnp.zeros((G, tq, LANES), F32)
    causal = lambda qpos, kpos: kpos <= qpos

    def sel_body(c, carry):
        attend(ks_ref, vs_ref, c * SEL_CHUNK, SEL_CHUNK, None, True)
        return carry

    n_full = i // SEL_CHUNK
    lax.fori_loop(0, n_full, sel_body, 0)
    attend(ks_ref, vs_ref, n_full * SEL_CHUNK, 2, causal, True)

    @pl.when(i % SEL_CHUNK >= 2)
    def _():
        attend(ks_ref, vs_ref, n_full * SEL_CHUNK + 2, 2, causal, True)

    o_sel = normalise(acc_scr[...].reshape(R, LANES))

    assert WINDOW == 2 * tk
    band = lambda qpos, kpos: (kpos <= qpos) & (kpos > qpos - WINDOW)
    o_win = normalise(jnp.concatenate(attend(kw_ref, vw_ref, jnp.maximum(i - 2, 0), 3, band, False), axis=0))

    o_cmp = oc_ref[0, 0].reshape(R, HEAD_DIM)
    gates = g_ref[0, 0]
    outs = []
    for g in range(G):
        r = slice(g * tq, (g + 1) * tq)
        outs.append(gates[:, 3 * g:3 * g + 1] * o_cmp[r]
                    + gates[:, 3 * g + 1:3 * g + 2] * o_sel[r]
                    + gates[:, 3 * g + 2:3 * g + 3] * o_win[r])
    o_ref[0] = jnp.concatenate(outs, axis=-1).astype(BF16)


def _nsa_flash_call(qa, ks, vs, kw, vw, oc, gates):
    B, H, G, S, _ = qa.shape
    tq = WINDOW // 2
    assert (S // tq) % SEL_CHUNK == 0
    kt = pl.BlockSpec((1, 1, LANES, S), lambda b, h, i: (b, h, 0, 0))
    vv = pl.BlockSpec((1, 1, S, LANES), lambda b, h, i: (b, h, 0, 0))
    return pl.pallas_call(
        _nsa_flash_kernel,
        grid=(B, H, S // tq),
        in_specs=[pl.BlockSpec((1, 1, G, tq, LANES), lambda b, h, i: (b, h, 0, i, 0)),
                  kt, vv, kt, vv,
                  pl.BlockSpec((1, 1, G, tq, HEAD_DIM), lambda b, h, i: (b, h, 0, i, 0)),
                  pl.BlockSpec((1, 1, tq, LANES), lambda b, h, i: (b, h, i, 0))],
        out_specs=pl.BlockSpec((1, tq, G * HEAD_DIM), lambda b, h, i: (b, i, h)),
        out_shape=jax.ShapeDtypeStruct((B, S, Q_WIDTH), BF16),
        scratch_shapes=[pltpu.VMEM((G, tq, LANES), F32), pltpu.VMEM((G, tq, LANES), F32)],
        compiler_params=_cparams("arbitrary", "arbitrary", "arbitrary"),
        name="nsa_flash",
    )(qa, ks, vs, kw, vw, oc, gates)


def _gmlp_kernel(u_ref, v_ref, ws_ref, bs_ref, o_ref):
    tm = u_ref.shape[1]
    C = GM_CHUNK
    r = lax.broadcasted_iota(jnp.int32, (C, C), 0)
    c = lax.broadcasted_iota(jnp.int32, (C, C), 1)
    causal = c <= r
    ws = [jnp.where(causal, ws_ref[g], 0.0).astype(BF16) for g in range(GM_GROUPS)]
    for n in range(tm // C):
        rows = slice(n * C, (n + 1) * C)
        vn = v_ref[0, rows, :]
        mixed = jnp.concatenate(
            [_dot(ws[g], vn[:, g * GM_GROUP_DIM:(g + 1) * GM_GROUP_DIM]) for g in range(GM_GROUPS)],
            axis=-1)
        o_ref[0, rows, :] = (u_ref[0, rows, :].astype(F32) * (mixed + bs_ref[...])).astype(BF16)


def _gmlp_call(u, v, ws, bs_full, tm):
    B, S, W = u.shape
    tok = pl.BlockSpec((1, tm, W), lambda b, i: (b, i, 0))
    return pl.pallas_call(
        _gmlp_kernel,
        grid=(B, S // tm),
        in_specs=[tok, tok,
                  pl.BlockSpec((GM_GROUPS, GM_CHUNK, GM_CHUNK), lambda b, i: (0, 0, 0)),
                  pl.BlockSpec((GM_CHUNK, W), lambda b, i: (0, 0))],
        out_specs=tok,
        out_shape=jax.ShapeDtypeStruct((B, S, W), BF16),
        compiler_params=_cparams("arbitrary", "arbitrary"),
        name="gmlp",
    )(u, v, ws, bs_full)


def _mixout_kernel(on_ref, og_ref, ga_ref, gb_ref, x_ref, gt_ref, sc_ref, sh_ref,
                   wpn_ref, wpg_ref, wo_ref, wr_ref, wgs_ref, wus_ref, wds_ref,
                   x1_ref, h2_ref, lg_ref, shd_ref):
    ya = _dot(on_ref[0], wpn_ref[...])
    yb = _dot(og_ref[0], wpg_ref[...])
    merged = ga_ref[0].astype(F32) * ya + gb_ref[0].astype(F32) * yb
    x1 = x_ref[0] + gt_ref[0] * _dot(merged.astype(BF16), wo_ref[...])
    x1_ref[0] = x1
    ms = jnp.mean(x1 * x1, axis=-1, keepdims=True)
    h2 = (x1 * lax.rsqrt(ms + EPS)) * (1.0 + sc_ref[0]) + sh_ref[0]
    hb = h2.astype(BF16)
    lg_ref[0] = _dot_nt(wr_ref[...], hb)
    h2_ref[0] = _pack_halves(hb)
    act = _silu(_dot(hb, wgs_ref[...])) * _dot(hb, wus_ref[...])
    shd_ref[0] = _dot(act.astype(BF16), wds_ref[...])


def _mixout_call(o_nsa, o_gm, ga, gb, x, gt, sc, sh, wpn, wpg, wo, wr, wgs, wus, wds, tm):
    B, S, D = x.shape
    tok = lambda w: pl.BlockSpec((1, tm, w), lambda b, i: (b, i, 0))
    per_b = pl.BlockSpec((1, 1, D), lambda b, i: (b, 0, 0))
    full = lambda a: pl.BlockSpec(a.shape, lambda b, i: (0,) * a.ndim)
    sds = jax.ShapeDtypeStruct
    return pl.pallas_call(
        _mixout_kernel,
        grid=(B, S // tm),
        in_specs=[tok(Q_WIDTH), tok(GM_WIDTH), tok(D), tok(D), tok(D), per_b, per_b, per_b,
                  full(wpn), full(wpg), full(wo), full(wr), full(wgs), full(wus), full(wds)],
        out_specs=[tok(D), tok(D // 2), pl.BlockSpec((1, N_EXPERTS, tm), lambda b, i: (b, 0, i)), tok(D)],
        out_shape=[sds((B, S, D), F32), sds((B, S, D // 2), jnp.uint32), sds((B, N_EXPERTS, S), F32),
                   sds((B, S, D), F32)],
        compiler_params=_cparams("arbitrary", "arbitrary"),
        name="mix_out",
    )(o_nsa, o_gm, ga, gb, x, gt, sc, sh, wpn, wpg, wo, wr, wgs, wus, wds)


def _route_kernel(lg_ref, br_ref, idx_ref, wt_ref, rank_ref, cnt_ref, run_scr):
    E, tm = lg_ref.shape[1], lg_ref.shape[2]

    @pl.when(pl.program_id(0) == 0)
    def _():
        run_scr[...] = jnp.zeros_like(run_scr)

    aff = _sigmoid(lg_ref[0])
    work = aff + br_ref[...]
    row = lax.broadcasted_iota(jnp.int32, (E, tm), 0).astype(F32)
    picked = jnp.zeros((E, tm), F32)
    idxs, tops = [], []
    for _ in range(TOP_K):
        m = jnp.max(work, axis=0, keepdims=True)
        idx = jnp.min(jnp.where(work == m, row, float(E)), axis=0, keepdims=True)
        hit = row == idx
        tops.append(jnp.sum(jnp.where(hit, aff, 0.0), axis=0, keepdims=True))
        idxs.append(idx)
        picked = jnp.where(hit, 1.0, picked)
        work = jnp.where(hit, -jnp.inf, work)
    total = functools.reduce(jnp.add, tops)
    r = lax.broadcasted_iota(jnp.int32, (tm, tm), 0)
    c = lax.broadcasted_iota(jnp.int32, (tm, tm), 1)
    before = _dot(picked.astype(BF16), jnp.where(r < c, 1.0, 0.0).astype(BF16)) + run_scr[...]
    ranks = [jnp.sum(jnp.where(row == idx, before, 0.0), axis=0, keepdims=True) for idx in idxs]
    run_scr[...] = run_scr[...] + jnp.sum(picked, axis=1, keepdims=True)
    cnt_ref[...] = run_scr[...]
    idx_ref[...] = jnp.concatenate(idxs, axis=0).astype(jnp.int32)
    rank_ref[...] = jnp.concatenate(ranks, axis=0).astype(jnp.int32)
    wt = jnp.concatenate([t / total * ROUTE_SCALE for t in tops]
                         + [jnp.zeros((LANES - TOP_K, tm), F32)], axis=0)
    eye = jnp.where(r == c, 1.0, 0.0).astype(BF16)
    cols = jnp.zeros((tm, LANES), F32)
    rest = wt
    for _ in range(3):
        part = rest.astype(BF16)
        cols = cols + _dot_nt(eye, part)
        rest = rest - part.astype(F32)
    wt_ref[...] = cols


def _route_call(logits_t, b_router, tm):
    B, E, S = logits_t.shape
    per_b = S // tm
    T = B * S
    kt = pl.BlockSpec((TOP_K, tm), lambda i: (0, i))
    col = pl.BlockSpec((E, 1), lambda i: (0, 0))
    sds = jax.ShapeDtypeStruct
    return pl.pallas_call(
        _route_kernel,
        grid=(T // tm,),
        in_specs=[pl.BlockSpec((1, E, tm), lambda i: (i // per_b, 0, i % per_b)), col],
        out_specs=[kt, pl.BlockSpec((tm, LANES), lambda i: (i, 0)), kt, col],
        out_shape=[sds((TOP_K, T), jnp.int32), sds((T, LANES), F32), sds((TOP_K, T), jnp.int32),
                   sds((E, 1), F32)],
        scratch_shapes=[pltpu.VMEM((E, 1), F32)],
        compiler_params=_cparams("arbitrary"),
        name="route",
    )(logits_t, b_router.reshape(E, 1))


def _slot_kernel(idx_ref, rank_ref, ps_ref, o_ref):
    tm = idx_ref.shape[1]
    E = ps_ref.shape[0]
    row = lax.broadcasted_iota(jnp.int32, (E, tm), 0)
    idx, rank, ps = idx_ref[...], rank_ref[...], ps_ref[...]
    base = [jnp.sum(jnp.where(row == idx[k:k + 1, :], ps, 0.0), axis=0, keepdims=True) for k in range(TOP_K)]
    o_ref[...] = jnp.concatenate(base, axis=0).astype(jnp.int32) + rank


def _slot_call(idx, rank, pad_start, tm):
    T = idx.shape[1]
    E = pad_start.shape[0]
    kt = pl.BlockSpec((TOP_K, tm), lambda i: (0, i))
    return pl.pallas_call(
        _slot_kernel,
        grid=(T // tm,),
        in_specs=[kt, kt, pl.BlockSpec((E, 1), lambda i: (0, 0))],
        out_specs=kt,
        out_shape=jax.ShapeDtypeStruct((TOP_K, T), jnp.int32),
        compiler_params=_cparams("arbitrary"),
        name="slot",
    )(idx, rank, pad_start.astype(F32).reshape(E, 1))


def _sc_mesh():
    mesh = plsc.VectorSubcoreMesh(core_axis_name="core", subcore_axis_name="subcore")
    return mesh, mesh.num_cores * mesh.num_subcores


def _sc_worker(mesh):
    return lax.axis_index("subcore") * mesh.num_cores + lax.axis_index("core")


def _scatter_rows(rows, indices_by_k, out_ref):
    n, width = rows.shape
    mesh, workers = _sc_mesh()
    per_worker = n // workers
    assert per_worker * workers == n and per_worker % SC_WINDOW == 0

    @pl.kernel(out_type=(), mesh=mesh, name="scatter_rows",
               scratch_types=[pltpu.VMEM((SC_WINDOW,), jnp.int32), pltpu.VMEM((SC_WINDOW, width), rows.dtype)])
    def scatter(rows_hbm, idx_hbm, out_hbm, idx_v, rows_v):
        worker = _sc_worker(mesh)

        @pl.loop(0, per_worker // SC_WINDOW)
        def _(j):
            base = pl.multiple_of(worker * per_worker + j * SC_WINDOW, SC_WINDOW)
            pltpu.sync_copy(rows_hbm.at[pl.ds(base, SC_WINDOW)], rows_v)
            for k in range(TOP_K):
                pltpu.sync_copy(idx_hbm.at[pl.ds(pl.multiple_of(k * n + base, SC_WINDOW), SC_WINDOW)], idx_v)
                pltpu.sync_copy(rows_v, out_hbm.at[idx_v])

    scatter(rows, indices_by_k, out_ref)


def _expert_kernel(ie_ref, ir_ref, ig_ref, n_ref, x_hbm, wg_ref, wu_ref, wd_ref, y_hbm,
                   xbuf, ybuf, act_scr, xsem, ysem):
    w = pl.program_id(0)
    n = n_ref[0]
    R = xbuf.shape[1]
    last = ie_ref.shape[0] - 1

    def x_copy(item, slot):
        row = pl.multiple_of(ir_ref[jnp.minimum(item, last)], MOE_GRAN)
        return pltpu.make_async_copy(x_hbm.at[pl.ds(row, R), :], xbuf.at[slot], xsem.at[slot])

    def y_copy(item, slot, gi):
        row = pl.multiple_of(ir_ref[jnp.minimum(item, last)] + gi * MOE_GRAN, MOE_GRAN)
        src = ybuf.at[slot, pl.ds(pl.multiple_of(gi * MOE_GRAN, MOE_GRAN), MOE_GRAN), :]
        return pltpu.make_async_copy(src, y_hbm.at[pl.ds(row, MOE_GRAN), :], ysem.at[slot])

    def for_granules(item, fn):
        def body(gi, carry):
            fn(gi)
            return carry
        lax.fori_loop(0, ig_ref[jnp.clip(item, 0, last)], body, 0)

    def gate_up(slot):
        lo, hi = _unpack_halves(xbuf[slot])
        x = jnp.concatenate([lo, hi], axis=1).astype(BF16)
        a = _dot(x, wg_ref[0, 0].astype(BF16))
        b = _dot(x, wu_ref[0, 0].astype(BF16))
        return (_silu(a) * b).astype(BF16)

    def down(act):
        return _pack_halves(_dot(act, wd_ref[0, 0].astype(BF16)))

    cur, prv = w % 2, (w + 1) % 2

    @pl.when(w == 0)
    def _():
        x_copy(0, 0).start()

    @pl.when(w < n)
    def _():
        x_copy(w, cur).wait()

    @pl.when(w + 1 < n)
    def _():
        x_copy(w + 1, prv).start()

    @pl.when((w >= 3) & (w - 3 < n))
    def _():
        for_granules(w - 3, lambda gi: y_copy(w - 3, prv, gi).wait())

    @pl.when(w == 0)
    def _():
        act_scr[...] = gate_up(0)

    @pl.when((w >= 1) & (w < n))
    def _():
        prev = act_scr[...]
        ybuf[prv] = down(prev)
        act_scr[...] = gate_up(cur)

    @pl.when((w >= 1) & (w == n))
    def _():
        ybuf[prv] = down(act_scr[...])

    @pl.when((w >= 1) & (w <= n))
    def _():
        for_granules(w - 1, lambda gi: y_copy(w - 1, prv, gi).start())


def _expert_call(layer, item_e, item_row, item_ng, n_items, xs, wg, wu, wd):
    n_alloc, W = xs.shape
    R = MOE_ROWS
    nw = item_e.shape[0]
    D, F = wg.shape[2], wg.shape[3]
    cur = lambda w, ie, ir, ig, n: (layer, ie[jnp.minimum(w, nw - 1)], 0, 0)
    prev = lambda w, ie, ir, ig, n: (layer, ie[jnp.clip(w - 1, 0, nw - 1)], 0, 0)
    grid_spec = pltpu.PrefetchScalarGridSpec(
        num_scalar_prefetch=4,
        grid=(nw + 3,),
        in_specs=[pl.BlockSpec(memory_space=pl.ANY),
                  pl.BlockSpec((1, 1, D, F), cur), pl.BlockSpec((1, 1, D, F), cur),
                  pl.BlockSpec((1, 1, F, D), prev)],
        out_specs=pl.BlockSpec(memory_space=pl.ANY),
        scratch_shapes=[pltpu.VMEM((2, R, W), jnp.uint32), pltpu.VMEM((2, R, W), jnp.uint32),
                        pltpu.VMEM((R, F), BF16),
                        pltpu.SemaphoreType.DMA((2,)), pltpu.SemaphoreType.DMA((2,))],
    )
    return pl.pallas_call(
        _expert_kernel,
        grid_spec=grid_spec,
        out_shape=jax.ShapeDtypeStruct((n_alloc, W), jnp.uint32),
        input_output_aliases={4: 0},
        compiler_params=pltpu.CompilerParams(dimension_semantics=("arbitrary",), vmem_limit_bytes=VMEM_LIMIT,
                                             has_side_effects=True),
        name="expert",
    )(item_e, item_row, item_ng, n_items, xs, wg, wu, wd)


def _gather_rows(table, indices):
    n = indices.shape[0]
    width = table.shape[1]
    mesh, workers = _sc_mesh()
    per_worker = n // workers
    assert per_worker * workers == n and per_worker % SC_WINDOW == 0

    @pl.kernel(out_type=jax.ShapeDtypeStruct((n, width), table.dtype), mesh=mesh, name="gather_rows",
               scratch_types=[pltpu.VMEM((SC_WINDOW,), jnp.int32), pltpu.VMEM((SC_WINDOW, width), table.dtype),
                              pltpu.SemaphoreType.DMA])
    def gather(table_hbm, idx_hbm, out_hbm, idx_v, rows_v, sem):
        worker = _sc_worker(mesh)

        @pl.loop(0, per_worker // SC_WINDOW)
        def _(j):
            base = pl.multiple_of(worker * per_worker + j * SC_WINDOW, SC_WINDOW)
            pltpu.sync_copy(idx_hbm.at[pl.ds(base, SC_WINDOW)], idx_v)
            pltpu.async_copy(table_hbm.at[idx_v], rows_v, sem).wait()
            pltpu.sync_copy(rows_v, out_hbm.at[pl.ds(base, SC_WINDOW)])

    return gather(table, indices)


def _combine_kernel(y_ref, wt_ref, shd_ref, x_ref, gt_ref, o_ref):
    wt = wt_ref[...]
    acc_lo = acc_hi = None
    for k in range(TOP_K):
        lo, hi = _unpack_halves(y_ref[k])
        w = wt[:, k:k + 1]
        acc_lo = w * lo if k == 0 else acc_lo + w * lo
        acc_hi = w * hi if k == 0 else acc_hi + w * hi
    routed = jnp.concatenate([acc_lo, acc_hi], axis=1)
    o_ref[...] = x_ref[...] + gt_ref[0] * (routed + shd_ref[...])


def _combine_call(yk, wts, shared, x1, gt, tm):
    T, D = x1.shape
    B = gt.shape[0]
    per_b = T // B // tm
    tok = lambda w: pl.BlockSpec((tm, w), lambda i: (i, 0))
    return pl.pallas_call(
        _combine_kernel,
        grid=(T // tm,),
        in_specs=[pl.BlockSpec((TOP_K, tm, yk.shape[2]), lambda i: (0, i, 0)),
                  tok(LANES), tok(D), tok(D),
                  pl.BlockSpec((1, 1, D), lambda i: (i // per_b, 0, 0))],
        out_specs=tok(D),
        out_shape=jax.ShapeDtypeStruct((T, D), F32),
        compiler_params=_cparams("arbitrary"),
        name="combine",
    )(yk, wts, shared, x1, gt)


def _rope_tables(positions):
    half = HEAD_DIM // 2
    inv = ROPE_THETA ** (-jnp.arange(half, dtype=F32) / half)
    ang = positions.astype(F32)[..., None] * inv
    cos, sin, zero = jnp.cos(ang), jnp.sin(ang), jnp.zeros_like(ang)
    cos_t = jnp.concatenate([cos, cos] * 2, axis=-1)
    sin_lo = jnp.concatenate([-sin, zero] * 2, axis=-1)
    sin_hi = jnp.concatenate([zero, sin] * 2, axis=-1)
    return cos_t, sin_lo, sin_hi


def _reorder_w_in(w):
    o = np.cumsum([0, Q_WIDTH] + [KV_WIDTH] * 6 + [NSA_Q_HEADS * 3, GM_WIDTH, GM_WIDTH, D_MODEL, D_MODEL])
    q, kc, vc, ks, vs, kw, vw, g, u, v, ga, gb = [w[:, o[i]:o[i + 1]] for i in range(12)]
    per = NSA_GROUP * 3
    pad = jnp.zeros((w.shape[0], LANES - per), w.dtype)
    return jnp.concatenate([q, ks, kw, kc, vc, vs, vw, g[:, :per], pad, g[:, per:], pad, u, v, ga, gb], axis=1)


def _owner(ends, pos):
    return jnp.minimum(jnp.sum((ends[None, :] <= pos[:, None]).astype(jnp.int32), axis=1), ends.shape[0] - 1)


def _lookup(table, idx):
    hit = idx[:, None] == jnp.arange(table.shape[0], dtype=jnp.int32)[None, :]
    return jnp.sum(jnp.where(hit, table[None, :], 0), axis=1)


def _expert_plan(counts, n_items_max):
    per_item = MOE_ROWS // MOE_GRAN
    counts = counts.astype(jnp.int32)
    gran = (counts + MOE_GRAN - 1) // MOE_GRAN
    gran_start = jnp.cumsum(gran) - gran
    items = (gran + per_item - 1) // per_item
    item_end = jnp.cumsum(items)
    w = jnp.arange(n_items_max, dtype=jnp.int32)
    ie = _owner(item_end, w)
    part = w - (_lookup(item_end, ie) - _lookup(items, ie))
    live = w < item_end[-1]
    item_ng = jnp.where(live, jnp.clip(_lookup(gran, ie) - part * per_item, 0, per_item), 0).astype(jnp.int32)
    item_row = jnp.where(live, (_lookup(gran_start, ie) + part * per_item) * MOE_GRAN, 0).astype(jnp.int32)
    return gran_start * MOE_GRAN, ie.astype(jnp.int32), item_row, item_ng, item_end[-1:].astype(jnp.int32)


def kernel(x, c, positions, w_mod, b_mod, w_in, q_gain, k_gain, cmp_pos_k, cmp_pos_v, cmp_w1_k, cmp_w2_k, cmp_w1_v, cmp_w2_v, gm_ln_g, gm_ln_b, gm_ws, gm_bs, w_proj_nsa, w_proj_gm, w_out, w_router, b_router, w_gate_e, w_up_e, w_down_e, w_gate_sh, w_up_sh, w_down_sh):
    B, S, D = x.shape
    L = w_mod.shape[0]
    T = B * S
    tm = 256
    tm_dense = 512
    scale = HEAD_DIM ** -0.5
    cos_t, sin_lo, sin_hi = _rope_tables(positions)
    mod = _mod_call(c, w_mod, b_mod)
    n_alloc = T * TOP_K + N_EXPERTS * MOE_GRAN + MOE_ROWS
    n_items_max = N_EXPERTS + T * TOP_K // MOE_ROWS + 1
    rows_buf = jnp.zeros((n_alloc, D // 2), jnp.uint32)

    for l in range(L):
        sh_a, sc_a, gt_a, sh_f, sc_f, gt_f = [mod[l, :, i * D:(i + 1) * D].reshape(B, 1, D) for i in range(6)]
        qg = (jnp.tile(q_gain[l], NSA_Q_HEADS) * scale).reshape(1, Q_WIDTH)
        kg = jnp.tile(k_gain[l], 2 * NSA_KV_HEADS).reshape(1, 2 * KV_WIDTH)
        (qn, qr, ks, kw, vs, vw, kc_raw, vc_raw, gates, u, v, ga, gb) = _inproj_call(
            x, sc_a, sh_a, _reorder_w_in(w_in[l]).astype(BF16), qg, kg, cos_t, sin_lo, sin_hi,
            gm_ln_g[l].reshape(1, GM_WIDTH), gm_ln_b[l].reshape(1, GM_WIDTH), tm_dense)
        kc, vc = _compress_call(
            kc_raw, vc_raw, cmp_w1_k[l].astype(BF16), cmp_w2_k[l].astype(BF16), cmp_pos_k[l].reshape(1, -1),
            cmp_w1_v[l].astype(BF16), cmp_w2_v[l].astype(BF16), cmp_pos_v[l].reshape(1, -1),
            k_gain[l].reshape(1, HEAD_DIM))
        o_cmp, q_aug = _nsa_cmp_call(qn, qr, kc, vc, WINDOW // 2)
        o_nsa = _nsa_flash_call(q_aug, ks, vs, kw, vw, o_cmp, gates)
        bs_full = jnp.repeat(gm_bs[l].T, GM_GROUP_DIM, axis=1)
        o_gm = _gmlp_call(u, v, gm_ws[l], bs_full, 512)
        x1, h2, logits, shared = _mixout_call(
            o_nsa, o_gm, ga, gb, x, gt_a, sc_f, sh_f,
            w_proj_nsa[l].astype(BF16), w_proj_gm[l].astype(BF16), w_out[l].astype(BF16), w_router[l].T.astype(BF16),
            w_gate_sh[l].astype(BF16), w_up_sh[l].astype(BF16), w_down_sh[l].astype(BF16), tm_dense)
        idx, wts, rank, counts = _route_call(logits, b_router[l], tm)
        row_start, item_e, item_row, item_ng, n_items = _expert_plan(counts[:, 0], n_items_max)
        slots_by_k = _slot_call(idx, rank, row_start, tm).reshape(TOP_K * T)
        xs_ref = jax.new_ref(rows_buf)
        _scatter_rows(h2.reshape(T, D // 2), slots_by_k, xs_ref)
        y = _expert_call(l, item_e, item_row, item_ng, n_items, jax.freeze(xs_ref), w_gate_e, w_up_e, w_down_e)
        yk = _gather_rows(y, slots_by_k).reshape(TOP_K, T, D // 2)
        rows_buf = y
        x = _combine_call(yk, wts, shared.reshape(T, D), x1.reshape(T, D), gt_f, tm).reshape(B, S, D)
    return x
```

```python
import functools

import jax
import jax.numpy as jnp
import numpy as np
from jax import lax
from jax.experimental import pallas as pl
from jax.experimental.pallas import tpu as pltpu
from jax.experimental.pallas import tpu_sc as plsc

D_MODEL = 1024
NSA_Q_HEADS = 8
NSA_KV_HEADS = 2
HEAD_DIM = 64
NSA_GROUP = NSA_Q_HEADS // NSA_KV_HEADS
CMP_LEN = 32
CMP_STRIDE = 16
CMP_HIDDEN = 256
SEL_LEN = 64
SEL_TOPN = 16
WINDOW = 512
ROPE_THETA = 10000.0
Q_WIDTH = NSA_Q_HEADS * HEAD_DIM
KV_WIDTH = NSA_KV_HEADS * HEAD_DIM
GM_GROUPS = 8
GM_GROUP_DIM = 64
GM_WIDTH = GM_GROUPS * GM_GROUP_DIM
GM_CHUNK = 128
N_EXPERTS = 256
TOP_K = 8
D_EXPERT = 256
D_SHARED = 256
ROUTE_SCALE = 2.5
EPS = 1e-6

LANES = 128
SEL_BIAS_WIDTH = 64
MASK_NEG = -1e30
SEL_NEG = -30000.0
SC_WINDOW = 128
MOE_GRAN = 128
MOE_ROWS = 9 * MOE_GRAN
VMEM_LIMIT = 56 * 1024 * 1024

F32 = jnp.float32
BF16 = jnp.bfloat16
HI = lax.Precision.HIGHEST


def _cparams(*sem):
    return pltpu.CompilerParams(dimension_semantics=sem, vmem_limit_bytes=VMEM_LIMIT)


def _dot(a, b, **kw):
    return jnp.dot(a, b, preferred_element_type=F32, **kw)


def _dot_nt(a, b, **kw):
    return lax.dot_general(a, b, (((1,), (1,)), ((), ())), preferred_element_type=F32, **kw)


def _gelu(x):
    return 0.5 * x * (1.0 + jnp.tanh(0.7978845608028654 * (x + 0.044715 * (x * x * x))))


def _sigmoid(x):
    return 1.0 / (1.0 + jnp.exp(-x))


def _silu(x):
    return x * _sigmoid(x)


_HI_MASK = np.uint32(0xFFFF0000)


def _pack_halves(a):
    w = a.shape[1] // 2
    bits = lax.bitcast_convert_type(a.astype(BF16).astype(F32), jnp.uint32)
    return (bits[:, w:] & _HI_MASK) | (bits[:, :w] >> 16)


def _unpack_halves(words):
    lo = lax.bitcast_convert_type(words << 16, F32)
    hi = lax.bitcast_convert_type(words & _HI_MASK, F32)
    return lo, hi


def _mod_kernel(c_ref, w_ref, b_ref, o_ref):
    c = c_ref[...]
    o_ref[0] = _dot(_silu(c), w_ref[0], precision=HI) + b_ref[0]


def _mod_call(c, w_mod, b_mod):
    L, D, N = w_mod.shape
    B = c.shape[0]
    tn = 1536
    return pl.pallas_call(
        _mod_kernel,
        grid=(L, N // tn),
        in_specs=[pl.BlockSpec((B, D), lambda l, j: (0, 0)),
                  pl.BlockSpec((1, D, tn), lambda l, j: (l, 0, j)),
                  pl.BlockSpec((1, 1, tn), lambda l, j: (l, 0, j))],
        out_specs=pl.BlockSpec((1, B, tn), lambda l, j: (l, 0, j)),
        out_shape=jax.ShapeDtypeStruct((L, B, N), F32),
        compiler_params=_cparams("arbitrary", "arbitrary"),
        name="mod",
    )(c, w_mod, b_mod.reshape(L, 1, N))


_C_Q = 0
_C_K = 512
_C_KC = 768
_C_VC = 896
_C_VS = 1024
_C_VW = 1152
_C_G = 1280
_C_U = 1536
_C_V = 2048
_C_GA = 2560
_C_GB = 3584
IN_COLS_P = 4608


def _dot_split(a, b):
    hi = a.astype(BF16)
    lo = (a - hi.astype(F32)).astype(BF16)
    return _dot(hi, b) + _dot(lo, b)


def _head_norm(z, bd):
    ms = _dot_split(z * z, bd)
    return z * lax.rsqrt(ms + EPS)


def _rope(z, cos, sin_lo, sin_hi):
    w = z.shape[-1]
    half = HEAD_DIM // 2
    return z * cos + pltpu.roll(z, w - half, 1) * sin_lo + pltpu.roll(z, half, 1) * sin_hi


def _tile_lanes(t, n):
    return t if n == 1 else jnp.concatenate([t] * n, axis=-1)


def _inproj_kernel(x_ref, sc_ref, sh_ref, w_ref, bdq_ref, bdk_ref, qg_ref, kg_ref,
                   cos_ref, sl_ref, shi_ref, lng_ref, lnb_ref,
                   qn_ref, qr_ref, ks_ref, kw_ref, vs_ref, vw_ref, kc_ref, vc_ref,
                   g_ref, u_ref, v_ref, ga_ref, gb_ref):
    tm = x_ref.shape[1]
    x = x_ref[0]
    ms = jnp.mean(x * x, axis=-1, keepdims=True)
    h = (x * lax.rsqrt(ms + EPS)) * (1.0 + sc_ref[0]) + sh_ref[0]
    hb = h.astype(BF16)

    def mm(lo, width):
        return _dot(hb, w_ref[:, lo:lo + width])

    cos, sl, shi = cos_ref[0], sl_ref[0], shi_ref[0]

    zq = mm(_C_Q, Q_WIDTH)
    qn = _head_norm(zq, bdq_ref[...]) * qg_ref[...]
    qr = _rope(qn, _tile_lanes(cos, 4), _tile_lanes(sl, 4), _tile_lanes(shi, 4))
    qn_ref[0] = qn.astype(BF16)
    qr_ref[0] = qr.astype(BF16)

    zk = mm(_C_K, 2 * KV_WIDTH)
    kn = _head_norm(zk, bdk_ref[...]) * kg_ref[...]
    kr = _rope(kn, _tile_lanes(cos, 2), _tile_lanes(sl, 2), _tile_lanes(shi, 2))
    lane = lax.broadcasted_iota(jnp.int32, (tm, LANES), 1)
    tok = pl.program_id(1) * tm + lax.broadcasted_iota(jnp.int32, (tm, LANES), 0)
    onehot = jnp.where(lane - HEAD_DIM == tok // SEL_LEN, 1.0, 0.0)
    ones_col = jnp.where(lane == HEAD_DIM, 1.0, 0.0)
    low = lane < HEAD_DIM
    zvs = mm(_C_VS, KV_WIDTH)
    zvw = mm(_C_VW, KV_WIDTH)
    zkc = mm(_C_KC, KV_WIDTH)
    zvc = mm(_C_VC, KV_WIDTH)
    for kv in range(NSA_KV_HEADS):
        def head(a):
            return a if kv == 0 else pltpu.roll(a, HEAD_DIM, 1)
        ks_ref[0, kv] = jnp.where(low, head(kr[:, :KV_WIDTH]), onehot).T.astype(BF16)
        kw_ref[0, kv] = jnp.where(low, head(kr[:, KV_WIDTH:]), 0.0).T.astype(BF16)
        vs_ref[0, kv] = jnp.where(low, head(zvs), ones_col).astype(BF16)
        vw_ref[0, kv] = jnp.where(low, head(zvw), ones_col).astype(BF16)
        kc_ref[0, kv] = head(zkc)[:, :HEAD_DIM]
        vc_ref[0, kv] = head(zvc)[:, :HEAD_DIM]

    zg = mm(_C_G, 2 * LANES)
    sg = _sigmoid(zg)
    g_ref[0, 0] = sg[:, :LANES]
    g_ref[0, 1] = sg[:, LANES:]

    u_ref[0] = _gelu(mm(_C_U, GM_WIDTH)).astype(BF16)
    gv = _gelu(mm(_C_V, GM_WIDTH))
    mu = jnp.mean(gv, axis=-1, keepdims=True)
    cen = gv - mu
    var = jnp.mean(cen * cen, axis=-1, keepdims=True)
    v_ref[0] = ((cen * lax.rsqrt(var + EPS)) * lng_ref[...] + lnb_ref[...]).astype(BF16)

    ga_ref[0] = _sigmoid(mm(_C_GA, D_MODEL)).astype(BF16)
    gb_ref[0] = _sigmoid(mm(_C_GB, D_MODEL)).astype(BF16)


def _block_diag_mean(width):
    idx = np.arange(width) // HEAD_DIM
    return jnp.asarray((idx[:, None] == idx[None, :]).astype(np.float32) / HEAD_DIM).astype(BF16)


def _inproj_call(x, sc, sh, w_p, qg, kg, cos, sl, shi, lng, lnb, tm):
    B, S, D = x.shape
    H = NSA_KV_HEADS
    full = lambda *shape: pl.BlockSpec(shape, lambda b, i: (0,) * len(shape))
    tok3 = lambda w: pl.BlockSpec((1, tm, w), lambda b, i: (b, i, 0))
    per_b = pl.BlockSpec((1, 1, D), lambda b, i: (b, 0, 0))
    kv4 = lambda w: pl.BlockSpec((1, H, tm, w), lambda b, i: (b, 0, i, 0))
    kt4 = pl.BlockSpec((1, H, LANES, tm), lambda b, i: (b, 0, 0, i))
    sds = jax.ShapeDtypeStruct
    out_shape = [
        sds((B, S, Q_WIDTH), BF16), sds((B, S, Q_WIDTH), BF16),
        sds((B, H, LANES, S), BF16), sds((B, H, LANES, S), BF16),
        sds((B, H, S, LANES), BF16), sds((B, H, S, LANES), BF16),
        sds((B, H, S, HEAD_DIM), F32), sds((B, H, S, HEAD_DIM), F32),
        sds((B, H, S, LANES), F32),
        sds((B, S, GM_WIDTH), BF16), sds((B, S, GM_WIDTH), BF16),
        sds((B, S, D), BF16), sds((B, S, D), BF16),
    ]
    out_specs = [
        tok3(Q_WIDTH), tok3(Q_WIDTH), kt4, kt4, kv4(LANES), kv4(LANES),
        kv4(HEAD_DIM), kv4(HEAD_DIM), kv4(LANES),
        tok3(GM_WIDTH), tok3(GM_WIDTH), tok3(D), tok3(D),
    ]
    return pl.pallas_call(
        _inproj_kernel,
        grid=(B, S // tm),
        in_specs=[tok3(D), per_b, per_b, full(D, IN_COLS_P),
                  full(Q_WIDTH, Q_WIDTH), full(2 * KV_WIDTH, 2 * KV_WIDTH),
                  full(1, Q_WIDTH), full(1, 2 * KV_WIDTH),
                  tok3(LANES), tok3(LANES), tok3(LANES),
                  full(1, GM_WIDTH), full(1, GM_WIDTH)],
        out_specs=out_specs,
        out_shape=out_shape,
        compiler_params=_cparams("arbitrary", "arbitrary"),
        name="in_proj",
    )(x, sc, sh, w_p, _block_diag_mean(Q_WIDTH), _block_diag_mean(2 * KV_WIDTH), qg, kg,
      cos, sl, shi, lng, lnb)


def _compress_kernel(kr_ref, vr_ref, w1k_ref, w2k_ref, pek_ref, w1v_ref, w2v_ref, pev_ref,
                     kg_ref, kc_ref, vc_ref):
    nc = kr_ref.shape[2]
    half = CMP_STRIDE * HEAD_DIM

    def mlp(raw, w1_ref, w2_ref, pe_ref):
        a = raw.astype(BF16)
        top = _dot(a, w1_ref[:half, :])
        bot = _dot(a, w1_ref[half:, :])
        pe = jnp.broadcast_to(pe_ref[...], (8, 2 * half)).astype(BF16)
        pe_row = _dot(pe, w1_ref[...])[0:1, :]
        hid = top + pltpu.roll(bot, nc - 1, 0) + pe_row
        return _dot(_gelu(hid).astype(BF16), w2_ref[...])

    kc = mlp(kr_ref[0, 0], w1k_ref, w2k_ref, pek_ref)
    ms = jnp.mean(kc * kc, axis=-1, keepdims=True)
    kc_ref[0, 0] = (kc * lax.rsqrt(ms + EPS) * kg_ref[...]).astype(BF16)
    vc_ref[0, 0] = mlp(vr_ref[0, 0], w1v_ref, w2v_ref, pev_ref).astype(BF16)


def _compress_call(kc_raw, vc_raw, w1k, w2k, pek, w1v, w2v, pev, kg):
    B, H, S, hd = kc_raw.shape
    nc = S // CMP_STRIDE
    feat = CMP_STRIDE * hd
    raw = pl.BlockSpec((1, 1, nc, feat), lambda b, h: (b, h, 0, 0))
    full = lambda *shape: pl.BlockSpec(shape, lambda b, h: (0,) * len(shape))
    out = pl.BlockSpec((1, 1, nc, hd), lambda b, h: (b, h, 0, 0))
    return pl.pallas_call(
        _compress_kernel,
        grid=(B, H),
        in_specs=[raw, raw, full(2 * feat, CMP_HIDDEN), full(CMP_HIDDEN, hd), full(1, 2 * feat),
                  full(2 * feat, CMP_HIDDEN), full(CMP_HIDDEN, hd), full(1, 2 * feat),
                  full(1, hd)],
        out_specs=[out, out],
        out_shape=[jax.ShapeDtypeStruct((B, H, nc, hd), BF16)] * 2,
        compiler_params=_cparams("arbitrary", "arbitrary"),
        name="compress",
    )(kc_raw.reshape(B, H, nc, feat), vc_raw.reshape(B, H, nc, feat),
      w1k, w2k, pek, w1v, w2v, pev, kg)


def _group_rows(a):
    return jnp.concatenate([a[:, g * HEAD_DIM:(g + 1) * HEAD_DIM] for g in range(NSA_GROUP)], axis=0)


def _nsa_cmp_kernel(qn_ref, qr_ref, kc_ref, vc_ref, ovl_ref, oc_ref, qa_ref, *, n_sel):
    tq = qn_ref.shape[1]
    nc = kc_ref.shape[2]
    G = NSA_GROUP
    q0 = pl.program_id(2) * tq
    q4 = _group_rows(qn_ref[0])
    s = _dot_nt(q4, kc_ref[0, 0])
    row = lax.broadcasted_iota(jnp.int32, (G, tq, nc), 1).reshape(G * tq, nc)
    col = lax.broadcasted_iota(jnp.int32, (G * tq, nc), 1)
    vis = col * CMP_STRIDE + (CMP_LEN - 1) <= q0 + row
    s = jnp.where(vis, s, MASK_NEG)
    m = jnp.max(s, axis=-1, keepdims=True)
    e = jnp.where(vis, jnp.exp(s - m), 0.0)
    p = e / jnp.maximum(jnp.sum(e, axis=-1, keepdims=True), 1e-30)
    oc = _dot(p.astype(BF16), vc_ref[0, 0])
    oc_ref[0, 0] = oc.reshape(G, tq, HEAD_DIM)

    psum = p[0:tq] + p[tq:2 * tq] + p[2 * tq:3 * tq] + p[3 * tq:4 * tq]
    p_hi = psum.astype(BF16)
    p_lo = (psum - p_hi.astype(F32)).astype(BF16)
    imp = (_dot_nt(ovl_ref[...], p_hi) + _dot_nt(ovl_ref[...], p_lo))[:SEL_BIAS_WIDTH]
    blk = lax.broadcasted_iota(jnp.int32, (SEL_BIAS_WIDTH, tq), 0)
    cur = (q0 + lax.broadcasted_iota(jnp.int32, (SEL_BIAS_WIDTH, tq), 1)) // SEL_LEN
    valid = blk <= cur
    forced = (blk == 0) | (blk == cur) | (blk == cur - 1)
    cand = valid & jnp.logical_not(forced)
    n_forced = jnp.minimum(cur, 2) + 1
    val = jnp.where(cand, imp, -1.0)
    cnt = jnp.zeros((SEL_BIAS_WIDTH, tq), F32)
    for j in range(n_sel):
        vj = jnp.broadcast_to(val[j:j + 1, :], (SEL_BIAS_WIDTH, tq))
        cnt = cnt + jnp.where(blk > j, jnp.where(vj >= val, 1.0, 0.0), jnp.where(vj > val, 1.0, 0.0))
    free = (min(SEL_TOPN, n_sel) - n_forced).astype(F32)
    sel = (forced & valid) | (cand & (cnt < free))
    sel_t = jnp.concatenate([jnp.where(sel, 1.0, 0.0), jnp.zeros((LANES - SEL_BIAS_WIDTH, tq), F32)], axis=0)
    sel_q = sel_t.T
    bias = jnp.where(sel_q > 0.5, 0.0, SEL_NEG)
    lane = lax.broadcasted_iota(jnp.int32, (tq, LANES), 1)
    bias_hi = pltpu.roll(bias, SEL_BIAS_WIDTH, 1)
    qr = qr_ref[0]
    for g in range(G):
        qg = qr[:, g * HEAD_DIM:(g + 1) * HEAD_DIM].astype(F32)
        qg = jnp.concatenate([qg, qg], axis=-1)
        qa_ref[0, 0, g] = jnp.where(lane < HEAD_DIM, qg, bias_hi).astype(BF16)


def _overlap_t(S):
    n_cmp = (S - CMP_LEN) // CMP_STRIDE + 1
    nc = S // CMP_STRIDE
    n_sel = S // SEL_LEN
    start = np.arange(nc) * CMP_STRIDE
    end = start + CMP_LEN - 1
    sel_start = np.arange(n_sel) * SEL_LEN
    ov = (start[None, :] <= sel_start[:, None] + SEL_LEN - 1) & (end[None, :] >= sel_start[:, None])
    ov = ov & (np.arange(nc) < n_cmp)[None, :]
    out = np.zeros((LANES, nc), np.float32)
    out[:n_sel] = ov.astype(np.float32)
    return jnp.asarray(out).astype(BF16)


def _nsa_cmp_call(qn, qr, kc, vc, tq):
    B, S, _ = qn.shape
    H, G = NSA_KV_HEADS, NSA_GROUP
    nc = kc.shape[2]
    n_sel = S // SEL_LEN
    assert n_sel <= SEL_BIAS_WIDTH and nc % LANES == 0
    qspec = pl.BlockSpec((1, tq, G * HEAD_DIM), lambda b, h, i: (b, i, h))
    cspec = pl.BlockSpec((1, 1, nc, HEAD_DIM), lambda b, h, i: (b, h, 0, 0))
    return pl.pallas_call(
        functools.partial(_nsa_cmp_kernel, n_sel=n_sel),
        grid=(B, H, S // tq),
        in_specs=[qspec, qspec, cspec, cspec, pl.BlockSpec((LANES, nc), lambda b, h, i: (0, 0))],
        out_specs=[pl.BlockSpec((1, 1, G, tq, HEAD_DIM), lambda b, h, i: (b, h, 0, i, 0)),
                   pl.BlockSpec((1, 1, G, tq, LANES), lambda b, h, i: (b, h, 0, i, 0))],
        out_shape=[jax.ShapeDtypeStruct((B, H, G, S, HEAD_DIM), F32),
                   jax.ShapeDtypeStruct((B, H, G, S, LANES), BF16)],
        compiler_params=_cparams("arbitrary", "arbitrary", "arbitrary"),
        name="nsa_cmp",
    )(qn, qr, kc, vc, _overlap_t(S))


SEL_CHUNK = 4


def _nsa_flash_kernel(qa_ref, ks_ref, vs_ref, kw_ref, vw_ref, oc_ref, g_ref, o_ref, m_scr, acc_scr):
    G = NSA_GROUP
    tq = qa_ref.shape[3]
    R = G * tq
    tk = tq
    i = pl.program_id(2)
    q0 = i * tq

    def rows_of(ref, j0, nt):
        return ref[0, 0, pl.ds(pl.multiple_of(j0 * tk, tk), nt * tk), :]

    def attend(k_ref, v_ref, j0, nt, visible, state):
        v = rows_of(v_ref, j0, nt)
        kt = k_ref[0, 0, :, pl.ds(pl.multiple_of(j0 * tk, tk), nt * tk)]
        s_all = _dot(qa_ref[0, 0].reshape(R, LANES), kt)
        ss = [s_all[g * tq:(g + 1) * tq] for g in range(G)]
        if visible is not None:
            qpos = q0 + lax.broadcasted_iota(jnp.int32, (tq, nt * tk), 0)
            kpos = j0 * tk + lax.broadcasted_iota(jnp.int32, (tq, nt * tk), 1)
            mask = visible(qpos, kpos)
        ps, alphas = [], []
        for g in range(G):
            s = ss[g] if visible is None else jnp.where(mask, ss[g], MASK_NEG)
            cols = [s[:, c * LANES:(c + 1) * LANES] for c in range(nt * tk // LANES)]
            m_new = jnp.max(functools.reduce(jnp.maximum, cols), axis=-1, keepdims=True)
            if state:
                m_prev = m_scr[g]
                m_new = jnp.maximum(m_prev, m_new)
                alphas.append(jnp.exp(m_prev - m_new))
                m_scr[g] = m_new
            ps.append(jnp.concatenate([jnp.exp(c - m_new) for c in cols], axis=-1).astype(BF16))
        if not state:
            return [_dot(ps[g], v) for g in range(G)]
        for g in range(G):
            acc_scr[g] = alphas[g] * acc_scr[g] + _dot(ps[g], v)

    def normalise(acc):
        return acc[:, :HEAD_DIM] / acc[:, HEAD_DIM:HEAD_DIM + 1]

    m_scr[...] = jnp.full((G, tq, LANES), MASK_NEG, F32)
    acc_scr[...] = jnp.zeros((G, tq, LANES), F32)
    causal = lambda qpos, kpos: kpos <= qpos

    def sel_body(c, carry):
        attend(ks_ref, vs_ref, c * SEL_CHUNK, SEL_CHUNK, None, True)
        return carry

    n_full = i // SEL_CHUNK
    lax.fori_loop(0, n_full, sel_body, 0)
    attend(ks_ref, vs_ref, n_full * SEL_CHUNK, 2, causal, True)

    @pl.when(i % SEL_CHUNK >= 2)
    def _():
        attend(ks_ref, vs_ref, n_full * SEL_CHUNK + 2, 2, causal, True)

    o_sel = normalise(acc_scr[...].reshape(R, LANES))

    assert WINDOW == 2 * tk
    band = lambda qpos, kpos: (kpos <= qpos) & (kpos > qpos - WINDOW)
    o_win = normalise(jnp.concatenate(attend(kw_ref, vw_ref, jnp.maximum(i - 2, 0), 3, band, False), axis=0))

    o_cmp = oc_ref[0, 0].reshape(R, HEAD_DIM)
    gates = g_ref[0, 0]
    outs = []
    for g in range(G):
        r = slice(g * tq, (g + 1) * tq)
        outs.append(gates[:, 3 * g:3 * g + 1] * o_cmp[r]
                    + gates[:, 3 * g + 1:3 * g + 2] * o_sel[r]
                    + gates[:, 3 * g + 2:3 * g + 3] * o_win[r])
    o_ref[0] = jnp.concatenate(outs, axis=-1).astype(BF16)


def _nsa_flash_call(qa, ks, vs, kw, vw, oc, gates):
    B, H, G, S, _ = qa.shape
    tq = WINDOW // 2
    assert (S // tq) % SEL_CHUNK == 0
    kt = pl.BlockSpec((1, 1, LANES, S), lambda b, h, i: (b, h, 0, 0))
    vv = pl.BlockSpec((1, 1, S, LANES), lambda b, h, i: (b, h, 0, 0))
    return pl.pallas_call(
        _nsa_flash_kernel,
        grid=(B, H, S // tq),
        in_specs=[pl.BlockSpec((1, 1, G, tq, LANES), lambda b, h, i: (b, h, 0, i, 0)),
                  kt, vv, kt, vv,
                  pl.BlockSpec((1, 1, G, tq, HEAD_DIM), lambda b, h, i: (b, h, 0, i, 0)),
                  pl.BlockSpec((1, 1, tq, LANES), lambda b, h, i: (b, h, i, 0))],
        out_specs=pl.BlockSpec((1, tq, G * HEAD_DIM), lambda b, h, i: (b, i, h)),
        out_shape=jax.ShapeDtypeStruct((B, S, Q_WIDTH), BF16),
        scratch_shapes=[pltpu.VMEM((G, tq, LANES), F32), pltpu.VMEM((G, tq, LANES), F32)],
        compiler_params=_cparams("arbitrary", "arbitrary", "arbitrary"),
        name="nsa_flash",
    )(qa, ks, vs, kw, vw, oc, gates)


def _gmlp_kernel(u_ref, v_ref, ws_ref, bs_ref, o_ref):
    tm = u_ref.shape[1]
    C = GM_CHUNK
    r = lax.broadcasted_iota(jnp.int32, (C, C), 0)
    c = lax.broadcasted_iota(jnp.int32, (C, C), 1)
    causal = c <= r
    ws = [jnp.where(causal, ws_ref[g], 0.0).astype(BF16) for g in range(GM_GROUPS)]
    for n in range(tm // C):
        rows = slice(n * C, (n + 1) * C)
        vn = v_ref[0, rows, :]
        mixed = jnp.concatenate(
            [_dot(ws[g], vn[:, g * GM_GROUP_DIM:(g + 1) * GM_GROUP_DIM]) for g in range(GM_GROUPS)],
            axis=-1)
        o_ref[0, rows, :] = (u_ref[0, rows, :].astype(F32) * (mixed + bs_ref[...])).astype(BF16)


def _gmlp_call(u, v, ws, bs_full, tm):
    B, S, W = u.shape
    tok = pl.BlockSpec((1, tm, W), lambda b, i: (b, i, 0))
    return pl.pallas_call(
        _gmlp_kernel,
        grid=(B, S // tm),
        in_specs=[tok, tok,
                  pl.BlockSpec((GM_GROUPS, GM_CHUNK, GM_CHUNK), lambda b, i: (0, 0, 0)),
                  pl.BlockSpec((GM_CHUNK, W), lambda b, i: (0, 0))],
        out_specs=tok,
        out_shape=jax.ShapeDtypeStruct((B, S, W), BF16),
        compiler_params=_cparams("arbitrary", "arbitrary"),
        name="gmlp",
    )(u, v, ws, bs_full)


def _mixout_kernel(on_ref, og_ref, ga_ref, gb_ref, x_ref, gt_ref, sc_ref, sh_ref,
                   wpn_ref, wpg_ref, wo_ref, wr_ref, wgs_ref, wus_ref, wds_ref,
                   x1_ref, h2_ref, lg_ref, shd_ref):
    ya = _dot(on_ref[0], wpn_ref[...])
    yb = _dot(og_ref[0], wpg_ref[...])
    merged = ga_ref[0].astype(F32) * ya + gb_ref[0].astype(F32) * yb
    x1 = x_ref[0] + gt_ref[0] * _dot(merged.astype(BF16), wo_ref[...])
    x1_ref[0] = x1
    ms = jnp.mean(x1 * x1, axis=-1, keepdims=True)
    h2 = (x1 * lax.rsqrt(ms + EPS)) * (1.0 + sc_ref[0]) + sh_ref[0]
    hb = h2.astype(BF16)
    lg_ref[0] = _dot_nt(wr_ref[...], hb)
    h2_ref[0] = _pack_halves(hb)
    act = _silu(_dot(hb, wgs_ref[...])) * _dot(hb, wus_ref[...])
    shd_ref[0] = _dot(act.astype(BF16), wds_ref[...])


def _mixout_call(o_nsa, o_gm, ga, gb, x, gt, sc, sh, wpn, wpg, wo, wr, wgs, wus, wds, tm):
    B, S, D = x.shape
    tok = lambda w: pl.BlockSpec((1, tm, w), lambda b, i: (b, i, 0))
    per_b = pl.BlockSpec((1, 1, D), lambda b, i: (b, 0, 0))
    full = lambda a: pl.BlockSpec(a.shape, lambda b, i: (0,) * a.ndim)
    sds = jax.ShapeDtypeStruct
    return pl.pallas_call(
        _mixout_kernel,
        grid=(B, S // tm),
        in_specs=[tok(Q_WIDTH), tok(GM_WIDTH), tok(D), tok(D), tok(D), per_b, per_b, per_b,
                  full(wpn), full(wpg), full(wo), full(wr), full(wgs), full(wus), full(wds)],
        out_specs=[tok(D), tok(D // 2), pl.BlockSpec((1, N_EXPERTS, tm), lambda b, i: (b, 0, i)), tok(D)],
        out_shape=[sds((B, S, D), F32), sds((B, S, D // 2), jnp.uint32), sds((B, N_EXPERTS, S), F32),
                   sds((B, S, D), F32)],
        compiler_params=_cparams("arbitrary", "arbitrary"),
        name="mix_out",
    )(o_nsa, o_gm, ga, gb, x, gt, sc, sh, wpn, wpg, wo, wr, wgs, wus, wds)


def _route_kernel(lg_ref, br_ref, idx_ref, wt_ref, rank_ref, cnt_ref, run_scr):
    E, tm = lg_ref.shape[1], lg_ref.shape[2]

    @pl.when(pl.program_id(0) == 0)
    def _():
        run_scr[...] = jnp.zeros_like(run_scr)

    aff = _sigmoid(lg_ref[0])
    work = aff + br_ref[...]
    row = lax.broadcasted_iota(jnp.int32, (E, tm), 0).astype(F32)
    picked = jnp.zeros((E, tm), F32)
    idxs, tops = [], []
    for _ in range(TOP_K):
        m = jnp.max(work, axis=0, keepdims=True)
        idx = jnp.min(jnp.where(work == m, row, float(E)), axis=0, keepdims=True)
        hit = row == idx
        tops.append(jnp.sum(jnp.where(hit, aff, 0.0), axis=0, keepdims=True))
        idxs.append(idx)
        picked = jnp.where(hit, 1.0, picked)
        work = jnp.where(hit, -jnp.inf, work)
    total = functools.reduce(jnp.add, tops)
    r = lax.broadcasted_iota(jnp.int32, (tm, tm), 0)
    c = lax.broadcasted_iota(jnp.int32, (tm, tm), 1)
    before = _dot(picked.astype(BF16), jnp.where(r < c, 1.0, 0.0).astype(BF16)) + run_scr[...]
    ranks = [jnp.sum(jnp.where(row == idx, before, 0.0), axis=0, keepdims=True) for idx in idxs]
    run_scr[...] = run_scr[...] + jnp.sum(picked, axis=1, keepdims=True)
    cnt_ref[...] = run_scr[...]
    idx_ref[...] = jnp.concatenate(idxs, axis=0).astype(jnp.int32)
    rank_ref[...] = jnp.concatenate(ranks, axis=0).astype(jnp.int32)
    wt = jnp.concatenate([t / total * ROUTE_SCALE for t in tops]
                         + [jnp.zeros((LANES - TOP_K, tm), F32)], axis=0)
    eye = jnp.where(r == c, 1.0, 0.0).astype(BF16)
    cols = jnp.zeros((tm, LANES), F32)
    rest = wt
    for _ in range(3):
        part = rest.astype(BF16)
        cols = cols + _dot_nt(eye, part)
        rest = rest - part.astype(F32)
    wt_ref[...] = cols


def _route_call(logits_t, b_router, tm):
    B, E, S = logits_t.shape
    per_b = S // tm
    T = B * S
    kt = pl.BlockSpec((TOP_K, tm), lambda i: (0, i))
    col = pl.BlockSpec((E, 1), lambda i: (0, 0))
    sds = jax.ShapeDtypeStruct
    return pl.pallas_call(
        _route_kernel,
        grid=(T // tm,),
        in_specs=[pl.BlockSpec((1, E, tm), lambda i: (i // per_b, 0, i % per_b)), col],
        out_specs=[kt, pl.BlockSpec((tm, LANES), lambda i: (i, 0)), kt, col],
        out_shape=[sds((TOP_K, T), jnp.int32), sds((T, LANES), F32), sds((TOP_K, T), jnp.int32),
                   sds((E, 1), F32)],
        scratch_shapes=[pltpu.VMEM((E, 1), F32)],
        compiler_params=_cparams("arbitrary"),
        name="route",
    )(logits_t, b_router.reshape(E, 1))


def _slot_kernel(idx_ref, rank_ref, ps_ref, o_ref):
    tm = idx_ref.shape[1]
    E = ps_ref.shape[0]
    row = lax.broadcasted_iota(jnp.int32, (E, tm), 0)
    idx, rank, ps = idx_ref[...], rank_ref[...], ps_ref[...]
    base = [jnp.sum(jnp.where(row == idx[k:k + 1, :], ps, 0.0), axis=0, keepdims=True) for k in range(TOP_K)]
    o_ref[...] = jnp.concatenate(base, axis=0).astype(jnp.int32) + rank


def _slot_call(idx, rank, pad_start, tm):
    T = idx.shape[1]
    E = pad_start.shape[0]
    kt = pl.BlockSpec((TOP_K, tm), lambda i: (0, i))
    return pl.pallas_call(
        _slot_kernel,
        grid=(T // tm,),
        in_specs=[kt, kt, pl.BlockSpec((E, 1), lambda i: (0, 0))],
        out_specs=kt,
        out_shape=jax.ShapeDtypeStruct((TOP_K, T), jnp.int32),
        compiler_params=_cparams("arbitrary"),
        name="slot",
    )(idx, rank, pad_start.astype(F32).reshape(E, 1))


def _sc_mesh():
    mesh = plsc.VectorSubcoreMesh(core_axis_name="core", subcore_axis_name="subcore")
    return mesh, mesh.num_cores * mesh.num_subcores


def _sc_worker(mesh):
    return lax.axis_index("subcore") * mesh.num_cores + lax.axis_index("core")


def _scatter_rows(rows, indices_by_k, out_ref):
    n, width = rows.shape
    mesh, workers = _sc_mesh()
    per_worker = n // workers
    assert per_worker * workers == n and per_worker % SC_WINDOW == 0

    @pl.kernel(out_type=(), mesh=mesh, name="scatter_rows",
               scratch_types=[pltpu.VMEM((SC_WINDOW,), jnp.int32), pltpu.VMEM((SC_WINDOW, width), rows.dtype)])
    def scatter(rows_hbm, idx_hbm, out_hbm, idx_v, rows_v):
        worker = _sc_worker(mesh)

        @pl.loop(0, per_worker // SC_WINDOW)
        def _(j):
            base = pl.multiple_of(worker * per_worker + j * SC_WINDOW, SC_WINDOW)
            pltpu.sync_copy(rows_hbm.at[pl.ds(base, SC_WINDOW)], rows_v)
            for k in range(TOP_K):
                pltpu.sync_copy(idx_hbm.at[pl.ds(pl.multiple_of(k * n + base, SC_WINDOW), SC_WINDOW)], idx_v)
                pltpu.sync_copy(rows_v, out_hbm.at[idx_v])

    scatter(rows, indices_by_k, out_ref)


def _expert_kernel(ie_ref, ir_ref, ig_ref, n_ref, x_hbm, wg_ref, wu_ref, wd_ref, y_hbm,
                   xbuf, ybuf, act_scr, xsem, ysem):
    w = pl.program_id(0)
    n = n_ref[0]
    R = xbuf.shape[1]
    last = ie_ref.shape[0] - 1

    def x_copy(item, slot):
        row = pl.multiple_of(ir_ref[jnp.minimum(item, last)], MOE_GRAN)
        return pltpu.make_async_copy(x_hbm.at[pl.ds(row, R), :], xbuf.at[slot], xsem.at[slot])

    def y_copy(item, slot, gi):
        row = pl.multiple_of(ir_ref[jnp.minimum(item, last)] + gi * MOE_GRAN, MOE_GRAN)
        src = ybuf.at[slot, pl.ds(pl.multiple_of(gi * MOE_GRAN, MOE_GRAN), MOE_GRAN), :]
        return pltpu.make_async_copy(src, y_hbm.at[pl.ds(row, MOE_GRAN), :], ysem.at[slot])

    def for_granules(item, fn):
        def body(gi, carry):
            fn(gi)
            return carry
        lax.fori_loop(0, ig_ref[jnp.clip(item, 0, last)], body, 0)

    def gate_up(slot):
        lo, hi = _unpack_halves(xbuf[slot])
        x = jnp.concatenate([lo, hi], axis=1).astype(BF16)
        a = _dot(x, wg_ref[0, 0].astype(BF16))
        b = _dot(x, wu_ref[0, 0].astype(BF16))
        return (_silu(a) * b).astype(BF16)

    def down(act):
        return _pack_halves(_dot(act, wd_ref[0, 0].astype(BF16)))

    nx = xbuf.shape[0]
    cur, prv = w % nx, (w + 1) % 2

    @pl.when(w == 0)
    def _():
        for j in range(nx - 1):
            @pl.when(j < n)
            def _():
                x_copy(j, j).start()

    @pl.when(w < n)
    def _():
        x_copy(w, cur).wait()

    @pl.when(w + nx - 1 < n)
    def _():
        x_copy(w + nx - 1, (w + nx - 1) % nx).start()

    @pl.when((w >= 3) & (w - 3 < n))
    def _():
        for_granules(w - 3, lambda gi: y_copy(w - 3, prv, gi).wait())

    @pl.when(w == 0)
    def _():
        act_scr[...] = gate_up(0)

    @pl.when((w >= 1) & (w < n))
    def _():
        prev = act_scr[...]
        ybuf[prv] = down(prev)
        act_scr[...] = gate_up(cur)

    @pl.when((w >= 1) & (w == n))
    def _():
        ybuf[prv] = down(act_scr[...])

    @pl.when((w >= 1) & (w <= n))
    def _():
        for_granules(w - 1, lambda gi: y_copy(w - 1, prv, gi).start())


def _expert_call(layer, item_e, item_row, item_ng, n_items, xs, wg, wu, wd):
    n_alloc, W = xs.shape
    R = MOE_ROWS
    nw = item_e.shape[0]
    D, F = wg.shape[2], wg.shape[3]
    cur = lambda w, ie, ir, ig, n: (layer, ie[jnp.minimum(w, nw - 1)], 0, 0)
    prev = lambda w, ie, ir, ig, n: (layer, ie[jnp.clip(w - 1, 0, nw - 1)], 0, 0)
    grid_spec = pltpu.PrefetchScalarGridSpec(
        num_scalar_prefetch=4,
        grid=(nw + 3,),
        in_specs=[pl.BlockSpec(memory_space=pl.ANY),
                  pl.BlockSpec((1, 1, D, F), cur), pl.BlockSpec((1, 1, D, F), cur),
                  pl.BlockSpec((1, 1, F, D), prev)],
        out_specs=pl.BlockSpec(memory_space=pl.ANY),
        scratch_shapes=[pltpu.VMEM((3, R, W), jnp.uint32), pltpu.VMEM((2, R, W), jnp.uint32),
                        pltpu.VMEM((R, F), BF16),
                        pltpu.SemaphoreType.DMA((3,)), pltpu.SemaphoreType.DMA((2,))],
    )
    return pl.pallas_call(
        _expert_kernel,
        grid_spec=grid_spec,
        out_shape=jax.ShapeDtypeStruct((n_alloc, W), jnp.uint32),
        input_output_aliases={4: 0},
        compiler_params=pltpu.CompilerParams(dimension_semantics=("arbitrary",), vmem_limit_bytes=VMEM_LIMIT,
                                             has_side_effects=True),
        name="expert",
    )(item_e, item_row, item_ng, n_items, xs, wg, wu, wd)


def _gather_rows(table, indices):
    n = indices.shape[0]
    width = table.shape[1]
    mesh, workers = _sc_mesh()
    per_worker = n // workers
    assert per_worker * workers == n and per_worker % SC_WINDOW == 0

    @pl.kernel(out_type=jax.ShapeDtypeStruct((n, width), table.dtype), mesh=mesh, name="gather_rows",
               scratch_types=[pltpu.VMEM((SC_WINDOW,), jnp.int32), pltpu.VMEM((SC_WINDOW, width), table.dtype),
                              pltpu.SemaphoreType.DMA])
    def gather(table_hbm, idx_hbm, out_hbm, idx_v, rows_v, sem):
        worker = _sc_worker(mesh)

        @pl.loop(0, per_worker // SC_WINDOW)
        def _(j):
            base = pl.multiple_of(worker * per_worker + j * SC_WINDOW, SC_WINDOW)
            pltpu.sync_copy(idx_hbm.at[pl.ds(base, SC_WINDOW)], idx_v)
            pltpu.async_copy(table_hbm.at[idx_v], rows_v, sem).wait()
            pltpu.sync_copy(rows_v, out_hbm.at[pl.ds(base, SC_WINDOW)])

    return gather(table, indices)


def _combine_kernel(y_ref, wt_ref, shd_ref, x_ref, gt_ref, o_ref):
    wt = wt_ref[...]
    acc_lo = acc_hi = None
    for k in range(TOP_K):
        lo, hi = _unpack_halves(y_ref[k])
        w = wt[:, k:k + 1]
        acc_lo = w * lo if k == 0 else acc_lo + w * lo
        acc_hi = w * hi if k == 0 else acc_hi + w * hi
    routed = jnp.concatenate([acc_lo, acc_hi], axis=1)
    o_ref[...] = x_ref[...] + gt_ref[0] * (routed + shd_ref[...])


def _combine_call(yk, wts, shared, x1, gt, tm):
    T, D = x1.shape
    B = gt.shape[0]
    per_b = T // B // tm
    tok = lambda w: pl.BlockSpec((tm, w), lambda i: (i, 0))
    return pl.pallas_call(
        _combine_kernel,
        grid=(T // tm,),
        in_specs=[pl.BlockSpec((TOP_K, tm, yk.shape[2]), lambda i: (0, i, 0)),
                  tok(LANES), tok(D), tok(D),
                  pl.BlockSpec((1, 1, D), lambda i: (i // per_b, 0, 0))],
        out_specs=tok(D),
        out_shape=jax.ShapeDtypeStruct((T, D), F32),
        compiler_params=_cparams("arbitrary"),
        name="combine",
    )(yk, wts, shared, x1, gt)


def _rope_tables(positions):
    half = HEAD_DIM // 2
    inv = ROPE_THETA ** (-jnp.arange(half, dtype=F32) / half)
    ang = positions.astype(F32)[..., None] * inv
    cos, sin, zero = jnp.cos(ang), jnp.sin(ang), jnp.zeros_like(ang)
    cos_t = jnp.concatenate([cos, cos] * 2, axis=-1)
    sin_lo = jnp.concatenate([-sin, zero] * 2, axis=-1)
    sin_hi = jnp.concatenate([zero, sin] * 2, axis=-1)
    return cos_t, sin_lo, sin_hi


def _reorder_w_in(w):
    o = np.cumsum([0, Q_WIDTH] + [KV_WIDTH] * 6 + [NSA_Q_HEADS * 3, GM_WIDTH, GM_WIDTH, D_MODEL, D_MODEL])
    q, kc, vc, ks, vs, kw, vw, g, u, v, ga, gb = [w[:, o[i]:o[i + 1]] for i in range(12)]
    per = NSA_GROUP * 3
    pad = jnp.zeros((w.shape[0], LANES - per), w.dtype)
    return jnp.concatenate([q, ks, kw, kc, vc, vs, vw, g[:, :per], pad, g[:, per:], pad, u, v, ga, gb], axis=1)


def _owner(ends, pos):
    return jnp.minimum(jnp.sum((ends[None, :] <= pos[:, None]).astype(jnp.int32), axis=1), ends.shape[0] - 1)


def _lookup(table, idx):
    hit = idx[:, None] == jnp.arange(table.shape[0], dtype=jnp.int32)[None, :]
    return jnp.sum(jnp.where(hit, table[None, :], 0), axis=1)


def _expert_plan(counts, n_items_max):
    per_item = MOE_ROWS // MOE_GRAN
    counts = counts.astype(jnp.int32)
    gran = (counts + MOE_GRAN - 1) // MOE_GRAN
    gran_start = jnp.cumsum(gran) - gran
    items = (gran + per_item - 1) // per_item
    item_end = jnp.cumsum(items)
    w = jnp.arange(n_items_max, dtype=jnp.int32)
    ie = _owner(item_end, w)
    part = w - (_lookup(item_end, ie) - _lookup(items, ie))
    live = w < item_end[-1]
    item_ng = jnp.where(live, jnp.clip(_lookup(gran, ie) - part * per_item, 0, per_item), 0).astype(jnp.int32)
    item_row = jnp.where(live, (_lookup(gran_start, ie) + part * per_item) * MOE_GRAN, 0).astype(jnp.int32)
    return gran_start * MOE_GRAN, ie.astype(jnp.int32), item_row, item_ng, item_end[-1:].astype(jnp.int32)


def kernel(x, c, positions, w_mod, b_mod, w_in, q_gain, k_gain, cmp_pos_k, cmp_pos_v, cmp_w1_k, cmp_w2_k, cmp_w1_v, cmp_w2_v, gm_ln_g, gm_ln_b, gm_ws, gm_bs, w_proj_nsa, w_proj_gm, w_out, w_router, b_router, w_gate_e, w_up_e, w_down_e, w_gate_sh, w_up_sh, w_down_sh):
    B, S, D = x.shape
    L = w_mod.shape[0]
    T = B * S
    tm = 256
    tm_dense = 512
    scale = HEAD_DIM ** -0.5
    cos_t, sin_lo, sin_hi = _rope_tables(positions)
    mod = _mod_call(c, w_mod, b_mod)
    n_alloc = T * TOP_K + N_EXPERTS * MOE_GRAN + MOE_ROWS
    n_items_max = N_EXPERTS + T * TOP_K // MOE_ROWS + 1
    rows_buf = jnp.zeros((n_alloc, D // 2), jnp.uint32)

    for l in range(L):
        sh_a, sc_a, gt_a, sh_f, sc_f, gt_f = [mod[l, :, i * D:(i + 1) * D].reshape(B, 1, D) for i in range(6)]
        qg = (jnp.tile(q_gain[l], NSA_Q_HEADS) * scale).reshape(1, Q_WIDTH)
        kg = jnp.tile(k_gain[l], 2 * NSA_KV_HEADS).reshape(1, 2 * KV_WIDTH)
        (qn, qr, ks, kw, vs, vw, kc_raw, vc_raw, gates, u, v, ga, gb) = _inproj_call(
            x, sc_a, sh_a, _reorder_w_in(w_in[l]).astype(BF16), qg, kg, cos_t, sin_lo, sin_hi,
            gm_ln_g[l].reshape(1, GM_WIDTH), gm_ln_b[l].reshape(1, GM_WIDTH), tm_dense)
        kc, vc = _compress_call(
            kc_raw, vc_raw, cmp_w1_k[l].astype(BF16), cmp_w2_k[l].astype(BF16), cmp_pos_k[l].reshape(1, -1),
            cmp_w1_v[l].astype(BF16), cmp_w2_v[l].astype(BF16), cmp_pos_v[l].reshape(1, -1),
            k_gain[l].reshape(1, HEAD_DIM))
        o_cmp, q_aug = _nsa_cmp_call(qn, qr, kc, vc, WINDOW // 2)
        o_nsa = _nsa_flash_call(q_aug, ks, vs, kw, vw, o_cmp, gates)
        bs_full = jnp.repeat(gm_bs[l].T, GM_GROUP_DIM, axis=1)
        o_gm = _gmlp_call(u, v, gm_ws[l], bs_full, 512)
        x1, h2, logits, shared = _mixout_call(
            o_nsa, o_gm, ga, gb, x, gt_a, sc_f, sh_f,
            w_proj_nsa[l].astype(BF16), w_proj_gm[l].astype(BF16), w_out[l].astype(BF16), w_router[l].T.astype(BF16),
            w_gate_sh[l].astype(BF16), w_up_sh[l].astype(BF16), w_down_sh[l].astype(BF16), tm_dense)
        idx, wts, rank, counts = _route_call(logits, b_router[l], tm)
        row_start, item_e, item_row, item_ng, n_items = _expert_plan(counts[:, 0], n_items_max)
        slots_by_k = _slot_call(idx, rank, row_start, tm).reshape(TOP_K * T)
        xs_ref = jax.new_ref(rows_buf)
        _scatter_rows(h2.reshape(T, D // 2), slots_by_k, xs_ref)
        y = _expert_call(l, item_e, item_row, item_ng, n_items, jax.freeze(xs_ref), w_gate_e, w_up_e, w_down_e)
        yk = _gather_rows(y, slots_by_k).reshape(TOP_K, T, D // 2)
        rows_buf = y
        x = _combine_call(yk, wts, shared.reshape(T, D), x1.reshape(T, D), gt_f, tm).reshape(B, S, D)
    return x
```

```python
import functools

import jax
import jax.numpy as jnp
import numpy as np
from jax import lax
from jax.experimental import pallas as pl
from jax.experimental.pallas import tpu as pltpu
from jax.experimental.pallas import tpu_sc as plsc

D_MODEL = 1024
NSA_Q_HEADS = 8
NSA_KV_HEADS = 2
HEAD_DIM = 64
NSA_GROUP = NSA_Q_HEADS // NSA_KV_HEADS
CMP_LEN = 32
CMP_STRIDE = 16
CMP_HIDDEN = 256
SEL_LEN = 64
SEL_TOPN = 16
WINDOW = 512
ROPE_THETA = 10000.0
Q_WIDTH = NSA_Q_HEADS * HEAD_DIM
KV_WIDTH = NSA_KV_HEADS * HEAD_DIM
GM_GROUPS = 8
GM_GROUP_DIM = 64
GM_WIDTH = GM_GROUPS * GM_GROUP_DIM
GM_CHUNK = 128
N_EXPERTS = 256
TOP_K = 8
D_EXPERT = 256
D_SHARED = 256
ROUTE_SCALE = 2.5
EPS = 1e-6

LANES = 128
SEL_BIAS_WIDTH = 64
MASK_NEG = -1e30
SEL_NEG = -30000.0
SC_WINDOW = 128
MOE_GRAN = 128
MOE_ROWS = 9 * MOE_GRAN
VMEM_LIMIT = 56 * 1024 * 1024

F32 = jnp.float32
BF16 = jnp.bfloat16
HI = lax.Precision.HIGHEST


def _cparams(*sem):
    return pltpu.CompilerParams(dimension_semantics=sem, vmem_limit_bytes=VMEM_LIMIT)


def _dot(a, b, **kw):
    return jnp.dot(a, b, preferred_element_type=F32, **kw)


def _dot_nt(a, b, **kw):
    return lax.dot_general(a, b, (((1,), (1,)), ((), ())), preferred_element_type=F32, **kw)


def _gelu(x):
    return 0.5 * x * (1.0 + jnp.tanh(0.7978845608028654 * (x + 0.044715 * (x * x * x))))


def _sigmoid(x):
    return 1.0 / (1.0 + jnp.exp(-x))


def _silu(x):
    return x * _sigmoid(x)


_HI_MASK = np.uint32(0xFFFF0000)


def _pack_halves(a):
    w = a.shape[1] // 2
    bits = lax.bitcast_convert_type(a.astype(BF16).astype(F32), jnp.uint32)
    return (bits[:, w:] & _HI_MASK) | (bits[:, :w] >> 16)


def _unpack_halves(words):
    lo = lax.bitcast_convert_type(words << 16, F32)
    hi = lax.bitcast_convert_type(words & _HI_MASK, F32)
    return lo, hi


def _mod_kernel(c_ref, w_ref, b_ref, o_ref):
    c = c_ref[...]
    o_ref[0] = _dot(_silu(c), w_ref[0], precision=HI) + b_ref[0]


def _mod_call(c, w_mod, b_mod):
    L, D, N = w_mod.shape
    B = c.shape[0]
    tn = 1536
    return pl.pallas_call(
        _mod_kernel,
        grid=(L, N // tn),
        in_specs=[pl.BlockSpec((B, D), lambda l, j: (0, 0)),
                  pl.BlockSpec((1, D, tn), lambda l, j: (l, 0, j)),
                  pl.BlockSpec((1, 1, tn), lambda l, j: (l, 0, j))],
        out_specs=pl.BlockSpec((1, B, tn), lambda l, j: (l, 0, j)),
        out_shape=jax.ShapeDtypeStruct((L, B, N), F32),
        compiler_params=_cparams("arbitrary", "arbitrary"),
        name="mod",
    )(c, w_mod, b_mod.reshape(L, 1, N))


_C_Q = 0
_C_K = 512
_C_KC = 768
_C_VC = 896
_C_VS = 1024
_C_VW = 1152
_C_G = 1280
_C_U = 1536
_C_V = 2048
_C_GA = 2560
_C_GB = 3584
IN_COLS_P = 4608


def _dot_split(a, b):
    hi = a.astype(BF16)
    lo = (a - hi.astype(F32)).astype(BF16)
    return _dot(hi, b) + _dot(lo, b)


def _head_norm(z, bd):
    ms = _dot_split(z * z, bd)
    return z * lax.rsqrt(ms + EPS)


def _rope(z, cos, sin_lo, sin_hi):
    w = z.shape[-1]
    half = HEAD_DIM // 2
    return z * cos + pltpu.roll(z, w - half, 1) * sin_lo + pltpu.roll(z, half, 1) * sin_hi


def _tile_lanes(t, n):
    return t if n == 1 else jnp.concatenate([t] * n, axis=-1)


def _inproj_kernel(x_ref, sc_ref, sh_ref, w_ref, bdq_ref, bdk_ref, qg_ref, kg_ref,
                   cos_ref, sl_ref, shi_ref, lng_ref, lnb_ref,
                   qn_ref, qr_ref, ks_ref, kw_ref, vs_ref, vw_ref, kc_ref, vc_ref,
                   g_ref, u_ref, v_ref, ga_ref, gb_ref):
    tm = x_ref.shape[1]
    x = x_ref[0]
    ms = jnp.mean(x * x, axis=-1, keepdims=True)
    h = (x * lax.rsqrt(ms + EPS)) * (1.0 + sc_ref[0]) + sh_ref[0]
    hb = h.astype(BF16)

    def mm(lo, width):
        return _dot(hb, w_ref[:, lo:lo + width])

    cos, sl, shi = cos_ref[0], sl_ref[0], shi_ref[0]

    zq = mm(_C_Q, Q_WIDTH)
    qn = _head_norm(zq, bdq_ref[...]) * qg_ref[...]
    qr = _rope(qn, _tile_lanes(cos, 4), _tile_lanes(sl, 4), _tile_lanes(shi, 4))
    qn_ref[0] = qn.astype(BF16)
    qr_ref[0] = qr.astype(BF16)

    zk = mm(_C_K, 2 * KV_WIDTH)
    kn = _head_norm(zk, bdk_ref[...]) * kg_ref[...]
    kr = _rope(kn, _tile_lanes(cos, 2), _tile_lanes(sl, 2), _tile_lanes(shi, 2))
    lane = lax.broadcasted_iota(jnp.int32, (tm, LANES), 1)
    tok = pl.program_id(1) * tm + lax.broadcasted_iota(jnp.int32, (tm, LANES), 0)
    onehot = jnp.where(lane - HEAD_DIM == tok // SEL_LEN, 1.0, 0.0)
    ones_col = jnp.where(lane == HEAD_DIM, 1.0, 0.0)
    low = lane < HEAD_DIM
    zvs = mm(_C_VS, KV_WIDTH)
    zvw = mm(_C_VW, KV_WIDTH)
    zkc = mm(_C_KC, KV_WIDTH)
    zvc = mm(_C_VC, KV_WIDTH)
    for kv in range(NSA_KV_HEADS):
        def head(a):
            return a if kv == 0 else pltpu.roll(a, HEAD_DIM, 1)
        ks_ref[0, kv] = jnp.where(low, head(kr[:, :KV_WIDTH]), onehot).T.astype(BF16)
        kw_ref[0, kv] = jnp.where(low, head(kr[:, KV_WIDTH:]), 0.0).T.astype(BF16)
        vs_ref[0, kv] = jnp.where(low, head(zvs), ones_col).astype(BF16)
        vw_ref[0, kv] = jnp.where(low, head(zvw), ones_col).astype(BF16)
        kc_ref[0, kv] = head(zkc)[:, :HEAD_DIM]
        vc_ref[0, kv] = head(zvc)[:, :HEAD_DIM]

    zg = mm(_C_G, 2 * LANES)
    sg = _sigmoid(zg)
    g_ref[0, 0] = sg[:, :LANES]
    g_ref[0, 1] = sg[:, LANES:]

    u_ref[0] = _gelu(mm(_C_U, GM_WIDTH)).astype(BF16)
    gv = _gelu(mm(_C_V, GM_WIDTH))
    mu = jnp.mean(gv, axis=-1, keepdims=True)
    cen = gv - mu
    var = jnp.mean(cen * cen, axis=-1, keepdims=True)
    v_ref[0] = ((cen * lax.rsqrt(var + EPS)) * lng_ref[...] + lnb_ref[...]).astype(BF16)

    ga_ref[0] = _sigmoid(mm(_C_GA, D_MODEL)).astype(BF16)
    gb_ref[0] = _sigmoid(mm(_C_GB, D_MODEL)).astype(BF16)


def _block_diag_mean(width):
    idx = np.arange(width) // HEAD_DIM
    return jnp.asarray((idx[:, None] == idx[None, :]).astype(np.float32) / HEAD_DIM).astype(BF16)


def _inproj_call(x, sc, sh, w_p, qg, kg, cos, sl, shi, lng, lnb, tm):
    B, S, D = x.shape
    H = NSA_KV_HEADS
    full = lambda *shape: pl.BlockSpec(shape, lambda b, i: (0,) * len(shape))
    tok3 = lambda w: pl.BlockSpec((1, tm, w), lambda b, i: (b, i, 0))
    per_b = pl.BlockSpec((1, 1, D), lambda b, i: (b, 0, 0))
    kv4 = lambda w: pl.BlockSpec((1, H, tm, w), lambda b, i: (b, 0, i, 0))
    kt4 = pl.BlockSpec((1, H, LANES, tm), lambda b, i: (b, 0, 0, i))
    sds = jax.ShapeDtypeStruct
    out_shape = [
        sds((B, S, Q_WIDTH), BF16), sds((B, S, Q_WIDTH), BF16),
        sds((B, H, LANES, S), BF16), sds((B, H, LANES, S), BF16),
        sds((B, H, S, LANES), BF16), sds((B, H, S, LANES), BF16),
        sds((B, H, S, HEAD_DIM), F32), sds((B, H, S, HEAD_DIM), F32),
        sds((B, H, S, LANES), F32),
        sds((B, S, GM_WIDTH), BF16), sds((B, S, GM_WIDTH), BF16),
        sds((B, S, D), BF16), sds((B, S, D), BF16),
    ]
    out_specs = [
        tok3(Q_WIDTH), tok3(Q_WIDTH), kt4, kt4, kv4(LANES), kv4(LANES),
        kv4(HEAD_DIM), kv4(HEAD_DIM), kv4(LANES),
        tok3(GM_WIDTH), tok3(GM_WIDTH), tok3(D), tok3(D),
    ]
    return pl.pallas_call(
        _inproj_kernel,
        grid=(B, S // tm),
        in_specs=[tok3(D), per_b, per_b, full(D, IN_COLS_P),
                  full(Q_WIDTH, Q_WIDTH), full(2 * KV_WIDTH, 2 * KV_WIDTH),
                  full(1, Q_WIDTH), full(1, 2 * KV_WIDTH),
                  tok3(LANES), tok3(LANES), tok3(LANES),
                  full(1, GM_WIDTH), full(1, GM_WIDTH)],
        out_specs=out_specs,
        out_shape=out_shape,
        compiler_params=_cparams("arbitrary", "arbitrary"),
        name="in_proj",
    )(x, sc, sh, w_p, _block_diag_mean(Q_WIDTH), _block_diag_mean(2 * KV_WIDTH), qg, kg,
      cos, sl, shi, lng, lnb)


def _compress_kernel(kr_ref, vr_ref, w1k_ref, w2k_ref, pek_ref, w1v_ref, w2v_ref, pev_ref,
                     kg_ref, kc_ref, vc_ref):
    nc = kr_ref.shape[2]
    half = CMP_STRIDE * HEAD_DIM

    def mlp(raw, w1_ref, w2_ref, pe_ref):
        a = raw.astype(BF16)
        top = _dot(a, w1_ref[:half, :])
        bot = _dot(a, w1_ref[half:, :])
        pe = jnp.broadcast_to(pe_ref[...], (8, 2 * half)).astype(BF16)
        pe_row = _dot(pe, w1_ref[...])[0:1, :]
        hid = top + pltpu.roll(bot, nc - 1, 0) + pe_row
        return _dot(_gelu(hid).astype(BF16), w2_ref[...])

    kc = mlp(kr_ref[0, 0], w1k_ref, w2k_ref, pek_ref)
    ms = jnp.mean(kc * kc, axis=-1, keepdims=True)
    kc_ref[0, 0] = (kc * lax.rsqrt(ms + EPS) * kg_ref[...]).astype(BF16)
    vc_ref[0, 0] = mlp(vr_ref[0, 0], w1v_ref, w2v_ref, pev_ref).astype(BF16)


def _compress_call(kc_raw, vc_raw, w1k, w2k, pek, w1v, w2v, pev, kg):
    B, H, S, hd = kc_raw.shape
    nc = S // CMP_STRIDE
    feat = CMP_STRIDE * hd
    raw = pl.BlockSpec((1, 1, nc, feat), lambda b, h: (b, h, 0, 0))
    full = lambda *shape: pl.BlockSpec(shape, lambda b, h: (0,) * len(shape))
    out = pl.BlockSpec((1, 1, nc, hd), lambda b, h: (b, h, 0, 0))
    return pl.pallas_call(
        _compress_kernel,
        grid=(B, H),
        in_specs=[raw, raw, full(2 * feat, CMP_HIDDEN), full(CMP_HIDDEN, hd), full(1, 2 * feat),
                  full(2 * feat, CMP_HIDDEN), full(CMP_HIDDEN, hd), full(1, 2 * feat),
                  full(1, hd)],
        out_specs=[out, out],
        out_shape=[jax.ShapeDtypeStruct((B, H, nc, hd), BF16)] * 2,
        compiler_params=_cparams("arbitrary", "arbitrary"),
        name="compress",
    )(kc_raw.reshape(B, H, nc, feat), vc_raw.reshape(B, H, nc, feat),
      w1k, w2k, pek, w1v, w2v, pev, kg)


def _group_rows(a):
    return jnp.concatenate([a[:, g * HEAD_DIM:(g + 1) * HEAD_DIM] for g in range(NSA_GROUP)], axis=0)


def _nsa_cmp_kernel(qn_ref, qr_ref, kc_ref, vc_ref, ovl_ref, oc_ref, qa_ref, *, n_sel):
    tq = qn_ref.shape[1]
    nc = kc_ref.shape[2]
    G = NSA_GROUP
    q0 = pl.program_id(2) * tq
    q4 = _group_rows(qn_ref[0])
    s = _dot_nt(q4, kc_ref[0, 0])
    row = lax.broadcasted_iota(jnp.int32, (G, tq, nc), 1).reshape(G * tq, nc)
    col = lax.broadcasted_iota(jnp.int32, (G * tq, nc), 1)
    vis = col * CMP_STRIDE + (CMP_LEN - 1) <= q0 + row
    s = jnp.where(vis, s, MASK_NEG)
    m = jnp.max(s, axis=-1, keepdims=True)
    e = jnp.where(vis, jnp.exp(s - m), 0.0)
    p = e / jnp.maximum(jnp.sum(e, axis=-1, keepdims=True), 1e-30)
    oc = _dot(p.astype(BF16), vc_ref[0, 0])
    oc_ref[0, 0] = oc.reshape(G, tq, HEAD_DIM)

    psum = p[0:tq] + p[tq:2 * tq] + p[2 * tq:3 * tq] + p[3 * tq:4 * tq]
    p_hi = psum.astype(BF16)
    p_lo = (psum - p_hi.astype(F32)).astype(BF16)
    imp = (_dot_nt(ovl_ref[...], p_hi) + _dot_nt(ovl_ref[...], p_lo))[:SEL_BIAS_WIDTH]
    blk = lax.broadcasted_iota(jnp.int32, (SEL_BIAS_WIDTH, tq), 0)
    cur = (q0 + lax.broadcasted_iota(jnp.int32, (SEL_BIAS_WIDTH, tq), 1)) // SEL_LEN
    valid = blk <= cur
    forced = (blk == 0) | (blk == cur) | (blk == cur - 1)
    cand = valid & jnp.logical_not(forced)
    n_forced = jnp.minimum(cur, 2) + 1
    val = jnp.where(cand, imp, -1.0)
    cnt = jnp.zeros((SEL_BIAS_WIDTH, tq), F32)
    for j in range(n_sel):
        vj = jnp.broadcast_to(val[j:j + 1, :], (SEL_BIAS_WIDTH, tq))
        cnt = cnt + jnp.where(blk > j, jnp.where(vj >= val, 1.0, 0.0), jnp.where(vj > val, 1.0, 0.0))
    free = (min(SEL_TOPN, n_sel) - n_forced).astype(F32)
    sel = (forced & valid) | (cand & (cnt < free))
    sel_t = jnp.concatenate([jnp.where(sel, 1.0, 0.0), jnp.zeros((LANES - SEL_BIAS_WIDTH, tq), F32)], axis=0)
    sel_q = sel_t.T
    bias = jnp.where(sel_q > 0.5, 0.0, SEL_NEG)
    lane = lax.broadcasted_iota(jnp.int32, (tq, LANES), 1)
    bias_hi = pltpu.roll(bias, SEL_BIAS_WIDTH, 1)
    qr = qr_ref[0]
    for g in range(G):
        qg = qr[:, g * HEAD_DIM:(g + 1) * HEAD_DIM].astype(F32)
        qg = jnp.concatenate([qg, qg], axis=-1)
        qa_ref[0, 0, g] = jnp.where(lane < HEAD_DIM, qg, bias_hi).astype(BF16)


def _overlap_t(S):
    n_cmp = (S - CMP_LEN) // CMP_STRIDE + 1
    nc = S // CMP_STRIDE
    n_sel = S // SEL_LEN
    start = np.arange(nc) * CMP_STRIDE
    end = start + CMP_LEN - 1
    sel_start = np.arange(n_sel) * SEL_LEN
    ov = (start[None, :] <= sel_start[:, None] + SEL_LEN - 1) & (end[None, :] >= sel_start[:, None])
    ov = ov & (np.arange(nc) < n_cmp)[None, :]
    out = np.zeros((LANES, nc), np.float32)
    out[:n_sel] = ov.astype(np.float32)
    return jnp.asarray(out).astype(BF16)


def _nsa_cmp_call(qn, qr, kc, vc, tq):
    B, S, _ = qn.shape
    H, G = NSA_KV_HEADS, NSA_GROUP
    nc = kc.shape[2]
    n_sel = S // SEL_LEN
    assert n_sel <= SEL_BIAS_WIDTH and nc % LANES == 0
    qspec = pl.BlockSpec((1, tq, G * HEAD_DIM), lambda b, h, i: (b, i, h))
    cspec = pl.BlockSpec((1, 1, nc, HEAD_DIM), lambda b, h, i: (b, h, 0, 0))
    return pl.pallas_call(
        functools.partial(_nsa_cmp_kernel, n_sel=n_sel),
        grid=(B, H, S // tq),
        in_specs=[qspec, qspec, cspec, cspec, pl.BlockSpec((LANES, nc), lambda b, h, i: (0, 0))],
        out_specs=[pl.BlockSpec((1, 1, G, tq, HEAD_DIM), lambda b, h, i: (b, h, 0, i, 0)),
                   pl.BlockSpec((1, 1, G, tq, LANES), lambda b, h, i: (b, h, 0, i, 0))],
        out_shape=[jax.ShapeDtypeStruct((B, H, G, S, HEAD_DIM), F32),
                   jax.ShapeDtypeStruct((B, H, G, S, LANES), BF16)],
        compiler_params=_cparams("arbitrary", "arbitrary", "arbitrary"),
        name="nsa_cmp",
    )(qn, qr, kc, vc, _overlap_t(S))


SEL_CHUNK = 4


def _nsa_flash_kernel(qa_ref, ks_ref, vs_ref, kw_ref, vw_ref, oc_ref, g_ref, o_ref, m_scr, acc_scr):
    G = NSA_GROUP
    tq = qa_ref.shape[3]
    R = G * tq
    tk = tq
    i = pl.program_id(2)
    q0 = i * tq

    def rows_of(ref, j0, nt):
        return ref[0, 0, pl.ds(pl.multiple_of(j0 * tk, tk), nt * tk), :]

    def attend(k_ref, v_ref, j0, nt, visible, state):
        v = rows_of(v_ref, j0, nt)
        kt = k_ref[0, 0, :, pl.ds(pl.multiple_of(j0 * tk, tk), nt * tk)]
        s_all = _dot(qa_ref[0, 0].reshape(R, LANES), kt)
        ss = [s_all[g * tq:(g + 1) * tq] for g in range(G)]
        if visible is not None:
            qpos = q0 + lax.broadcasted_iota(jnp.int32, (tq, nt * tk), 0)
            kpos = j0 * tk + lax.broadcasted_iota(jnp.int32, (tq, nt * tk), 1)
            mask = visible(qpos, kpos)
        ps, alphas = [], []
        for g in range(G):
            s = ss[g] if visible is None else jnp.where(mask, ss[g], MASK_NEG)
            cols = [s[:, c * LANES:(c + 1) * LANES] for c in range(nt * tk // LANES)]
            m_new = jnp.max(functools.reduce(jnp.maximum, cols), axis=-1, keepdims=True)
            if state:
                m_prev = m_scr[g]
                m_new = jnp.maximum(m_prev, m_new)
                alphas.append(jnp.exp(m_prev - m_new))
                m_scr[g] = m_new
            ps.append(jnp.concatenate([jnp.exp(c - m_new) for c in cols], axis=-1).astype(BF16))
        if not state:
            return [_dot(ps[g], v) for g in range(G)]
        for g in range(G):
            acc_scr[g] = alphas[g] * acc_scr[g] + _dot(ps[g], v)

    def normalise(acc):
        return acc[:, :HEAD_DIM] / acc[:, HEAD_DIM:HEAD_DIM + 1]

    m_scr[...] = jnp.full((G, tq, LANES), MASK_NEG, F32)
    acc_scr[...] = jnp.zeros((G, tq, LANES), F32)
    causal = lambda qpos, kpos: kpos <= qpos

    def sel_body(c, carry):
        attend(ks_ref, vs_ref, c * SEL_CHUNK, SEL_CHUNK, None, True)
        return carry

    n_full = i // SEL_CHUNK
    lax.fori_loop(0, n_full, sel_body, 0)
    attend(ks_ref, vs_ref, n_full * SEL_CHUNK, 2, causal, True)

    @pl.when(i % SEL_CHUNK >= 2)
    def _():
        attend(ks_ref, vs_ref, n_full * SEL_CHUNK + 2, 2, causal, True)

    o_sel = normalise(acc_scr[...].reshape(R, LANES))

    assert WINDOW == 2 * tk
    band = lambda qpos, kpos: (kpos <= qpos) & (kpos > qpos - WINDOW)
    o_win = normalise(jnp.concatenate(attend(kw_ref, vw_ref, jnp.maximum(i - 2, 0), 3, band, False), axis=0))

    o_cmp = oc_ref[0, 0].reshape(R, HEAD_DIM)
    gates = g_ref[0, 0]
    outs = []
    for g in range(G):
        r = slice(g * tq, (g + 1) * tq)
        outs.append(gates[:, 3 * g:3 * g + 1] * o_cmp[r]
                    + gates[:, 3 * g + 1:3 * g + 2] * o_sel[r]
                    + gates[:, 3 * g + 2:3 * g + 3] * o_win[r])
    o_ref[0] = jnp.concatenate(outs, axis=-1).astype(BF16)


def _nsa_flash_call(qa, ks, vs, kw, vw, oc, gates):
    B, H, G, S, _ = qa.shape
    tq = WINDOW // 2
    assert (S // tq) % SEL_CHUNK == 0
    kt = pl.BlockSpec((1, 1, LANES, S), lambda b, h, i: (b, h, 0, 0))
    vv = pl.BlockSpec((1, 1, S, LANES), lambda b, h, i: (b, h, 0, 0))
    return pl.pallas_call(
        _nsa_flash_kernel,
        grid=(B, H, S // tq),
        in_specs=[pl.BlockSpec((1, 1, G, tq, LANES), lambda b, h, i: (b, h, 0, i, 0)),
                  kt, vv, kt, vv,
                  pl.BlockSpec((1, 1, G, tq, HEAD_DIM), lambda b, h, i: (b, h, 0, i, 0)),
                  pl.BlockSpec((1, 1, tq, LANES), lambda b, h, i: (b, h, i, 0))],
        out_specs=pl.BlockSpec((1, tq, G * HEAD_DIM), lambda b, h, i: (b, i, h)),
        out_shape=jax.ShapeDtypeStruct((B, S, Q_WIDTH), BF16),
        scratch_shapes=[pltpu.VMEM((G, tq, LANES), F32), pltpu.VMEM((G, tq, LANES), F32)],
        compiler_params=_cparams("arbitrary", "arbitrary", "arbitrary"),
        name="nsa_flash",
    )(qa, ks, vs, kw, vw, oc, gates)


def _gmlp_kernel(u_ref, v_ref, ws_ref, bs_ref, o_ref):
    tm = u_ref.shape[1]
    C = GM_CHUNK
    r = lax.broadcasted_iota(jnp.int32, (C, C), 0)
    c = lax.broadcasted_iota(jnp.int32, (C, C), 1)
    causal = c <= r
    ws = [jnp.where(causal, ws_ref[g], 0.0).astype(BF16) for g in range(GM_GROUPS)]
    for n in range(tm // C):
        rows = slice(n * C, (n + 1) * C)
        vn = v_ref[0, rows, :]
        mixed = jnp.concatenate(
            [_dot(ws[g], vn[:, g * GM_GROUP_DIM:(g + 1) * GM_GROUP_DIM]) for g in range(GM_GROUPS)],
            axis=-1)
        o_ref[0, rows, :] = (u_ref[0, rows, :].astype(F32) * (mixed + bs_ref[...])).astype(BF16)


def _gmlp_call(u, v, ws, bs_full, tm):
    B, S, W = u.shape
    tok = pl.BlockSpec((1, tm, W), lambda b, i: (b, i, 0))
    return pl.pallas_call(
        _gmlp_kernel,
        grid=(B, S // tm),
        in_specs=[tok, tok,
                  pl.BlockSpec((GM_GROUPS, GM_CHUNK, GM_CHUNK), lambda b, i: (0, 0, 0)),
                  pl.BlockSpec((GM_CHUNK, W), lambda b, i: (0, 0))],
        out_specs=tok,
        out_shape=jax.ShapeDtypeStruct((B, S, W), BF16),
        compiler_params=_cparams("arbitrary", "arbitrary"),
        name="gmlp",
    )(u, v, ws, bs_full)


def _mixout_kernel(on_ref, og_ref, ga_ref, gb_ref, x_ref, gt_ref, sc_ref, sh_ref,
                   wpn_ref, wpg_ref, wo_ref, wr_ref, wgs_ref, wus_ref, wds_ref,
                   x1_ref, h2_ref, lg_ref, shd_ref):
    ya = _dot(on_ref[0], wpn_ref[...])
    yb = _dot(og_ref[0], wpg_ref[...])
    merged = ga_ref[0].astype(F32) * ya + gb_ref[0].astype(F32) * yb
    x1 = x_ref[0] + gt_ref[0] * _dot(merged.astype(BF16), wo_ref[...])
    x1_ref[0] = x1
    ms = jnp.mean(x1 * x1, axis=-1, keepdims=True)
    h2 = (x1 * lax.rsqrt(ms + EPS)) * (1.0 + sc_ref[0]) + sh_ref[0]
    hb = h2.astype(BF16)
    lg_ref[0] = _dot_nt(wr_ref[...], hb)
    h2_ref[0] = _pack_halves(hb)
    act = _silu(_dot(hb, wgs_ref[...])) * _dot(hb, wus_ref[...])
    shd_ref[0] = _dot(act.astype(BF16), wds_ref[...])


def _mixout_call(o_nsa, o_gm, ga, gb, x, gt, sc, sh, wpn, wpg, wo, wr, wgs, wus, wds, tm):
    B, S, D = x.shape
    tok = lambda w: pl.BlockSpec((1, tm, w), lambda b, i: (b, i, 0))
    per_b = pl.BlockSpec((1, 1, D), lambda b, i: (b, 0, 0))
    full = lambda a: pl.BlockSpec(a.shape, lambda b, i: (0,) * a.ndim)
    sds = jax.ShapeDtypeStruct
    return pl.pallas_call(
        _mixout_kernel,
        grid=(B, S // tm),
        in_specs=[tok(Q_WIDTH), tok(GM_WIDTH), tok(D), tok(D), tok(D), per_b, per_b, per_b,
                  full(wpn), full(wpg), full(wo), full(wr), full(wgs), full(wus), full(wds)],
        out_specs=[tok(D), tok(D // 2), pl.BlockSpec((1, N_EXPERTS, tm), lambda b, i: (b, 0, i)), tok(D)],
        out_shape=[sds((B, S, D), F32), sds((B, S, D // 2), jnp.uint32), sds((B, N_EXPERTS, S), F32),
                   sds((B, S, D), F32)],
        compiler_params=_cparams("arbitrary", "arbitrary"),
        name="mix_out",
    )(o_nsa, o_gm, ga, gb, x, gt, sc, sh, wpn, wpg, wo, wr, wgs, wus, wds)


def _route_kernel(lg_ref, br_ref, idx_ref, wt_ref, rank_ref, cnt_ref, run_scr):
    E, tm = lg_ref.shape[1], lg_ref.shape[2]

    @pl.when(pl.program_id(0) == 0)
    def _():
        run_scr[...] = jnp.zeros_like(run_scr)

    aff = _sigmoid(lg_ref[0])
    work = aff + br_ref[...]
    row = lax.broadcasted_iota(jnp.int32, (E, tm), 0).astype(F32)
    picked = jnp.zeros((E, tm), F32)
    idxs, tops = [], []
    for _ in range(TOP_K):
        m = jnp.max(work, axis=0, keepdims=True)
        idx = jnp.min(jnp.where(work == m, row, float(E)), axis=0, keepdims=True)
        hit = row == idx
        tops.append(jnp.sum(jnp.where(hit, aff, 0.0), axis=0, keepdims=True))
        idxs.append(idx)
        picked = jnp.where(hit, 1.0, picked)
        work = jnp.where(hit, -jnp.inf, work)
    total = functools.reduce(jnp.add, tops)
    r = lax.broadcasted_iota(jnp.int32, (tm, tm), 0)
    c = lax.broadcasted_iota(jnp.int32, (tm, tm), 1)
    before = _dot(picked.astype(BF16), jnp.where(r < c, 1.0, 0.0).astype(BF16)) + run_scr[...]
    ranks = [jnp.sum(jnp.where(row == idx, before, 0.0), axis=0, keepdims=True) for idx in idxs]
    run_scr[...] = run_scr[...] + jnp.sum(picked, axis=1, keepdims=True)
    cnt_ref[...] = run_scr[...]
    idx_ref[...] = jnp.concatenate(idxs, axis=0).astype(jnp.int32)
    rank_ref[...] = jnp.concatenate(ranks, axis=0).astype(jnp.int32)
    wt = jnp.concatenate([t / total * ROUTE_SCALE for t in tops]
                         + [jnp.zeros((LANES - TOP_K, tm), F32)], axis=0)
    eye = jnp.where(r == c, 1.0, 0.0).astype(BF16)
    cols = jnp.zeros((tm, LANES), F32)
    rest = wt
    for _ in range(3):
        part = rest.astype(BF16)
        cols = cols + _dot_nt(eye, part)
        rest = rest - part.astype(F32)
    wt_ref[...] = cols


def _route_call(logits_t, b_router, tm):
    B, E, S = logits_t.shape
    per_b = S // tm
    T = B * S
    kt = pl.BlockSpec((TOP_K, tm), lambda i: (0, i))
    col = pl.BlockSpec((E, 1), lambda i: (0, 0))
    sds = jax.ShapeDtypeStruct
    return pl.pallas_call(
        _route_kernel,
        grid=(T // tm,),
        in_specs=[pl.BlockSpec((1, E, tm), lambda i: (i // per_b, 0, i % per_b)), col],
        out_specs=[kt, pl.BlockSpec((tm, LANES), lambda i: (i, 0)), kt, col],
        out_shape=[sds((TOP_K, T), jnp.int32), sds((T, LANES), F32), sds((TOP_K, T), jnp.int32),
                   sds((E, 1), F32)],
        scratch_shapes=[pltpu.VMEM((E, 1), F32)],
        compiler_params=_cparams("arbitrary"),
        name="route",
    )(logits_t, b_router.reshape(E, 1))


def _slot_kernel(idx_ref, rank_ref, ps_ref, o_ref):
    tm = idx_ref.shape[1]
    E = ps_ref.shape[0]
    row = lax.broadcasted_iota(jnp.int32, (E, tm), 0)
    idx, rank, ps = idx_ref[...], rank_ref[...], ps_ref[...]
    base = [jnp.sum(jnp.where(row == idx[k:k + 1, :], ps, 0.0), axis=0, keepdims=True) for k in range(TOP_K)]
    o_ref[...] = jnp.concatenate(base, axis=0).astype(jnp.int32) + rank


def _slot_call(idx, rank, pad_start, tm):
    T = idx.shape[1]
    E = pad_start.shape[0]
    kt = pl.BlockSpec((TOP_K, tm), lambda i: (0, i))
    return pl.pallas_call(
        _slot_kernel,
        grid=(T // tm,),
        in_specs=[kt, kt, pl.BlockSpec((E, 1), lambda i: (0, 0))],
        out_specs=kt,
        out_shape=jax.ShapeDtypeStruct((TOP_K, T), jnp.int32),
        compiler_params=_cparams("arbitrary"),
        name="slot",
    )(idx, rank, pad_start.astype(F32).reshape(E, 1))


def _sc_mesh():
    mesh = plsc.VectorSubcoreMesh(core_axis_name="core", subcore_axis_name="subcore")
    return mesh, mesh.num_cores * mesh.num_subcores


def _sc_worker(mesh):
    return lax.axis_index("subcore") * mesh.num_cores + lax.axis_index("core")


def _scatter_rows(rows, indices_by_k, out_ref):
    n, width = rows.shape
    mesh, workers = _sc_mesh()
    per_worker = n // workers
    assert per_worker * workers == n and per_worker % SC_WINDOW == 0

    @pl.kernel(out_type=(), mesh=mesh, name="scatter_rows",
               scratch_types=[pltpu.VMEM((SC_WINDOW,), jnp.int32), pltpu.VMEM((SC_WINDOW, width), rows.dtype)])
    def scatter(rows_hbm, idx_hbm, out_hbm, idx_v, rows_v):
        worker = _sc_worker(mesh)

        @pl.loop(0, per_worker // SC_WINDOW)
        def _(j):
            base = pl.multiple_of(worker * per_worker + j * SC_WINDOW, SC_WINDOW)
            pltpu.sync_copy(rows_hbm.at[pl.ds(base, SC_WINDOW)], rows_v)
            for k in range(TOP_K):
                pltpu.sync_copy(idx_hbm.at[pl.ds(pl.multiple_of(k * n + base, SC_WINDOW), SC_WINDOW)], idx_v)
                pltpu.sync_copy(rows_v, out_hbm.at[idx_v])

    scatter(rows, indices_by_k, out_ref)


def _expert_kernel(ie_ref, ir_ref, ig_ref, n_ref, x_hbm, wg_ref, wu_ref, wd_ref, y_hbm,
                   xbuf, ybuf, act_scr, xsem, ysem):
    w = pl.program_id(0)
    n = n_ref[0]
    R = xbuf.shape[1]
    last = ie_ref.shape[0] - 1

    def x_copy(item, slot):
        row = pl.multiple_of(ir_ref[jnp.minimum(item, last)], MOE_GRAN)
        return pltpu.make_async_copy(x_hbm.at[pl.ds(row, R), :], xbuf.at[slot], xsem.at[slot])

    def y_copy(item, slot, gi):
        row = pl.multiple_of(ir_ref[jnp.minimum(item, last)] + gi * MOE_GRAN, MOE_GRAN)
        src = ybuf.at[slot, pl.ds(pl.multiple_of(gi * MOE_GRAN, MOE_GRAN), MOE_GRAN), :]
        return pltpu.make_async_copy(src, y_hbm.at[pl.ds(row, MOE_GRAN), :], ysem.at[slot])

    def for_granules(item, fn):
        def body(gi, carry):
            fn(gi)
            return carry
        lax.fori_loop(0, ig_ref[jnp.clip(item, 0, last)], body, 0)

    def gate_up(slot):
        lo, hi = _unpack_halves(xbuf[slot])
        x = jnp.concatenate([lo, hi], axis=1).astype(BF16)
        a = _dot(x, wg_ref[0, 0].astype(BF16))
        b = _dot(x, wu_ref[0, 0].astype(BF16))
        return (_silu(a) * b).astype(BF16)

    def down(act):
        return _pack_halves(_dot(act, wd_ref[0, 0].astype(BF16)))

    nx = xbuf.shape[0]
    cur, prv = w % nx, (w + 1) % 2

    @pl.when(w == 0)
    def _():
        for j in range(nx - 1):
            @pl.when(j < n)
            def _():
                x_copy(j, j).start()

    @pl.when(w < n)
    def _():
        x_copy(w, cur).wait()

    @pl.when(w + nx - 1 < n)
    def _():
        x_copy(w + nx - 1, (w + nx - 1) % nx).start()

    @pl.when((w >= 3) & (w - 3 < n))
    def _():
        for_granules(w - 3, lambda gi: y_copy(w - 3, prv, gi).wait())

    @pl.when(w == 0)
    def _():
        act_scr[...] = gate_up(0)

    @pl.when((w >= 1) & (w < n))
    def _():
        prev = act_scr[...]
        ybuf[prv] = down(prev)
        act_scr[...] = gate_up(cur)

    @pl.when((w >= 1) & (w == n))
    def _():
        ybuf[prv] = down(act_scr[...])

    @pl.when((w >= 1) & (w <= n))
    def _():
        for_granules(w - 1, lambda gi: y_copy(w - 1, prv, gi).start())


def _expert_call(layer, item_e, item_row, item_ng, n_items, xs, wg, wu, wd):
    n_alloc, W = xs.shape
    R = MOE_ROWS
    nw = item_e.shape[0]
    D, F = wg.shape[2], wg.shape[3]
    cur = lambda w, ie, ir, ig, n: (layer, ie[jnp.minimum(w, nw - 1)], 0, 0)
    prev = lambda w, ie, ir, ig, n: (layer, ie[jnp.clip(w - 1, 0, nw - 1)], 0, 0)
    grid_spec = pltpu.PrefetchScalarGridSpec(
        num_scalar_prefetch=4,
        grid=(nw + 3,),
        in_specs=[pl.BlockSpec(memory_space=pl.ANY),
                  pl.BlockSpec((1, 1, D, F), cur), pl.BlockSpec((1, 1, D, F), cur),
                  pl.BlockSpec((1, 1, F, D), prev)],
        out_specs=pl.BlockSpec(memory_space=pl.ANY),
        scratch_shapes=[pltpu.VMEM((3, R, W), jnp.uint32), pltpu.VMEM((2, R, W), jnp.uint32),
                        pltpu.VMEM((R, F), BF16),
                        pltpu.SemaphoreType.DMA((3,)), pltpu.SemaphoreType.DMA((2,))],
    )
    return pl.pallas_call(
        _expert_kernel,
        grid_spec=grid_spec,
        out_shape=jax.ShapeDtypeStruct((n_alloc, W), jnp.uint32),
        input_output_aliases={4: 0},
        compiler_params=pltpu.CompilerParams(dimension_semantics=("arbitrary",), vmem_limit_bytes=VMEM_LIMIT,
                                             has_side_effects=True),
        name="expert",
    )(item_e, item_row, item_ng, n_items, xs, wg, wu, wd)


def _gather_rows(table, indices):
    n = indices.shape[0]
    width = table.shape[1]
    mesh, workers = _sc_mesh()
    per_worker = n // workers
    assert per_worker * workers == n and per_worker % SC_WINDOW == 0

    half = SC_WINDOW // 2
    steps = per_worker // SC_WINDOW
    rows_t = pltpu.VMEM((half, width), table.dtype)

    @pl.kernel(out_type=jax.ShapeDtypeStruct((n, width), table.dtype), mesh=mesh, name="gather_rows",
               scratch_types=[pltpu.VMEM((SC_WINDOW,), jnp.int32), rows_t, rows_t] + [pltpu.SemaphoreType.DMA] * 4)
    def gather(table_hbm, idx_hbm, out_hbm, idx_v, rows_a, rows_b, gsem_a, gsem_b, wsem_a, wsem_b):
        worker = _sc_worker(mesh)
        first = worker * per_worker
        halves = ((rows_a, gsem_a, wsem_a, 0), (rows_b, gsem_b, wsem_b, half))

        def write(buf, wsem, row):
            return pltpu.make_async_copy(buf, out_hbm.at[pl.ds(pl.multiple_of(row, half), half)], wsem)

        @pl.loop(0, steps)
        def _(j):
            base = pl.multiple_of(first + j * SC_WINDOW, SC_WINDOW)
            pltpu.sync_copy(idx_hbm.at[pl.ds(base, SC_WINDOW)], idx_v)
            gathers = []
            for buf, gsem, wsem, off in halves:
                @pl.when(j > 0)
                def _():
                    write(buf, wsem, base).wait()
                gathers.append(pltpu.async_copy(table_hbm.at[idx_v.at[pl.ds(off, half)]], buf, gsem))
            for (buf, gsem, wsem, off), g in zip(halves, gathers):
                g.wait()
                write(buf, wsem, base + off).start()

        for buf, gsem, wsem, off in halves:
            write(buf, wsem, first).wait()

    return gather(table, indices)


def _combine_kernel(y_ref, wt_ref, shd_ref, x_ref, gt_ref, o_ref):
    wt = wt_ref[...]
    acc_lo = acc_hi = None
    for k in range(TOP_K):
        lo, hi = _unpack_halves(y_ref[k])
        w = wt[:, k:k + 1]
        acc_lo = w * lo if k == 0 else acc_lo + w * lo
        acc_hi = w * hi if k == 0 else acc_hi + w * hi
    routed = jnp.concatenate([acc_lo, acc_hi], axis=1)
    o_ref[...] = x_ref[...] + gt_ref[0] * (routed + shd_ref[...])


def _combine_call(yk, wts, shared, x1, gt, tm):
    T, D = x1.shape
    B = gt.shape[0]
    per_b = T // B // tm
    tok = lambda w: pl.BlockSpec((tm, w), lambda i: (i, 0))
    return pl.pallas_call(
        _combine_kernel,
        grid=(T // tm,),
        in_specs=[pl.BlockSpec((TOP_K, tm, yk.shape[2]), lambda i: (0, i, 0)),
                  tok(LANES), tok(D), tok(D),
                  pl.BlockSpec((1, 1, D), lambda i: (i // per_b, 0, 0))],
        out_specs=tok(D),
        out_shape=jax.ShapeDtypeStruct((T, D), F32),
        compiler_params=_cparams("arbitrary"),
        name="combine",
    )(yk, wts, shared, x1, gt)


def _rope_tables(positions):
    half = HEAD_DIM // 2
    inv = ROPE_THETA ** (-jnp.arange(half, dtype=F32) / half)
    ang = positions.astype(F32)[..., None] * inv
    cos, sin, zero = jnp.cos(ang), jnp.sin(ang), jnp.zeros_like(ang)
    cos_t = jnp.concatenate([cos, cos] * 2, axis=-1)
    sin_lo = jnp.concatenate([-sin, zero] * 2, axis=-1)
    sin_hi = jnp.concatenate([zero, sin] * 2, axis=-1)
    return cos_t, sin_lo, sin_hi


def _reorder_w_in(w):
    o = np.cumsum([0, Q_WIDTH] + [KV_WIDTH] * 6 + [NSA_Q_HEADS * 3, GM_WIDTH, GM_WIDTH, D_MODEL, D_MODEL])
    q, kc, vc, ks, vs, kw, vw, g, u, v, ga, gb = [w[:, o[i]:o[i + 1]] for i in range(12)]
    per = NSA_GROUP * 3
    pad = jnp.zeros((w.shape[0], LANES - per), w.dtype)
    return jnp.concatenate([q, ks, kw, kc, vc, vs, vw, g[:, :per], pad, g[:, per:], pad, u, v, ga, gb], axis=1)


def _owner(ends, pos):
    return jnp.minimum(jnp.sum((ends[None, :] <= pos[:, None]).astype(jnp.int32), axis=1), ends.shape[0] - 1)


def _lookup(table, idx):
    hit = idx[:, None] == jnp.arange(table.shape[0], dtype=jnp.int32)[None, :]
    return jnp.sum(jnp.where(hit, table[None, :], 0), axis=1)


def _expert_plan(counts, n_items_max):
    per_item = MOE_ROWS // MOE_GRAN
    counts = counts.astype(jnp.int32)
    gran = (counts + MOE_GRAN - 1) // MOE_GRAN
    gran_start = jnp.cumsum(gran) - gran
    items = (gran + per_item - 1) // per_item
    item_end = jnp.cumsum(items)
    w = jnp.arange(n_items_max, dtype=jnp.int32)
    ie = _owner(item_end, w)
    part = w - (_lookup(item_end, ie) - _lookup(items, ie))
    live = w < item_end[-1]
    item_ng = jnp.where(live, jnp.clip(_lookup(gran, ie) - part * per_item, 0, per_item), 0).astype(jnp.int32)
    item_row = jnp.where(live, (_lookup(gran_start, ie) + part * per_item) * MOE_GRAN, 0).astype(jnp.int32)
    return gran_start * MOE_GRAN, ie.astype(jnp.int32), item_row, item_ng, item_end[-1:].astype(jnp.int32)


def kernel(x, c, positions, w_mod, b_mod, w_in, q_gain, k_gain, cmp_pos_k, cmp_pos_v, cmp_w1_k, cmp_w2_k, cmp_w1_v, cmp_w2_v, gm_ln_g, gm_ln_b, gm_ws, gm_bs, w_proj_nsa, w_proj_gm, w_out, w_router, b_router, w_gate_e, w_up_e, w_down_e, w_gate_sh, w_up_sh, w_down_sh):
    B, S, D = x.shape
    L = w_mod.shape[0]
    T = B * S
    tm = 256
    tm_dense = 512
    scale = HEAD_DIM ** -0.5
    cos_t, sin_lo, sin_hi = _rope_tables(positions)
    mod = _mod_call(c, w_mod, b_mod)
    n_alloc = T * TOP_K + N_EXPERTS * MOE_GRAN + MOE_ROWS
    n_items_max = N_EXPERTS + T * TOP_K // MOE_ROWS + 1
    rows_buf = jnp.zeros((n_alloc, D // 2), jnp.uint32)

    for l in range(L):
        sh_a, sc_a, gt_a, sh_f, sc_f, gt_f = [mod[l, :, i * D:(i + 1) * D].reshape(B, 1, D) for i in range(6)]
        qg = (jnp.tile(q_gain[l], NSA_Q_HEADS) * scale).reshape(1, Q_WIDTH)
        kg = jnp.tile(k_gain[l], 2 * NSA_KV_HEADS).reshape(1, 2 * KV_WIDTH)
        (qn, qr, ks, kw, vs, vw, kc_raw, vc_raw, gates, u, v, ga, gb) = _inproj_call(
            x, sc_a, sh_a, _reorder_w_in(w_in[l]).astype(BF16), qg, kg, cos_t, sin_lo, sin_hi,
            gm_ln_g[l].reshape(1, GM_WIDTH), gm_ln_b[l].reshape(1, GM_WIDTH), tm_dense)
        kc, vc = _compress_call(
            kc_raw, vc_raw, cmp_w1_k[l].astype(BF16), cmp_w2_k[l].astype(BF16), cmp_pos_k[l].reshape(1, -1),
            cmp_w1_v[l].astype(BF16), cmp_w2_v[l].astype(BF16), cmp_pos_v[l].reshape(1, -1),
            k_gain[l].reshape(1, HEAD_DIM))
        o_cmp, q_aug = _nsa_cmp_call(qn, qr, kc, vc, WINDOW // 2)
        o_nsa = _nsa_flash_call(q_aug, ks, vs, kw, vw, o_cmp, gates)
        bs_full = jnp.repeat(gm_bs[l].T, GM_GROUP_DIM, axis=1)
        o_gm = _gmlp_call(u, v, gm_ws[l], bs_full, 512)
        x1, h2, logits, shared = _mixout_call(
            o_nsa, o_gm, ga, gb, x, gt_a, sc_f, sh_f,
            w_proj_nsa[l].astype(BF16), w_proj_gm[l].astype(BF16), w_out[l].astype(BF16), w_router[l].T.astype(BF16),
            w_gate_sh[l].astype(BF16), w_up_sh[l].astype(BF16), w_down_sh[l].astype(BF16), tm_dense)
        idx, wts, rank, counts = _route_call(logits, b_router[l], tm)
        row_start, item_e, item_row, item_ng, n_items = _expert_plan(counts[:, 0], n_items_max)
        slots_by_k = _slot_call(idx, rank, row_start, tm).reshape(TOP_K * T)
        xs_ref = jax.new_ref(rows_buf)
        _scatter_rows(h2.reshape(T, D // 2), slots_by_k, xs_ref)
        y = _expert_call(l, item_e, item_row, item_ng, n_items, jax.freeze(xs_ref), w_gate_e, w_up_e, w_down_e)
        yk = _gather_rows(y, slots_by_k).reshape(TOP_K, T, D // 2)
        rows_buf = y
        x = _combine_call(yk, wts, shared.reshape(T, D), x1.reshape(T, D), gt_f, tm).reshape(B, S, D)
    return x
```

```python
import functools

import jax
import jax.numpy as jnp
import numpy as np
from jax import lax
from jax.experimental import pallas as pl
from jax.experimental.pallas import tpu as pltpu
from jax.experimental.pallas import tpu_sc as plsc

D_MODEL = 1024
NSA_Q_HEADS = 8
NSA_KV_HEADS = 2
HEAD_DIM = 64
NSA_GROUP = NSA_Q_HEADS // NSA_KV_HEADS
CMP_LEN = 32
CMP_STRIDE = 16
CMP_HIDDEN = 256
SEL_LEN = 64
SEL_TOPN = 16
WINDOW = 512
ROPE_THETA = 10000.0
Q_WIDTH = NSA_Q_HEADS * HEAD_DIM
KV_WIDTH = NSA_KV_HEADS * HEAD_DIM
GM_GROUPS = 8
GM_GROUP_DIM = 64
GM_WIDTH = GM_GROUPS * GM_GROUP_DIM
GM_CHUNK = 128
N_EXPERTS = 256
TOP_K = 8
D_EXPERT = 256
D_SHARED = 256
ROUTE_SCALE = 2.5
EPS = 1e-6

LANES = 128
SEL_BIAS_WIDTH = 64
MASK_NEG = -1e30
SEL_NEG = -30000.0
SC_WINDOW = 128
MOE_PARTS = 2
MOE_GRAN = 128
MOE_ROWS = 9 * MOE_GRAN
VMEM_LIMIT = 56 * 1024 * 1024

F32 = jnp.float32
BF16 = jnp.bfloat16
HI = lax.Precision.HIGHEST


def _cparams(*sem):
    return pltpu.CompilerParams(dimension_semantics=sem, vmem_limit_bytes=VMEM_LIMIT)


def _dot(a, b, **kw):
    return jnp.dot(a, b, preferred_element_type=F32, **kw)


def _dot_nt(a, b, **kw):
    return lax.dot_general(a, b, (((1,), (1,)), ((), ())), preferred_element_type=F32, **kw)


def _gelu(x):
    return 0.5 * x * (1.0 + jnp.tanh(0.7978845608028654 * (x + 0.044715 * (x * x * x))))


def _sigmoid(x):
    return 1.0 / (1.0 + jnp.exp(-x))


def _silu(x):
    return x * _sigmoid(x)


_HI_MASK = np.uint32(0xFFFF0000)


def _pack_halves(a):
    w = a.shape[1] // 2
    bits = lax.bitcast_convert_type(a.astype(BF16).astype(F32), jnp.uint32)
    return (bits[:, w:] & _HI_MASK) | (bits[:, :w] >> 16)


def _unpack_halves(words):
    lo = lax.bitcast_convert_type(words << 16, F32)
    hi = lax.bitcast_convert_type(words & _HI_MASK, F32)
    return lo, hi


def _mod_kernel(c_ref, w_ref, b_ref, o_ref):
    c = c_ref[...]
    o_ref[0] = _dot(_silu(c), w_ref[0], precision=HI) + b_ref[0]


def _mod_call(c, w_mod, b_mod):
    L, D, N = w_mod.shape
    B = c.shape[0]
    tn = 1536
    return pl.pallas_call(
        _mod_kernel,
        grid=(L, N // tn),
        in_specs=[pl.BlockSpec((B, D), lambda l, j: (0, 0)),
                  pl.BlockSpec((1, D, tn), lambda l, j: (l, 0, j)),
                  pl.BlockSpec((1, 1, tn), lambda l, j: (l, 0, j))],
        out_specs=pl.BlockSpec((1, B, tn), lambda l, j: (l, 0, j)),
        out_shape=jax.ShapeDtypeStruct((L, B, N), F32),
        compiler_params=_cparams("arbitrary", "arbitrary"),
        name="mod",
    )(c, w_mod, b_mod.reshape(L, 1, N))


_C_Q = 0
_C_K = 512
_C_KC = 768
_C_VC = 896
_C_VS = 1024
_C_VW = 1152
_C_G = 1280
_C_U = 1536
_C_V = 2048
_C_GA = 2560
_C_GB = 3584
IN_COLS_P = 4608


def _dot_split(a, b):
    hi = a.astype(BF16)
    lo = (a - hi.astype(F32)).astype(BF16)
    return _dot(hi, b) + _dot(lo, b)


def _head_norm(z, bd):
    ms = _dot_split(z * z, bd)
    return z * lax.rsqrt(ms + EPS)


def _rope(z, cos, sin_lo, sin_hi):
    w = z.shape[-1]
    half = HEAD_DIM // 2
    return z * cos + pltpu.roll(z, w - half, 1) * sin_lo + pltpu.roll(z, half, 1) * sin_hi


def _tile_lanes(t, n):
    return t if n == 1 else jnp.concatenate([t] * n, axis=-1)


def _inproj_kernel(x_ref, sc_ref, sh_ref, w_ref, bdq_ref, bdk_ref, qg_ref, kg_ref,
                   cos_ref, sl_ref, shi_ref, lng_ref, lnb_ref,
                   qn_ref, qr_ref, ks_ref, kw_ref, vs_ref, vw_ref, kc_ref, vc_ref,
                   g_ref, u_ref, v_ref, ga_ref, gb_ref):
    tm = x_ref.shape[1]
    x = x_ref[0]
    ms = jnp.mean(x * x, axis=-1, keepdims=True)
    h = (x * lax.rsqrt(ms + EPS)) * (1.0 + sc_ref[0]) + sh_ref[0]
    hb = h.astype(BF16)

    def mm(lo, width):
        return _dot(hb, w_ref[:, lo:lo + width])

    cos, sl, shi = cos_ref[0], sl_ref[0], shi_ref[0]

    zq = mm(_C_Q, Q_WIDTH)
    qn = _head_norm(zq, bdq_ref[...]) * qg_ref[...]
    qr = _rope(qn, _tile_lanes(cos, 4), _tile_lanes(sl, 4), _tile_lanes(shi, 4))
    qn_ref[0] = qn.astype(BF16)
    qr_ref[0] = qr.astype(BF16)

    zk = mm(_C_K, 2 * KV_WIDTH)
    kn = _head_norm(zk, bdk_ref[...]) * kg_ref[...]
    kr = _rope(kn, _tile_lanes(cos, 2), _tile_lanes(sl, 2), _tile_lanes(shi, 2))
    lane = lax.broadcasted_iota(jnp.int32, (tm, LANES), 1)
    tok = pl.program_id(1) * tm + lax.broadcasted_iota(jnp.int32, (tm, LANES), 0)
    onehot = jnp.where(lane - HEAD_DIM == tok // SEL_LEN, 1.0, 0.0)
    ones_col = jnp.where(lane == HEAD_DIM, 1.0, 0.0)
    low = lane < HEAD_DIM
    zvs = mm(_C_VS, KV_WIDTH)
    zvw = mm(_C_VW, KV_WIDTH)
    zkc = mm(_C_KC, KV_WIDTH)
    zvc = mm(_C_VC, KV_WIDTH)
    for kv in range(NSA_KV_HEADS):
        def head(a):
            return a if kv == 0 else pltpu.roll(a, HEAD_DIM, 1)
        ks_ref[0, kv] = jnp.where(low, head(kr[:, :KV_WIDTH]), onehot).T.astype(BF16)
        kw_ref[0, kv] = jnp.where(low, head(kr[:, KV_WIDTH:]), 0.0).T.astype(BF16)
        vs_ref[0, kv] = jnp.where(low, head(zvs), ones_col).astype(BF16)
        vw_ref[0, kv] = jnp.where(low, head(zvw), ones_col).astype(BF16)
        kc_ref[0, kv] = head(zkc)[:, :HEAD_DIM]
        vc_ref[0, kv] = head(zvc)[:, :HEAD_DIM]

    zg = mm(_C_G, 2 * LANES)
    sg = _sigmoid(zg)
    g_ref[0, 0] = sg[:, :LANES]
    g_ref[0, 1] = sg[:, LANES:]

    u_ref[0] = _gelu(mm(_C_U, GM_WIDTH)).astype(BF16)
    gv = _gelu(mm(_C_V, GM_WIDTH))
    mu = jnp.mean(gv, axis=-1, keepdims=True)
    cen = gv - mu
    var = jnp.mean(cen * cen, axis=-1, keepdims=True)
    v_ref[0] = ((cen * lax.rsqrt(var + EPS)) * lng_ref[...] + lnb_ref[...]).astype(BF16)

    ga_ref[0] = _sigmoid(mm(_C_GA, D_MODEL)).astype(BF16)
    gb_ref[0] = _sigmoid(mm(_C_GB, D_MODEL)).astype(BF16)


def _block_diag_mean(width):
    idx = np.arange(width) // HEAD_DIM
    return jnp.asarray((idx[:, None] == idx[None, :]).astype(np.float32) / HEAD_DIM).astype(BF16)


def _inproj_call(x, sc, sh, w_p, qg, kg, cos, sl, shi, lng, lnb, tm):
    B, S, D = x.shape
    H = NSA_KV_HEADS
    full = lambda *shape: pl.BlockSpec(shape, lambda b, i: (0,) * len(shape))
    tok3 = lambda w: pl.BlockSpec((1, tm, w), lambda b, i: (b, i, 0))
    per_b = pl.BlockSpec((1, 1, D), lambda b, i: (b, 0, 0))
    kv4 = lambda w: pl.BlockSpec((1, H, tm, w), lambda b, i: (b, 0, i, 0))
    kt4 = pl.BlockSpec((1, H, LANES, tm), lambda b, i: (b, 0, 0, i))
    sds = jax.ShapeDtypeStruct
    out_shape = [
        sds((B, S, Q_WIDTH), BF16), sds((B, S, Q_WIDTH), BF16),
        sds((B, H, LANES, S), BF16), sds((B, H, LANES, S), BF16),
        sds((B, H, S, LANES), BF16), sds((B, H, S, LANES), BF16),
        sds((B, H, S, HEAD_DIM), F32), sds((B, H, S, HEAD_DIM), F32),
        sds((B, H, S, LANES), F32),
        sds((B, S, GM_WIDTH), BF16), sds((B, S, GM_WIDTH), BF16),
        sds((B, S, D), BF16), sds((B, S, D), BF16),
    ]
    out_specs = [
        tok3(Q_WIDTH), tok3(Q_WIDTH), kt4, kt4, kv4(LANES), kv4(LANES),
        kv4(HEAD_DIM), kv4(HEAD_DIM), kv4(LANES),
        tok3(GM_WIDTH), tok3(GM_WIDTH), tok3(D), tok3(D),
    ]
    return pl.pallas_call(
        _inproj_kernel,
        grid=(B, S // tm),
        in_specs=[tok3(D), per_b, per_b, full(D, IN_COLS_P),
                  full(Q_WIDTH, Q_WIDTH), full(2 * KV_WIDTH, 2 * KV_WIDTH),
                  full(1, Q_WIDTH), full(1, 2 * KV_WIDTH),
                  tok3(LANES), tok3(LANES), tok3(LANES),
                  full(1, GM_WIDTH), full(1, GM_WIDTH)],
        out_specs=out_specs,
        out_shape=out_shape,
        compiler_params=_cparams("arbitrary", "arbitrary"),
        name="in_proj",
    )(x, sc, sh, w_p, _block_diag_mean(Q_WIDTH), _block_diag_mean(2 * KV_WIDTH), qg, kg,
      cos, sl, shi, lng, lnb)


def _compress_kernel(kr_ref, vr_ref, w1k_ref, w2k_ref, pek_ref, w1v_ref, w2v_ref, pev_ref,
                     kg_ref, kc_ref, vc_ref):
    nc = kr_ref.shape[2]
    half = CMP_STRIDE * HEAD_DIM

    def mlp(raw, w1_ref, w2_ref, pe_ref):
        a = raw.astype(BF16)
        top = _dot(a, w1_ref[:half, :])
        bot = _dot(a, w1_ref[half:, :])
        pe = jnp.broadcast_to(pe_ref[...], (8, 2 * half)).astype(BF16)
        pe_row = _dot(pe, w1_ref[...])[0:1, :]
        hid = top + pltpu.roll(bot, nc - 1, 0) + pe_row
        return _dot(_gelu(hid).astype(BF16), w2_ref[...])

    kc = mlp(kr_ref[0, 0], w1k_ref, w2k_ref, pek_ref)
    ms = jnp.mean(kc * kc, axis=-1, keepdims=True)
    kc_ref[0, 0] = (kc * lax.rsqrt(ms + EPS) * kg_ref[...]).astype(BF16)
    vc_ref[0, 0] = mlp(vr_ref[0, 0], w1v_ref, w2v_ref, pev_ref).astype(BF16)


def _compress_call(kc_raw, vc_raw, w1k, w2k, pek, w1v, w2v, pev, kg):
    B, H, S, hd = kc_raw.shape
    nc = S // CMP_STRIDE
    feat = CMP_STRIDE * hd
    raw = pl.BlockSpec((1, 1, nc, feat), lambda b, h: (b, h, 0, 0))
    full = lambda *shape: pl.BlockSpec(shape, lambda b, h: (0,) * len(shape))
    out = pl.BlockSpec((1, 1, nc, hd), lambda b, h: (b, h, 0, 0))
    return pl.pallas_call(
        _compress_kernel,
        grid=(B, H),
        in_specs=[raw, raw, full(2 * feat, CMP_HIDDEN), full(CMP_HIDDEN, hd), full(1, 2 * feat),
                  full(2 * feat, CMP_HIDDEN), full(CMP_HIDDEN, hd), full(1, 2 * feat),
                  full(1, hd)],
        out_specs=[out, out],
        out_shape=[jax.ShapeDtypeStruct((B, H, nc, hd), BF16)] * 2,
        compiler_params=_cparams("arbitrary", "arbitrary"),
        name="compress",
    )(kc_raw.reshape(B, H, nc, feat), vc_raw.reshape(B, H, nc, feat),
      w1k, w2k, pek, w1v, w2v, pev, kg)


def _group_rows(a):
    return jnp.concatenate([a[:, g * HEAD_DIM:(g + 1) * HEAD_DIM] for g in range(NSA_GROUP)], axis=0)


def _nsa_cmp_kernel(qn_ref, qr_ref, kc_ref, vc_ref, ovl_ref, oc_ref, qa_ref, *, n_sel):
    tq = qn_ref.shape[1]
    nc = kc_ref.shape[2]
    G = NSA_GROUP
    q0 = pl.program_id(2) * tq
    q4 = _group_rows(qn_ref[0])
    s = _dot_nt(q4, kc_ref[0, 0])
    row = lax.broadcasted_iota(jnp.int32, (G, tq, nc), 1).reshape(G * tq, nc)
    col = lax.broadcasted_iota(jnp.int32, (G * tq, nc), 1)
    vis = col * CMP_STRIDE + (CMP_LEN - 1) <= q0 + row
    s = jnp.where(vis, s, MASK_NEG)
    m = jnp.max(s, axis=-1, keepdims=True)
    e = jnp.where(vis, jnp.exp(s - m), 0.0)
    p = e / jnp.maximum(jnp.sum(e, axis=-1, keepdims=True), 1e-30)
    oc = _dot(p.astype(BF16), vc_ref[0, 0])
    oc_ref[0, 0] = oc.reshape(G, tq, HEAD_DIM)

    psum = p[0:tq] + p[tq:2 * tq] + p[2 * tq:3 * tq] + p[3 * tq:4 * tq]
    p_hi = psum.astype(BF16)
    p_lo = (psum - p_hi.astype(F32)).astype(BF16)
    imp = (_dot_nt(ovl_ref[...], p_hi) + _dot_nt(ovl_ref[...], p_lo))[:SEL_BIAS_WIDTH]
    blk = lax.broadcasted_iota(jnp.int32, (SEL_BIAS_WIDTH, tq), 0)
    cur = (q0 + lax.broadcasted_iota(jnp.int32, (SEL_BIAS_WIDTH, tq), 1)) // SEL_LEN
    valid = blk <= cur
    forced = (blk == 0) | (blk == cur) | (blk == cur - 1)
    cand = valid & jnp.logical_not(forced)
    n_forced = jnp.minimum(cur, 2) + 1
    val = jnp.where(cand, imp, -1.0)
    cnt = jnp.zeros((SEL_BIAS_WIDTH, tq), F32)
    for j in range(n_sel):
        vj = jnp.broadcast_to(val[j:j + 1, :], (SEL_BIAS_WIDTH, tq))
        cnt = cnt + jnp.where(blk > j, jnp.where(vj >= val, 1.0, 0.0), jnp.where(vj > val, 1.0, 0.0))
    free = (min(SEL_TOPN, n_sel) - n_forced).astype(F32)
    sel = (forced & valid) | (cand & (cnt < free))
    sel_t = jnp.concatenate([jnp.where(sel, 1.0, 0.0), jnp.zeros((LANES - SEL_BIAS_WIDTH, tq), F32)], axis=0)
    sel_q = sel_t.T
    bias = jnp.where(sel_q > 0.5, 0.0, SEL_NEG)
    lane = lax.broadcasted_iota(jnp.int32, (tq, LANES), 1)
    bias_hi = pltpu.roll(bias, SEL_BIAS_WIDTH, 1)
    qr = qr_ref[0]
    for g in range(G):
        qg = qr[:, g * HEAD_DIM:(g + 1) * HEAD_DIM].astype(F32)
        qg = jnp.concatenate([qg, qg], axis=-1)
        qa_ref[0, 0, g] = jnp.where(lane < HEAD_DIM, qg, bias_hi).astype(BF16)


def _overlap_t(S):
    n_cmp = (S - CMP_LEN) // CMP_STRIDE + 1
    nc = S // CMP_STRIDE
    n_sel = S // SEL_LEN
    start = np.arange(nc) * CMP_STRIDE
    end = start + CMP_LEN - 1
    sel_start = np.arange(n_sel) * SEL_LEN
    ov = (start[None, :] <= sel_start[:, None] + SEL_LEN - 1) & (end[None, :] >= sel_start[:, None])
    ov = ov & (np.arange(nc) < n_cmp)[None, :]
    out = np.zeros((LANES, nc), np.float32)
    out[:n_sel] = ov.astype(np.float32)
    return jnp.asarray(out).astype(BF16)


def _nsa_cmp_call(qn, qr, kc, vc, tq):
    B, S, _ = qn.shape
    H, G = NSA_KV_HEADS, NSA_GROUP
    nc = kc.shape[2]
    n_sel = S // SEL_LEN
    assert n_sel <= SEL_BIAS_WIDTH and nc % LANES == 0
    qspec = pl.BlockSpec((1, tq, G * HEAD_DIM), lambda b, h, i: (b, i, h))
    cspec = pl.BlockSpec((1, 1, nc, HEAD_DIM), lambda b, h, i: (b, h, 0, 0))
    return pl.pallas_call(
        functools.partial(_nsa_cmp_kernel, n_sel=n_sel),
        grid=(B, H, S // tq),
        in_specs=[qspec, qspec, cspec, cspec, pl.BlockSpec((LANES, nc), lambda b, h, i: (0, 0))],
        out_specs=[pl.BlockSpec((1, 1, G, tq, HEAD_DIM), lambda b, h, i: (b, h, 0, i, 0)),
                   pl.BlockSpec((1, 1, G, tq, LANES), lambda b, h, i: (b, h, 0, i, 0))],
        out_shape=[jax.ShapeDtypeStruct((B, H, G, S, HEAD_DIM), F32),
                   jax.ShapeDtypeStruct((B, H, G, S, LANES), BF16)],
        compiler_params=_cparams("arbitrary", "arbitrary", "arbitrary"),
        name="nsa_cmp",
    )(qn, qr, kc, vc, _overlap_t(S))


SEL_CHUNK = 4


def _nsa_flash_kernel(qa_ref, ks_ref, vs_ref, kw_ref, vw_ref, oc_ref, g_ref, o_ref, m_scr, acc_scr):
    G = NSA_GROUP
    tq = qa_ref.shape[3]
    R = G * tq
    tk = tq
    i = pl.program_id(2)
    q0 = i * tq

    def rows_of(ref, j0, nt):
        return ref[0, 0, pl.ds(pl.multiple_of(j0 * tk, tk), nt * tk), :]

    def attend(k_ref, v_ref, j0, nt, visible, state):
        v = rows_of(v_ref, j0, nt)
        kt = k_ref[0, 0, :, pl.ds(pl.multiple_of(j0 * tk, tk), nt * tk)]
        s_all = _dot(qa_ref[0, 0].reshape(R, LANES), kt)
        ss = [s_all[g * tq:(g + 1) * tq] for g in range(G)]
        if visible is not None:
            qpos = q0 + lax.broadcasted_iota(jnp.int32, (tq, nt * tk), 0)
            kpos = j0 * tk + lax.broadcasted_iota(jnp.int32, (tq, nt * tk), 1)
            mask = visible(qpos, kpos)
        ps, alphas = [], []
        for g in range(G):
            s = ss[g] if visible is None else jnp.where(mask, ss[g], MASK_NEG)
            cols = [s[:, c * LANES:(c + 1) * LANES] for c in range(nt * tk // LANES)]
            m_new = jnp.max(functools.reduce(jnp.maximum, cols), axis=-1, keepdims=True)
            if state:
                m_prev = m_scr[g]
                m_new = jnp.maximum(m_prev, m_new)
                alphas.append(jnp.exp(m_prev - m_new))
                m_scr[g] = m_new
            ps.append(jnp.concatenate([jnp.exp(c - m_new) for c in cols], axis=-1).astype(BF16))
        if not state:
            return [_dot(ps[g], v) for g in range(G)]
        for g in range(G):
            acc_scr[g] = alphas[g] * acc_scr[g] + _dot(ps[g], v)

    def normalise(acc):
        return acc[:, :HEAD_DIM] / acc[:, HEAD_DIM:HEAD_DIM + 1]

    m_scr[...] = jnp.full((G, tq, LANES), MASK_NEG, F32)
    acc_scr[...] = jnp.zeros((G, tq, LANES), F32)
    causal = lambda qpos, kpos: kpos <= qpos

    def sel_body(c, carry):
        attend(ks_ref, vs_ref, c * SEL_CHUNK, SEL_CHUNK, None, True)
        return carry

    n_full = i // SEL_CHUNK
    lax.fori_loop(0, n_full, sel_body, 0)
    attend(ks_ref, vs_ref, n_full * SEL_CHUNK, 2, causal, True)

    @pl.when(i % SEL_CHUNK >= 2)
    def _():
        attend(ks_ref, vs_ref, n_full * SEL_CHUNK + 2, 2, causal, True)

    o_sel = normalise(acc_scr[...].reshape(R, LANES))

    assert WINDOW == 2 * tk
    band = lambda qpos, kpos: (kpos <= qpos) & (kpos > qpos - WINDOW)
    o_win = normalise(jnp.concatenate(attend(kw_ref, vw_ref, jnp.maximum(i - 2, 0), 3, band, False), axis=0))

    o_cmp = oc_ref[0, 0].reshape(R, HEAD_DIM)
    gates = g_ref[0, 0]
    outs = []
    for g in range(G):
        r = slice(g * tq, (g + 1) * tq)
        outs.append(gates[:, 3 * g:3 * g + 1] * o_cmp[r]
                    + gates[:, 3 * g + 1:3 * g + 2] * o_sel[r]
                    + gates[:, 3 * g + 2:3 * g + 3] * o_win[r])
    o_ref[0] = jnp.concatenate(outs, axis=-1).astype(BF16)


def _nsa_flash_call(qa, ks, vs, kw, vw, oc, gates):
    B, H, G, S, _ = qa.shape
    tq = WINDOW // 2
    assert (S // tq) % SEL_CHUNK == 0
    kt = pl.BlockSpec((1, 1, LANES, S), lambda b, h, i: (b, h, 0, 0))
    vv = pl.BlockSpec((1, 1, S, LANES), lambda b, h, i: (b, h, 0, 0))
    return pl.pallas_call(
        _nsa_flash_kernel,
        grid=(B, H, S // tq),
        in_specs=[pl.BlockSpec((1, 1, G, tq, LANES), lambda b, h, i: (b, h, 0, i, 0)),
                  kt, vv, kt, vv,
                  pl.BlockSpec((1, 1, G, tq, HEAD_DIM), lambda b, h, i: (b, h, 0, i, 0)),
                  pl.BlockSpec((1, 1, tq, LANES), lambda b, h, i: (b, h, i, 0))],
        out_specs=pl.BlockSpec((1, tq, G * HEAD_DIM), lambda b, h, i: (b, i, h)),
        out_shape=jax.ShapeDtypeStruct((B, S, Q_WIDTH), BF16),
        scratch_shapes=[pltpu.VMEM((G, tq, LANES), F32), pltpu.VMEM((G, tq, LANES), F32)],
        compiler_params=_cparams("arbitrary", "arbitrary", "arbitrary"),
        name="nsa_flash",
    )(qa, ks, vs, kw, vw, oc, gates)


def _gmlp_kernel(u_ref, v_ref, ws_ref, bs_ref, o_ref):
    tm = u_ref.shape[1]
    C = GM_CHUNK
    r = lax.broadcasted_iota(jnp.int32, (C, C), 0)
    c = lax.broadcasted_iota(jnp.int32, (C, C), 1)
    causal = c <= r
    ws = [jnp.where(causal, ws_ref[g], 0.0).astype(BF16) for g in range(GM_GROUPS)]
    for n in range(tm // C):
        rows = slice(n * C, (n + 1) * C)
        vn = v_ref[0, rows, :]
        mixed = jnp.concatenate(
            [_dot(ws[g], vn[:, g * GM_GROUP_DIM:(g + 1) * GM_GROUP_DIM]) for g in range(GM_GROUPS)],
            axis=-1)
        o_ref[0, rows, :] = (u_ref[0, rows, :].astype(F32) * (mixed + bs_ref[...])).astype(BF16)


def _gmlp_call(u, v, ws, bs_full, tm):
    B, S, W = u.shape
    tok = pl.BlockSpec((1, tm, W), lambda b, i: (b, i, 0))
    return pl.pallas_call(
        _gmlp_kernel,
        grid=(B, S // tm),
        in_specs=[tok, tok,
                  pl.BlockSpec((GM_GROUPS, GM_CHUNK, GM_CHUNK), lambda b, i: (0, 0, 0)),
                  pl.BlockSpec((GM_CHUNK, W), lambda b, i: (0, 0))],
        out_specs=tok,
        out_shape=jax.ShapeDtypeStruct((B, S, W), BF16),
        compiler_params=_cparams("arbitrary", "arbitrary"),
        name="gmlp",
    )(u, v, ws, bs_full)


def _mixout_kernel(on_ref, og_ref, ga_ref, gb_ref, x_ref, gt_ref, sc_ref, sh_ref,
                   wpn_ref, wpg_ref, wo_ref, wr_ref, wgs_ref, wus_ref, wds_ref,
                   x1_ref, h2_ref, lg_ref, shd_ref):
    ya = _dot(on_ref[0], wpn_ref[...])
    yb = _dot(og_ref[0], wpg_ref[...])
    merged = ga_ref[0].astype(F32) * ya + gb_ref[0].astype(F32) * yb
    x1 = x_ref[0] + gt_ref[0] * _dot(merged.astype(BF16), wo_ref[...])
    x1_ref[0] = x1
    ms = jnp.mean(x1 * x1, axis=-1, keepdims=True)
    h2 = (x1 * lax.rsqrt(ms + EPS)) * (1.0 + sc_ref[0]) + sh_ref[0]
    hb = h2.astype(BF16)
    lg_ref[0] = _dot_nt(wr_ref[...], hb)
    h2_ref[0] = _pack_halves(hb)
    act = _silu(_dot(hb, wgs_ref[...])) * _dot(hb, wus_ref[...])
    shd_ref[0] = _dot(act.astype(BF16), wds_ref[...])


def _mixout_call(o_nsa, o_gm, ga, gb, x, gt, sc, sh, wpn, wpg, wo, wr, wgs, wus, wds, tm):
    B, S, D = x.shape
    tok = lambda w: pl.BlockSpec((1, tm, w), lambda b, i: (b, i, 0))
    per_b = pl.BlockSpec((1, 1, D), lambda b, i: (b, 0, 0))
    full = lambda a: pl.BlockSpec(a.shape, lambda b, i: (0,) * a.ndim)
    sds = jax.ShapeDtypeStruct
    return pl.pallas_call(
        _mixout_kernel,
        grid=(B, S // tm),
        in_specs=[tok(Q_WIDTH), tok(GM_WIDTH), tok(D), tok(D), tok(D), per_b, per_b, per_b,
                  full(wpn), full(wpg), full(wo), full(wr), full(wgs), full(wus), full(wds)],
        out_specs=[tok(D), tok(D // 2), pl.BlockSpec((1, N_EXPERTS, tm), lambda b, i: (b, 0, i)), tok(D)],
        out_shape=[sds((B, S, D), F32), sds((B, S, D // 2), jnp.uint32), sds((B, N_EXPERTS, S), F32),
                   sds((B, S, D), F32)],
        compiler_params=_cparams("arbitrary", "arbitrary"),
        name="mix_out",
    )(o_nsa, o_gm, ga, gb, x, gt, sc, sh, wpn, wpg, wo, wr, wgs, wus, wds)


def _route_kernel(lg_ref, br_ref, idx_ref, wt_ref, rank_ref, cnt_ref, run_scr):
    E, tm = lg_ref.shape[1], lg_ref.shape[2]

    @pl.when(pl.program_id(0) == 0)
    def _():
        run_scr[...] = jnp.zeros_like(run_scr)

    aff = _sigmoid(lg_ref[0])
    work = aff + br_ref[...]
    row = lax.broadcasted_iota(jnp.int32, (E, tm), 0).astype(F32)
    picked = jnp.zeros((E, tm), F32)
    idxs, tops = [], []
    for _ in range(TOP_K):
        m = jnp.max(work, axis=0, keepdims=True)
        idx = jnp.min(jnp.where(work == m, row, float(E)), axis=0, keepdims=True)
        hit = row == idx
        tops.append(jnp.sum(jnp.where(hit, aff, 0.0), axis=0, keepdims=True))
        idxs.append(idx)
        picked = jnp.where(hit, 1.0, picked)
        work = jnp.where(hit, -jnp.inf, work)
    total = functools.reduce(jnp.add, tops)
    r = lax.broadcasted_iota(jnp.int32, (tm, tm), 0)
    c = lax.broadcasted_iota(jnp.int32, (tm, tm), 1)
    before = _dot(picked.astype(BF16), jnp.where(r < c, 1.0, 0.0).astype(BF16)) + run_scr[...]
    ranks = [jnp.sum(jnp.where(row == idx, before, 0.0), axis=0, keepdims=True) for idx in idxs]
    run_scr[...] = run_scr[...] + jnp.sum(picked, axis=1, keepdims=True)
    cnt_ref[...] = run_scr[...]
    idx_ref[...] = jnp.concatenate(idxs, axis=0).astype(jnp.int32)
    rank_ref[...] = jnp.concatenate(ranks, axis=0).astype(jnp.int32)
    wt = jnp.concatenate([t / total * ROUTE_SCALE for t in tops]
                         + [jnp.zeros((LANES - TOP_K, tm), F32)], axis=0)
    eye = jnp.where(r == c, 1.0, 0.0).astype(BF16)
    cols = jnp.zeros((tm, LANES), F32)
    rest = wt
    for _ in range(3):
        part = rest.astype(BF16)
        cols = cols + _dot_nt(eye, part)
        rest = rest - part.astype(F32)
    wt_ref[...] = cols


def _route_call(logits_t, b_router, tm):
    B, E, S = logits_t.shape
    per_b = S // tm
    T = B * S
    kt = pl.BlockSpec((TOP_K, tm), lambda i: (0, i))
    col = pl.BlockSpec((E, 1), lambda i: (0, 0))
    sds = jax.ShapeDtypeStruct
    return pl.pallas_call(
        _route_kernel,
        grid=(T // tm,),
        in_specs=[pl.BlockSpec((1, E, tm), lambda i: (i // per_b, 0, i % per_b)), col],
        out_specs=[kt, pl.BlockSpec((tm, LANES), lambda i: (i, 0)), kt, col],
        out_shape=[sds((TOP_K, T), jnp.int32), sds((T, LANES), F32), sds((TOP_K, T), jnp.int32),
                   sds((E, 1), F32)],
        scratch_shapes=[pltpu.VMEM((E, 1), F32)],
        compiler_params=_cparams("arbitrary"),
        name="route",
    )(logits_t, b_router.reshape(E, 1))


def _slot_kernel(idx_ref, rank_ref, ps_ref, o_ref):
    tm = idx_ref.shape[1]
    E = ps_ref.shape[0]
    row = lax.broadcasted_iota(jnp.int32, (E, tm), 0)
    idx, rank, ps = idx_ref[...], rank_ref[...], ps_ref[...]
    base = [jnp.sum(jnp.where(row == idx[k:k + 1, :], ps, 0.0), axis=0, keepdims=True) for k in range(TOP_K)]
    o_ref[...] = jnp.concatenate(base, axis=0).astype(jnp.int32) + rank


def _slot_call(idx, rank, pad_start, tm):
    T = idx.shape[1]
    E = pad_start.shape[0]
    kt = pl.BlockSpec((TOP_K, tm), lambda i: (0, i))
    return pl.pallas_call(
        _slot_kernel,
        grid=(T // tm,),
        in_specs=[kt, kt, pl.BlockSpec((E, 1), lambda i: (0, 0))],
        out_specs=kt,
        out_shape=jax.ShapeDtypeStruct((TOP_K, T), jnp.int32),
        compiler_params=_cparams("arbitrary"),
        name="slot",
    )(idx, rank, pad_start.astype(F32).reshape(E, 1))


def _sc_mesh():
    mesh = plsc.VectorSubcoreMesh(core_axis_name="core", subcore_axis_name="subcore")
    return mesh, mesh.num_cores * mesh.num_subcores


def _sc_worker(mesh):
    return lax.axis_index("subcore") * mesh.num_cores + lax.axis_index("core")


def _scatter_rows(rows, indices_by_k, out_ref):
    n, width = rows.shape
    mesh, workers = _sc_mesh()
    per_worker = n // workers
    assert per_worker * workers == n and per_worker % SC_WINDOW == 0

    @pl.kernel(out_type=(), mesh=mesh, name="scatter_rows",
               scratch_types=[pltpu.VMEM((SC_WINDOW,), jnp.int32), pltpu.VMEM((SC_WINDOW, width), rows.dtype)])
    def scatter(rows_hbm, idx_hbm, out_hbm, idx_v, rows_v):
        worker = _sc_worker(mesh)

        @pl.loop(0, per_worker // SC_WINDOW)
        def _(j):
            base = pl.multiple_of(worker * per_worker + j * SC_WINDOW, SC_WINDOW)
            pltpu.sync_copy(rows_hbm.at[pl.ds(base, SC_WINDOW)], rows_v)
            for k in range(TOP_K):
                pltpu.sync_copy(idx_hbm.at[pl.ds(pl.multiple_of(k * n + base, SC_WINDOW), SC_WINDOW)], idx_v)
                pltpu.sync_copy(rows_v, out_hbm.at[idx_v])

    scatter(rows, indices_by_k, out_ref)


def _expert_kernel(ie_ref, ir_ref, ig_ref, n_ref, x_hbm, wg_ref, wu_ref, wd_ref, y_hbm,
                   xbuf, ybuf, act_scr, xsem, ysem):
    w = pl.program_id(0)
    n = n_ref[0]
    R = xbuf.shape[1]
    last = ie_ref.shape[0] - 1

    def x_copy(item, slot):
        row = pl.multiple_of(ir_ref[jnp.minimum(item, last)], MOE_GRAN)
        return pltpu.make_async_copy(x_hbm.at[pl.ds(row, R), :], xbuf.at[slot], xsem.at[slot])

    def y_copy(item, slot, gi):
        row = pl.multiple_of(ir_ref[jnp.minimum(item, last)] + gi * MOE_GRAN, MOE_GRAN)
        src = ybuf.at[slot, pl.ds(pl.multiple_of(gi * MOE_GRAN, MOE_GRAN), MOE_GRAN), :]
        return pltpu.make_async_copy(src, y_hbm.at[pl.ds(row, MOE_GRAN), :], ysem.at[slot])

    def for_granules(item, fn):
        def body(gi, carry):
            fn(gi)
            return carry
        lax.fori_loop(0, ig_ref[jnp.clip(item, 0, last)], body, 0)

    def gate_up(slot):
        lo, hi = _unpack_halves(xbuf[slot])
        x = jnp.concatenate([lo, hi], axis=1).astype(BF16)
        a = _dot(x, wg_ref[0, 0].astype(BF16))
        b = _dot(x, wu_ref[0, 0].astype(BF16))
        return (_silu(a) * b).astype(BF16)

    def down(act):
        return _pack_halves(_dot(act, wd_ref[0, 0].astype(BF16)))

    nx = xbuf.shape[0]
    cur, prv = w % nx, (w + 1) % 2

    @pl.when(w == 0)
    def _():
        for j in range(nx - 1):
            @pl.when(j < n)
            def _():
                x_copy(j, j).start()

    @pl.when(w < n)
    def _():
        x_copy(w, cur).wait()

    @pl.when(w + nx - 1 < n)
    def _():
        x_copy(w + nx - 1, (w + nx - 1) % nx).start()

    @pl.when((w >= 3) & (w - 3 < n))
    def _():
        for_granules(w - 3, lambda gi: y_copy(w - 3, prv, gi).wait())

    @pl.when(w == 0)
    def _():
        act_scr[...] = gate_up(0)

    @pl.when((w >= 1) & (w < n))
    def _():
        prev = act_scr[...]
        ybuf[prv] = down(prev)
        act_scr[...] = gate_up(cur)

    @pl.when((w >= 1) & (w == n))
    def _():
        ybuf[prv] = down(act_scr[...])

    @pl.when((w >= 1) & (w <= n))
    def _():
        for_granules(w - 1, lambda gi: y_copy(w - 1, prv, gi).start())


def _expert_call(layer, item_e, item_row, item_ng, n_items, xs, wg, wu, wd):
    n_alloc, W = xs.shape
    R = MOE_ROWS
    nw = item_e.shape[0]
    D, F = wg.shape[2], wg.shape[3]
    cur = lambda w, ie, ir, ig, n: (layer, ie[jnp.minimum(w, nw - 1)], 0, 0)
    prev = lambda w, ie, ir, ig, n: (layer, ie[jnp.clip(w - 1, 0, nw - 1)], 0, 0)
    grid_spec = pltpu.PrefetchScalarGridSpec(
        num_scalar_prefetch=4,
        grid=(nw + 3,),
        in_specs=[pl.BlockSpec(memory_space=pl.ANY),
                  pl.BlockSpec((1, 1, D, F), cur), pl.BlockSpec((1, 1, D, F), cur),
                  pl.BlockSpec((1, 1, F, D), prev)],
        out_specs=pl.BlockSpec(memory_space=pl.ANY),
        scratch_shapes=[pltpu.VMEM((3, R, W), jnp.uint32), pltpu.VMEM((2, R, W), jnp.uint32),
                        pltpu.VMEM((R, F), BF16),
                        pltpu.SemaphoreType.DMA((3,)), pltpu.SemaphoreType.DMA((2,))],
    )
    return pl.pallas_call(
        _expert_kernel,
        grid_spec=grid_spec,
        out_shape=jax.ShapeDtypeStruct((n_alloc, W), jnp.uint32),
        input_output_aliases={4: 0},
        compiler_params=pltpu.CompilerParams(dimension_semantics=("arbitrary",), vmem_limit_bytes=VMEM_LIMIT,
                                             has_side_effects=True),
        name="expert",
    )(item_e, item_row, item_ng, n_items, xs, wg, wu, wd)


def _gather_rows(table, indices):
    n = indices.shape[0]
    width = table.shape[1]
    mesh, workers = _sc_mesh()
    per_worker = n // workers
    assert per_worker * workers == n and per_worker % SC_WINDOW == 0

    half = SC_WINDOW // 2
    steps = per_worker // SC_WINDOW
    rows_t = pltpu.VMEM((half, width), table.dtype)

    @pl.kernel(out_type=jax.ShapeDtypeStruct((n, width), table.dtype), mesh=mesh, name="gather_rows",
               scratch_types=[pltpu.VMEM((SC_WINDOW,), jnp.int32), rows_t, rows_t] + [pltpu.SemaphoreType.DMA] * 4)
    def gather(table_hbm, idx_hbm, out_hbm, idx_v, rows_a, rows_b, gsem_a, gsem_b, wsem_a, wsem_b):
        worker = _sc_worker(mesh)
        first = worker * per_worker
        halves = ((rows_a, gsem_a, wsem_a, 0), (rows_b, gsem_b, wsem_b, half))

        def write(buf, wsem, row):
            return pltpu.make_async_copy(buf, out_hbm.at[pl.ds(pl.multiple_of(row, half), half)], wsem)

        @pl.loop(0, steps)
        def _(j):
            base = pl.multiple_of(first + j * SC_WINDOW, SC_WINDOW)
            pltpu.sync_copy(idx_hbm.at[pl.ds(base, SC_WINDOW)], idx_v)
            gathers = []
            for buf, gsem, wsem, off in halves:
                @pl.when(j > 0)
                def _():
                    write(buf, wsem, base).wait()
                gathers.append(pltpu.async_copy(table_hbm.at[idx_v.at[pl.ds(off, half)]], buf, gsem))
            for (buf, gsem, wsem, off), g in zip(halves, gathers):
                g.wait()
                write(buf, wsem, base + off).start()

        for buf, gsem, wsem, off in halves:
            write(buf, wsem, first).wait()

    return gather(table, indices)


def _combine_kernel(y_ref, wt_ref, shd_ref, x_ref, gt_ref, o_ref):
    wt = wt_ref[...]
    acc_lo = acc_hi = None
    for k in range(TOP_K):
        lo, hi = _unpack_halves(y_ref[k])
        w = wt[:, k:k + 1]
        acc_lo = w * lo if k == 0 else acc_lo + w * lo
        acc_hi = w * hi if k == 0 else acc_hi + w * hi
    routed = jnp.concatenate([acc_lo, acc_hi], axis=1)
    o_ref[...] = x_ref[...] + gt_ref[0] * (routed + shd_ref[...])


def _combine_part_kernel(y_ref, wt_ref, shd_ref, x_ref, gt_ref, prev_ref, o_ref):
    del prev_ref
    _combine_kernel(y_ref, wt_ref, shd_ref, x_ref, gt_ref, o_ref)


def _combine_call(yk, wts, shared, x1, gt, tm, part, out_prev):
    T, D = x1.shape
    B = gt.shape[0]
    per_b = T // B // tm
    steps = yk.shape[1] // tm
    first = part * steps
    tok = lambda w: pl.BlockSpec((tm, w), lambda i: (first + i, 0))
    in_specs = [pl.BlockSpec((TOP_K, tm, yk.shape[2]), lambda i: (0, i, 0)),
                tok(LANES), tok(D), tok(D),
                pl.BlockSpec((1, 1, D), lambda i: ((first + i) // per_b, 0, 0))]
    args = [yk, wts, shared, x1, gt]
    if out_prev is not None:
        in_specs.append(pl.BlockSpec(memory_space=pl.ANY))
        args.append(out_prev)
    return pl.pallas_call(
        _combine_kernel if out_prev is None else _combine_part_kernel,
        grid=(steps,),
        in_specs=in_specs,
        out_specs=tok(D),
        out_shape=jax.ShapeDtypeStruct((T, D), F32),
        input_output_aliases={} if out_prev is None else {5: 0},
        compiler_params=_cparams("arbitrary"),
        name="combine",
    )(*args)


def _rope_tables(positions):
    half = HEAD_DIM // 2
    inv = ROPE_THETA ** (-jnp.arange(half, dtype=F32) / half)
    ang = positions.astype(F32)[..., None] * inv
    cos, sin, zero = jnp.cos(ang), jnp.sin(ang), jnp.zeros_like(ang)
    cos_t = jnp.concatenate([cos, cos] * 2, axis=-1)
    sin_lo = jnp.concatenate([-sin, zero] * 2, axis=-1)
    sin_hi = jnp.concatenate([zero, sin] * 2, axis=-1)
    return cos_t, sin_lo, sin_hi


def _reorder_w_in(w):
    o = np.cumsum([0, Q_WIDTH] + [KV_WIDTH] * 6 + [NSA_Q_HEADS * 3, GM_WIDTH, GM_WIDTH, D_MODEL, D_MODEL])
    q, kc, vc, ks, vs, kw, vw, g, u, v, ga, gb = [w[:, o[i]:o[i + 1]] for i in range(12)]
    per = NSA_GROUP * 3
    pad = jnp.zeros((w.shape[0], LANES - per), w.dtype)
    return jnp.concatenate([q, ks, kw, kc, vc, vs, vw, g[:, :per], pad, g[:, per:], pad, u, v, ga, gb], axis=1)


def _owner(ends, pos):
    return jnp.minimum(jnp.sum((ends[None, :] <= pos[:, None]).astype(jnp.int32), axis=1), ends.shape[0] - 1)


def _lookup(table, idx):
    hit = idx[:, None] == jnp.arange(table.shape[0], dtype=jnp.int32)[None, :]
    return jnp.sum(jnp.where(hit, table[None, :], 0), axis=1)


def _expert_plan(counts, n_items_max):
    per_item = MOE_ROWS // MOE_GRAN
    counts = counts.astype(jnp.int32)
    gran = (counts + MOE_GRAN - 1) // MOE_GRAN
    gran_start = jnp.cumsum(gran) - gran
    items = (gran + per_item - 1) // per_item
    item_end = jnp.cumsum(items)
    w = jnp.arange(n_items_max, dtype=jnp.int32)
    ie = _owner(item_end, w)
    part = w - (_lookup(item_end, ie) - _lookup(items, ie))
    live = w < item_end[-1]
    item_ng = jnp.where(live, jnp.clip(_lookup(gran, ie) - part * per_item, 0, per_item), 0).astype(jnp.int32)
    item_row = jnp.where(live, (_lookup(gran_start, ie) + part * per_item) * MOE_GRAN, 0).astype(jnp.int32)
    return gran_start * MOE_GRAN, ie.astype(jnp.int32), item_row, item_ng, item_end[-1:].astype(jnp.int32)


def kernel(x, c, positions, w_mod, b_mod, w_in, q_gain, k_gain, cmp_pos_k, cmp_pos_v, cmp_w1_k, cmp_w2_k, cmp_w1_v, cmp_w2_v, gm_ln_g, gm_ln_b, gm_ws, gm_bs, w_proj_nsa, w_proj_gm, w_out, w_router, b_router, w_gate_e, w_up_e, w_down_e, w_gate_sh, w_up_sh, w_down_sh):
    B, S, D = x.shape
    L = w_mod.shape[0]
    T = B * S
    tm = 256
    tm_dense = 512
    scale = HEAD_DIM ** -0.5
    cos_t, sin_lo, sin_hi = _rope_tables(positions)
    mod = _mod_call(c, w_mod, b_mod)
    n_alloc = T * TOP_K + N_EXPERTS * MOE_GRAN + MOE_ROWS
    n_items_max = N_EXPERTS + T * TOP_K // MOE_ROWS + 1
    rows_buf = jnp.zeros((n_alloc, D // 2), jnp.uint32)

    for l in range(L):
        sh_a, sc_a, gt_a, sh_f, sc_f, gt_f = [mod[l, :, i * D:(i + 1) * D].reshape(B, 1, D) for i in range(6)]
        qg = (jnp.tile(q_gain[l], NSA_Q_HEADS) * scale).reshape(1, Q_WIDTH)
        kg = jnp.tile(k_gain[l], 2 * NSA_KV_HEADS).reshape(1, 2 * KV_WIDTH)
        (qn, qr, ks, kw, vs, vw, kc_raw, vc_raw, gates, u, v, ga, gb) = _inproj_call(
            x, sc_a, sh_a, _reorder_w_in(w_in[l]).astype(BF16), qg, kg, cos_t, sin_lo, sin_hi,
            gm_ln_g[l].reshape(1, GM_WIDTH), gm_ln_b[l].reshape(1, GM_WIDTH), tm_dense)
        kc, vc = _compress_call(
            kc_raw, vc_raw, cmp_w1_k[l].astype(BF16), cmp_w2_k[l].astype(BF16), cmp_pos_k[l].reshape(1, -1),
            cmp_w1_v[l].astype(BF16), cmp_w2_v[l].astype(BF16), cmp_pos_v[l].reshape(1, -1),
            k_gain[l].reshape(1, HEAD_DIM))
        o_cmp, q_aug = _nsa_cmp_call(qn, qr, kc, vc, WINDOW // 2)
        o_nsa = _nsa_flash_call(q_aug, ks, vs, kw, vw, o_cmp, gates)
        bs_full = jnp.repeat(gm_bs[l].T, GM_GROUP_DIM, axis=1)
        o_gm = _gmlp_call(u, v, gm_ws[l], bs_full, 512)
        x1, h2, logits, shared = _mixout_call(
            o_nsa, o_gm, ga, gb, x, gt_a, sc_f, sh_f,
            w_proj_nsa[l].astype(BF16), w_proj_gm[l].astype(BF16), w_out[l].astype(BF16), w_router[l].T.astype(BF16),
            w_gate_sh[l].astype(BF16), w_up_sh[l].astype(BF16), w_down_sh[l].astype(BF16), tm_dense)
        idx, wts, rank, counts = _route_call(logits, b_router[l], tm)
        row_start, item_e, item_row, item_ng, n_items = _expert_plan(counts[:, 0], n_items_max)
        slots_by_k = _slot_call(idx, rank, row_start, tm).reshape(TOP_K * T)
        xs_ref = jax.new_ref(rows_buf)
        _scatter_rows(h2.reshape(T, D // 2), slots_by_k, xs_ref)
        y = _expert_call(l, item_e, item_row, item_ng, n_items, jax.freeze(xs_ref), w_gate_e, w_up_e, w_down_e)
        rows_buf = y
        slots_kt = slots_by_k.reshape(TOP_K, T)
        out = None
        for part in range(MOE_PARTS):
            tp = T // MOE_PARTS
            idx_p = slots_kt[:, part * tp:(part + 1) * tp].reshape(TOP_K * tp)
            yk = _gather_rows(y, idx_p).reshape(TOP_K, tp, D // 2)
            out = _combine_call(yk, wts, shared.reshape(T, D), x1.reshape(T, D), gt_f, tm, part, out)
        x = out.reshape(B, S, D)
    return x
```

```python
import functools

import jax
import jax.numpy as jnp
import numpy as np
from jax import lax
from jax.experimental import pallas as pl
from jax.experimental.pallas import tpu as pltpu
from jax.experimental.pallas import tpu_sc as plsc

D_MODEL = 1024
NSA_Q_HEADS = 8
NSA_KV_HEADS = 2
HEAD_DIM = 64
NSA_GROUP = NSA_Q_HEADS // NSA_KV_HEADS
CMP_LEN = 32
CMP_STRIDE = 16
CMP_HIDDEN = 256
SEL_LEN = 64
SEL_TOPN = 16
WINDOW = 512
ROPE_THETA = 10000.0
Q_WIDTH = NSA_Q_HEADS * HEAD_DIM
KV_WIDTH = NSA_KV_HEADS * HEAD_DIM
GM_GROUPS = 8
GM_GROUP_DIM = 64
GM_WIDTH = GM_GROUPS * GM_GROUP_DIM
GM_CHUNK = 128
N_EXPERTS = 256
TOP_K = 8
D_EXPERT = 256
D_SHARED = 256
ROUTE_SCALE = 2.5
EPS = 1e-6

LANES = 128
SEL_BIAS_WIDTH = 64
MASK_NEG = -1e30
SEL_NEG = -30000.0
SC_WINDOW = 128
MOE_PARTS = 2
MOE_GRAN = 128
MOE_ROWS = 9 * MOE_GRAN
VMEM_LIMIT = 56 * 1024 * 1024

F32 = jnp.float32
BF16 = jnp.bfloat16
HI = lax.Precision.HIGHEST


def _cparams(*sem):
    return pltpu.CompilerParams(dimension_semantics=sem, vmem_limit_bytes=VMEM_LIMIT)


def _dot(a, b, **kw):
    return jnp.dot(a, b, preferred_element_type=F32, **kw)


def _dot_nt(a, b, **kw):
    return lax.dot_general(a, b, (((1,), (1,)), ((), ())), preferred_element_type=F32, **kw)


def _gelu(x):
    return 0.5 * x * (1.0 + jnp.tanh(0.7978845608028654 * (x + 0.044715 * (x * x * x))))


def _sigmoid(x):
    return 1.0 / (1.0 + jnp.exp(-x))


def _silu(x):
    return x * _sigmoid(x)


_HI_MASK = np.uint32(0xFFFF0000)


def _pack_halves(a):
    w = a.shape[1] // 2
    bits = lax.bitcast_convert_type(a.astype(BF16).astype(F32), jnp.uint32)
    return (bits[:, w:] & _HI_MASK) | (bits[:, :w] >> 16)


def _unpack_halves(words):
    lo = lax.bitcast_convert_type(words << 16, F32)
    hi = lax.bitcast_convert_type(words & _HI_MASK, F32)
    return lo, hi


def _mod_kernel(c_ref, w_ref, b_ref, o_ref):
    c = c_ref[...]
    o_ref[0] = _dot(_silu(c), w_ref[0], precision=HI) + b_ref[0]


def _mod_call(c, w_mod, b_mod):
    L, D, N = w_mod.shape
    B = c.shape[0]
    tn = 1536
    return pl.pallas_call(
        _mod_kernel,
        grid=(L, N // tn),
        in_specs=[pl.BlockSpec((B, D), lambda l, j: (0, 0)),
                  pl.BlockSpec((1, D, tn), lambda l, j: (l, 0, j)),
                  pl.BlockSpec((1, 1, tn), lambda l, j: (l, 0, j))],
        out_specs=pl.BlockSpec((1, B, tn), lambda l, j: (l, 0, j)),
        out_shape=jax.ShapeDtypeStruct((L, B, N), F32),
        compiler_params=_cparams("arbitrary", "arbitrary"),
        name="mod",
    )(c, w_mod, b_mod.reshape(L, 1, N))


_C_Q = 0
_C_K = 512
_C_KC = 768
_C_VC = 896
_C_VS = 1024
_C_VW = 1152
_C_G = 1280
_C_U = 1536
_C_V = 2048
_C_GA = 2560
_C_GB = 3584
IN_COLS_P = 4608


def _dot_split(a, b):
    hi = a.astype(BF16)
    lo = (a - hi.astype(F32)).astype(BF16)
    return _dot(hi, b) + _dot(lo, b)


def _head_norm(z, bd):
    ms = _dot_split(z * z, bd)
    return z * lax.rsqrt(ms + EPS)


def _rope(z, cos, sin_lo, sin_hi):
    w = z.shape[-1]
    half = HEAD_DIM // 2
    return z * cos + pltpu.roll(z, w - half, 1) * sin_lo + pltpu.roll(z, half, 1) * sin_hi


def _tile_lanes(t, n):
    return t if n == 1 else jnp.concatenate([t] * n, axis=-1)


def _inproj_kernel(x_ref, sc_ref, sh_ref, w_ref, bdq_ref, bdk_ref, qg_ref, kg_ref,
                   cos_ref, sl_ref, shi_ref, lng_ref, lnb_ref,
                   qn_ref, qr_ref, ks_ref, kw_ref, vs_ref, vw_ref, kc_ref, vc_ref,
                   g_ref, u_ref, v_ref, ga_ref, gb_ref):
    tm = x_ref.shape[1]
    x = x_ref[0]
    ms = jnp.mean(x * x, axis=-1, keepdims=True)
    h = (x * lax.rsqrt(ms + EPS)) * (1.0 + sc_ref[0]) + sh_ref[0]
    hb = h.astype(BF16)

    def mm(lo, width):
        return _dot(hb, w_ref[:, lo:lo + width])

    cos, sl, shi = cos_ref[0], sl_ref[0], shi_ref[0]

    zq = mm(_C_Q, Q_WIDTH)
    qn = _head_norm(zq, bdq_ref[...]) * qg_ref[...]
    qr = _rope(qn, _tile_lanes(cos, 4), _tile_lanes(sl, 4), _tile_lanes(shi, 4))
    qn_ref[0] = qn.astype(BF16)
    qr_ref[0] = qr.astype(BF16)

    zk = mm(_C_K, 2 * KV_WIDTH)
    kn = _head_norm(zk, bdk_ref[...]) * kg_ref[...]
    kr = _rope(kn, _tile_lanes(cos, 2), _tile_lanes(sl, 2), _tile_lanes(shi, 2))
    lane = lax.broadcasted_iota(jnp.int32, (tm, LANES), 1)
    tok = pl.program_id(1) * tm + lax.broadcasted_iota(jnp.int32, (tm, LANES), 0)
    onehot = jnp.where(lane - HEAD_DIM == tok // SEL_LEN, 1.0, 0.0)
    ones_col = jnp.where(lane == HEAD_DIM, 1.0, 0.0)
    low = lane < HEAD_DIM
    zvs = mm(_C_VS, KV_WIDTH)
    zvw = mm(_C_VW, KV_WIDTH)
    zkc = mm(_C_KC, KV_WIDTH)
    zvc = mm(_C_VC, KV_WIDTH)
    for kv in range(NSA_KV_HEADS):
        def head(a):
            return a if kv == 0 else pltpu.roll(a, HEAD_DIM, 1)
        ks_ref[0, kv] = jnp.where(low, head(kr[:, :KV_WIDTH]), onehot).T.astype(BF16)
        kw_ref[0, kv] = jnp.where(low, head(kr[:, KV_WIDTH:]), 0.0).T.astype(BF16)
        vs_ref[0, kv] = jnp.where(low, head(zvs), ones_col).astype(BF16)
        vw_ref[0, kv] = jnp.where(low, head(zvw), ones_col).astype(BF16)
        kc_ref[0, kv] = head(zkc)[:, :HEAD_DIM]
        vc_ref[0, kv] = head(zvc)[:, :HEAD_DIM]

    zg = mm(_C_G, 2 * LANES)
    sg = _sigmoid(zg)
    g_ref[0, 0] = sg[:, :LANES]
    g_ref[0, 1] = sg[:, LANES:]

    u_ref[0] = _gelu(mm(_C_U, GM_WIDTH)).astype(BF16)
    gv = _gelu(mm(_C_V, GM_WIDTH))
    mu = jnp.mean(gv, axis=-1, keepdims=True)
    cen = gv - mu
    var = jnp.mean(cen * cen, axis=-1, keepdims=True)
    v_ref[0] = ((cen * lax.rsqrt(var + EPS)) * lng_ref[...] + lnb_ref[...]).astype(BF16)

    ga_ref[0] = _sigmoid(mm(_C_GA, D_MODEL)).astype(BF16)
    gb_ref[0] = _sigmoid(mm(_C_GB, D_MODEL)).astype(BF16)


def _block_diag_mean(width):
    idx = np.arange(width) // HEAD_DIM
    return jnp.asarray((idx[:, None] == idx[None, :]).astype(np.float32) / HEAD_DIM).astype(BF16)


def _inproj_call(x, sc, sh, w_p, qg, kg, cos, sl, shi, lng, lnb, tm):
    B, S, D = x.shape
    H = NSA_KV_HEADS
    full = lambda *shape: pl.BlockSpec(shape, lambda b, i: (0,) * len(shape))
    tok3 = lambda w: pl.BlockSpec((1, tm, w), lambda b, i: (b, i, 0))
    per_b = pl.BlockSpec((1, 1, D), lambda b, i: (b, 0, 0))
    kv4 = lambda w: pl.BlockSpec((1, H, tm, w), lambda b, i: (b, 0, i, 0))
    kt4 = pl.BlockSpec((1, H, LANES, tm), lambda b, i: (b, 0, 0, i))
    sds = jax.ShapeDtypeStruct
    out_shape = [
        sds((B, S, Q_WIDTH), BF16), sds((B, S, Q_WIDTH), BF16),
        sds((B, H, LANES, S), BF16), sds((B, H, LANES, S), BF16),
        sds((B, H, S, LANES), BF16), sds((B, H, S, LANES), BF16),
        sds((B, H, S, HEAD_DIM), F32), sds((B, H, S, HEAD_DIM), F32),
        sds((B, H, S, LANES), F32),
        sds((B, S, GM_WIDTH), BF16), sds((B, S, GM_WIDTH), BF16),
        sds((B, S, D), BF16), sds((B, S, D), BF16),
    ]
    out_specs = [
        tok3(Q_WIDTH), tok3(Q_WIDTH), kt4, kt4, kv4(LANES), kv4(LANES),
        kv4(HEAD_DIM), kv4(HEAD_DIM), kv4(LANES),
        tok3(GM_WIDTH), tok3(GM_WIDTH), tok3(D), tok3(D),
    ]
    return pl.pallas_call(
        _inproj_kernel,
        grid=(B, S // tm),
        in_specs=[tok3(D), per_b, per_b, full(D, IN_COLS_P),
                  full(Q_WIDTH, Q_WIDTH), full(2 * KV_WIDTH, 2 * KV_WIDTH),
                  full(1, Q_WIDTH), full(1, 2 * KV_WIDTH),
                  tok3(LANES), tok3(LANES), tok3(LANES),
                  full(1, GM_WIDTH), full(1, GM_WIDTH)],
        out_specs=out_specs,
        out_shape=out_shape,
        compiler_params=_cparams("arbitrary", "arbitrary"),
        name="in_proj",
    )(x, sc, sh, w_p, _block_diag_mean(Q_WIDTH), _block_diag_mean(2 * KV_WIDTH), qg, kg,
      cos, sl, shi, lng, lnb)


def _compress_kernel(kr_ref, vr_ref, w1k_ref, w2k_ref, pek_ref, w1v_ref, w2v_ref, pev_ref,
                     kg_ref, kc_ref, vc_ref):
    nc = kr_ref.shape[2]
    half = CMP_STRIDE * HEAD_DIM

    def mlp(raw, w1_ref, w2_ref, pe_ref):
        a = raw.astype(BF16)
        top = _dot(a, w1_ref[:half, :])
        bot = _dot(a, w1_ref[half:, :])
        pe = jnp.broadcast_to(pe_ref[...], (8, 2 * half)).astype(BF16)
        pe_row = _dot(pe, w1_ref[...])[0:1, :]
        hid = top + pltpu.roll(bot, nc - 1, 0) + pe_row
        return _dot(_gelu(hid).astype(BF16), w2_ref[...])

    kc = mlp(kr_ref[0, 0], w1k_ref, w2k_ref, pek_ref)
    ms = jnp.mean(kc * kc, axis=-1, keepdims=True)
    kc_ref[0, 0] = (kc * lax.rsqrt(ms + EPS) * kg_ref[...]).astype(BF16)
    vc_ref[0, 0] = mlp(vr_ref[0, 0], w1v_ref, w2v_ref, pev_ref).astype(BF16)


def _compress_call(kc_raw, vc_raw, w1k, w2k, pek, w1v, w2v, pev, kg):
    B, H, S, hd = kc_raw.shape
    nc = S // CMP_STRIDE
    feat = CMP_STRIDE * hd
    raw = pl.BlockSpec((1, 1, nc, feat), lambda b, h: (b, h, 0, 0))
    full = lambda *shape: pl.BlockSpec(shape, lambda b, h: (0,) * len(shape))
    out = pl.BlockSpec((1, 1, nc, hd), lambda b, h: (b, h, 0, 0))
    return pl.pallas_call(
        _compress_kernel,
        grid=(B, H),
        in_specs=[raw, raw, full(2 * feat, CMP_HIDDEN), full(CMP_HIDDEN, hd), full(1, 2 * feat),
                  full(2 * feat, CMP_HIDDEN), full(CMP_HIDDEN, hd), full(1, 2 * feat),
                  full(1, hd)],
        out_specs=[out, out],
        out_shape=[jax.ShapeDtypeStruct((B, H, nc, hd), BF16)] * 2,
        compiler_params=_cparams("arbitrary", "arbitrary"),
        name="compress",
    )(kc_raw.reshape(B, H, nc, feat), vc_raw.reshape(B, H, nc, feat),
      w1k, w2k, pek, w1v, w2v, pev, kg)


def _group_rows(a):
    return jnp.concatenate([a[:, g * HEAD_DIM:(g + 1) * HEAD_DIM] for g in range(NSA_GROUP)], axis=0)


def _nsa_cmp_kernel(qn_ref, qr_ref, kc_ref, vc_ref, ovl_ref, oc_ref, qa_ref, *, n_sel):
    tq = qn_ref.shape[1]
    nc = kc_ref.shape[2]
    G = NSA_GROUP
    q0 = pl.program_id(2) * tq
    q4 = _group_rows(qn_ref[0])
    s = _dot_nt(q4, kc_ref[0, 0])
    row = lax.broadcasted_iota(jnp.int32, (G, tq, nc), 1).reshape(G * tq, nc)
    col = lax.broadcasted_iota(jnp.int32, (G * tq, nc), 1)
    vis = col * CMP_STRIDE + (CMP_LEN - 1) <= q0 + row
    s = jnp.where(vis, s, MASK_NEG)
    m = jnp.max(s, axis=-1, keepdims=True)
    e = jnp.where(vis, jnp.exp(s - m), 0.0)
    p = e / jnp.maximum(jnp.sum(e, axis=-1, keepdims=True), 1e-30)
    oc = _dot(p.astype(BF16), vc_ref[0, 0])
    oc_ref[0, 0] = oc.reshape(G, tq, HEAD_DIM)

    psum = p[0:tq] + p[tq:2 * tq] + p[2 * tq:3 * tq] + p[3 * tq:4 * tq]
    p_hi = psum.astype(BF16)
    p_lo = (psum - p_hi.astype(F32)).astype(BF16)
    imp = (_dot_nt(ovl_ref[...], p_hi) + _dot_nt(ovl_ref[...], p_lo))[:SEL_BIAS_WIDTH]
    blk = lax.broadcasted_iota(jnp.int32, (SEL_BIAS_WIDTH, tq), 0)
    cur = (q0 + lax.broadcasted_iota(jnp.int32, (SEL_BIAS_WIDTH, tq), 1)) // SEL_LEN
    valid = blk <= cur
    forced = (blk == 0) | (blk == cur) | (blk == cur - 1)
    cand = valid & jnp.logical_not(forced)
    n_forced = jnp.minimum(cur, 2) + 1
    val = jnp.where(cand, imp, -1.0)
    cnt = jnp.zeros((SEL_BIAS_WIDTH, tq), F32)
    for j in range(n_sel):
        vj = jnp.broadcast_to(val[j:j + 1, :], (SEL_BIAS_WIDTH, tq))
        cnt = cnt + jnp.where(blk > j, jnp.where(vj >= val, 1.0, 0.0), jnp.where(vj > val, 1.0, 0.0))
    free = (min(SEL_TOPN, n_sel) - n_forced).astype(F32)
    sel = (forced & valid) | (cand & (cnt < free))
    sel_t = jnp.concatenate([jnp.where(sel, 1.0, 0.0), jnp.zeros((LANES - SEL_BIAS_WIDTH, tq), F32)], axis=0)
    sel_q = sel_t.T
    bias = jnp.where(sel_q > 0.5, 0.0, SEL_NEG)
    lane = lax.broadcasted_iota(jnp.int32, (tq, LANES), 1)
    bias_hi = pltpu.roll(bias, SEL_BIAS_WIDTH, 1)
    qr = qr_ref[0]
    for g in range(G):
        qg = qr[:, g * HEAD_DIM:(g + 1) * HEAD_DIM].astype(F32)
        qg = jnp.concatenate([qg, qg], axis=-1)
        qa_ref[0, 0, g] = jnp.where(lane < HEAD_DIM, qg, bias_hi).astype(BF16)


def _overlap_t(S):
    n_cmp = (S - CMP_LEN) // CMP_STRIDE + 1
    nc = S // CMP_STRIDE
    n_sel = S // SEL_LEN
    start = np.arange(nc) * CMP_STRIDE
    end = start + CMP_LEN - 1
    sel_start = np.arange(n_sel) * SEL_LEN
    ov = (start[None, :] <= sel_start[:, None] + SEL_LEN - 1) & (end[None, :] >= sel_start[:, None])
    ov = ov & (np.arange(nc) < n_cmp)[None, :]
    out = np.zeros((LANES, nc), np.float32)
    out[:n_sel] = ov.astype(np.float32)
    return jnp.asarray(out).astype(BF16)


def _nsa_cmp_call(qn, qr, kc, vc, tq):
    B, S, _ = qn.shape
    H, G = NSA_KV_HEADS, NSA_GROUP
    nc = kc.shape[2]
    n_sel = S // SEL_LEN
    assert n_sel <= SEL_BIAS_WIDTH and nc % LANES == 0
    qspec = pl.BlockSpec((1, tq, G * HEAD_DIM), lambda b, h, i: (b, i, h))
    cspec = pl.BlockSpec((1, 1, nc, HEAD_DIM), lambda b, h, i: (b, h, 0, 0))
    return pl.pallas_call(
        functools.partial(_nsa_cmp_kernel, n_sel=n_sel),
        grid=(B, H, S // tq),
        in_specs=[qspec, qspec, cspec, cspec, pl.BlockSpec((LANES, nc), lambda b, h, i: (0, 0))],
        out_specs=[pl.BlockSpec((1, 1, G, tq, HEAD_DIM), lambda b, h, i: (b, h, 0, i, 0)),
                   pl.BlockSpec((1, 1, G, tq, LANES), lambda b, h, i: (b, h, 0, i, 0))],
        out_shape=[jax.ShapeDtypeStruct((B, H, G, S, HEAD_DIM), F32),
                   jax.ShapeDtypeStruct((B, H, G, S, LANES), BF16)],
        compiler_params=_cparams("arbitrary", "arbitrary", "arbitrary"),
        name="nsa_cmp",
    )(qn, qr, kc, vc, _overlap_t(S))


SEL_CHUNK = 4


def _nsa_flash_kernel(qa_ref, ks_ref, vs_ref, kw_ref, vw_ref, oc_ref, g_ref, o_ref, m_scr, acc_scr):
    G = NSA_GROUP
    tq = qa_ref.shape[3]
    R = G * tq
    tk = tq
    i = pl.program_id(2)
    q0 = i * tq

    def rows_of(ref, j0, nt):
        return ref[0, 0, pl.ds(pl.multiple_of(j0 * tk, tk), nt * tk), :]

    def attend(k_ref, v_ref, j0, nt, visible, state):
        v = rows_of(v_ref, j0, nt)
        kt = k_ref[0, 0, :, pl.ds(pl.multiple_of(j0 * tk, tk), nt * tk)]
        s_all = _dot(qa_ref[0, 0].reshape(R, LANES), kt)
        ss = [s_all[g * tq:(g + 1) * tq] for g in range(G)]
        if visible is not None:
            qpos = q0 + lax.broadcasted_iota(jnp.int32, (tq, nt * tk), 0)
            kpos = j0 * tk + lax.broadcasted_iota(jnp.int32, (tq, nt * tk), 1)
            mask = visible(qpos, kpos)
        ps, alphas = [], []
        for g in range(G):
            s = ss[g] if visible is None else jnp.where(mask, ss[g], MASK_NEG)
            cols = [s[:, c * LANES:(c + 1) * LANES] for c in range(nt * tk // LANES)]
            m_new = jnp.max(functools.reduce(jnp.maximum, cols), axis=-1, keepdims=True)
            if state:
                m_prev = m_scr[g]
                m_new = jnp.maximum(m_prev, m_new)
                alphas.append(jnp.exp(m_prev - m_new))
                m_scr[g] = m_new
            ps.append(jnp.concatenate([jnp.exp(c - m_new) for c in cols], axis=-1).astype(BF16))
        if not state:
            return [_dot(ps[g], v) for g in range(G)]
        for g in range(G):
            acc_scr[g] = alphas[g] * acc_scr[g] + _dot(ps[g], v)

    def normalise(acc):
        return acc[:, :HEAD_DIM] / acc[:, HEAD_DIM:HEAD_DIM + 1]

    m_scr[...] = jnp.full((G, tq, LANES), MASK_NEG, F32)
    acc_scr[...] = jnp.zeros((G, tq, LANES), F32)
    causal = lambda qpos, kpos: kpos <= qpos

    def sel_body(c, carry):
        attend(ks_ref, vs_ref, c * SEL_CHUNK, SEL_CHUNK, None, True)
        return carry

    n_full = i // SEL_CHUNK
    lax.fori_loop(0, n_full, sel_body, 0)
    attend(ks_ref, vs_ref, n_full * SEL_CHUNK, 2, causal, True)

    @pl.when(i % SEL_CHUNK >= 2)
    def _():
        attend(ks_ref, vs_ref, n_full * SEL_CHUNK + 2, 2, causal, True)

    o_sel = normalise(acc_scr[...].reshape(R, LANES))

    assert WINDOW == 2 * tk
    band = lambda qpos, kpos: (kpos <= qpos) & (kpos > qpos - WINDOW)
    o_win = normalise(jnp.concatenate(attend(kw_ref, vw_ref, jnp.maximum(i - 2, 0), 3, band, False), axis=0))

    o_cmp = oc_ref[0, 0].reshape(R, HEAD_DIM)
    gates = g_ref[0, 0]
    outs = []
    for g in range(G):
        r = slice(g * tq, (g + 1) * tq)
        outs.append(gates[:, 3 * g:3 * g + 1] * o_cmp[r]
                    + gates[:, 3 * g + 1:3 * g + 2] * o_sel[r]
                    + gates[:, 3 * g + 2:3 * g + 3] * o_win[r])
    o_ref[0] = jnp.concatenate(outs, axis=-1).astype(BF16)


def _nsa_flash_call(qa, ks, vs, kw, vw, oc, gates):
    B, H, G, S, _ = qa.shape
    tq = WINDOW // 2
    assert (S // tq) % SEL_CHUNK == 0
    kt = pl.BlockSpec((1, 1, LANES, S), lambda b, h, i: (b, h, 0, 0))
    vv = pl.BlockSpec((1, 1, S, LANES), lambda b, h, i: (b, h, 0, 0))
    return pl.pallas_call(
        _nsa_flash_kernel,
        grid=(B, H, S // tq),
        in_specs=[pl.BlockSpec((1, 1, G, tq, LANES), lambda b, h, i: (b, h, 0, i, 0)),
                  kt, vv, kt, vv,
                  pl.BlockSpec((1, 1, G, tq, HEAD_DIM), lambda b, h, i: (b, h, 0, i, 0)),
                  pl.BlockSpec((1, 1, tq, LANES), lambda b, h, i: (b, h, i, 0))],
        out_specs=pl.BlockSpec((1, tq, G * HEAD_DIM), lambda b, h, i: (b, i, h)),
        out_shape=jax.ShapeDtypeStruct((B, S, Q_WIDTH), BF16),
        scratch_shapes=[pltpu.VMEM((G, tq, LANES), F32), pltpu.VMEM((G, tq, LANES), F32)],
        compiler_params=_cparams("arbitrary", "arbitrary", "arbitrary"),
        name="nsa_flash",
    )(qa, ks, vs, kw, vw, oc, gates)


def _gmlp_kernel(u_ref, v_ref, ws_ref, bs_ref, o_ref):
    tm = u_ref.shape[1]
    C = GM_CHUNK
    r = lax.broadcasted_iota(jnp.int32, (C, C), 0)
    c = lax.broadcasted_iota(jnp.int32, (C, C), 1)
    causal = c <= r
    ws = [jnp.where(causal, ws_ref[g], 0.0).astype(BF16) for g in range(GM_GROUPS)]
    for n in range(tm // C):
        rows = slice(n * C, (n + 1) * C)
        vn = v_ref[0, rows, :]
        mixed = jnp.concatenate(
            [_dot(ws[g], vn[:, g * GM_GROUP_DIM:(g + 1) * GM_GROUP_DIM]) for g in range(GM_GROUPS)],
            axis=-1)
        o_ref[0, rows, :] = (u_ref[0, rows, :].astype(F32) * (mixed + bs_ref[...])).astype(BF16)


def _gmlp_call(u, v, ws, bs_full, tm):
    B, S, W = u.shape
    tok = pl.BlockSpec((1, tm, W), lambda b, i: (b, i, 0))
    return pl.pallas_call(
        _gmlp_kernel,
        grid=(B, S // tm),
        in_specs=[tok, tok,
                  pl.BlockSpec((GM_GROUPS, GM_CHUNK, GM_CHUNK), lambda b, i: (0, 0, 0)),
                  pl.BlockSpec((GM_CHUNK, W), lambda b, i: (0, 0))],
        out_specs=tok,
        out_shape=jax.ShapeDtypeStruct((B, S, W), BF16),
        compiler_params=_cparams("arbitrary", "arbitrary"),
        name="gmlp",
    )(u, v, ws, bs_full)


def _mixout_kernel(on_ref, og_ref, ga_ref, gb_ref, x_ref, gt_ref, sc_ref, sh_ref,
                   wpn_ref, wpg_ref, wo_ref, wr_ref, wgs_ref, wus_ref, wds_ref,
                   x1_ref, h2_ref, lg_ref, shd_ref):
    ya = _dot(on_ref[0], wpn_ref[...])
    yb = _dot(og_ref[0], wpg_ref[...])
    merged = ga_ref[0].astype(F32) * ya + gb_ref[0].astype(F32) * yb
    x1 = x_ref[0] + gt_ref[0] * _dot(merged.astype(BF16), wo_ref[...])
    x1_ref[0] = x1
    ms = jnp.mean(x1 * x1, axis=-1, keepdims=True)
    h2 = (x1 * lax.rsqrt(ms + EPS)) * (1.0 + sc_ref[0]) + sh_ref[0]
    hb = h2.astype(BF16)
    lg_ref[0] = _dot_nt(wr_ref[...], hb)
    h2_ref[0] = _pack_halves(hb)
    act = _silu(_dot(hb, wgs_ref[...])) * _dot(hb, wus_ref[...])
    shd_ref[0] = _dot(act.astype(BF16), wds_ref[...])


def _mixout_call(o_nsa, o_gm, ga, gb, x, gt, sc, sh, wpn, wpg, wo, wr, wgs, wus, wds, tm):
    B, S, D = x.shape
    tok = lambda w: pl.BlockSpec((1, tm, w), lambda b, i: (b, i, 0))
    per_b = pl.BlockSpec((1, 1, D), lambda b, i: (b, 0, 0))
    full = lambda a: pl.BlockSpec(a.shape, lambda b, i: (0,) * a.ndim)
    sds = jax.ShapeDtypeStruct
    return pl.pallas_call(
        _mixout_kernel,
        grid=(B, S // tm),
        in_specs=[tok(Q_WIDTH), tok(GM_WIDTH), tok(D), tok(D), tok(D), per_b, per_b, per_b,
                  full(wpn), full(wpg), full(wo), full(wr), full(wgs), full(wus), full(wds)],
        out_specs=[tok(D), tok(D // 2), pl.BlockSpec((1, N_EXPERTS, tm), lambda b, i: (b, 0, i)), tok(D)],
        out_shape=[sds((B, S, D), F32), sds((B, S, D // 2), jnp.uint32), sds((B, N_EXPERTS, S), F32),
                   sds((B, S, D), F32)],
        compiler_params=_cparams("arbitrary", "arbitrary"),
        name="mix_out",
    )(o_nsa, o_gm, ga, gb, x, gt, sc, sh, wpn, wpg, wo, wr, wgs, wus, wds)


def _route_kernel(lg_ref, br_ref, idx_ref, wt_ref, rank_ref, cnt_ref, run_scr):
    E, tm = lg_ref.shape[1], lg_ref.shape[2]

    @pl.when(pl.program_id(0) == 0)
    def _():
        run_scr[...] = jnp.zeros_like(run_scr)

    aff = _sigmoid(lg_ref[0])
    work = aff + br_ref[...]
    row = lax.broadcasted_iota(jnp.int32, (E, tm), 0).astype(F32)
    picked = jnp.zeros((E, tm), F32)
    idxs, tops = [], []
    for _ in range(TOP_K):
        m = jnp.max(work, axis=0, keepdims=True)
        idx = jnp.min(jnp.where(work == m, row, float(E)), axis=0, keepdims=True)
        hit = row == idx
        tops.append(jnp.sum(jnp.where(hit, aff, 0.0), axis=0, keepdims=True))
        idxs.append(idx)
        picked = jnp.where(hit, 1.0, picked)
        work = jnp.where(hit, -jnp.inf, work)
    total = functools.reduce(jnp.add, tops)
    r = lax.broadcasted_iota(jnp.int32, (tm, tm), 0)
    c = lax.broadcasted_iota(jnp.int32, (tm, tm), 1)
    before = _dot(picked.astype(BF16), jnp.where(r < c, 1.0, 0.0).astype(BF16)) + run_scr[...]
    ranks = [jnp.sum(jnp.where(row == idx, before, 0.0), axis=0, keepdims=True) for idx in idxs]
    run_scr[...] = run_scr[...] + jnp.sum(picked, axis=1, keepdims=True)
    cnt_ref[...] = run_scr[...]
    idx_ref[...] = jnp.concatenate(idxs, axis=0).astype(jnp.int32)
    rank_ref[...] = jnp.concatenate(ranks, axis=0).astype(jnp.int32)
    wt = jnp.concatenate([t / total * ROUTE_SCALE for t in tops]
                         + [jnp.zeros((LANES - TOP_K, tm), F32)], axis=0)
    eye = jnp.where(r == c, 1.0, 0.0).astype(BF16)
    cols = jnp.zeros((tm, LANES), F32)
    rest = wt
    for _ in range(3):
        part = rest.astype(BF16)
        cols = cols + _dot_nt(eye, part)
        rest = rest - part.astype(F32)
    wt_ref[...] = cols


def _route_call(logits_t, b_router, tm):
    B, E, S = logits_t.shape
    per_b = S // tm
    T = B * S
    kt = pl.BlockSpec((TOP_K, tm), lambda i: (0, i))
    col = pl.BlockSpec((E, 1), lambda i: (0, 0))
    sds = jax.ShapeDtypeStruct
    return pl.pallas_call(
        _route_kernel,
        grid=(T // tm,),
        in_specs=[pl.BlockSpec((1, E, tm), lambda i: (i // per_b, 0, i % per_b)), col],
        out_specs=[kt, pl.BlockSpec((tm, LANES), lambda i: (i, 0)), kt, col],
        out_shape=[sds((TOP_K, T), jnp.int32), sds((T, LANES), F32), sds((TOP_K, T), jnp.int32),
                   sds((E, 1), F32)],
        scratch_shapes=[pltpu.VMEM((E, 1), F32)],
        compiler_params=_cparams("arbitrary"),
        name="route",
    )(logits_t, b_router.reshape(E, 1))


def _slot_kernel(idx_ref, rank_ref, ps_ref, o_ref):
    tm = idx_ref.shape[1]
    E = ps_ref.shape[0]
    row = lax.broadcasted_iota(jnp.int32, (E, tm), 0)
    idx, rank, ps = idx_ref[...], rank_ref[...], ps_ref[...]
    base = [jnp.sum(jnp.where(row == idx[k:k + 1, :], ps, 0.0), axis=0, keepdims=True) for k in range(TOP_K)]
    o_ref[...] = jnp.concatenate(base, axis=0).astype(jnp.int32) + rank


def _slot_call(idx, rank, pad_start, tm):
    T = idx.shape[1]
    E = pad_start.shape[0]
    kt = pl.BlockSpec((TOP_K, tm), lambda i: (0, i))
    return pl.pallas_call(
        _slot_kernel,
        grid=(T // tm,),
        in_specs=[kt, kt, pl.BlockSpec((E, 1), lambda i: (0, 0))],
        out_specs=kt,
        out_shape=jax.ShapeDtypeStruct((TOP_K, T), jnp.int32),
        compiler_params=_cparams("arbitrary"),
        name="slot",
    )(idx, rank, pad_start.astype(F32).reshape(E, 1))


def _sc_mesh():
    mesh = plsc.VectorSubcoreMesh(core_axis_name="core", subcore_axis_name="subcore")
    return mesh, mesh.num_cores * mesh.num_subcores


def _sc_worker(mesh):
    return lax.axis_index("subcore") * mesh.num_cores + lax.axis_index("core")


def _scatter_rows(rows, indices_by_k, out_ref):
    n, width = rows.shape
    mesh, workers = _sc_mesh()
    per_worker = n // workers
    assert per_worker * workers == n and per_worker % SC_WINDOW == 0

    @pl.kernel(out_type=(), mesh=mesh, name="scatter_rows",
               scratch_types=[pltpu.VMEM((SC_WINDOW,), jnp.int32), pltpu.VMEM((SC_WINDOW, width), rows.dtype)])
    def scatter(rows_hbm, idx_hbm, out_hbm, idx_v, rows_v):
        worker = _sc_worker(mesh)

        @pl.loop(0, per_worker // SC_WINDOW)
        def _(j):
            base = pl.multiple_of(worker * per_worker + j * SC_WINDOW, SC_WINDOW)
            pltpu.sync_copy(rows_hbm.at[pl.ds(base, SC_WINDOW)], rows_v)
            for k in range(TOP_K):
                pltpu.sync_copy(idx_hbm.at[pl.ds(pl.multiple_of(k * n + base, SC_WINDOW), SC_WINDOW)], idx_v)
                pltpu.sync_copy(rows_v, out_hbm.at[idx_v])

    scatter(rows, indices_by_k, out_ref)


def _expert_kernel(ie_ref, ir_ref, ig_ref, n_ref, x_hbm, wg_ref, wu_ref, wd_ref, y_hbm,
                   xbuf, ybuf, act_scr, xsem, ysem):
    w = pl.program_id(0)
    n = n_ref[0]
    R = xbuf.shape[1]
    last = ie_ref.shape[0] - 1

    def x_copy(item, slot):
        row = pl.multiple_of(ir_ref[jnp.minimum(item, last)], MOE_GRAN)
        return pltpu.make_async_copy(x_hbm.at[pl.ds(row, R), :], xbuf.at[slot], xsem.at[slot])

    def y_copy(item, slot, gi):
        row = pl.multiple_of(ir_ref[jnp.minimum(item, last)] + gi * MOE_GRAN, MOE_GRAN)
        src = ybuf.at[slot, pl.ds(pl.multiple_of(gi * MOE_GRAN, MOE_GRAN), MOE_GRAN), :]
        return pltpu.make_async_copy(src, y_hbm.at[pl.ds(row, MOE_GRAN), :], ysem.at[slot])

    def for_granules(item, fn):
        def body(gi, carry):
            fn(gi)
            return carry
        lax.fori_loop(0, ig_ref[jnp.clip(item, 0, last)], body, 0)

    def gate_up(slot):
        lo, hi = _unpack_halves(xbuf[slot])
        x = jnp.concatenate([lo, hi], axis=1).astype(BF16)
        a = _dot(x, wg_ref[0, 0].astype(BF16))
        b = _dot(x, wu_ref[0, 0].astype(BF16))
        return (_silu(a) * b).astype(BF16)

    def down(act):
        return _pack_halves(_dot(act, wd_ref[0, 0].astype(BF16)))

    nx = xbuf.shape[0]
    cur, prv = w % nx, (w + 1) % 2

    @pl.when(w == 0)
    def _():
        for j in range(nx - 1):
            @pl.when(j < n)
            def _():
                x_copy(j, j).start()

    @pl.when(w < n)
    def _():
        x_copy(w, cur).wait()

    @pl.when(w + nx - 1 < n)
    def _():
        x_copy(w + nx - 1, (w + nx - 1) % nx).start()

    @pl.when((w >= 3) & (w - 3 < n))
    def _():
        for_granules(w - 3, lambda gi: y_copy(w - 3, prv, gi).wait())

    @pl.when(w == 0)
    def _():
        act_scr[...] = gate_up(0)

    @pl.when((w >= 1) & (w < n))
    def _():
        prev = act_scr[...]
        ybuf[prv] = down(prev)
        act_scr[...] = gate_up(cur)

    @pl.when((w >= 1) & (w == n))
    def _():
        ybuf[prv] = down(act_scr[...])

    @pl.when((w >= 1) & (w <= n))
    def _():
        for_granules(w - 1, lambda gi: y_copy(w - 1, prv, gi).start())


def _expert_call(layer, item_e, item_row, item_ng, n_items, xs, wg, wu, wd):
    n_alloc, W = xs.shape
    R = MOE_ROWS
    nw = item_e.shape[0]
    D, F = wg.shape[2], wg.shape[3]
    cur = lambda w, ie, ir, ig, n: (layer, ie[jnp.minimum(w, nw - 1)], 0, 0)
    prev = lambda w, ie, ir, ig, n: (layer, ie[jnp.clip(w - 1, 0, nw - 1)], 0, 0)
    grid_spec = pltpu.PrefetchScalarGridSpec(
        num_scalar_prefetch=4,
        grid=(nw + 3,),
        in_specs=[pl.BlockSpec(memory_space=pl.ANY),
                  pl.BlockSpec((1, 1, D, F), cur), pl.BlockSpec((1, 1, D, F), cur),
                  pl.BlockSpec((1, 1, F, D), prev)],
        out_specs=pl.BlockSpec(memory_space=pl.ANY),
        scratch_shapes=[pltpu.VMEM((3, R, W), jnp.uint32), pltpu.VMEM((2, R, W), jnp.uint32),
                        pltpu.VMEM((R, F), BF16),
                        pltpu.SemaphoreType.DMA((3,)), pltpu.SemaphoreType.DMA((2,))],
    )
    return pl.pallas_call(
        _expert_kernel,
        grid_spec=grid_spec,
        out_shape=jax.ShapeDtypeStruct((n_alloc, W), jnp.uint32),
        input_output_aliases={4: 0},
        compiler_params=pltpu.CompilerParams(dimension_semantics=("arbitrary",), vmem_limit_bytes=VMEM_LIMIT,
                                             has_side_effects=True),
        name="expert",
    )(item_e, item_row, item_ng, n_items, xs, wg, wu, wd)


def _gather_rows(table, indices):
    n = indices.shape[0]
    width = table.shape[1]
    mesh, workers = _sc_mesh()
    per_worker = n // workers
    assert per_worker * workers == n and per_worker % SC_WINDOW == 0

    half = SC_WINDOW // 2
    steps = per_worker // SC_WINDOW
    rows_t = pltpu.VMEM((half, width), table.dtype)

    @pl.kernel(out_type=jax.ShapeDtypeStruct((n, width), table.dtype), mesh=mesh, name="gather_rows",
               scratch_types=[pltpu.VMEM((SC_WINDOW,), jnp.int32), rows_t, rows_t] + [pltpu.SemaphoreType.DMA] * 4)
    def gather(table_hbm, idx_hbm, out_hbm, idx_v, rows_a, rows_b, gsem_a, gsem_b, wsem_a, wsem_b):
        worker = _sc_worker(mesh)
        first = worker * per_worker
        halves = ((rows_a, gsem_a, wsem_a, 0), (rows_b, gsem_b, wsem_b, half))

        def write(buf, wsem, row):
            return pltpu.make_async_copy(buf, out_hbm.at[pl.ds(pl.multiple_of(row, half), half)], wsem)

        @pl.loop(0, steps)
        def _(j):
            base = pl.multiple_of(first + j * SC_WINDOW, SC_WINDOW)
            pltpu.sync_copy(idx_hbm.at[pl.ds(base, SC_WINDOW)], idx_v)
            gathers = []
            for buf, gsem, wsem, off in halves:
                @pl.when(j > 0)
                def _():
                    write(buf, wsem, base).wait()
                gathers.append(pltpu.async_copy(table_hbm.at[idx_v.at[pl.ds(off, half)]], buf, gsem))
            for (buf, gsem, wsem, off), g in zip(halves, gathers):
                g.wait()
                write(buf, wsem, base + off).start()

        for buf, gsem, wsem, off in halves:
            write(buf, wsem, first).wait()

    return gather(table, indices)


def _combine_kernel(y_ref, wt_ref, shd_ref, x_ref, gt_ref, o_ref):
    wt = wt_ref[...]
    acc_lo = acc_hi = None
    for k in range(TOP_K):
        lo, hi = _unpack_halves(y_ref[k])
        w = wt[:, k:k + 1]
        acc_lo = w * lo if k == 0 else acc_lo + w * lo
        acc_hi = w * hi if k == 0 else acc_hi + w * hi
    routed = jnp.concatenate([acc_lo, acc_hi], axis=1)
    o_ref[...] = x_ref[...] + gt_ref[0] * (routed + shd_ref[...])


def _combine_call(yk, wts, shared, x1, gt, tm, part):
    T, D = x1.shape
    B = gt.shape[0]
    per_b = T // B // tm
    steps = yk.shape[1] // tm
    first = part * steps
    tok = lambda w: pl.BlockSpec((tm, w), lambda i: (first + i, 0))
    return pl.pallas_call(
        _combine_kernel,
        grid=(steps,),
        in_specs=[pl.BlockSpec((TOP_K, tm, yk.shape[2]), lambda i: (0, i, 0)),
                  tok(LANES), tok(D), tok(D),
                  pl.BlockSpec((1, 1, D), lambda i: ((first + i) // per_b, 0, 0))],
        out_specs=tok(D),
        out_shape=jax.ShapeDtypeStruct((T, D), F32),
        input_output_aliases={3: 0},
        compiler_params=_cparams("arbitrary"),
        name="combine",
    )(yk, wts, shared, x1, gt)


def _rope_tables(positions):
    half = HEAD_DIM // 2
    inv = ROPE_THETA ** (-jnp.arange(half, dtype=F32) / half)
    ang = positions.astype(F32)[..., None] * inv
    cos, sin, zero = jnp.cos(ang), jnp.sin(ang), jnp.zeros_like(ang)
    cos_t = jnp.concatenate([cos, cos] * 2, axis=-1)
    sin_lo = jnp.concatenate([-sin, zero] * 2, axis=-1)
    sin_hi = jnp.concatenate([zero, sin] * 2, axis=-1)
    return cos_t, sin_lo, sin_hi


def _reorder_w_in(w):
    o = np.cumsum([0, Q_WIDTH] + [KV_WIDTH] * 6 + [NSA_Q_HEADS * 3, GM_WIDTH, GM_WIDTH, D_MODEL, D_MODEL])
    q, kc, vc, ks, vs, kw, vw, g, u, v, ga, gb = [w[:, o[i]:o[i + 1]] for i in range(12)]
    per = NSA_GROUP * 3
    pad = jnp.zeros((w.shape[0], LANES - per), w.dtype)
    return jnp.concatenate([q, ks, kw, kc, vc, vs, vw, g[:, :per], pad, g[:, per:], pad, u, v, ga, gb], axis=1)


def _owner(ends, pos):
    return jnp.minimum(jnp.sum((ends[None, :] <= pos[:, None]).astype(jnp.int32), axis=1), ends.shape[0] - 1)


def _lookup(table, idx):
    hit = idx[:, None] == jnp.arange(table.shape[0], dtype=jnp.int32)[None, :]
    return jnp.sum(jnp.where(hit, table[None, :], 0), axis=1)


def _expert_plan(counts, n_items_max):
    per_item = MOE_ROWS // MOE_GRAN
    counts = counts.astype(jnp.int32)
    gran = (counts + MOE_GRAN - 1) // MOE_GRAN
    gran_start = jnp.cumsum(gran) - gran
    items = (gran + per_item - 1) // per_item
    item_end = jnp.cumsum(items)
    w = jnp.arange(n_items_max, dtype=jnp.int32)
    ie = _owner(item_end, w)
    part = w - (_lookup(item_end, ie) - _lookup(items, ie))
    live = w < item_end[-1]
    item_ng = jnp.where(live, jnp.clip(_lookup(gran, ie) - part * per_item, 0, per_item), 0).astype(jnp.int32)
    item_row = jnp.where(live, (_lookup(gran_start, ie) + part * per_item) * MOE_GRAN, 0).astype(jnp.int32)
    return gran_start * MOE_GRAN, ie.astype(jnp.int32), item_row, item_ng, item_end[-1:].astype(jnp.int32)


def kernel(x, c, positions, w_mod, b_mod, w_in, q_gain, k_gain, cmp_pos_k, cmp_pos_v, cmp_w1_k, cmp_w2_k, cmp_w1_v, cmp_w2_v, gm_ln_g, gm_ln_b, gm_ws, gm_bs, w_proj_nsa, w_proj_gm, w_out, w_router, b_router, w_gate_e, w_up_e, w_down_e, w_gate_sh, w_up_sh, w_down_sh):
    B, S, D = x.shape
    L = w_mod.shape[0]
    T = B * S
    tm = 256
    tm_dense = 512
    scale = HEAD_DIM ** -0.5
    cos_t, sin_lo, sin_hi = _rope_tables(positions)
    mod = _mod_call(c, w_mod, b_mod)
    n_alloc = T * TOP_K + N_EXPERTS * MOE_GRAN + MOE_ROWS
    n_items_max = N_EXPERTS + T * TOP_K // MOE_ROWS + 1
    rows_buf = jnp.zeros((n_alloc, D // 2), jnp.uint32)

    for l in range(L):
        sh_a, sc_a, gt_a, sh_f, sc_f, gt_f = [mod[l, :, i * D:(i + 1) * D].reshape(B, 1, D) for i in range(6)]
        qg = (jnp.tile(q_gain[l], NSA_Q_HEADS) * scale).reshape(1, Q_WIDTH)
        kg = jnp.tile(k_gain[l], 2 * NSA_KV_HEADS).reshape(1, 2 * KV_WIDTH)
        (qn, qr, ks, kw, vs, vw, kc_raw, vc_raw, gates, u, v, ga, gb) = _inproj_call(
            x, sc_a, sh_a, _reorder_w_in(w_in[l]).astype(BF16), qg, kg, cos_t, sin_lo, sin_hi,
            gm_ln_g[l].reshape(1, GM_WIDTH), gm_ln_b[l].reshape(1, GM_WIDTH), tm_dense)
        kc, vc = _compress_call(
            kc_raw, vc_raw, cmp_w1_k[l].astype(BF16), cmp_w2_k[l].astype(BF16), cmp_pos_k[l].reshape(1, -1),
            cmp_w1_v[l].astype(BF16), cmp_w2_v[l].astype(BF16), cmp_pos_v[l].reshape(1, -1),
            k_gain[l].reshape(1, HEAD_DIM))
        o_cmp, q_aug = _nsa_cmp_call(qn, qr, kc, vc, WINDOW // 2)
        o_nsa = _nsa_flash_call(q_aug, ks, vs, kw, vw, o_cmp, gates)
        bs_full = jnp.repeat(gm_bs[l].T, GM_GROUP_DIM, axis=1)
        o_gm = _gmlp_call(u, v, gm_ws[l], bs_full, 512)
        x1, h2, logits, shared = _mixout_call(
            o_nsa, o_gm, ga, gb, x, gt_a, sc_f, sh_f,
            w_proj_nsa[l].astype(BF16), w_proj_gm[l].astype(BF16), w_out[l].astype(BF16), w_router[l].T.astype(BF16),
            w_gate_sh[l].astype(BF16), w_up_sh[l].astype(BF16), w_down_sh[l].astype(BF16), tm_dense)
        idx, wts, rank, counts = _route_call(logits, b_router[l], tm)
        row_start, item_e, item_row, item_ng, n_items = _expert_plan(counts[:, 0], n_items_max)
        slots_by_k = _slot_call(idx, rank, row_start, tm).reshape(TOP_K * T)
        xs_ref = jax.new_ref(rows_buf)
        _scatter_rows(h2.reshape(T, D // 2), slots_by_k, xs_ref)
        y = _expert_call(l, item_e, item_row, item_ng, n_items, jax.freeze(xs_ref), w_gate_e, w_up_e, w_down_e)
        rows_buf = y
        slots_kt = slots_by_k.reshape(TOP_K, T)
        xt = x1.reshape(T, D)
        for part in range(MOE_PARTS):
            tp = T // MOE_PARTS
            idx_p = slots_kt[:, part * tp:(part + 1) * tp].reshape(TOP_K * tp)
            yk = _gather_rows(y, idx_p).reshape(TOP_K, tp, D // 2)
            xt = _combine_call(yk, wts, shared.reshape(T, D), xt, gt_f, tm, part)
        x = xt.reshape(B, S, D)
    return x
```

```python
import functools

import jax
import jax.numpy as jnp
import numpy as np
from jax import lax
from jax.experimental import pallas as pl
from jax.experimental.pallas import tpu as pltpu
from jax.experimental.pallas import tpu_sc as plsc

D_MODEL = 1024
NSA_Q_HEADS = 8
NSA_KV_HEADS = 2
HEAD_DIM = 64
NSA_GROUP = NSA_Q_HEADS // NSA_KV_HEADS
CMP_LEN = 32
CMP_STRIDE = 16
CMP_HIDDEN = 256
SEL_LEN = 64
SEL_TOPN = 16
WINDOW = 512
ROPE_THETA = 10000.0
Q_WIDTH = NSA_Q_HEADS * HEAD_DIM
KV_WIDTH = NSA_KV_HEADS * HEAD_DIM
GM_GROUPS = 8
GM_GROUP_DIM = 64
GM_WIDTH = GM_GROUPS * GM_GROUP_DIM
GM_CHUNK = 128
N_EXPERTS = 256
TOP_K = 8
D_EXPERT = 256
D_SHARED = 256
ROUTE_SCALE = 2.5
EPS = 1e-6

LANES = 128
SEL_BIAS_WIDTH = 64
MASK_NEG = -1e30
SEL_NEG = -30000.0
SC_WINDOW = 128
MOE_GRAN = 128
MOE_ROWS = 9 * MOE_GRAN
VMEM_LIMIT = 56 * 1024 * 1024

F32 = jnp.float32
BF16 = jnp.bfloat16
HI = lax.Precision.HIGHEST


def _cparams(*sem):
    return pltpu.CompilerParams(dimension_semantics=sem, vmem_limit_bytes=VMEM_LIMIT)


def _dot(a, b, **kw):
    return jnp.dot(a, b, preferred_element_type=F32, **kw)


def _dot_nt(a, b, **kw):
    return lax.dot_general(a, b, (((1,), (1,)), ((), ())), preferred_element_type=F32, **kw)


def _gelu(x):
    return 0.5 * x * (1.0 + jnp.tanh(0.7978845608028654 * (x + 0.044715 * (x * x * x))))


def _sigmoid(x):
    return 1.0 / (1.0 + jnp.exp(-x))


def _silu(x):
    return x * _sigmoid(x)


_HI_MASK = np.uint32(0xFFFF0000)


def _pack_halves(a):
    w = a.shape[1] // 2
    bits = lax.bitcast_convert_type(a.astype(BF16).astype(F32), jnp.uint32)
    return (bits[:, w:] & _HI_MASK) | (bits[:, :w] >> 16)


def _unpack_halves(words):
    lo = lax.bitcast_convert_type(words << 16, F32)
    hi = lax.bitcast_convert_type(words & _HI_MASK, F32)
    return lo, hi


def _mod_kernel(c_ref, w_ref, b_ref, o_ref):
    c = c_ref[...]
    o_ref[0] = _dot(_silu(c), w_ref[0], precision=HI) + b_ref[0]


def _mod_call(c, w_mod, b_mod):
    L, D, N = w_mod.shape
    B = c.shape[0]
    tn = 1536
    return pl.pallas_call(
        _mod_kernel,
        grid=(L, N // tn),
        in_specs=[pl.BlockSpec((B, D), lambda l, j: (0, 0)),
                  pl.BlockSpec((1, D, tn), lambda l, j: (l, 0, j)),
                  pl.BlockSpec((1, 1, tn), lambda l, j: (l, 0, j))],
        out_specs=pl.BlockSpec((1, B, tn), lambda l, j: (l, 0, j)),
        out_shape=jax.ShapeDtypeStruct((L, B, N), F32),
        compiler_params=_cparams("arbitrary", "arbitrary"),
        name="mod",
    )(c, w_mod, b_mod.reshape(L, 1, N))


_C_Q = 0
_C_K = 512
_C_KC = 768
_C_VC = 896
_C_VS = 1024
_C_VW = 1152
_C_G = 1280
_C_U = 1536
_C_V = 2048
_C_GA = 2560
_C_GB = 3584
IN_COLS_P = 4608


def _dot_split(a, b):
    hi = a.astype(BF16)
    lo = (a - hi.astype(F32)).astype(BF16)
    return _dot(hi, b) + _dot(lo, b)


def _head_norm(z, bd):
    ms = _dot_split(z * z, bd)
    return z * lax.rsqrt(ms + EPS)


def _rope(z, cos, sin_lo, sin_hi):
    w = z.shape[-1]
    half = HEAD_DIM // 2
    return z * cos + pltpu.roll(z, w - half, 1) * sin_lo + pltpu.roll(z, half, 1) * sin_hi


def _tile_lanes(t, n):
    return t if n == 1 else jnp.concatenate([t] * n, axis=-1)


def _inproj_kernel(x_ref, sc_ref, sh_ref, w_ref, bdq_ref, bdk_ref, qg_ref, kg_ref,
                   cos_ref, sl_ref, shi_ref, lng_ref, lnb_ref,
                   qn_ref, qr_ref, ks_ref, kw_ref, vs_ref, vw_ref, kc_ref, vc_ref,
                   g_ref, u_ref, v_ref, ga_ref, gb_ref):
    tm = x_ref.shape[1]
    x = x_ref[0]
    ms = jnp.mean(x * x, axis=-1, keepdims=True)
    h = (x * lax.rsqrt(ms + EPS)) * (1.0 + sc_ref[0]) + sh_ref[0]
    hb = h.astype(BF16)

    def mm(lo, width):
        return _dot(hb, w_ref[:, lo:lo + width])

    cos, sl, shi = cos_ref[0], sl_ref[0], shi_ref[0]

    zq = mm(_C_Q, Q_WIDTH)
    qn = _head_norm(zq, bdq_ref[...]) * qg_ref[...]
    qr = _rope(qn, _tile_lanes(cos, 4), _tile_lanes(sl, 4), _tile_lanes(shi, 4))
    qn_ref[0] = qn.astype(BF16)
    qr_ref[0] = qr.astype(BF16)

    zk = mm(_C_K, 2 * KV_WIDTH)
    kn = _head_norm(zk, bdk_ref[...]) * kg_ref[...]
    kr = _rope(kn, _tile_lanes(cos, 2), _tile_lanes(sl, 2), _tile_lanes(shi, 2))
    lane = lax.broadcasted_iota(jnp.int32, (tm, LANES), 1)
    tok = pl.program_id(1) * tm + lax.broadcasted_iota(jnp.int32, (tm, LANES), 0)
    onehot = jnp.where(lane - HEAD_DIM == tok // SEL_LEN, 1.0, 0.0)
    ones_col = jnp.where(lane == HEAD_DIM, 1.0, 0.0)
    low = lane < HEAD_DIM
    zvs = mm(_C_VS, KV_WIDTH)
    zvw = mm(_C_VW, KV_WIDTH)
    zkc = mm(_C_KC, KV_WIDTH)
    zvc = mm(_C_VC, KV_WIDTH)
    for kv in range(NSA_KV_HEADS):
        def head(a):
            return a if kv == 0 else pltpu.roll(a, HEAD_DIM, 1)
        ks_ref[0, kv] = jnp.where(low, head(kr[:, :KV_WIDTH]), onehot).T.astype(BF16)
        kw_ref[0, kv] = jnp.where(low, head(kr[:, KV_WIDTH:]), 0.0).T.astype(BF16)
        vs_ref[0, kv] = jnp.where(low, head(zvs), ones_col).astype(BF16)
        vw_ref[0, kv] = jnp.where(low, head(zvw), ones_col).astype(BF16)
        kc_ref[0, kv] = head(zkc)[:, :HEAD_DIM]
        vc_ref[0, kv] = head(zvc)[:, :HEAD_DIM]

    zg = mm(_C_G, 2 * LANES)
    sg = _sigmoid(zg)
    g_ref[0, 0] = sg[:, :LANES]
    g_ref[0, 1] = sg[:, LANES:]

    u_ref[0] = _gelu(mm(_C_U, GM_WIDTH)).astype(BF16)
    gv = _gelu(mm(_C_V, GM_WIDTH))
    mu = jnp.mean(gv, axis=-1, keepdims=True)
    cen = gv - mu
    var = jnp.mean(cen * cen, axis=-1, keepdims=True)
    v_ref[0] = ((cen * lax.rsqrt(var + EPS)) * lng_ref[...] + lnb_ref[...]).astype(BF16)

    ga_ref[0] = _sigmoid(mm(_C_GA, D_MODEL)).astype(BF16)
    gb_ref[0] = _sigmoid(mm(_C_GB, D_MODEL)).astype(BF16)


def _block_diag_mean(width):
    idx = np.arange(width) // HEAD_DIM
    return jnp.asarray((idx[:, None] == idx[None, :]).astype(np.float32) / HEAD_DIM).astype(BF16)


def _inproj_call(x, sc, sh, w_p, qg, kg, cos, sl, shi, lng, lnb, tm):
    B, S, D = x.shape
    H = NSA_KV_HEADS
    full = lambda *shape: pl.BlockSpec(shape, lambda b, i: (0,) * len(shape))
    tok3 = lambda w: pl.BlockSpec((1, tm, w), lambda b, i: (b, i, 0))
    per_b = pl.BlockSpec((1, 1, D), lambda b, i: (b, 0, 0))
    kv4 = lambda w: pl.BlockSpec((1, H, tm, w), lambda b, i: (b, 0, i, 0))
    kt4 = pl.BlockSpec((1, H, LANES, tm), lambda b, i: (b, 0, 0, i))
    sds = jax.ShapeDtypeStruct
    out_shape = [
        sds((B, S, Q_WIDTH), BF16), sds((B, S, Q_WIDTH), BF16),
        sds((B, H, LANES, S), BF16), sds((B, H, LANES, S), BF16),
        sds((B, H, S, LANES), BF16), sds((B, H, S, LANES), BF16),
        sds((B, H, S, HEAD_DIM), F32), sds((B, H, S, HEAD_DIM), F32),
        sds((B, H, S, LANES), F32),
        sds((B, S, GM_WIDTH), BF16), sds((B, S, GM_WIDTH), BF16),
        sds((B, S, D), BF16), sds((B, S, D), BF16),
    ]
    out_specs = [
        tok3(Q_WIDTH), tok3(Q_WIDTH), kt4, kt4, kv4(LANES), kv4(LANES),
        kv4(HEAD_DIM), kv4(HEAD_DIM), kv4(LANES),
        tok3(GM_WIDTH), tok3(GM_WIDTH), tok3(D), tok3(D),
    ]
    return pl.pallas_call(
        _inproj_kernel,
        grid=(B, S // tm),
        in_specs=[tok3(D), per_b, per_b, full(D, IN_COLS_P),
                  full(Q_WIDTH, Q_WIDTH), full(2 * KV_WIDTH, 2 * KV_WIDTH),
                  full(1, Q_WIDTH), full(1, 2 * KV_WIDTH),
                  tok3(LANES), tok3(LANES), tok3(LANES),
                  full(1, GM_WIDTH), full(1, GM_WIDTH)],
        out_specs=out_specs,
        out_shape=out_shape,
        compiler_params=_cparams("arbitrary", "arbitrary"),
        name="in_proj",
    )(x, sc, sh, w_p, _block_diag_mean(Q_WIDTH), _block_diag_mean(2 * KV_WIDTH), qg, kg,
      cos, sl, shi, lng, lnb)


def _compress_kernel(kr_ref, vr_ref, w1k_ref, w2k_ref, pek_ref, w1v_ref, w2v_ref, pev_ref,
                     kg_ref, kc_ref, vc_ref):
    nc = kr_ref.shape[2]
    half = CMP_STRIDE * HEAD_DIM

    def mlp(raw, w1_ref, w2_ref, pe_ref):
        a = raw.astype(BF16)
        top = _dot(a, w1_ref[:half, :])
        bot = _dot(a, w1_ref[half:, :])
        pe = jnp.broadcast_to(pe_ref[...], (8, 2 * half)).astype(BF16)
        pe_row = _dot(pe, w1_ref[...])[0:1, :]
        hid = top + pltpu.roll(bot, nc - 1, 0) + pe_row
        return _dot(_gelu(hid).astype(BF16), w2_ref[...])

    kc = mlp(kr_ref[0, 0], w1k_ref, w2k_ref, pek_ref)
    ms = jnp.mean(kc * kc, axis=-1, keepdims=True)
    kc_ref[0, 0] = (kc * lax.rsqrt(ms + EPS) * kg_ref[...]).astype(BF16)
    vc_ref[0, 0] = mlp(vr_ref[0, 0], w1v_ref, w2v_ref, pev_ref).astype(BF16)


def _compress_call(kc_raw, vc_raw, w1k, w2k, pek, w1v, w2v, pev, kg):
    B, H, S, hd = kc_raw.shape
    nc = S // CMP_STRIDE
    feat = CMP_STRIDE * hd
    raw = pl.BlockSpec((1, 1, nc, feat), lambda b, h: (b, h, 0, 0))
    full = lambda *shape: pl.BlockSpec(shape, lambda b, h: (0,) * len(shape))
    out = pl.BlockSpec((1, 1, nc, hd), lambda b, h: (b, h, 0, 0))
    return pl.pallas_call(
        _compress_kernel,
        grid=(B, H),
        in_specs=[raw, raw, full(2 * feat, CMP_HIDDEN), full(CMP_HIDDEN, hd), full(1, 2 * feat),
                  full(2 * feat, CMP_HIDDEN), full(CMP_HIDDEN, hd), full(1, 2 * feat),
                  full(1, hd)],
        out_specs=[out, out],
        out_shape=[jax.ShapeDtypeStruct((B, H, nc, hd), BF16)] * 2,
        compiler_params=_cparams("arbitrary", "arbitrary"),
        name="compress",
    )(kc_raw.reshape(B, H, nc, feat), vc_raw.reshape(B, H, nc, feat),
      w1k, w2k, pek, w1v, w2v, pev, kg)


def _group_rows(a):
    return jnp.concatenate([a[:, g * HEAD_DIM:(g + 1) * HEAD_DIM] for g in range(NSA_GROUP)], axis=0)


def _nsa_cmp_kernel(qn_ref, qr_ref, kc_ref, vc_ref, ovl_ref, oc_ref, qa_ref, *, n_sel):
    tq = qn_ref.shape[1]
    nc = kc_ref.shape[2]
    G = NSA_GROUP
    q0 = pl.program_id(2) * tq
    q4 = _group_rows(qn_ref[0])
    s = _dot_nt(q4, kc_ref[0, 0])
    row = lax.broadcasted_iota(jnp.int32, (G, tq, nc), 1).reshape(G * tq, nc)
    col = lax.broadcasted_iota(jnp.int32, (G * tq, nc), 1)
    vis = col * CMP_STRIDE + (CMP_LEN - 1) <= q0 + row
    s = jnp.where(vis, s, MASK_NEG)
    m = jnp.max(s, axis=-1, keepdims=True)
    e = jnp.where(vis, jnp.exp(s - m), 0.0)
    p = e / jnp.maximum(jnp.sum(e, axis=-1, keepdims=True), 1e-30)
    oc = _dot(p.astype(BF16), vc_ref[0, 0])
    oc_ref[0, 0] = oc.reshape(G, tq, HEAD_DIM)

    psum = p[0:tq] + p[tq:2 * tq] + p[2 * tq:3 * tq] + p[3 * tq:4 * tq]
    p_hi = psum.astype(BF16)
    p_lo = (psum - p_hi.astype(F32)).astype(BF16)
    imp = (_dot_nt(ovl_ref[...], p_hi) + _dot_nt(ovl_ref[...], p_lo))[:SEL_BIAS_WIDTH]
    blk = lax.broadcasted_iota(jnp.int32, (SEL_BIAS_WIDTH, tq), 0)
    cur = (q0 + lax.broadcasted_iota(jnp.int32, (SEL_BIAS_WIDTH, tq), 1)) // SEL_LEN
    valid = blk <= cur
    forced = (blk == 0) | (blk == cur) | (blk == cur - 1)
    cand = valid & jnp.logical_not(forced)
    n_forced = jnp.minimum(cur, 2) + 1
    val = jnp.where(cand, imp, -1.0)
    cnt = jnp.zeros((SEL_BIAS_WIDTH, tq), F32)
    for j in range(n_sel):
        vj = jnp.broadcast_to(val[j:j + 1, :], (SEL_BIAS_WIDTH, tq))
        cnt = cnt + jnp.where(blk > j, jnp.where(vj >= val, 1.0, 0.0), jnp.where(vj > val, 1.0, 0.0))
    free = (min(SEL_TOPN, n_sel) - n_forced).astype(F32)
    sel = (forced & valid) | (cand & (cnt < free))
    sel_t = jnp.concatenate([jnp.where(sel, 1.0, 0.0), jnp.zeros((LANES - SEL_BIAS_WIDTH, tq), F32)], axis=0)
    sel_q = sel_t.T
    bias = jnp.where(sel_q > 0.5, 0.0, SEL_NEG)
    lane = lax.broadcasted_iota(jnp.int32, (tq, LANES), 1)
    bias_hi = pltpu.roll(bias, SEL_BIAS_WIDTH, 1)
    qr = qr_ref[0]
    for g in range(G):
        qg = qr[:, g * HEAD_DIM:(g + 1) * HEAD_DIM].astype(F32)
        qg = jnp.concatenate([qg, qg], axis=-1)
        qa_ref[0, 0, g] = jnp.where(lane < HEAD_DIM, qg, bias_hi).astype(BF16)


def _overlap_t(S):
    n_cmp = (S - CMP_LEN) // CMP_STRIDE + 1
    nc = S // CMP_STRIDE
    n_sel = S // SEL_LEN
    start = np.arange(nc) * CMP_STRIDE
    end = start + CMP_LEN - 1
    sel_start = np.arange(n_sel) * SEL_LEN
    ov = (start[None, :] <= sel_start[:, None] + SEL_LEN - 1) & (end[None, :] >= sel_start[:, None])
    ov = ov & (np.arange(nc) < n_cmp)[None, :]
    out = np.zeros((LANES, nc), np.float32)
    out[:n_sel] = ov.astype(np.float32)
    return jnp.asarray(out).astype(BF16)


def _nsa_cmp_call(qn, qr, kc, vc, tq):
    B, S, _ = qn.shape
    H, G = NSA_KV_HEADS, NSA_GROUP
    nc = kc.shape[2]
    n_sel = S // SEL_LEN
    assert n_sel <= SEL_BIAS_WIDTH and nc % LANES == 0
    qspec = pl.BlockSpec((1, tq, G * HEAD_DIM), lambda b, h, i: (b, i, h))
    cspec = pl.BlockSpec((1, 1, nc, HEAD_DIM), lambda b, h, i: (b, h, 0, 0))
    return pl.pallas_call(
        functools.partial(_nsa_cmp_kernel, n_sel=n_sel),
        grid=(B, H, S // tq),
        in_specs=[qspec, qspec, cspec, cspec, pl.BlockSpec((LANES, nc), lambda b, h, i: (0, 0))],
        out_specs=[pl.BlockSpec((1, 1, G, tq, HEAD_DIM), lambda b, h, i: (b, h, 0, i, 0)),
                   pl.BlockSpec((1, 1, G, tq, LANES), lambda b, h, i: (b, h, 0, i, 0))],
        out_shape=[jax.ShapeDtypeStruct((B, H, G, S, HEAD_DIM), F32),
                   jax.ShapeDtypeStruct((B, H, G, S, LANES), BF16)],
        compiler_params=_cparams("arbitrary", "arbitrary", "arbitrary"),
        name="nsa_cmp",
    )(qn, qr, kc, vc, _overlap_t(S))


SEL_CHUNK = 4


def _nsa_flash_kernel(qa_ref, ks_ref, vs_ref, kw_ref, vw_ref, oc_ref, g_ref, o_ref, m_scr, acc_scr):
    G = NSA_GROUP
    tq = qa_ref.shape[3]
    R = G * tq
    tk = tq
    i = pl.program_id(2)
    q0 = i * tq

    def rows_of(ref, j0, nt):
        return ref[0, 0, pl.ds(pl.multiple_of(j0 * tk, tk), nt * tk), :]

    def attend(k_ref, v_ref, j0, nt, visible, state):
        v = rows_of(v_ref, j0, nt)
        kt = k_ref[0, 0, :, pl.ds(pl.multiple_of(j0 * tk, tk), nt * tk)]
        s_all = _dot(qa_ref[0, 0].reshape(R, LANES), kt)
        ss = [s_all[g * tq:(g + 1) * tq] for g in range(G)]
        if visible is not None:
            qpos = q0 + lax.broadcasted_iota(jnp.int32, (tq, nt * tk), 0)
            kpos = j0 * tk + lax.broadcasted_iota(jnp.int32, (tq, nt * tk), 1)
            mask = visible(qpos, kpos)
        ps, alphas = [], []
        for g in range(G):
            s = ss[g] if visible is None else jnp.where(mask, ss[g], MASK_NEG)
            cols = [s[:, c * LANES:(c + 1) * LANES] for c in range(nt * tk // LANES)]
            m_new = jnp.max(functools.reduce(jnp.maximum, cols), axis=-1, keepdims=True)
            if state:
                m_prev = m_scr[g]
                m_new = jnp.maximum(m_prev, m_new)
                alphas.append(jnp.exp(m_prev - m_new))
                m_scr[g] = m_new
            ps.append(jnp.concatenate([jnp.exp(c - m_new) for c in cols], axis=-1).astype(BF16))
        pv = []
        for a in range(0, G, 2):
            both = _dot(jnp.concatenate([ps[a], ps[a + 1]], axis=0), v)
            pv += [both[:tq], both[tq:]]
        if not state:
            return pv
        for g in range(G):
            acc_scr[g] = alphas[g] * acc_scr[g] + pv[g]

    def normalise(acc):
        return acc[:, :HEAD_DIM] / acc[:, HEAD_DIM:HEAD_DIM + 1]

    m_scr[...] = jnp.full((G, tq, LANES), MASK_NEG, F32)
    acc_scr[...] = jnp.zeros((G, tq, LANES), F32)
    causal = lambda qpos, kpos: kpos <= qpos

    def sel_body(c, carry):
        attend(ks_ref, vs_ref, c * SEL_CHUNK, SEL_CHUNK, None, True)
        return carry

    n_full = i // SEL_CHUNK
    lax.fori_loop(0, n_full, sel_body, 0)
    attend(ks_ref, vs_ref, n_full * SEL_CHUNK, 2, causal, True)

    @pl.when(i % SEL_CHUNK >= 2)
    def _():
        attend(ks_ref, vs_ref, n_full * SEL_CHUNK + 2, 2, causal, True)

    o_sel = normalise(acc_scr[...].reshape(R, LANES))

    assert WINDOW == 2 * tk
    band = lambda qpos, kpos: (kpos <= qpos) & (kpos > qpos - WINDOW)
    o_win = normalise(jnp.concatenate(attend(kw_ref, vw_ref, jnp.maximum(i - 2, 0), 3, band, False), axis=0))

    o_cmp = oc_ref[0, 0].reshape(R, HEAD_DIM)
    gates = g_ref[0, 0]
    outs = []
    for g in range(G):
        r = slice(g * tq, (g + 1) * tq)
        outs.append(gates[:, 3 * g:3 * g + 1] * o_cmp[r]
                    + gates[:, 3 * g + 1:3 * g + 2] * o_sel[r]
                    + gates[:, 3 * g + 2:3 * g + 3] * o_win[r])
    o_ref[0] = jnp.concatenate(outs, axis=-1).astype(BF16)


def _nsa_flash_call(qa, ks, vs, kw, vw, oc, gates):
    B, H, G, S, _ = qa.shape
    tq = WINDOW // 2
    assert (S // tq) % SEL_CHUNK == 0
    kt = pl.BlockSpec((1, 1, LANES, S), lambda b, h, i: (b, h, 0, 0))
    vv = pl.BlockSpec((1, 1, S, LANES), lambda b, h, i: (b, h, 0, 0))
    return pl.pallas_call(
        _nsa_flash_kernel,
        grid=(B, H, S // tq),
        in_specs=[pl.BlockSpec((1, 1, G, tq, LANES), lambda b, h, i: (b, h, 0, i, 0)),
                  kt, vv, kt, vv,
                  pl.BlockSpec((1, 1, G, tq, HEAD_DIM), lambda b, h, i: (b, h, 0, i, 0)),
                  pl.BlockSpec((1, 1, tq, LANES), lambda b, h, i: (b, h, i, 0))],
        out_specs=pl.BlockSpec((1, tq, G * HEAD_DIM), lambda b, h, i: (b, i, h)),
        out_shape=jax.ShapeDtypeStruct((B, S, Q_WIDTH), BF16),
        scratch_shapes=[pltpu.VMEM((G, tq, LANES), F32), pltpu.VMEM((G, tq, LANES), F32)],
        compiler_params=_cparams("arbitrary", "arbitrary", "arbitrary"),
        name="nsa_flash",
    )(qa, ks, vs, kw, vw, oc, gates)


def _gmlp_kernel(u_ref, v_ref, ws_ref, bs_ref, o_ref):
    tm = u_ref.shape[1]
    C = GM_CHUNK
    r = lax.broadcasted_iota(jnp.int32, (C, C), 0)
    c = lax.broadcasted_iota(jnp.int32, (C, C), 1)
    causal = c <= r
    ws = [jnp.where(causal, ws_ref[g], 0.0).astype(BF16) for g in range(GM_GROUPS)]
    for n in range(tm // C):
        rows = slice(n * C, (n + 1) * C)
        vn = v_ref[0, rows, :]
        mixed = jnp.concatenate(
            [_dot(ws[g], vn[:, g * GM_GROUP_DIM:(g + 1) * GM_GROUP_DIM]) for g in range(GM_GROUPS)],
            axis=-1)
        o_ref[0, rows, :] = (u_ref[0, rows, :].astype(F32) * (mixed + bs_ref[...])).astype(BF16)


def _gmlp_call(u, v, ws, bs_full, tm):
    B, S, W = u.shape
    tok = pl.BlockSpec((1, tm, W), lambda b, i: (b, i, 0))
    return pl.pallas_call(
        _gmlp_kernel,
        grid=(B, S // tm),
        in_specs=[tok, tok,
                  pl.BlockSpec((GM_GROUPS, GM_CHUNK, GM_CHUNK), lambda b, i: (0, 0, 0)),
                  pl.BlockSpec((GM_CHUNK, W), lambda b, i: (0, 0))],
        out_specs=tok,
        out_shape=jax.ShapeDtypeStruct((B, S, W), BF16),
        compiler_params=_cparams("arbitrary", "arbitrary"),
        name="gmlp",
    )(u, v, ws, bs_full)


def _mixout_kernel(on_ref, og_ref, ga_ref, gb_ref, x_ref, gt_ref, sc_ref, sh_ref,
                   wpn_ref, wpg_ref, wo_ref, wr_ref, wgs_ref, wus_ref, wds_ref,
                   x1_ref, h2_ref, lg_ref, shd_ref):
    ya = _dot(on_ref[0], wpn_ref[...])
    yb = _dot(og_ref[0], wpg_ref[...])
    merged = ga_ref[0].astype(F32) * ya + gb_ref[0].astype(F32) * yb
    x1 = x_ref[0] + gt_ref[0] * _dot(merged.astype(BF16), wo_ref[...])
    x1_ref[0] = x1
    ms = jnp.mean(x1 * x1, axis=-1, keepdims=True)
    h2 = (x1 * lax.rsqrt(ms + EPS)) * (1.0 + sc_ref[0]) + sh_ref[0]
    hb = h2.astype(BF16)
    lg_ref[0] = _dot_nt(wr_ref[...], hb)
    h2_ref[0] = _pack_halves(hb)
    act = _silu(_dot(hb, wgs_ref[...])) * _dot(hb, wus_ref[...])
    shd_ref[0] = _dot(act.astype(BF16), wds_ref[...])


def _mixout_call(o_nsa, o_gm, ga, gb, x, gt, sc, sh, wpn, wpg, wo, wr, wgs, wus, wds, tm):
    B, S, D = x.shape
    tok = lambda w: pl.BlockSpec((1, tm, w), lambda b, i: (b, i, 0))
    per_b = pl.BlockSpec((1, 1, D), lambda b, i: (b, 0, 0))
    full = lambda a: pl.BlockSpec(a.shape, lambda b, i: (0,) * a.ndim)
    sds = jax.ShapeDtypeStruct
    return pl.pallas_call(
        _mixout_kernel,
        grid=(B, S // tm),
        in_specs=[tok(Q_WIDTH), tok(GM_WIDTH), tok(D), tok(D), tok(D), per_b, per_b, per_b,
                  full(wpn), full(wpg), full(wo), full(wr), full(wgs), full(wus), full(wds)],
        out_specs=[tok(D), tok(D // 2), pl.BlockSpec((1, N_EXPERTS, tm), lambda b, i: (b, 0, i)), tok(D)],
        out_shape=[sds((B, S, D), F32), sds((B, S, D // 2), jnp.uint32), sds((B, N_EXPERTS, S), F32),
                   sds((B, S, D), F32)],
        compiler_params=_cparams("arbitrary", "arbitrary"),
        name="mix_out",
    )(o_nsa, o_gm, ga, gb, x, gt, sc, sh, wpn, wpg, wo, wr, wgs, wus, wds)


def _route_kernel(lg_ref, br_ref, idx_ref, wt_ref, rank_ref, cnt_ref, run_scr):
    E, tm = lg_ref.shape[1], lg_ref.shape[2]

    @pl.when(pl.program_id(0) == 0)
    def _():
        run_scr[...] = jnp.zeros_like(run_scr)

    aff = _sigmoid(lg_ref[0])
    work = aff + br_ref[...]
    row = lax.broadcasted_iota(jnp.int32, (E, tm), 0).astype(F32)
    picked = jnp.zeros((E, tm), F32)
    idxs, tops = [], []
    for _ in range(TOP_K):
        m = jnp.max(work, axis=0, keepdims=True)
        idx = jnp.min(jnp.where(work == m, row, float(E)), axis=0, keepdims=True)
        hit = row == idx
        tops.append(jnp.sum(jnp.where(hit, aff, 0.0), axis=0, keepdims=True))
        idxs.append(idx)
        picked = jnp.where(hit, 1.0, picked)
        work = jnp.where(hit, -jnp.inf, work)
    total = functools.reduce(jnp.add, tops)
    r = lax.broadcasted_iota(jnp.int32, (tm, tm), 0)
    c = lax.broadcasted_iota(jnp.int32, (tm, tm), 1)
    before = _dot(picked.astype(BF16), jnp.where(r < c, 1.0, 0.0).astype(BF16)) + run_scr[...]
    ranks = [jnp.sum(jnp.where(row == idx, before, 0.0), axis=0, keepdims=True) for idx in idxs]
    run_scr[...] = run_scr[...] + jnp.sum(picked, axis=1, keepdims=True)
    cnt_ref[...] = run_scr[...]
    idx_ref[...] = jnp.concatenate(idxs, axis=0).astype(jnp.int32)
    rank_ref[...] = jnp.concatenate(ranks, axis=0).astype(jnp.int32)
    wt = jnp.concatenate([t / total * ROUTE_SCALE for t in tops]
                         + [jnp.zeros((LANES - TOP_K, tm), F32)], axis=0)
    eye = jnp.where(r == c, 1.0, 0.0).astype(BF16)
    cols = jnp.zeros((tm, LANES), F32)
    rest = wt
    for _ in range(3):
        part = rest.astype(BF16)
        cols = cols + _dot_nt(eye, part)
        rest = rest - part.astype(F32)
    wt_ref[...] = cols


def _route_call(logits_t, b_router, tm):
    B, E, S = logits_t.shape
    per_b = S // tm
    T = B * S
    kt = pl.BlockSpec((TOP_K, tm), lambda i: (0, i))
    col = pl.BlockSpec((E, 1), lambda i: (0, 0))
    sds = jax.ShapeDtypeStruct
    return pl.pallas_call(
        _route_kernel,
        grid=(T // tm,),
        in_specs=[pl.BlockSpec((1, E, tm), lambda i: (i // per_b, 0, i % per_b)), col],
        out_specs=[kt, pl.BlockSpec((tm, LANES), lambda i: (i, 0)), kt, col],
        out_shape=[sds((TOP_K, T), jnp.int32), sds((T, LANES), F32), sds((TOP_K, T), jnp.int32),
                   sds((E, 1), F32)],
        scratch_shapes=[pltpu.VMEM((E, 1), F32)],
        compiler_params=_cparams("arbitrary"),
        name="route",
    )(logits_t, b_router.reshape(E, 1))


def _slot_kernel(idx_ref, rank_ref, ps_ref, o_ref):
    tm = idx_ref.shape[1]
    E = ps_ref.shape[0]
    row = lax.broadcasted_iota(jnp.int32, (E, tm), 0)
    idx, rank, ps = idx_ref[...], rank_ref[...], ps_ref[...]
    base = [jnp.sum(jnp.where(row == idx[k:k + 1, :], ps, 0.0), axis=0, keepdims=True) for k in range(TOP_K)]
    o_ref[...] = jnp.concatenate(base, axis=0).astype(jnp.int32) + rank


def _slot_call(idx, rank, pad_start, tm):
    T = idx.shape[1]
    E = pad_start.shape[0]
    kt = pl.BlockSpec((TOP_K, tm), lambda i: (0, i))
    return pl.pallas_call(
        _slot_kernel,
        grid=(T // tm,),
        in_specs=[kt, kt, pl.BlockSpec((E, 1), lambda i: (0, 0))],
        out_specs=kt,
        out_shape=jax.ShapeDtypeStruct((TOP_K, T), jnp.int32),
        compiler_params=_cparams("arbitrary"),
        name="slot",
    )(idx, rank, pad_start.astype(F32).reshape(E, 1))


def _sc_mesh():
    mesh = plsc.VectorSubcoreMesh(core_axis_name="core", subcore_axis_name="subcore")
    return mesh, mesh.num_cores * mesh.num_subcores


def _sc_worker(mesh):
    return lax.axis_index("subcore") * mesh.num_cores + lax.axis_index("core")


def _scatter_rows(rows, indices_by_k, out_ref):
    n, width = rows.shape
    mesh, workers = _sc_mesh()
    per_worker = n // workers
    assert per_worker * workers == n and per_worker % SC_WINDOW == 0

    @pl.kernel(out_type=(), mesh=mesh, name="scatter_rows",
               scratch_types=[pltpu.VMEM((SC_WINDOW,), jnp.int32), pltpu.VMEM((SC_WINDOW, width), rows.dtype)])
    def scatter(rows_hbm, idx_hbm, out_hbm, idx_v, rows_v):
        worker = _sc_worker(mesh)

        @pl.loop(0, per_worker // SC_WINDOW)
        def _(j):
            base = pl.multiple_of(worker * per_worker + j * SC_WINDOW, SC_WINDOW)
            pltpu.sync_copy(rows_hbm.at[pl.ds(base, SC_WINDOW)], rows_v)
            for k in range(TOP_K):
                pltpu.sync_copy(idx_hbm.at[pl.ds(pl.multiple_of(k * n + base, SC_WINDOW), SC_WINDOW)], idx_v)
                pltpu.sync_copy(rows_v, out_hbm.at[idx_v])

    scatter(rows, indices_by_k, out_ref)


def _expert_kernel(ie_ref, ir_ref, ig_ref, n_ref, x_hbm, wg_ref, wu_ref, wd_ref, y_hbm,
                   xbuf, ybuf, act_scr, xsem, ysem):
    w = pl.program_id(0)
    n = n_ref[0]
    R = xbuf.shape[1]
    last = ie_ref.shape[0] - 1

    def x_copy(item, slot):
        row = pl.multiple_of(ir_ref[jnp.minimum(item, last)], MOE_GRAN)
        return pltpu.make_async_copy(x_hbm.at[pl.ds(row, R), :], xbuf.at[slot], xsem.at[slot])

    def y_copy(item, slot, gi):
        row = pl.multiple_of(ir_ref[jnp.minimum(item, last)] + gi * MOE_GRAN, MOE_GRAN)
        src = ybuf.at[slot, pl.ds(pl.multiple_of(gi * MOE_GRAN, MOE_GRAN), MOE_GRAN), :]
        return pltpu.make_async_copy(src, y_hbm.at[pl.ds(row, MOE_GRAN), :], ysem.at[slot])

    def for_granules(item, fn):
        def body(gi, carry):
            fn(gi)
            return carry
        lax.fori_loop(0, ig_ref[jnp.clip(item, 0, last)], body, 0)

    def gate_up(slot):
        lo, hi = _unpack_halves(xbuf[slot])
        x = jnp.concatenate([lo, hi], axis=1).astype(BF16)
        a = _dot(x, wg_ref[0, 0].astype(BF16))
        b = _dot(x, wu_ref[0, 0].astype(BF16))
        return (_silu(a) * b).astype(BF16)

    def down(act):
        return _pack_halves(_dot(act, wd_ref[0, 0].astype(BF16)))

    nx = xbuf.shape[0]
    cur, prv = w % nx, (w + 1) % 2

    @pl.when(w == 0)
    def _():
        for j in range(nx - 1):
            @pl.when(j < n)
            def _():
                x_copy(j, j).start()

    @pl.when(w < n)
    def _():
        x_copy(w, cur).wait()

    @pl.when(w + nx - 1 < n)
    def _():
        x_copy(w + nx - 1, (w + nx - 1) % nx).start()

    @pl.when((w >= 3) & (w - 3 < n))
    def _():
        for_granules(w - 3, lambda gi: y_copy(w - 3, prv, gi).wait())

    @pl.when(w == 0)
    def _():
        act_scr[...] = gate_up(0)

    @pl.when((w >= 1) & (w < n))
    def _():
        prev = act_scr[...]
        ybuf[prv] = down(prev)
        act_scr[...] = gate_up(cur)

    @pl.when((w >= 1) & (w == n))
    def _():
        ybuf[prv] = down(act_scr[...])

    @pl.when((w >= 1) & (w <= n))
    def _():
        for_granules(w - 1, lambda gi: y_copy(w - 1, prv, gi).start())


def _expert_call(layer, item_e, item_row, item_ng, n_items, xs, wg, wu, wd):
    n_alloc, W = xs.shape
    R = MOE_ROWS
    nw = item_e.shape[0]
    D, F = wg.shape[2], wg.shape[3]
    cur = lambda w, ie, ir, ig, n: (layer, ie[jnp.minimum(w, nw - 1)], 0, 0)
    prev = lambda w, ie, ir, ig, n: (layer, ie[jnp.clip(w - 1, 0, nw - 1)], 0, 0)
    grid_spec = pltpu.PrefetchScalarGridSpec(
        num_scalar_prefetch=4,
        grid=(nw + 3,),
        in_specs=[pl.BlockSpec(memory_space=pl.ANY),
                  pl.BlockSpec((1, 1, D, F), cur), pl.BlockSpec((1, 1, D, F), cur),
                  pl.BlockSpec((1, 1, F, D), prev)],
        out_specs=pl.BlockSpec(memory_space=pl.ANY),
        scratch_shapes=[pltpu.VMEM((3, R, W), jnp.uint32), pltpu.VMEM((2, R, W), jnp.uint32),
                        pltpu.VMEM((R, F), BF16),
                        pltpu.SemaphoreType.DMA((3,)), pltpu.SemaphoreType.DMA((2,))],
    )
    return pl.pallas_call(
        _expert_kernel,
        grid_spec=grid_spec,
        out_shape=jax.ShapeDtypeStruct((n_alloc, W), jnp.uint32),
        input_output_aliases={4: 0},
        compiler_params=pltpu.CompilerParams(dimension_semantics=("arbitrary",), vmem_limit_bytes=VMEM_LIMIT,
                                             has_side_effects=True),
        name="expert",
    )(item_e, item_row, item_ng, n_items, xs, wg, wu, wd)


def _gather_rows(table, indices):
    n = indices.shape[0]
    width = table.shape[1]
    mesh, workers = _sc_mesh()
    per_worker = n // workers
    assert per_worker * workers == n and per_worker % SC_WINDOW == 0

    half = SC_WINDOW // 2
    steps = per_worker // SC_WINDOW
    rows_t = pltpu.VMEM((half, width), table.dtype)

    @pl.kernel(out_type=jax.ShapeDtypeStruct((n, width), table.dtype), mesh=mesh, name="gather_rows",
               scratch_types=[pltpu.VMEM((SC_WINDOW,), jnp.int32), rows_t, rows_t] + [pltpu.SemaphoreType.DMA] * 4)
    def gather(table_hbm, idx_hbm, out_hbm, idx_v, rows_a, rows_b, gsem_a, gsem_b, wsem_a, wsem_b):
        worker = _sc_worker(mesh)
        first = worker * per_worker
        halves = ((rows_a, gsem_a, wsem_a, 0), (rows_b, gsem_b, wsem_b, half))

        def write(buf, wsem, row):
            return pltpu.make_async_copy(buf, out_hbm.at[pl.ds(pl.multiple_of(row, half), half)], wsem)

        @pl.loop(0, steps)
        def _(j):
            base = pl.multiple_of(first + j * SC_WINDOW, SC_WINDOW)
            pltpu.sync_copy(idx_hbm.at[pl.ds(base, SC_WINDOW)], idx_v)
            gathers = []
            for buf, gsem, wsem, off in halves:
                @pl.when(j > 0)
                def _():
                    write(buf, wsem, base).wait()
                gathers.append(pltpu.async_copy(table_hbm.at[idx_v.at[pl.ds(off, half)]], buf, gsem))
            for (buf, gsem, wsem, off), g in zip(halves, gathers):
                g.wait()
                write(buf, wsem, base + off).start()

        for buf, gsem, wsem, off in halves:
            write(buf, wsem, first).wait()

    return gather(table, indices)


def _combine_kernel(y_ref, wt_ref, shd_ref, x_ref, gt_ref, o_ref):
    wt = wt_ref[...]
    acc_lo = acc_hi = None
    for k in range(TOP_K):
        lo, hi = _unpack_halves(y_ref[k])
        w = wt[:, k:k + 1]
        acc_lo = w * lo if k == 0 else acc_lo + w * lo
        acc_hi = w * hi if k == 0 else acc_hi + w * hi
    routed = jnp.concatenate([acc_lo, acc_hi], axis=1)
    o_ref[...] = x_ref[...] + gt_ref[0] * (routed + shd_ref[...])


def _combine_call(yk, wts, shared, x1, gt, tm):
    T, D = x1.shape
    B = gt.shape[0]
    per_b = T // B // tm
    tok = lambda w: pl.BlockSpec((tm, w), lambda i: (i, 0))
    return pl.pallas_call(
        _combine_kernel,
        grid=(T // tm,),
        in_specs=[pl.BlockSpec((TOP_K, tm, yk.shape[2]), lambda i: (0, i, 0)),
                  tok(LANES), tok(D), tok(D),
                  pl.BlockSpec((1, 1, D), lambda i: (i // per_b, 0, 0))],
        out_specs=tok(D),
        out_shape=jax.ShapeDtypeStruct((T, D), F32),
        compiler_params=_cparams("arbitrary"),
        name="combine",
    )(yk, wts, shared, x1, gt)


def _rope_tables(positions):
    half = HEAD_DIM // 2
    inv = ROPE_THETA ** (-jnp.arange(half, dtype=F32) / half)
    ang = positions.astype(F32)[..., None] * inv
    cos, sin, zero = jnp.cos(ang), jnp.sin(ang), jnp.zeros_like(ang)
    cos_t = jnp.concatenate([cos, cos] * 2, axis=-1)
    sin_lo = jnp.concatenate([-sin, zero] * 2, axis=-1)
    sin_hi = jnp.concatenate([zero, sin] * 2, axis=-1)
    return cos_t, sin_lo, sin_hi


def _reorder_w_in(w):
    o = np.cumsum([0, Q_WIDTH] + [KV_WIDTH] * 6 + [NSA_Q_HEADS * 3, GM_WIDTH, GM_WIDTH, D_MODEL, D_MODEL])
    q, kc, vc, ks, vs, kw, vw, g, u, v, ga, gb = [w[:, o[i]:o[i + 1]] for i in range(12)]
    per = NSA_GROUP * 3
    pad = jnp.zeros((w.shape[0], LANES - per), w.dtype)
    return jnp.concatenate([q, ks, kw, kc, vc, vs, vw, g[:, :per], pad, g[:, per:], pad, u, v, ga, gb], axis=1)


def _owner(ends, pos):
    return jnp.minimum(jnp.sum((ends[None, :] <= pos[:, None]).astype(jnp.int32), axis=1), ends.shape[0] - 1)


def _lookup(table, idx):
    hit = idx[:, None] == jnp.arange(table.shape[0], dtype=jnp.int32)[None, :]
    return jnp.sum(jnp.where(hit, table[None, :], 0), axis=1)


def _expert_plan(counts, n_items_max):
    per_item = MOE_ROWS // MOE_GRAN
    counts = counts.astype(jnp.int32)
    gran = (counts + MOE_GRAN - 1) // MOE_GRAN
    gran_start = jnp.cumsum(gran) - gran
    items = (gran + per_item - 1) // per_item
    item_end = jnp.cumsum(items)
    w = jnp.arange(n_items_max, dtype=jnp.int32)
    ie = _owner(item_end, w)
    part = w - (_lookup(item_end, ie) - _lookup(items, ie))
    live = w < item_end[-1]
    item_ng = jnp.where(live, jnp.clip(_lookup(gran, ie) - part * per_item, 0, per_item), 0).astype(jnp.int32)
    item_row = jnp.where(live, (_lookup(gran_start, ie) + part * per_item) * MOE_GRAN, 0).astype(jnp.int32)
    return gran_start * MOE_GRAN, ie.astype(jnp.int32), item_row, item_ng, item_end[-1:].astype(jnp.int32)


def kernel(x, c, positions, w_mod, b_mod, w_in, q_gain, k_gain, cmp_pos_k, cmp_pos_v, cmp_w1_k, cmp_w2_k, cmp_w1_v, cmp_w2_v, gm_ln_g, gm_ln_b, gm_ws, gm_bs, w_proj_nsa, w_proj_gm, w_out, w_router, b_router, w_gate_e, w_up_e, w_down_e, w_gate_sh, w_up_sh, w_down_sh):
    B, S, D = x.shape
    L = w_mod.shape[0]
    T = B * S
    tm = 256
    tm_dense = 512
    scale = HEAD_DIM ** -0.5
    cos_t, sin_lo, sin_hi = _rope_tables(positions)
    mod = _mod_call(c, w_mod, b_mod)
    n_alloc = T * TOP_K + N_EXPERTS * MOE_GRAN + MOE_ROWS
    n_items_max = N_EXPERTS + T * TOP_K // MOE_ROWS + 1
    rows_buf = jnp.zeros((n_alloc, D // 2), jnp.uint32)

    for l in range(L):
        sh_a, sc_a, gt_a, sh_f, sc_f, gt_f = [mod[l, :, i * D:(i + 1) * D].reshape(B, 1, D) for i in range(6)]
        qg = (jnp.tile(q_gain[l], NSA_Q_HEADS) * scale).reshape(1, Q_WIDTH)
        kg = jnp.tile(k_gain[l], 2 * NSA_KV_HEADS).reshape(1, 2 * KV_WIDTH)
        (qn, qr, ks, kw, vs, vw, kc_raw, vc_raw, gates, u, v, ga, gb) = _inproj_call(
            x, sc_a, sh_a, _reorder_w_in(w_in[l]).astype(BF16), qg, kg, cos_t, sin_lo, sin_hi,
            gm_ln_g[l].reshape(1, GM_WIDTH), gm_ln_b[l].reshape(1, GM_WIDTH), tm_dense)
        kc, vc = _compress_call(
            kc_raw, vc_raw, cmp_w1_k[l].astype(BF16), cmp_w2_k[l].astype(BF16), cmp_pos_k[l].reshape(1, -1),
            cmp_w1_v[l].astype(BF16), cmp_w2_v[l].astype(BF16), cmp_pos_v[l].reshape(1, -1),
            k_gain[l].reshape(1, HEAD_DIM))
        o_cmp, q_aug = _nsa_cmp_call(qn, qr, kc, vc, WINDOW // 2)
        o_nsa = _nsa_flash_call(q_aug, ks, vs, kw, vw, o_cmp, gates)
        bs_full = jnp.repeat(gm_bs[l].T, GM_GROUP_DIM, axis=1)
        o_gm = _gmlp_call(u, v, gm_ws[l], bs_full, 512)
        x1, h2, logits, shared = _mixout_call(
            o_nsa, o_gm, ga, gb, x, gt_a, sc_f, sh_f,
            w_proj_nsa[l].astype(BF16), w_proj_gm[l].astype(BF16), w_out[l].astype(BF16), w_router[l].T.astype(BF16),
            w_gate_sh[l].astype(BF16), w_up_sh[l].astype(BF16), w_down_sh[l].astype(BF16), tm_dense)
        idx, wts, rank, counts = _route_call(logits, b_router[l], tm)
        row_start, item_e, item_row, item_ng, n_items = _expert_plan(counts[:, 0], n_items_max)
        slots_by_k = _slot_call(idx, rank, row_start, tm).reshape(TOP_K * T)
        xs_ref = jax.new_ref(rows_buf)
        _scatter_rows(h2.reshape(T, D // 2), slots_by_k, xs_ref)
        y = _expert_call(l, item_e, item_row, item_ng, n_items, jax.freeze(xs_ref), w_gate_e, w_up_e, w_down_e)
        yk = _gather_rows(y, slots_by_k).reshape(TOP_K, T, D // 2)
        rows_buf = y
        x = _combine_call(yk, wts, shared.reshape(T, D), x1.reshape(T, D), gt_f, tm).reshape(B, S, D)
    return x
```

```python
import functools

import jax
import jax.numpy as jnp
import numpy as np
from jax import lax
from jax.experimental import pallas as pl
from jax.experimental.pallas import tpu as pltpu
from jax.experimental.pallas import tpu_sc as plsc

D_MODEL = 1024
NSA_Q_HEADS = 8
NSA_KV_HEADS = 2
HEAD_DIM = 64
NSA_GROUP = NSA_Q_HEADS // NSA_KV_HEADS
CMP_LEN = 32
CMP_STRIDE = 16
CMP_HIDDEN = 256
SEL_LEN = 64
SEL_TOPN = 16
WINDOW = 512
ROPE_THETA = 10000.0
Q_WIDTH = NSA_Q_HEADS * HEAD_DIM
KV_WIDTH = NSA_KV_HEADS * HEAD_DIM
GM_GROUPS = 8
GM_GROUP_DIM = 64
GM_WIDTH = GM_GROUPS * GM_GROUP_DIM
GM_CHUNK = 128
N_EXPERTS = 256
TOP_K = 8
D_EXPERT = 256
D_SHARED = 256
ROUTE_SCALE = 2.5
EPS = 1e-6

LANES = 128
SEL_BIAS_WIDTH = 64
MASK_NEG = -1e30
SEL_NEG = MASK_NEG
SC_WINDOW = 128
MOE_GRAN = 128
MOE_ROWS = 9 * MOE_GRAN
VMEM_LIMIT = 56 * 1024 * 1024

F32 = jnp.float32
BF16 = jnp.bfloat16
HI = lax.Precision.HIGHEST


def _cparams(*sem):
    return pltpu.CompilerParams(dimension_semantics=sem, vmem_limit_bytes=VMEM_LIMIT)


def _dot(a, b, **kw):
    return jnp.dot(a, b, preferred_element_type=F32, **kw)


def _dot_nt(a, b, **kw):
    return lax.dot_general(a, b, (((1,), (1,)), ((), ())), preferred_element_type=F32, **kw)


def _gelu(x):
    return 0.5 * x * (1.0 + jnp.tanh(0.7978845608028654 * (x + 0.044715 * (x * x * x))))


def _sigmoid(x):
    return 1.0 / (1.0 + jnp.exp(-x))


def _silu(x):
    return x * _sigmoid(x)


_HI_MASK = np.uint32(0xFFFF0000)


def _pack_halves(a):
    w = a.shape[1] // 2
    bits = lax.bitcast_convert_type(a.astype(BF16).astype(F32), jnp.uint32)
    return (bits[:, w:] & _HI_MASK) | (bits[:, :w] >> 16)


def _unpack_halves(words):
    lo = lax.bitcast_convert_type(words << 16, F32)
    hi = lax.bitcast_convert_type(words & _HI_MASK, F32)
    return lo, hi


def _mod_kernel(c_ref, w_ref, b_ref, o_ref):
    c = c_ref[...]
    o_ref[0] = _dot(_silu(c), w_ref[0], precision=HI) + b_ref[0]


def _mod_call(c, w_mod, b_mod):
    L, D, N = w_mod.shape
    B = c.shape[0]
    tn = 1536
    return pl.pallas_call(
        _mod_kernel,
        grid=(L, N // tn),
        in_specs=[pl.BlockSpec((B, D), lambda l, j: (0, 0)),
                  pl.BlockSpec((1, D, tn), lambda l, j: (l, 0, j)),
                  pl.BlockSpec((1, 1, tn), lambda l, j: (l, 0, j))],
        out_specs=pl.BlockSpec((1, B, tn), lambda l, j: (l, 0, j)),
        out_shape=jax.ShapeDtypeStruct((L, B, N), F32),
        compiler_params=_cparams("arbitrary", "arbitrary"),
        name="mod",
    )(c, w_mod, b_mod.reshape(L, 1, N))


_C_Q = 0
_C_K = 512
_C_KC = 768
_C_VC = 896
_C_VS = 1024
_C_VW = 1152
_C_G = 1280
_C_U = 1536
_C_V = 2048
_C_GA = 2560
_C_GB = 3584
IN_COLS_P = 4608


def _dot_split(a, b):
    hi = a.astype(BF16)
    lo = (a - hi.astype(F32)).astype(BF16)
    return _dot(hi, b) + _dot(lo, b)


def _head_norm(z, bd):
    ms = _dot_split(z * z, bd)
    return z * lax.rsqrt(ms + EPS)


def _rope(z, cos, sin_lo, sin_hi):
    w = z.shape[-1]
    half = HEAD_DIM // 2
    return z * cos + pltpu.roll(z, w - half, 1) * sin_lo + pltpu.roll(z, half, 1) * sin_hi


def _tile_lanes(t, n):
    return t if n == 1 else jnp.concatenate([t] * n, axis=-1)


def _inproj_kernel(x_ref, sc_ref, sh_ref, w_ref, bdq_ref, bdk_ref, qg_ref, kg_ref,
                   cos_ref, sl_ref, shi_ref, lng_ref, lnb_ref,
                   qn_ref, qr_ref, ks_ref, kw_ref, vs_ref, vw_ref, kc_ref, vc_ref,
                   g_ref, u_ref, v_ref, ga_ref, gb_ref):
    tm = x_ref.shape[1]
    x = x_ref[0]
    ms = jnp.mean(x * x, axis=-1, keepdims=True)
    h = (x * lax.rsqrt(ms + EPS)) * (1.0 + sc_ref[0]) + sh_ref[0]
    hb = h.astype(BF16)

    def mm(lo, width):
        return _dot(hb, w_ref[:, lo:lo + width])

    cos, sl, shi = cos_ref[0], sl_ref[0], shi_ref[0]

    zq = mm(_C_Q, Q_WIDTH)
    qn = _head_norm(zq, bdq_ref[...]) * qg_ref[...]
    qr = _rope(qn, _tile_lanes(cos, 4), _tile_lanes(sl, 4), _tile_lanes(shi, 4))
    qn_ref[0] = qn.astype(BF16)
    qr_ref[0] = qr.astype(BF16)

    zk = mm(_C_K, 2 * KV_WIDTH)
    kn = _head_norm(zk, bdk_ref[...]) * kg_ref[...]
    kr = _rope(kn, _tile_lanes(cos, 2), _tile_lanes(sl, 2), _tile_lanes(shi, 2))
    lane = lax.broadcasted_iota(jnp.int32, (tm, LANES), 1)
    tok = pl.program_id(1) * tm + lax.broadcasted_iota(jnp.int32, (tm, LANES), 0)
    onehot = jnp.where(lane - HEAD_DIM == tok // SEL_LEN, 1.0, 0.0)
    ones_col = jnp.where(lane == HEAD_DIM, 1.0, 0.0)
    low = lane < HEAD_DIM
    zvs = mm(_C_VS, KV_WIDTH)
    zvw = mm(_C_VW, KV_WIDTH)
    zkc = mm(_C_KC, KV_WIDTH)
    zvc = mm(_C_VC, KV_WIDTH)
    for kv in range(NSA_KV_HEADS):
        def head(a):
            return a if kv == 0 else pltpu.roll(a, HEAD_DIM, 1)
        ks_ref[0, kv] = jnp.where(low, head(kr[:, :KV_WIDTH]), onehot).T.astype(BF16)
        kw_ref[0, kv] = jnp.where(low, head(kr[:, KV_WIDTH:]), 0.0).T.astype(BF16)
        vs_ref[0, kv] = jnp.where(low, head(zvs), ones_col).astype(BF16)
        vw_ref[0, kv] = jnp.where(low, head(zvw), ones_col).astype(BF16)
        kc_ref[0, kv] = head(zkc)[:, :HEAD_DIM]
        vc_ref[0, kv] = head(zvc)[:, :HEAD_DIM]

    zg = mm(_C_G, 2 * LANES)
    sg = _sigmoid(zg)
    g_ref[0, 0] = sg[:, :LANES]
    g_ref[0, 1] = sg[:, LANES:]

    u_ref[0] = _gelu(mm(_C_U, GM_WIDTH)).astype(BF16)
    gv = _gelu(mm(_C_V, GM_WIDTH))
    mu = jnp.mean(gv, axis=-1, keepdims=True)
    cen = gv - mu
    var = jnp.mean(cen * cen, axis=-1, keepdims=True)
    v_ref[0] = ((cen * lax.rsqrt(var + EPS)) * lng_ref[...] + lnb_ref[...]).astype(BF16)

    ga_ref[0] = _sigmoid(mm(_C_GA, D_MODEL)).astype(BF16)
    gb_ref[0] = _sigmoid(mm(_C_GB, D_MODEL)).astype(BF16)


def _block_diag_mean(width):
    idx = np.arange(width) // HEAD_DIM
    return jnp.asarray((idx[:, None] == idx[None, :]).astype(np.float32) / HEAD_DIM).astype(BF16)


def _inproj_call(x, sc, sh, w_p, qg, kg, cos, sl, shi, lng, lnb, tm):
    B, S, D = x.shape
    H = NSA_KV_HEADS
    full = lambda *shape: pl.BlockSpec(shape, lambda b, i: (0,) * len(shape))
    tok3 = lambda w: pl.BlockSpec((1, tm, w), lambda b, i: (b, i, 0))
    per_b = pl.BlockSpec((1, 1, D), lambda b, i: (b, 0, 0))
    kv4 = lambda w: pl.BlockSpec((1, H, tm, w), lambda b, i: (b, 0, i, 0))
    kt4 = pl.BlockSpec((1, H, LANES, tm), lambda b, i: (b, 0, 0, i))
    sds = jax.ShapeDtypeStruct
    out_shape = [
        sds((B, S, Q_WIDTH), BF16), sds((B, S, Q_WIDTH), BF16),
        sds((B, H, LANES, S), BF16), sds((B, H, LANES, S), BF16),
        sds((B, H, S, LANES), BF16), sds((B, H, S, LANES), BF16),
        sds((B, H, S, HEAD_DIM), F32), sds((B, H, S, HEAD_DIM), F32),
        sds((B, H, S, LANES), F32),
        sds((B, S, GM_WIDTH), BF16), sds((B, S, GM_WIDTH), BF16),
        sds((B, S, D), BF16), sds((B, S, D), BF16),
    ]
    out_specs = [
        tok3(Q_WIDTH), tok3(Q_WIDTH), kt4, kt4, kv4(LANES), kv4(LANES),
        kv4(HEAD_DIM), kv4(HEAD_DIM), kv4(LANES),
        tok3(GM_WIDTH), tok3(GM_WIDTH), tok3(D), tok3(D),
    ]
    return pl.pallas_call(
        _inproj_kernel,
        grid=(B, S // tm),
        in_specs=[tok3(D), per_b, per_b, full(D, IN_COLS_P),
                  full(Q_WIDTH, Q_WIDTH), full(2 * KV_WIDTH, 2 * KV_WIDTH),
                  full(1, Q_WIDTH), full(1, 2 * KV_WIDTH),
                  tok3(LANES), tok3(LANES), tok3(LANES),
                  full(1, GM_WIDTH), full(1, GM_WIDTH)],
        out_specs=out_specs,
        out_shape=out_shape,
        compiler_params=_cparams("arbitrary", "arbitrary"),
        name="in_proj",
    )(x, sc, sh, w_p, _block_diag_mean(Q_WIDTH), _block_diag_mean(2 * KV_WIDTH), qg, kg,
      cos, sl, shi, lng, lnb)


def _compress_kernel(kr_ref, vr_ref, w1k_ref, w2k_ref, pek_ref, w1v_ref, w2v_ref, pev_ref,
                     kg_ref, kc_ref, vc_ref):
    nc = kr_ref.shape[2]
    half = CMP_STRIDE * HEAD_DIM

    def mlp(raw, w1_ref, w2_ref, pe_ref):
        a = raw.astype(BF16)
        top = _dot(a, w1_ref[:half, :])
        bot = _dot(a, w1_ref[half:, :])
        pe = jnp.broadcast_to(pe_ref[...], (8, 2 * half)).astype(BF16)
        pe_row = _dot(pe, w1_ref[...])[0:1, :]
        hid = top + pltpu.roll(bot, nc - 1, 0) + pe_row
        return _dot(_gelu(hid).astype(BF16), w2_ref[...])

    kc = mlp(kr_ref[0, 0], w1k_ref, w2k_ref, pek_ref)
    ms = jnp.mean(kc * kc, axis=-1, keepdims=True)
    kc_ref[0, 0] = (kc * lax.rsqrt(ms + EPS) * kg_ref[...]).astype(BF16)
    vc_ref[0, 0] = mlp(vr_ref[0, 0], w1v_ref, w2v_ref, pev_ref).astype(BF16)


def _compress_call(kc_raw, vc_raw, w1k, w2k, pek, w1v, w2v, pev, kg):
    B, H, S, hd = kc_raw.shape
    nc = S // CMP_STRIDE
    feat = CMP_STRIDE * hd
    raw = pl.BlockSpec((1, 1, nc, feat), lambda b, h: (b, h, 0, 0))
    full = lambda *shape: pl.BlockSpec(shape, lambda b, h: (0,) * len(shape))
    out = pl.BlockSpec((1, 1, nc, hd), lambda b, h: (b, h, 0, 0))
    return pl.pallas_call(
        _compress_kernel,
        grid=(B, H),
        in_specs=[raw, raw, full(2 * feat, CMP_HIDDEN), full(CMP_HIDDEN, hd), full(1, 2 * feat),
                  full(2 * feat, CMP_HIDDEN), full(CMP_HIDDEN, hd), full(1, 2 * feat),
                  full(1, hd)],
        out_specs=[out, out],
        out_shape=[jax.ShapeDtypeStruct((B, H, nc, hd), BF16)] * 2,
        compiler_params=_cparams("arbitrary", "arbitrary"),
        name="compress",
    )(kc_raw.reshape(B, H, nc, feat), vc_raw.reshape(B, H, nc, feat),
      w1k, w2k, pek, w1v, w2v, pev, kg)


def _group_rows(a):
    return jnp.concatenate([a[:, g * HEAD_DIM:(g + 1) * HEAD_DIM] for g in range(NSA_GROUP)], axis=0)


def _nsa_cmp_kernel(qn_ref, qr_ref, kc_ref, vc_ref, ovl_ref, oc_ref, qa_ref, *, n_sel):
    tq = qn_ref.shape[1]
    nc = kc_ref.shape[2]
    G = NSA_GROUP
    q0 = pl.program_id(2) * tq
    q4 = _group_rows(qn_ref[0])
    s = _dot_nt(q4, kc_ref[0, 0])
    row = lax.broadcasted_iota(jnp.int32, (G, tq, nc), 1).reshape(G * tq, nc)
    col = lax.broadcasted_iota(jnp.int32, (G * tq, nc), 1)
    vis = col * CMP_STRIDE + (CMP_LEN - 1) <= q0 + row
    s = jnp.where(vis, s, MASK_NEG)
    m = jnp.max(s, axis=-1, keepdims=True)
    e = jnp.where(vis, jnp.exp(s - m), 0.0)
    p = e / jnp.maximum(jnp.sum(e, axis=-1, keepdims=True), 1e-30)
    oc = _dot(p.astype(BF16), vc_ref[0, 0])
    oc_ref[0, 0] = oc.reshape(G, tq, HEAD_DIM)

    psum = p[0:tq] + p[tq:2 * tq] + p[2 * tq:3 * tq] + p[3 * tq:4 * tq]
    p_hi = psum.astype(BF16)
    p_lo = (psum - p_hi.astype(F32)).astype(BF16)
    imp = (_dot_nt(ovl_ref[...], p_hi) + _dot_nt(ovl_ref[...], p_lo))[:SEL_BIAS_WIDTH]
    blk = lax.broadcasted_iota(jnp.int32, (SEL_BIAS_WIDTH, tq), 0)
    cur = (q0 + lax.broadcasted_iota(jnp.int32, (SEL_BIAS_WIDTH, tq), 1)) // SEL_LEN
    valid = blk <= cur
    forced = (blk == 0) | (blk == cur) | (blk == cur - 1)
    cand = valid & jnp.logical_not(forced)
    n_forced = jnp.minimum(cur, 2) + 1
    val = jnp.where(cand, imp, -1.0)
    cnt = jnp.zeros((SEL_BIAS_WIDTH, tq), F32)
    for j in range(n_sel):
        vj = jnp.broadcast_to(val[j:j + 1, :], (SEL_BIAS_WIDTH, tq))
        cnt = cnt + jnp.where(blk > j, jnp.where(vj >= val, 1.0, 0.0), jnp.where(vj > val, 1.0, 0.0))
    free = (min(SEL_TOPN, n_sel) - n_forced).astype(F32)
    sel = (forced & valid) | (cand & (cnt < free))
    sel_t = jnp.concatenate([jnp.where(sel, 1.0, 0.0), jnp.zeros((LANES - SEL_BIAS_WIDTH, tq), F32)], axis=0)
    sel_q = sel_t.T
    bias = jnp.where(sel_q > 0.5, 0.0, SEL_NEG)
    lane = lax.broadcasted_iota(jnp.int32, (tq, LANES), 1)
    bias_hi = pltpu.roll(bias, SEL_BIAS_WIDTH, 1)
    qr = qr_ref[0]
    for g in range(G):
        qg = qr[:, g * HEAD_DIM:(g + 1) * HEAD_DIM].astype(F32)
        qg = jnp.concatenate([qg, qg], axis=-1)
        qa_ref[0, 0, g] = jnp.where(lane < HEAD_DIM, qg, bias_hi).astype(BF16)


def _overlap_t(S):
    n_cmp = (S - CMP_LEN) // CMP_STRIDE + 1
    nc = S // CMP_STRIDE
    n_sel = S // SEL_LEN
    start = np.arange(nc) * CMP_STRIDE
    end = start + CMP_LEN - 1
    sel_start = np.arange(n_sel) * SEL_LEN
    ov = (start[None, :] <= sel_start[:, None] + SEL_LEN - 1) & (end[None, :] >= sel_start[:, None])
    ov = ov & (np.arange(nc) < n_cmp)[None, :]
    out = np.zeros((LANES, nc), np.float32)
    out[:n_sel] = ov.astype(np.float32)
    return jnp.asarray(out).astype(BF16)


def _nsa_cmp_call(qn, qr, kc, vc, tq):
    B, S, _ = qn.shape
    H, G = NSA_KV_HEADS, NSA_GROUP
    nc = kc.shape[2]
    n_sel = S // SEL_LEN
    assert n_sel <= SEL_BIAS_WIDTH and nc % LANES == 0
    qspec = pl.BlockSpec((1, tq, G * HEAD_DIM), lambda b, h, i: (b, i, h))
    cspec = pl.BlockSpec((1, 1, nc, HEAD_DIM), lambda b, h, i: (b, h, 0, 0))
    return pl.pallas_call(
        functools.partial(_nsa_cmp_kernel, n_sel=n_sel),
        grid=(B, H, S // tq),
        in_specs=[qspec, qspec, cspec, cspec, pl.BlockSpec((LANES, nc), lambda b, h, i: (0, 0))],
        out_specs=[pl.BlockSpec((1, 1, G, tq, HEAD_DIM), lambda b, h, i: (b, h, 0, i, 0)),
                   pl.BlockSpec((1, 1, G, tq, LANES), lambda b, h, i: (b, h, 0, i, 0))],
        out_shape=[jax.ShapeDtypeStruct((B, H, G, S, HEAD_DIM), F32),
                   jax.ShapeDtypeStruct((B, H, G, S, LANES), BF16)],
        compiler_params=_cparams("arbitrary", "arbitrary", "arbitrary"),
        name="nsa_cmp",
    )(qn, qr, kc, vc, _overlap_t(S))


SEL_CHUNK = 4


def _nsa_flash_kernel(qa_ref, ks_ref, vs_ref, kw_ref, vw_ref, oc_ref, g_ref, o_ref, m_scr, acc_scr):
    G = NSA_GROUP
    tq = qa_ref.shape[3]
    R = G * tq
    tk = tq
    i = pl.program_id(2)
    q0 = i * tq

    def rows_of(ref, j0, nt):
        return ref[0, 0, pl.ds(pl.multiple_of(j0 * tk, tk), nt * tk), :]

    def attend(k_ref, v_ref, j0, nt, visible, state):
        v = rows_of(v_ref, j0, nt)
        kt = k_ref[0, 0, :, pl.ds(pl.multiple_of(j0 * tk, tk), nt * tk)]
        s_all = _dot(qa_ref[0, 0].reshape(R, LANES), kt)
        ss = [s_all[g * tq:(g + 1) * tq] for g in range(G)]
        if visible is not None:
            qpos = q0 + lax.broadcasted_iota(jnp.int32, (tq, nt * tk), 0)
            kpos = j0 * tk + lax.broadcasted_iota(jnp.int32, (tq, nt * tk), 1)
            mask = visible(qpos, kpos)
        ps, alphas = [], []
        for g in range(G):
            s = ss[g] if visible is None else jnp.where(mask, ss[g], MASK_NEG)
            cols = [s[:, c * LANES:(c + 1) * LANES] for c in range(nt * tk // LANES)]
            m_new = jnp.max(functools.reduce(jnp.maximum, cols), axis=-1, keepdims=True)
            if state:
                m_prev = m_scr[g]
                m_new = jnp.maximum(m_prev, m_new)
                alphas.append(jnp.exp(m_prev - m_new))
                m_scr[g] = m_new
            ps.append(jnp.concatenate([jnp.exp(c - m_new) for c in cols], axis=-1).astype(BF16))
        pv = []
        for a in range(0, G, 2):
            both = _dot(jnp.concatenate([ps[a], ps[a + 1]], axis=0), v)
            pv += [both[:tq], both[tq:]]
        if not state:
            return pv
        for g in range(G):
            acc_scr[g] = alphas[g] * acc_scr[g] + pv[g]

    def normalise(acc):
        return acc[:, :HEAD_DIM] / acc[:, HEAD_DIM:HEAD_DIM + 1]

    m_scr[...] = jnp.full((G, tq, LANES), MASK_NEG, F32)
    acc_scr[...] = jnp.zeros((G, tq, LANES), F32)
    causal = lambda qpos, kpos: kpos <= qpos

    def sel_body(c, carry):
        attend(ks_ref, vs_ref, c * SEL_CHUNK, SEL_CHUNK, None, True)
        return carry

    n_full = i // SEL_CHUNK
    lax.fori_loop(0, n_full, sel_body, 0)
    attend(ks_ref, vs_ref, n_full * SEL_CHUNK, 2, causal, True)

    @pl.when(i % SEL_CHUNK >= 2)
    def _():
        attend(ks_ref, vs_ref, n_full * SEL_CHUNK + 2, 2, causal, True)

    o_sel = normalise(acc_scr[...].reshape(R, LANES))

    assert WINDOW == 2 * tk
    band = lambda qpos, kpos: (kpos <= qpos) & (kpos > qpos - WINDOW)
    o_win = normalise(jnp.concatenate(attend(kw_ref, vw_ref, jnp.maximum(i - 2, 0), 3, band, False), axis=0))

    o_cmp = oc_ref[0, 0].reshape(R, HEAD_DIM)
    gates = g_ref[0, 0]
    outs = []
    for g in range(G):
        r = slice(g * tq, (g + 1) * tq)
        outs.append(gates[:, 3 * g:3 * g + 1] * o_cmp[r]
                    + gates[:, 3 * g + 1:3 * g + 2] * o_sel[r]
                    + gates[:, 3 * g + 2:3 * g + 3] * o_win[r])
    o_ref[0] = jnp.concatenate(outs, axis=-1).astype(BF16)


def _nsa_flash_call(qa, ks, vs, kw, vw, oc, gates):
    B, H, G, S, _ = qa.shape
    tq = WINDOW // 2
    assert (S // tq) % SEL_CHUNK == 0
    kt = pl.BlockSpec((1, 1, LANES, S), lambda b, h, i: (b, h, 0, 0))
    vv = pl.BlockSpec((1, 1, S, LANES), lambda b, h, i: (b, h, 0, 0))
    return pl.pallas_call(
        _nsa_flash_kernel,
        grid=(B, H, S // tq),
        in_specs=[pl.BlockSpec((1, 1, G, tq, LANES), lambda b, h, i: (b, h, 0, i, 0)),
                  kt, vv, kt, vv,
                  pl.BlockSpec((1, 1, G, tq, HEAD_DIM), lambda b, h, i: (b, h, 0, i, 0)),
                  pl.BlockSpec((1, 1, tq, LANES), lambda b, h, i: (b, h, i, 0))],
        out_specs=pl.BlockSpec((1, tq, G * HEAD_DIM), lambda b, h, i: (b, i, h)),
        out_shape=jax.ShapeDtypeStruct((B, S, Q_WIDTH), BF16),
        scratch_shapes=[pltpu.VMEM((G, tq, LANES), F32), pltpu.VMEM((G, tq, LANES), F32)],
        compiler_params=_cparams("arbitrary", "arbitrary", "arbitrary"),
        name="nsa_flash",
    )(qa, ks, vs, kw, vw, oc, gates)


def _gmlp_kernel(u_ref, v_ref, ws_ref, bs_ref, o_ref):
    tm = u_ref.shape[1]
    C = GM_CHUNK
    r = lax.broadcasted_iota(jnp.int32, (C, C), 0)
    c = lax.broadcasted_iota(jnp.int32, (C, C), 1)
    causal = c <= r
    ws = [jnp.where(causal, ws_ref[g], 0.0).astype(BF16) for g in range(GM_GROUPS)]
    for n in range(tm // C):
        rows = slice(n * C, (n + 1) * C)
        vn = v_ref[0, rows, :]
        mixed = jnp.concatenate(
            [_dot(ws[g], vn[:, g * GM_GROUP_DIM:(g + 1) * GM_GROUP_DIM]) for g in range(GM_GROUPS)],
            axis=-1)
        o_ref[0, rows, :] = (u_ref[0, rows, :].astype(F32) * (mixed + bs_ref[...])).astype(BF16)


def _gmlp_call(u, v, ws, bs_full, tm):
    B, S, W = u.shape
    tok = pl.BlockSpec((1, tm, W), lambda b, i: (b, i, 0))
    return pl.pallas_call(
        _gmlp_kernel,
        grid=(B, S // tm),
        in_specs=[tok, tok,
                  pl.BlockSpec((GM_GROUPS, GM_CHUNK, GM_CHUNK), lambda b, i: (0, 0, 0)),
                  pl.BlockSpec((GM_CHUNK, W), lambda b, i: (0, 0))],
        out_specs=tok,
        out_shape=jax.ShapeDtypeStruct((B, S, W), BF16),
        compiler_params=_cparams("arbitrary", "arbitrary"),
        name="gmlp",
    )(u, v, ws, bs_full)


def _mixout_kernel(on_ref, og_ref, ga_ref, gb_ref, x_ref, gt_ref, sc_ref, sh_ref,
                   wpn_ref, wpg_ref, wo_ref, wr_ref, wgs_ref, wus_ref, wds_ref,
                   x1_ref, h2_ref, lg_ref, shd_ref):
    ya = _dot(on_ref[0], wpn_ref[...])
    yb = _dot(og_ref[0], wpg_ref[...])
    merged = ga_ref[0].astype(F32) * ya + gb_ref[0].astype(F32) * yb
    x1 = x_ref[0] + gt_ref[0] * _dot(merged.astype(BF16), wo_ref[...])
    x1_ref[0] = x1
    ms = jnp.mean(x1 * x1, axis=-1, keepdims=True)
    h2 = (x1 * lax.rsqrt(ms + EPS)) * (1.0 + sc_ref[0]) + sh_ref[0]
    hb = h2.astype(BF16)
    lg_ref[0] = _dot_nt(wr_ref[...], hb)
    h2_ref[0] = _pack_halves(hb)
    act = _silu(_dot(hb, wgs_ref[...])) * _dot(hb, wus_ref[...])
    shd_ref[0] = _dot(act.astype(BF16), wds_ref[...])


def _mixout_call(o_nsa, o_gm, ga, gb, x, gt, sc, sh, wpn, wpg, wo, wr, wgs, wus, wds, tm):
    B, S, D = x.shape
    tok = lambda w: pl.BlockSpec((1, tm, w), lambda b, i: (b, i, 0))
    per_b = pl.BlockSpec((1, 1, D), lambda b, i: (b, 0, 0))
    full = lambda a: pl.BlockSpec(a.shape, lambda b, i: (0,) * a.ndim)
    sds = jax.ShapeDtypeStruct
    return pl.pallas_call(
        _mixout_kernel,
        grid=(B, S // tm),
        in_specs=[tok(Q_WIDTH), tok(GM_WIDTH), tok(D), tok(D), tok(D), per_b, per_b, per_b,
                  full(wpn), full(wpg), full(wo), full(wr), full(wgs), full(wus), full(wds)],
        out_specs=[tok(D), tok(D // 2), pl.BlockSpec((1, N_EXPERTS, tm), lambda b, i: (b, 0, i)), tok(D)],
        out_shape=[sds((B, S, D), F32), sds((B, S, D // 2), jnp.uint32), sds((B, N_EXPERTS, S), F32),
                   sds((B, S, D), F32)],
        compiler_params=_cparams("arbitrary", "arbitrary"),
        name="mix_out",
    )(o_nsa, o_gm, ga, gb, x, gt, sc, sh, wpn, wpg, wo, wr, wgs, wus, wds)


def _route_kernel(lg_ref, br_ref, idx_ref, wt_ref, rank_ref, cnt_ref, run_scr):
    E, tm = lg_ref.shape[1], lg_ref.shape[2]

    @pl.when(pl.program_id(0) == 0)
    def _():
        run_scr[...] = jnp.zeros_like(run_scr)

    aff = _sigmoid(lg_ref[0])
    work = aff + br_ref[...]
    row = lax.broadcasted_iota(jnp.int32, (E, tm), 0).astype(F32)
    picked = jnp.zeros((E, tm), F32)
    idxs, tops = [], []
    for _ in range(TOP_K):
        m = jnp.max(work, axis=0, keepdims=True)
        idx = jnp.min(jnp.where(work == m, row, float(E)), axis=0, keepdims=True)
        hit = row == idx
        tops.append(jnp.sum(jnp.where(hit, aff, 0.0), axis=0, keepdims=True))
        idxs.append(idx)
        picked = jnp.where(hit, 1.0, picked)
        work = jnp.where(hit, -jnp.inf, work)
    total = functools.reduce(jnp.add, tops)
    r = lax.broadcasted_iota(jnp.int32, (tm, tm), 0)
    c = lax.broadcasted_iota(jnp.int32, (tm, tm), 1)
    before = _dot(picked.astype(BF16), jnp.where(r < c, 1.0, 0.0).astype(BF16)) + run_scr[...]
    ranks = [jnp.sum(jnp.where(row == idx, before, 0.0), axis=0, keepdims=True) for idx in idxs]
    run_scr[...] = run_scr[...] + jnp.sum(picked, axis=1, keepdims=True)
    cnt_ref[...] = run_scr[...]
    idx_ref[...] = jnp.concatenate(idxs, axis=0).astype(jnp.int32)
    rank_ref[...] = jnp.concatenate(ranks, axis=0).astype(jnp.int32)
    wt = jnp.concatenate([t / total * ROUTE_SCALE for t in tops]
                         + [jnp.zeros((LANES - TOP_K, tm), F32)], axis=0)
    eye = jnp.where(r == c, 1.0, 0.0).astype(BF16)
    cols = jnp.zeros((tm, LANES), F32)
    rest = wt
    for _ in range(3):
        part = rest.astype(BF16)
        cols = cols + _dot_nt(eye, part)
        rest = rest - part.astype(F32)
    wt_ref[...] = cols


def _route_call(logits_t, b_router, tm):
    B, E, S = logits_t.shape
    per_b = S // tm
    T = B * S
    kt = pl.BlockSpec((TOP_K, tm), lambda i: (0, i))
    col = pl.BlockSpec((E, 1), lambda i: (0, 0))
    sds = jax.ShapeDtypeStruct
    return pl.pallas_call(
        _route_kernel,
        grid=(T // tm,),
        in_specs=[pl.BlockSpec((1, E, tm), lambda i: (i // per_b, 0, i % per_b)), col],
        out_specs=[kt, pl.BlockSpec((tm, LANES), lambda i: (i, 0)), kt, col],
        out_shape=[sds((TOP_K, T), jnp.int32), sds((T, LANES), F32), sds((TOP_K, T), jnp.int32),
                   sds((E, 1), F32)],
        scratch_shapes=[pltpu.VMEM((E, 1), F32)],
        compiler_params=_cparams("arbitrary"),
        name="route",
    )(logits_t, b_router.reshape(E, 1))


def _slot_kernel(idx_ref, rank_ref, ps_ref, o_ref):
    tm = idx_ref.shape[1]
    E = ps_ref.shape[0]
    row = lax.broadcasted_iota(jnp.int32, (E, tm), 0)
    idx, rank, ps = idx_ref[...], rank_ref[...], ps_ref[...]
    base = [jnp.sum(jnp.where(row == idx[k:k + 1, :], ps, 0.0), axis=0, keepdims=True) for k in range(TOP_K)]
    o_ref[...] = jnp.concatenate(base, axis=0).astype(jnp.int32) + rank


def _slot_call(idx, rank, pad_start, tm):
    T = idx.shape[1]
    E = pad_start.shape[0]
    kt = pl.BlockSpec((TOP_K, tm), lambda i: (0, i))
    return pl.pallas_call(
        _slot_kernel,
        grid=(T // tm,),
        in_specs=[kt, kt, pl.BlockSpec((E, 1), lambda i: (0, 0))],
        out_specs=kt,
        out_shape=jax.ShapeDtypeStruct((TOP_K, T), jnp.int32),
        compiler_params=_cparams("arbitrary"),
        name="slot",
    )(idx, rank, pad_start.astype(F32).reshape(E, 1))


def _sc_mesh():
    mesh = plsc.VectorSubcoreMesh(core_axis_name="core", subcore_axis_name="subcore")
    return mesh, mesh.num_cores * mesh.num_subcores


def _sc_worker(mesh):
    return lax.axis_index("subcore") * mesh.num_cores + lax.axis_index("core")


def _scatter_rows(rows, indices_by_k, out_ref):
    n, width = rows.shape
    mesh, workers = _sc_mesh()
    per_worker = n // workers
    assert per_worker * workers == n and per_worker % SC_WINDOW == 0

    @pl.kernel(out_type=(), mesh=mesh, name="scatter_rows",
               scratch_types=[pltpu.VMEM((SC_WINDOW,), jnp.int32), pltpu.VMEM((SC_WINDOW, width), rows.dtype)])
    def scatter(rows_hbm, idx_hbm, out_hbm, idx_v, rows_v):
        worker = _sc_worker(mesh)

        @pl.loop(0, per_worker // SC_WINDOW)
        def _(j):
            base = pl.multiple_of(worker * per_worker + j * SC_WINDOW, SC_WINDOW)
            pltpu.sync_copy(rows_hbm.at[pl.ds(base, SC_WINDOW)], rows_v)
            for k in range(TOP_K):
                pltpu.sync_copy(idx_hbm.at[pl.ds(pl.multiple_of(k * n + base, SC_WINDOW), SC_WINDOW)], idx_v)
                pltpu.sync_copy(rows_v, out_hbm.at[idx_v])

    scatter(rows, indices_by_k, out_ref)


def _expert_kernel(ie_ref, ir_ref, ig_ref, n_ref, x_hbm, wg_ref, wu_ref, wd_ref, y_hbm,
                   xbuf, ybuf, act_scr, xsem, ysem):
    w = pl.program_id(0)
    n = n_ref[0]
    R = xbuf.shape[1]
    last = ie_ref.shape[0] - 1

    def x_copy(item, slot):
        row = pl.multiple_of(ir_ref[jnp.minimum(item, last)], MOE_GRAN)
        return pltpu.make_async_copy(x_hbm.at[pl.ds(row, R), :], xbuf.at[slot], xsem.at[slot])

    def y_copy(item, slot, gi):
        row = pl.multiple_of(ir_ref[jnp.minimum(item, last)] + gi * MOE_GRAN, MOE_GRAN)
        src = ybuf.at[slot, pl.ds(pl.multiple_of(gi * MOE_GRAN, MOE_GRAN), MOE_GRAN), :]
        return pltpu.make_async_copy(src, y_hbm.at[pl.ds(row, MOE_GRAN), :], ysem.at[slot])

    def for_granules(item, fn):
        def body(gi, carry):
            fn(gi)
            return carry
        lax.fori_loop(0, ig_ref[jnp.clip(item, 0, last)], body, 0)

    def gate_up(slot):
        lo, hi = _unpack_halves(xbuf[slot])
        x = jnp.concatenate([lo, hi], axis=1).astype(BF16)
        a = _dot(x, wg_ref[0, 0].astype(BF16))
        b = _dot(x, wu_ref[0, 0].astype(BF16))
        return (_silu(a) * b).astype(BF16)

    def down(act):
        return _pack_halves(_dot(act, wd_ref[0, 0].astype(BF16)))

    nx = xbuf.shape[0]
    cur, prv = w % nx, (w + 1) % 2

    @pl.when(w == 0)
    def _():
        for j in range(nx - 1):
            @pl.when(j < n)
            def _():
                x_copy(j, j).start()

    @pl.when(w < n)
    def _():
        x_copy(w, cur).wait()

    @pl.when(w + nx - 1 < n)
    def _():
        x_copy(w + nx - 1, (w + nx - 1) % nx).start()

    @pl.when((w >= 3) & (w - 3 < n))
    def _():
        for_granules(w - 3, lambda gi: y_copy(w - 3, prv, gi).wait())

    @pl.when(w == 0)
    def _():
        act_scr[...] = gate_up(0)

    @pl.when((w >= 1) & (w < n))
    def _():
        prev = act_scr[...]
        ybuf[prv] = down(prev)
        act_scr[...] = gate_up(cur)

    @pl.when((w >= 1) & (w == n))
    def _():
        ybuf[prv] = down(act_scr[...])

    @pl.when((w >= 1) & (w <= n))
    def _():
        for_granules(w - 1, lambda gi: y_copy(w - 1, prv, gi).start())


def _expert_call(layer, item_e, item_row, item_ng, n_items, xs, wg, wu, wd):
    n_alloc, W = xs.shape
    R = MOE_ROWS
    nw = item_e.shape[0]
    D, F = wg.shape[2], wg.shape[3]
    cur = lambda w, ie, ir, ig, n: (layer, ie[jnp.minimum(w, nw - 1)], 0, 0)
    prev = lambda w, ie, ir, ig, n: (layer, ie[jnp.clip(w - 1, 0, nw - 1)], 0, 0)
    grid_spec = pltpu.PrefetchScalarGridSpec(
        num_scalar_prefetch=4,
        grid=(nw + 3,),
        in_specs=[pl.BlockSpec(memory_space=pl.ANY),
                  pl.BlockSpec((1, 1, D, F), cur), pl.BlockSpec((1, 1, D, F), cur),
                  pl.BlockSpec((1, 1, F, D), prev)],
        out_specs=pl.BlockSpec(memory_space=pl.ANY),
        scratch_shapes=[pltpu.VMEM((3, R, W), jnp.uint32), pltpu.VMEM((2, R, W), jnp.uint32),
                        pltpu.VMEM((R, F), BF16),
                        pltpu.SemaphoreType.DMA((3,)), pltpu.SemaphoreType.DMA((2,))],
    )
    return pl.pallas_call(
        _expert_kernel,
        grid_spec=grid_spec,
        out_shape=jax.ShapeDtypeStruct((n_alloc, W), jnp.uint32),
        input_output_aliases={4: 0},
        compiler_params=pltpu.CompilerParams(dimension_semantics=("arbitrary",), vmem_limit_bytes=VMEM_LIMIT,
                                             has_side_effects=True),
        name="expert",
    )(item_e, item_row, item_ng, n_items, xs, wg, wu, wd)


def _gather_rows(table, indices):
    n = indices.shape[0]
    width = table.shape[1]
    mesh, workers = _sc_mesh()
    per_worker = n // workers
    assert per_worker * workers == n and per_worker % SC_WINDOW == 0

    half = SC_WINDOW // 2
    steps = per_worker // SC_WINDOW
    rows_t = pltpu.VMEM((half, width), table.dtype)

    @pl.kernel(out_type=jax.ShapeDtypeStruct((n, width), table.dtype), mesh=mesh, name="gather_rows",
               scratch_types=[pltpu.VMEM((SC_WINDOW,), jnp.int32), rows_t, rows_t] + [pltpu.SemaphoreType.DMA] * 4)
    def gather(table_hbm, idx_hbm, out_hbm, idx_v, rows_a, rows_b, gsem_a, gsem_b, wsem_a, wsem_b):
        worker = _sc_worker(mesh)
        first = worker * per_worker
        halves = ((rows_a, gsem_a, wsem_a, 0), (rows_b, gsem_b, wsem_b, half))

        def write(buf, wsem, row):
            return pltpu.make_async_copy(buf, out_hbm.at[pl.ds(pl.multiple_of(row, half), half)], wsem)

        @pl.loop(0, steps)
        def _(j):
            base = pl.multiple_of(first + j * SC_WINDOW, SC_WINDOW)
            pltpu.sync_copy(idx_hbm.at[pl.ds(base, SC_WINDOW)], idx_v)
            gathers = []
            for buf, gsem, wsem, off in halves:
                @pl.when(j > 0)
                def _():
                    write(buf, wsem, base).wait()
                gathers.append(pltpu.async_copy(table_hbm.at[idx_v.at[pl.ds(off, half)]], buf, gsem))
            for (buf, gsem, wsem, off), g in zip(halves, gathers):
                g.wait()
                write(buf, wsem, base + off).start()

        for buf, gsem, wsem, off in halves:
            write(buf, wsem, first).wait()

    return gather(table, indices)


def _combine_kernel(y_ref, wt_ref, shd_ref, x_ref, gt_ref, o_ref):
    wt = wt_ref[...]
    acc_lo = acc_hi = None
    for k in range(TOP_K):
        lo, hi = _unpack_halves(y_ref[k])
        w = wt[:, k:k + 1]
        acc_lo = w * lo if k == 0 else acc_lo + w * lo
        acc_hi = w * hi if k == 0 else acc_hi + w * hi
    routed = jnp.concatenate([acc_lo, acc_hi], axis=1)
    o_ref[...] = x_ref[...] + gt_ref[0] * (routed + shd_ref[...])


def _combine_call(yk, wts, shared, x1, gt, tm):
    T, D = x1.shape
    B = gt.shape[0]
    per_b = T // B // tm
    tok = lambda w: pl.BlockSpec((tm, w), lambda i: (i, 0))
    return pl.pallas_call(
        _combine_kernel,
        grid=(T // tm,),
        in_specs=[pl.BlockSpec((TOP_K, tm, yk.shape[2]), lambda i: (0, i, 0)),
                  tok(LANES), tok(D), tok(D),
                  pl.BlockSpec((1, 1, D), lambda i: (i // per_b, 0, 0))],
        out_specs=tok(D),
        out_shape=jax.ShapeDtypeStruct((T, D), F32),
        compiler_params=_cparams("arbitrary"),
        name="combine",
    )(yk, wts, shared, x1, gt)


def _rope_tables(positions):
    half = HEAD_DIM // 2
    inv = ROPE_THETA ** (-jnp.arange(half, dtype=F32) / half)
    ang = positions.astype(F32)[..., None] * inv
    cos, sin, zero = jnp.cos(ang), jnp.sin(ang), jnp.zeros_like(ang)
    cos_t = jnp.concatenate([cos, cos] * 2, axis=-1)
    sin_lo = jnp.concatenate([-sin, zero] * 2, axis=-1)
    sin_hi = jnp.concatenate([zero, sin] * 2, axis=-1)
    return cos_t, sin_lo, sin_hi


def _reorder_w_in(w):
    o = np.cumsum([0, Q_WIDTH] + [KV_WIDTH] * 6 + [NSA_Q_HEADS * 3, GM_WIDTH, GM_WIDTH, D_MODEL, D_MODEL])
    q, kc, vc, ks, vs, kw, vw, g, u, v, ga, gb = [w[:, o[i]:o[i + 1]] for i in range(12)]
    per = NSA_GROUP * 3
    pad = jnp.zeros((w.shape[0], LANES - per), w.dtype)
    return jnp.concatenate([q, ks, kw, kc, vc, vs, vw, g[:, :per], pad, g[:, per:], pad, u, v, ga, gb], axis=1)


def _owner(ends, pos):
    return jnp.minimum(jnp.sum((ends[None, :] <= pos[:, None]).astype(jnp.int32), axis=1), ends.shape[0] - 1)


def _lookup(table, idx):
    hit = idx[:, None] == jnp.arange(table.shape[0], dtype=jnp.int32)[None, :]
    return jnp.sum(jnp.where(hit, table[None, :], 0), axis=1)


def _expert_plan(counts, n_items_max):
    per_item = MOE_ROWS // MOE_GRAN
    counts = counts.astype(jnp.int32)
    gran = (counts + MOE_GRAN - 1) // MOE_GRAN
    gran_start = jnp.cumsum(gran) - gran
    items = (gran + per_item - 1) // per_item
    item_end = jnp.cumsum(items)
    w = jnp.arange(n_items_max, dtype=jnp.int32)
    ie = _owner(item_end, w)
    part = w - (_lookup(item_end, ie) - _lookup(items, ie))
    live = w < item_end[-1]
    item_ng = jnp.where(live, jnp.clip(_lookup(gran, ie) - part * per_item, 0, per_item), 0).astype(jnp.int32)
    item_row = jnp.where(live, (_lookup(gran_start, ie) + part * per_item) * MOE_GRAN, 0).astype(jnp.int32)
    return gran_start * MOE_GRAN, ie.astype(jnp.int32), item_row, item_ng, item_end[-1:].astype(jnp.int32)


def kernel(x, c, positions, w_mod, b_mod, w_in, q_gain, k_gain, cmp_pos_k, cmp_pos_v, cmp_w1_k, cmp_w2_k, cmp_w1_v, cmp_w2_v, gm_ln_g, gm_ln_b, gm_ws, gm_bs, w_proj_nsa, w_proj_gm, w_out, w_router, b_router, w_gate_e, w_up_e, w_down_e, w_gate_sh, w_up_sh, w_down_sh):
    B, S, D = x.shape
    L = w_mod.shape[0]
    T = B * S
    tm = 256
    tm_dense = 512
    scale = HEAD_DIM ** -0.5
    cos_t, sin_lo, sin_hi = _rope_tables(positions)
    mod = _mod_call(c, w_mod, b_mod)
    n_alloc = T * TOP_K + N_EXPERTS * MOE_GRAN + MOE_ROWS
    n_items_max = N_EXPERTS + T * TOP_K // MOE_ROWS + 1
    rows_buf = jnp.zeros((n_alloc, D // 2), jnp.uint32)

    for l in range(L):
        sh_a, sc_a, gt_a, sh_f, sc_f, gt_f = [mod[l, :, i * D:(i + 1) * D].reshape(B, 1, D) for i in range(6)]
        qg = (jnp.tile(q_gain[l], NSA_Q_HEADS) * scale).reshape(1, Q_WIDTH)
        kg = jnp.tile(k_gain[l], 2 * NSA_KV_HEADS).reshape(1, 2 * KV_WIDTH)
        (qn, qr, ks, kw, vs, vw, kc_raw, vc_raw, gates, u, v, ga, gb) = _inproj_call(
            x, sc_a, sh_a, _reorder_w_in(w_in[l]).astype(BF16), qg, kg, cos_t, sin_lo, sin_hi,
            gm_ln_g[l].reshape(1, GM_WIDTH), gm_ln_b[l].reshape(1, GM_WIDTH), tm_dense)
        kc, vc = _compress_call(
            kc_raw, vc_raw, cmp_w1_k[l].astype(BF16), cmp_w2_k[l].astype(BF16), cmp_pos_k[l].reshape(1, -1),
            cmp_w1_v[l].astype(BF16), cmp_w2_v[l].astype(BF16), cmp_pos_v[l].reshape(1, -1),
            k_gain[l].reshape(1, HEAD_DIM))
        o_cmp, q_aug = _nsa_cmp_call(qn, qr, kc, vc, WINDOW // 2)
        o_nsa = _nsa_flash_call(q_aug, ks, vs, kw, vw, o_cmp, gates)
        bs_full = jnp.repeat(gm_bs[l].T, GM_GROUP_DIM, axis=1)
        o_gm = _gmlp_call(u, v, gm_ws[l], bs_full, 512)
        x1, h2, logits, shared = _mixout_call(
            o_nsa, o_gm, ga, gb, x, gt_a, sc_f, sh_f,
            w_proj_nsa[l].astype(BF16), w_proj_gm[l].astype(BF16), w_out[l].astype(BF16), w_router[l].T.astype(BF16),
            w_gate_sh[l].astype(BF16), w_up_sh[l].astype(BF16), w_down_sh[l].astype(BF16), tm_dense)
        idx, wts, rank, counts = _route_call(logits, b_router[l], tm)
        row_start, item_e, item_row, item_ng, n_items = _expert_plan(counts[:, 0], n_items_max)
        slots_by_k = _slot_call(idx, rank, row_start, tm).reshape(TOP_K * T)
        xs_ref = jax.new_ref(rows_buf)
        _scatter_rows(h2.reshape(T, D // 2), slots_by_k, xs_ref)
        y = _expert_call(l, item_e, item_row, item_ng, n_items, jax.freeze(xs_ref), w_gate_e, w_up_e, w_down_e)
        yk = _gather_rows(y, slots_by_k).reshape(TOP_K, T, D // 2)
        rows_buf = y
        x = _combine_call(yk, wts, shared.reshape(T, D), x1.reshape(T, D), gt_f, tm).reshape(B, S, D)
    return x
```

```python
import functools

import jax
import jax.numpy as jnp
import numpy as np
from jax import lax
from jax.experimental import pallas as pl
from jax.experimental.pallas import tpu as pltpu
from jax.experimental.pallas import tpu_sc as plsc

D_MODEL = 1024
NSA_Q_HEADS = 8
NSA_KV_HEADS = 2
HEAD_DIM = 64
NSA_GROUP = NSA_Q_HEADS // NSA_KV_HEADS
CMP_LEN = 32
CMP_STRIDE = 16
CMP_HIDDEN = 256
SEL_LEN = 64
SEL_TOPN = 16
WINDOW = 512
ROPE_THETA = 10000.0
Q_WIDTH = NSA_Q_HEADS * HEAD_DIM
KV_WIDTH = NSA_KV_HEADS * HEAD_DIM
GM_GROUPS = 8
GM_GROUP_DIM = 64
GM_WIDTH = GM_GROUPS * GM_GROUP_DIM
GM_CHUNK = 128
N_EXPERTS = 256
TOP_K = 8
D_EXPERT = 256
D_SHARED = 256
ROUTE_SCALE = 2.5
EPS = 1e-6

LANES = 128
SEL_BIAS_WIDTH = 64
MASK_NEG = -1e30
SEL_NEG = MASK_NEG
SC_WINDOW = 128
MOE_GRAN = 128
MOE_ROWS = 9 * MOE_GRAN
VMEM_LIMIT = 56 * 1024 * 1024

F32 = jnp.float32
BF16 = jnp.bfloat16
HI = lax.Precision.HIGHEST


def _cparams(*sem):
    return pltpu.CompilerParams(dimension_semantics=sem, vmem_limit_bytes=VMEM_LIMIT)


def _dot(a, b, **kw):
    return jnp.dot(a, b, preferred_element_type=F32, **kw)


def _dot_nt(a, b, **kw):
    return lax.dot_general(a, b, (((1,), (1,)), ((), ())), preferred_element_type=F32, **kw)


def _gelu(x):
    return 0.5 * x * (1.0 + jnp.tanh(0.7978845608028654 * (x + 0.044715 * (x * x * x))))


def _sigmoid(x):
    return 1.0 / (1.0 + jnp.exp(-x))


def _silu(x):
    return x * _sigmoid(x)


_HI_MASK = np.uint32(0xFFFF0000)


def _pack_halves(a):
    w = a.shape[1] // 2
    bits = lax.bitcast_convert_type(a.astype(BF16).astype(F32), jnp.uint32)
    return (bits[:, w:] & _HI_MASK) | (bits[:, :w] >> 16)


def _unpack_halves(words):
    lo = lax.bitcast_convert_type(words << 16, F32)
    hi = lax.bitcast_convert_type(words & _HI_MASK, F32)
    return lo, hi


def _mod_kernel(c_ref, w_ref, b_ref, o_ref):
    c = c_ref[...]
    o_ref[0] = _dot(_silu(c), w_ref[0], precision=HI) + b_ref[0]


def _mod_call(c, w_mod, b_mod):
    L, D, N = w_mod.shape
    B = c.shape[0]
    tn = 1536
    return pl.pallas_call(
        _mod_kernel,
        grid=(L, N // tn),
        in_specs=[pl.BlockSpec((B, D), lambda l, j: (0, 0)),
                  pl.BlockSpec((1, D, tn), lambda l, j: (l, 0, j)),
                  pl.BlockSpec((1, 1, tn), lambda l, j: (l, 0, j))],
        out_specs=pl.BlockSpec((1, B, tn), lambda l, j: (l, 0, j)),
        out_shape=jax.ShapeDtypeStruct((L, B, N), F32),
        compiler_params=_cparams("arbitrary", "arbitrary"),
        name="mod",
    )(c, w_mod, b_mod.reshape(L, 1, N))


_C_Q = 0
_C_K = 512
_C_KC = 768
_C_VC = 896
_C_VS = 1024
_C_VW = 1152
_C_G = 1280
_C_U = 1536
_C_V = 2048
_C_GA = 2560
_C_GB = 3584
IN_COLS_P = 4608


def _dot_split(a, b):
    hi = a.astype(BF16)
    lo = (a - hi.astype(F32)).astype(BF16)
    return _dot(hi, b) + _dot(lo, b)


def _head_norm(z, bd):
    ms = _dot_split(z * z, bd)
    return z * lax.rsqrt(ms + EPS)


def _rope(z, cos, sin_lo, sin_hi):
    w = z.shape[-1]
    half = HEAD_DIM // 2
    return z * cos + pltpu.roll(z, w - half, 1) * sin_lo + pltpu.roll(z, half, 1) * sin_hi


def _tile_lanes(t, n):
    return t if n == 1 else jnp.concatenate([t] * n, axis=-1)


def _inproj_kernel(x_ref, sc_ref, sh_ref, w_ref, bdq_ref, bdk_ref, qg_ref, kg_ref,
                   cos_ref, sl_ref, shi_ref, lng_ref, lnb_ref,
                   qn_ref, qr_ref, ks_ref, kw_ref, vs_ref, vw_ref, kc_ref, vc_ref,
                   g_ref, u_ref, v_ref, ga_ref, gb_ref):
    tm = x_ref.shape[1]
    x = x_ref[0]
    ms = jnp.mean(x * x, axis=-1, keepdims=True)
    h = (x * lax.rsqrt(ms + EPS)) * (1.0 + sc_ref[0]) + sh_ref[0]
    hb = h.astype(BF16)

    def mm(lo, width):
        return _dot(hb, w_ref[:, lo:lo + width])

    cos, sl, shi = cos_ref[0], sl_ref[0], shi_ref[0]

    zq = mm(_C_Q, Q_WIDTH)
    qn = _head_norm(zq, bdq_ref[...]) * qg_ref[...]
    qr = _rope(qn, _tile_lanes(cos, 4), _tile_lanes(sl, 4), _tile_lanes(shi, 4))
    qn_ref[0] = qn.astype(BF16)
    qr_ref[0] = qr.astype(BF16)

    zk = mm(_C_K, 2 * KV_WIDTH)
    kn = _head_norm(zk, bdk_ref[...]) * kg_ref[...]
    kr = _rope(kn, _tile_lanes(cos, 2), _tile_lanes(sl, 2), _tile_lanes(shi, 2))
    lane = lax.broadcasted_iota(jnp.int32, (tm, LANES), 1)
    tok = pl.program_id(1) * tm + lax.broadcasted_iota(jnp.int32, (tm, LANES), 0)
    onehot = jnp.where(lane - HEAD_DIM == tok // SEL_LEN, 1.0, 0.0)
    ones_col = jnp.where(lane == HEAD_DIM, 1.0, 0.0)
    low = lane < HEAD_DIM
    zvs = mm(_C_VS, KV_WIDTH)
    zvw = mm(_C_VW, KV_WIDTH)
    zkc = mm(_C_KC, KV_WIDTH)
    zvc = mm(_C_VC, KV_WIDTH)
    for kv in range(NSA_KV_HEADS):
        def head(a):
            return a if kv == 0 else pltpu.roll(a, HEAD_DIM, 1)
        ks_ref[0, kv] = jnp.where(low, head(kr[:, :KV_WIDTH]), onehot).T.astype(BF16)
        kw_ref[0, kv] = jnp.where(low, head(kr[:, KV_WIDTH:]), 0.0).T.astype(BF16)
        vs_ref[0, kv] = jnp.where(low, head(zvs), ones_col).astype(BF16)
        vw_ref[0, kv] = jnp.where(low, head(zvw), ones_col).astype(BF16)
        kc_ref[0, kv] = head(zkc)[:, :HEAD_DIM]
        vc_ref[0, kv] = head(zvc)[:, :HEAD_DIM]

    zg = mm(_C_G, 2 * LANES)
    sg = _sigmoid(zg)
    g_ref[0, 0] = sg[:, :LANES]
    g_ref[0, 1] = sg[:, LANES:]

    u_ref[0] = _gelu(mm(_C_U, GM_WIDTH)).astype(BF16)
    gv = _gelu(mm(_C_V, GM_WIDTH))
    mu = jnp.mean(gv, axis=-1, keepdims=True)
    cen = gv - mu
    var = jnp.mean(cen * cen, axis=-1, keepdims=True)
    v_ref[0] = ((cen * lax.rsqrt(var + EPS)) * lng_ref[...] + lnb_ref[...]).astype(BF16)

    ga_ref[0] = _sigmoid(mm(_C_GA, D_MODEL)).astype(BF16)
    gb_ref[0] = _sigmoid(mm(_C_GB, D_MODEL)).astype(BF16)


def _block_diag_mean(width):
    idx = np.arange(width) // HEAD_DIM
    return jnp.asarray((idx[:, None] == idx[None, :]).astype(np.float32) / HEAD_DIM).astype(BF16)


def _inproj_call(x, sc, sh, w_p, qg, kg, cos, sl, shi, lng, lnb, tm):
    B, S, D = x.shape
    H = NSA_KV_HEADS
    full = lambda *shape: pl.BlockSpec(shape, lambda b, i: (0,) * len(shape))
    tok3 = lambda w: pl.BlockSpec((1, tm, w), lambda b, i: (b, i, 0))
    per_b = pl.BlockSpec((1, 1, D), lambda b, i: (b, 0, 0))
    kv4 = lambda w: pl.BlockSpec((1, H, tm, w), lambda b, i: (b, 0, i, 0))
    kt4 = pl.BlockSpec((1, H, LANES, tm), lambda b, i: (b, 0, 0, i))
    sds = jax.ShapeDtypeStruct
    out_shape = [
        sds((B, S, Q_WIDTH), BF16), sds((B, S, Q_WIDTH), BF16),
        sds((B, H, LANES, S), BF16), sds((B, H, LANES, S), BF16),
        sds((B, H, S, LANES), BF16), sds((B, H, S, LANES), BF16),
        sds((B, H, S, HEAD_DIM), F32), sds((B, H, S, HEAD_DIM), F32),
        sds((B, H, S, LANES), F32),
        sds((B, S, GM_WIDTH), BF16), sds((B, S, GM_WIDTH), BF16),
        sds((B, S, D), BF16), sds((B, S, D), BF16),
    ]
    out_specs = [
        tok3(Q_WIDTH), tok3(Q_WIDTH), kt4, kt4, kv4(LANES), kv4(LANES),
        kv4(HEAD_DIM), kv4(HEAD_DIM), kv4(LANES),
        tok3(GM_WIDTH), tok3(GM_WIDTH), tok3(D), tok3(D),
    ]
    return pl.pallas_call(
        _inproj_kernel,
        grid=(B, S // tm),
        in_specs=[tok3(D), per_b, per_b, full(D, IN_COLS_P),
                  full(Q_WIDTH, Q_WIDTH), full(2 * KV_WIDTH, 2 * KV_WIDTH),
                  full(1, Q_WIDTH), full(1, 2 * KV_WIDTH),
                  tok3(LANES), tok3(LANES), tok3(LANES),
                  full(1, GM_WIDTH), full(1, GM_WIDTH)],
        out_specs=out_specs,
        out_shape=out_shape,
        compiler_params=_cparams("arbitrary", "arbitrary"),
        name="in_proj",
    )(x, sc, sh, w_p, _block_diag_mean(Q_WIDTH), _block_diag_mean(2 * KV_WIDTH), qg, kg,
      cos, sl, shi, lng, lnb)


def _compress_kernel(kr_ref, vr_ref, w1k_ref, w2k_ref, pek_ref, w1v_ref, w2v_ref, pev_ref,
                     kg_ref, kc_ref, vc_ref):
    nc = kr_ref.shape[2]
    half = CMP_STRIDE * HEAD_DIM

    def mlp(raw, w1_ref, w2_ref, pe_ref):
        a = raw.astype(BF16)
        top = _dot(a, w1_ref[:half, :])
        bot = _dot(a, w1_ref[half:, :])
        pe = jnp.broadcast_to(pe_ref[...], (8, 2 * half)).astype(BF16)
        pe_row = _dot(pe, w1_ref[...])[0:1, :]
        hid = top + pltpu.roll(bot, nc - 1, 0) + pe_row
        return _dot(_gelu(hid).astype(BF16), w2_ref[...])

    kc = mlp(kr_ref[0, 0], w1k_ref, w2k_ref, pek_ref)
    ms = jnp.mean(kc * kc, axis=-1, keepdims=True)
    kc_ref[0, 0] = (kc * lax.rsqrt(ms + EPS) * kg_ref[...]).astype(BF16)
    vc_ref[0, 0] = mlp(vr_ref[0, 0], w1v_ref, w2v_ref, pev_ref).astype(BF16)


def _compress_call(kc_raw, vc_raw, w1k, w2k, pek, w1v, w2v, pev, kg):
    B, H, S, hd = kc_raw.shape
    nc = S // CMP_STRIDE
    feat = CMP_STRIDE * hd
    raw = pl.BlockSpec((1, 1, nc, feat), lambda b, h: (b, h, 0, 0))
    full = lambda *shape: pl.BlockSpec(shape, lambda b, h: (0,) * len(shape))
    out = pl.BlockSpec((1, 1, nc, hd), lambda b, h: (b, h, 0, 0))
    return pl.pallas_call(
        _compress_kernel,
        grid=(B, H),
        in_specs=[raw, raw, full(2 * feat, CMP_HIDDEN), full(CMP_HIDDEN, hd), full(1, 2 * feat),
                  full(2 * feat, CMP_HIDDEN), full(CMP_HIDDEN, hd), full(1, 2 * feat),
                  full(1, hd)],
        out_specs=[out, out],
        out_shape=[jax.ShapeDtypeStruct((B, H, nc, hd), BF16)] * 2,
        compiler_params=_cparams("arbitrary", "arbitrary"),
        name="compress",
    )(kc_raw.reshape(B, H, nc, feat), vc_raw.reshape(B, H, nc, feat),
      w1k, w2k, pek, w1v, w2v, pev, kg)


def _group_rows(a):
    return jnp.concatenate([a[:, g * HEAD_DIM:(g + 1) * HEAD_DIM] for g in range(NSA_GROUP)], axis=0)


def _nsa_cmp_kernel(qn_ref, qr_ref, kc_ref, vc_ref, ovl_ref, oc_ref, qa_ref, *, n_sel):
    tq = qn_ref.shape[1]
    nc = kc_ref.shape[2]
    G = NSA_GROUP
    q0 = pl.program_id(2) * tq
    q4 = _group_rows(qn_ref[0])
    s = _dot_nt(q4, kc_ref[0, 0])
    row = lax.broadcasted_iota(jnp.int32, (G, tq, nc), 1).reshape(G * tq, nc)
    col = lax.broadcasted_iota(jnp.int32, (G * tq, nc), 1)
    vis = col * CMP_STRIDE + (CMP_LEN - 1) <= q0 + row
    s = jnp.where(vis, s, MASK_NEG)
    m = jnp.max(s, axis=-1, keepdims=True)
    e = jnp.where(vis, jnp.exp(s - m), 0.0)
    p = e / jnp.maximum(jnp.sum(e, axis=-1, keepdims=True), 1e-30)
    oc = _dot(p.astype(BF16), vc_ref[0, 0])
    oc_ref[0, 0] = oc.reshape(G, tq, HEAD_DIM)

    psum = p[0:tq] + p[tq:2 * tq] + p[2 * tq:3 * tq] + p[3 * tq:4 * tq]
    p_hi = psum.astype(BF16)
    p_lo = (psum - p_hi.astype(F32)).astype(BF16)
    imp = (_dot_nt(ovl_ref[...], p_hi) + _dot_nt(ovl_ref[...], p_lo))[:SEL_BIAS_WIDTH]
    blk = lax.broadcasted_iota(jnp.int32, (SEL_BIAS_WIDTH, tq), 0)
    cur = (q0 + lax.broadcasted_iota(jnp.int32, (SEL_BIAS_WIDTH, tq), 1)) // SEL_LEN
    valid = blk <= cur
    forced = (blk == 0) | (blk == cur) | (blk == cur - 1)
    cand = valid & jnp.logical_not(forced)
    n_forced = jnp.minimum(cur, 2) + 1
    val = jnp.where(cand, imp, -1.0)
    cnt = jnp.zeros((SEL_BIAS_WIDTH, tq), F32)
    for j in range(n_sel):
        vj = jnp.broadcast_to(val[j:j + 1, :], (SEL_BIAS_WIDTH, tq))
        cnt = cnt + jnp.where(blk > j, jnp.where(vj >= val, 1.0, 0.0), jnp.where(vj > val, 1.0, 0.0))
    free = (min(SEL_TOPN, n_sel) - n_forced).astype(F32)
    sel = (forced & valid) | (cand & (cnt < free))
    sel_t = jnp.concatenate([jnp.where(sel, 1.0, 0.0), jnp.zeros((LANES - SEL_BIAS_WIDTH, tq), F32)], axis=0)
    sel_q = sel_t.T
    bias = jnp.where(sel_q > 0.5, 0.0, SEL_NEG)
    lane = lax.broadcasted_iota(jnp.int32, (tq, LANES), 1)
    bias_hi = pltpu.roll(bias, SEL_BIAS_WIDTH, 1)
    qr = qr_ref[0]
    for g in range(G):
        qg = qr[:, g * HEAD_DIM:(g + 1) * HEAD_DIM].astype(F32)
        qg = jnp.concatenate([qg, qg], axis=-1)
        qa_ref[0, 0, g] = jnp.where(lane < HEAD_DIM, qg, bias_hi).astype(BF16)


def _overlap_t(S):
    n_cmp = (S - CMP_LEN) // CMP_STRIDE + 1
    nc = S // CMP_STRIDE
    n_sel = S // SEL_LEN
    start = np.arange(nc) * CMP_STRIDE
    end = start + CMP_LEN - 1
    sel_start = np.arange(n_sel) * SEL_LEN
    ov = (start[None, :] <= sel_start[:, None] + SEL_LEN - 1) & (end[None, :] >= sel_start[:, None])
    ov = ov & (np.arange(nc) < n_cmp)[None, :]
    out = np.zeros((LANES, nc), np.float32)
    out[:n_sel] = ov.astype(np.float32)
    return jnp.asarray(out).astype(BF16)


def _nsa_cmp_call(qn, qr, kc, vc, tq):
    B, S, _ = qn.shape
    H, G = NSA_KV_HEADS, NSA_GROUP
    nc = kc.shape[2]
    n_sel = S // SEL_LEN
    assert n_sel <= SEL_BIAS_WIDTH and nc % LANES == 0
    qspec = pl.BlockSpec((1, tq, G * HEAD_DIM), lambda b, h, i: (b, i, h))
    cspec = pl.BlockSpec((1, 1, nc, HEAD_DIM), lambda b, h, i: (b, h, 0, 0))
    return pl.pallas_call(
        functools.partial(_nsa_cmp_kernel, n_sel=n_sel),
        grid=(B, H, S // tq),
        in_specs=[qspec, qspec, cspec, cspec, pl.BlockSpec((LANES, nc), lambda b, h, i: (0, 0))],
        out_specs=[pl.BlockSpec((1, 1, G, tq, HEAD_DIM), lambda b, h, i: (b, h, 0, i, 0)),
                   pl.BlockSpec((1, 1, G, tq, LANES), lambda b, h, i: (b, h, 0, i, 0))],
        out_shape=[jax.ShapeDtypeStruct((B, H, G, S, HEAD_DIM), F32),
                   jax.ShapeDtypeStruct((B, H, G, S, LANES), BF16)],
        compiler_params=_cparams("arbitrary", "arbitrary", "arbitrary"),
        name="nsa_cmp",
    )(qn, qr, kc, vc, _overlap_t(S))


SEL_CHUNK = 4


def _nsa_flash_kernel(qa_ref, ks_ref, vs_ref, kw_ref, vw_ref, oc_ref, g_ref, o_ref, m_scr, acc_scr):
    G = NSA_GROUP
    tq = qa_ref.shape[3]
    R = G * tq
    tk = tq
    i = pl.program_id(2)
    q0 = i * tq

    def rows_of(ref, j0, nt):
        return ref[0, 0, pl.ds(pl.multiple_of(j0 * tk, tk), nt * tk), :]

    def attend(k_ref, v_ref, j0, nt, visible, state):
        v = rows_of(v_ref, j0, nt)
        kt = k_ref[0, 0, :, pl.ds(pl.multiple_of(j0 * tk, tk), nt * tk)]
        s_all = _dot(qa_ref[0, 0].reshape(R, LANES), kt)
        ss = [s_all[g * tq:(g + 1) * tq] for g in range(G)]
        if visible is not None:
            qpos = q0 + lax.broadcasted_iota(jnp.int32, (tq, nt * tk), 0)
            kpos = j0 * tk + lax.broadcasted_iota(jnp.int32, (tq, nt * tk), 1)
            mask = visible(qpos, kpos)
        ps, alphas = [], []
        for g in range(G):
            s = ss[g] if visible is None else jnp.where(mask, ss[g], MASK_NEG)
            cols = [s[:, c * LANES:(c + 1) * LANES] for c in range(nt * tk // LANES)]
            m_new = jnp.max(functools.reduce(jnp.maximum, cols), axis=-1, keepdims=True)
            if state:
                m_prev = m_scr[g]
                m_new = jnp.maximum(m_prev, m_new)
                alphas.append(jnp.exp(m_prev - m_new))
                m_scr[g] = m_new
            ps.append(jnp.concatenate([jnp.exp(c - m_new) for c in cols], axis=-1).astype(BF16))
        pv = []
        for a in range(0, G, 2):
            both = _dot(jnp.concatenate([ps[a], ps[a + 1]], axis=0), v)
            pv += [both[:tq], both[tq:]]
        if not state:
            return pv
        for g in range(G):
            acc_scr[g] = alphas[g] * acc_scr[g] + pv[g]

    def normalise(acc):
        return acc[:, :HEAD_DIM] / acc[:, HEAD_DIM:HEAD_DIM + 1]

    m_scr[...] = jnp.full((G, tq, LANES), MASK_NEG, F32)
    acc_scr[...] = jnp.zeros((G, tq, LANES), F32)
    causal = lambda qpos, kpos: kpos <= qpos

    def sel_body(c, carry):
        attend(ks_ref, vs_ref, c * SEL_CHUNK, SEL_CHUNK, None, True)
        return carry

    n_full = i // SEL_CHUNK
    lax.fori_loop(0, n_full, sel_body, 0)
    attend(ks_ref, vs_ref, n_full * SEL_CHUNK, 2, causal, True)

    @pl.when(i % SEL_CHUNK >= 2)
    def _():
        attend(ks_ref, vs_ref, n_full * SEL_CHUNK + 2, 2, causal, True)

    o_sel = normalise(acc_scr[...].reshape(R, LANES))

    assert WINDOW == 2 * tk
    band = lambda qpos, kpos: (kpos <= qpos) & (kpos > qpos - WINDOW)
    o_win = normalise(jnp.concatenate(attend(kw_ref, vw_ref, jnp.maximum(i - 2, 0), 3, band, False), axis=0))

    o_cmp = oc_ref[0, 0].reshape(R, HEAD_DIM)
    gates = g_ref[0, 0]
    outs = []
    for g in range(G):
        r = slice(g * tq, (g + 1) * tq)
        outs.append(gates[:, 3 * g:3 * g + 1] * o_cmp[r]
                    + gates[:, 3 * g + 1:3 * g + 2] * o_sel[r]
                    + gates[:, 3 * g + 2:3 * g + 3] * o_win[r])
    o_ref[0] = jnp.concatenate(outs, axis=-1).astype(BF16)


def _nsa_flash_call(qa, ks, vs, kw, vw, oc, gates):
    B, H, G, S, _ = qa.shape
    tq = WINDOW // 2
    assert (S // tq) % SEL_CHUNK == 0
    kt = pl.BlockSpec((1, 1, LANES, S), lambda b, h, i: (b, h, 0, 0))
    vv = pl.BlockSpec((1, 1, S, LANES), lambda b, h, i: (b, h, 0, 0))
    return pl.pallas_call(
        _nsa_flash_kernel,
        grid=(B, H, S // tq),
        in_specs=[pl.BlockSpec((1, 1, G, tq, LANES), lambda b, h, i: (b, h, 0, i, 0)),
                  kt, vv, kt, vv,
                  pl.BlockSpec((1, 1, G, tq, HEAD_DIM), lambda b, h, i: (b, h, 0, i, 0)),
                  pl.BlockSpec((1, 1, tq, LANES), lambda b, h, i: (b, h, i, 0))],
        out_specs=pl.BlockSpec((1, tq, G * HEAD_DIM), lambda b, h, i: (b, i, h)),
        out_shape=jax.ShapeDtypeStruct((B, S, Q_WIDTH), BF16),
        scratch_shapes=[pltpu.VMEM((G, tq, LANES), F32), pltpu.VMEM((G, tq, LANES), F32)],
        compiler_params=_cparams("arbitrary", "arbitrary", "arbitrary"),
        name="nsa_flash",
    )(qa, ks, vs, kw, vw, oc, gates)


def _gmlp_kernel(u_ref, v_ref, ws_ref, bs_ref, o_ref):
    tm = u_ref.shape[1]
    C = GM_CHUNK
    r = lax.broadcasted_iota(jnp.int32, (C, C), 0)
    c = lax.broadcasted_iota(jnp.int32, (C, C), 1)
    causal = c <= r
    ws = [jnp.where(causal, ws_ref[g], 0.0).astype(BF16) for g in range(GM_GROUPS)]
    for n in range(tm // C):
        rows = slice(n * C, (n + 1) * C)
        vn = v_ref[0, rows, :]
        mixed = jnp.concatenate(
            [_dot(ws[g], vn[:, g * GM_GROUP_DIM:(g + 1) * GM_GROUP_DIM]) for g in range(GM_GROUPS)],
            axis=-1)
        o_ref[0, rows, :] = (u_ref[0, rows, :].astype(F32) * (mixed + bs_ref[...])).astype(BF16)


def _gmlp_call(u, v, ws, bs_full, tm):
    B, S, W = u.shape
    tok = pl.BlockSpec((1, tm, W), lambda b, i: (b, i, 0))
    return pl.pallas_call(
        _gmlp_kernel,
        grid=(B, S // tm),
        in_specs=[tok, tok,
                  pl.BlockSpec((GM_GROUPS, GM_CHUNK, GM_CHUNK), lambda b, i: (0, 0, 0)),
                  pl.BlockSpec((GM_CHUNK, W), lambda b, i: (0, 0))],
        out_specs=tok,
        out_shape=jax.ShapeDtypeStruct((B, S, W), BF16),
        compiler_params=_cparams("arbitrary", "arbitrary"),
        name="gmlp",
    )(u, v, ws, bs_full)


def _mixout_kernel(on_ref, og_ref, ga_ref, gb_ref, x_ref, gt_ref, sc_ref, sh_ref,
                   wpn_ref, wpg_ref, wo_ref, wr_ref, wgs_ref, wus_ref, wds_ref,
                   x1_ref, h2_ref, lg_ref, shd_ref):
    ya = _dot(on_ref[0], wpn_ref[...])
    yb = _dot(og_ref[0], wpg_ref[...])
    merged = ga_ref[0].astype(F32) * ya + gb_ref[0].astype(F32) * yb
    x1 = x_ref[0] + gt_ref[0] * _dot(merged.astype(BF16), wo_ref[...])
    x1_ref[0] = x1
    ms = jnp.mean(x1 * x1, axis=-1, keepdims=True)
    h2 = (x1 * lax.rsqrt(ms + EPS)) * (1.0 + sc_ref[0]) + sh_ref[0]
    hb = h2.astype(BF16)
    lg_ref[0] = _dot_nt(wr_ref[...], hb)
    h2_ref[0] = _pack_halves(hb)
    act = _silu(_dot(hb, wgs_ref[...])) * _dot(hb, wus_ref[...])
    shd_ref[0] = _dot(act.astype(BF16), wds_ref[...])


def _mixout_call(o_nsa, o_gm, ga, gb, x, gt, sc, sh, wpn, wpg, wo, wr, wgs, wus, wds, tm):
    B, S, D = x.shape
    tok = lambda w: pl.BlockSpec((1, tm, w), lambda b, i: (b, i, 0))
    per_b = pl.BlockSpec((1, 1, D), lambda b, i: (b, 0, 0))
    full = lambda a: pl.BlockSpec(a.shape, lambda b, i: (0,) * a.ndim)
    sds = jax.ShapeDtypeStruct
    return pl.pallas_call(
        _mixout_kernel,
        grid=(B, S // tm),
        in_specs=[tok(Q_WIDTH), tok(GM_WIDTH), tok(D), tok(D), tok(D), per_b, per_b, per_b,
                  full(wpn), full(wpg), full(wo), full(wr), full(wgs), full(wus), full(wds)],
        out_specs=[tok(D), tok(D // 2), pl.BlockSpec((1, N_EXPERTS, tm), lambda b, i: (b, 0, i)), tok(D)],
        out_shape=[sds((B, S, D), F32), sds((B, S, D // 2), jnp.uint32), sds((B, N_EXPERTS, S), F32),
                   sds((B, S, D), F32)],
        compiler_params=_cparams("arbitrary", "arbitrary"),
        name="mix_out",
    )(o_nsa, o_gm, ga, gb, x, gt, sc, sh, wpn, wpg, wo, wr, wgs, wus, wds)


def _route_kernel(lg_ref, br_ref, idx_ref, wt_ref, rank_ref, cnt_ref, run_scr):
    E, tm = lg_ref.shape[1], lg_ref.shape[2]

    @pl.when(pl.program_id(0) == 0)
    def _():
        run_scr[...] = jnp.zeros_like(run_scr)

    aff = _sigmoid(lg_ref[0])
    work = aff + br_ref[...]
    row = lax.broadcasted_iota(jnp.int32, (E, tm), 0).astype(F32)
    picked = jnp.zeros((E, tm), F32)
    idxs, tops = [], []
    for _ in range(TOP_K):
        m = jnp.max(work, axis=0, keepdims=True)
        idx = jnp.min(jnp.where(work == m, row, float(E)), axis=0, keepdims=True)
        hit = row == idx
        tops.append(jnp.sum(jnp.where(hit, aff, 0.0), axis=0, keepdims=True))
        idxs.append(idx)
        picked = jnp.where(hit, 1.0, picked)
        work = jnp.where(hit, -jnp.inf, work)
    total = functools.reduce(jnp.add, tops)
    r = lax.broadcasted_iota(jnp.int32, (tm, tm), 0)
    c = lax.broadcasted_iota(jnp.int32, (tm, tm), 1)
    before = _dot(picked.astype(BF16), jnp.where(r < c, 1.0, 0.0).astype(BF16)) + run_scr[...]
    ranks = [jnp.sum(jnp.where(row == idx, before, 0.0), axis=0, keepdims=True) for idx in idxs]
    run_scr[...] = run_scr[...] + jnp.sum(picked, axis=1, keepdims=True)
    cnt_ref[...] = run_scr[...]
    idx_ref[...] = jnp.concatenate(idxs, axis=0).astype(jnp.int32)
    rank_ref[...] = jnp.concatenate(ranks, axis=0).astype(jnp.int32)
    wt = jnp.concatenate([t / total * ROUTE_SCALE for t in tops]
                         + [jnp.zeros((LANES - TOP_K, tm), F32)], axis=0)
    eye = jnp.where(r == c, 1.0, 0.0).astype(BF16)
    cols = jnp.zeros((tm, LANES), F32)
    rest = wt
    for _ in range(3):
        part = rest.astype(BF16)
        cols = cols + _dot_nt(eye, part)
        rest = rest - part.astype(F32)
    wt_ref[...] = cols


def _route_call(logits_t, b_router, tm):
    B, E, S = logits_t.shape
    per_b = S // tm
    T = B * S
    kt = pl.BlockSpec((TOP_K, tm), lambda i: (0, i))
    col = pl.BlockSpec((E, 1), lambda i: (0, 0))
    sds = jax.ShapeDtypeStruct
    return pl.pallas_call(
        _route_kernel,
        grid=(T // tm,),
        in_specs=[pl.BlockSpec((1, E, tm), lambda i: (i // per_b, 0, i % per_b)), col],
        out_specs=[kt, pl.BlockSpec((tm, LANES), lambda i: (i, 0)), kt, col],
        out_shape=[sds((TOP_K, T), jnp.int32), sds((T, LANES), F32), sds((TOP_K, T), jnp.int32),
                   sds((E, 1), F32)],
        scratch_shapes=[pltpu.VMEM((E, 1), F32)],
        compiler_params=_cparams("arbitrary"),
        name="route",
    )(logits_t, b_router.reshape(E, 1))


def _slot_kernel(idx_ref, rank_ref, ps_ref, o_ref):
    tm = idx_ref.shape[1]
    E = ps_ref.shape[0]
    row = lax.broadcasted_iota(jnp.int32, (E, tm), 0)
    idx, rank, ps = idx_ref[...], rank_ref[...], ps_ref[...]
    base = [jnp.sum(jnp.where(row == idx[k:k + 1, :], ps, 0.0), axis=0, keepdims=True) for k in range(TOP_K)]
    o_ref[...] = jnp.concatenate(base, axis=0).astype(jnp.int32) + rank


def _slot_call(idx, rank, pad_start, tm):
    T = idx.shape[1]
    E = pad_start.shape[0]
    kt = pl.BlockSpec((TOP_K, tm), lambda i: (0, i))
    return pl.pallas_call(
        _slot_kernel,
        grid=(T // tm,),
        in_specs=[kt, kt, pl.BlockSpec((E, 1), lambda i: (0, 0))],
        out_specs=kt,
        out_shape=jax.ShapeDtypeStruct((TOP_K, T), jnp.int32),
        compiler_params=_cparams("arbitrary"),
        name="slot",
    )(idx, rank, pad_start.astype(F32).reshape(E, 1))


def _sc_mesh():
    mesh = plsc.VectorSubcoreMesh(core_axis_name="core", subcore_axis_name="subcore")
    return mesh, mesh.num_cores * mesh.num_subcores


def _sc_worker(mesh):
    return lax.axis_index("subcore") * mesh.num_cores + lax.axis_index("core")


def _scatter_rows(rows, indices_by_k, out_ref):
    n, width = rows.shape
    mesh, workers = _sc_mesh()
    per_worker = n // workers
    assert per_worker * workers == n and per_worker % SC_WINDOW == 0

    @pl.kernel(out_type=(), mesh=mesh, name="scatter_rows",
               scratch_types=[pltpu.VMEM((SC_WINDOW,), jnp.int32), pltpu.VMEM((SC_WINDOW, width), rows.dtype)])
    def scatter(rows_hbm, idx_hbm, out_hbm, idx_v, rows_v):
        worker = _sc_worker(mesh)

        @pl.loop(0, per_worker // SC_WINDOW)
        def _(j):
            base = pl.multiple_of(worker * per_worker + j * SC_WINDOW, SC_WINDOW)
            pltpu.sync_copy(rows_hbm.at[pl.ds(base, SC_WINDOW)], rows_v)
            for k in range(TOP_K):
                pltpu.sync_copy(idx_hbm.at[pl.ds(pl.multiple_of(k * n + base, SC_WINDOW), SC_WINDOW)], idx_v)
                pltpu.sync_copy(rows_v, out_hbm.at[idx_v])

    scatter(rows, indices_by_k, out_ref)


def _expert_kernel(ie_ref, ir_ref, ig_ref, n_ref, x_hbm, wg_ref, wu_ref, wd_ref, y_hbm,
                   xbuf, ybuf, act_scr, xsem, ysem):
    w = pl.program_id(0)
    n = n_ref[0]
    R = xbuf.shape[1]
    last = ie_ref.shape[0] - 1

    def x_copy(item, slot):
        row = pl.multiple_of(ir_ref[jnp.minimum(item, last)], MOE_GRAN)
        return pltpu.make_async_copy(x_hbm.at[pl.ds(row, R), :], xbuf.at[slot], xsem.at[slot])

    def y_copy(item, slot, gi):
        row = pl.multiple_of(ir_ref[jnp.minimum(item, last)] + gi * MOE_GRAN, MOE_GRAN)
        src = ybuf.at[slot, pl.ds(pl.multiple_of(gi * MOE_GRAN, MOE_GRAN), MOE_GRAN), :]
        return pltpu.make_async_copy(src, y_hbm.at[pl.ds(row, MOE_GRAN), :], ysem.at[slot])

    def for_granules(item, fn):
        def body(gi, carry):
            fn(gi)
            return carry
        lax.fori_loop(0, ig_ref[jnp.clip(item, 0, last)], body, 0)

    def gate_up(slot):
        lo, hi = _unpack_halves(xbuf[slot])
        x = jnp.concatenate([lo, hi], axis=1).astype(BF16)
        a = _dot(x, wg_ref[0, 0].astype(BF16))
        b = _dot(x, wu_ref[0, 0].astype(BF16))
        return (_silu(a) * b).astype(BF16)

    def down(act):
        return _pack_halves(_dot(act, wd_ref[0, 0].astype(BF16)))

    nx = xbuf.shape[0]
    cur, prv = w % nx, (w + 1) % 2

    @pl.when(w == 0)
    def _():
        for j in range(nx - 1):
            @pl.when(j < n)
            def _():
                x_copy(j, j).start()

    @pl.when(w < n)
    def _():
        x_copy(w, cur).wait()

    @pl.when(w + nx - 1 < n)
    def _():
        x_copy(w + nx - 1, (w + nx - 1) % nx).start()

    @pl.when((w >= 3) & (w - 3 < n))
    def _():
        for_granules(w - 3, lambda gi: y_copy(w - 3, prv, gi).wait())

    @pl.when(w == 0)
    def _():
        act_scr[...] = gate_up(0)

    @pl.when((w >= 1) & (w < n))
    def _():
        prev = act_scr[...]
        ybuf[prv] = down(prev)
        act_scr[...] = gate_up(cur)

    @pl.when((w >= 1) & (w == n))
    def _():
        ybuf[prv] = down(act_scr[...])

    @pl.when((w >= 1) & (w <= n))
    def _():
        for_granules(w - 1, lambda gi: y_copy(w - 1, prv, gi).start())


def _expert_call(layer, item_e, item_row, item_ng, n_items, xs, wg, wu, wd):
    n_alloc, W = xs.shape
    R = MOE_ROWS
    nw = item_e.shape[0]
    D, F = wg.shape[2], wg.shape[3]
    cur = lambda w, ie, ir, ig, n: (layer, ie[jnp.minimum(w, nw - 1)], 0, 0)
    prev = lambda w, ie, ir, ig, n: (layer, ie[jnp.clip(w - 1, 0, nw - 1)], 0, 0)
    grid_spec = pltpu.PrefetchScalarGridSpec(
        num_scalar_prefetch=4,
        grid=(nw + 3,),
        in_specs=[pl.BlockSpec(memory_space=pl.ANY),
                  pl.BlockSpec((1, 1, D, F), cur), pl.BlockSpec((1, 1, D, F), cur),
                  pl.BlockSpec((1, 1, F, D), prev)],
        out_specs=pl.BlockSpec(memory_space=pl.ANY),
        scratch_shapes=[pltpu.VMEM((3, R, W), jnp.uint32), pltpu.VMEM((2, R, W), jnp.uint32),
                        pltpu.VMEM((R, F), BF16),
                        pltpu.SemaphoreType.DMA((3,)), pltpu.SemaphoreType.DMA((2,))],
    )
    return pl.pallas_call(
        _expert_kernel,
        grid_spec=grid_spec,
        out_shape=jax.ShapeDtypeStruct((n_alloc, W), jnp.uint32),
        input_output_aliases={4: 0},
        compiler_params=pltpu.CompilerParams(dimension_semantics=("arbitrary",), vmem_limit_bytes=VMEM_LIMIT,
                                             has_side_effects=True),
        name="expert",
    )(item_e, item_row, item_ng, n_items, xs, wg, wu, wd)


def _gather_rows(table, indices):
    n = indices.shape[0]
    width = table.shape[1]
    mesh, workers = _sc_mesh()
    per_worker = n // workers
    assert per_worker * workers == n and per_worker % SC_WINDOW == 0

    half = SC_WINDOW // 2
    steps = per_worker // SC_WINDOW
    rows_t = pltpu.VMEM((half, width), table.dtype)

    @pl.kernel(out_type=jax.ShapeDtypeStruct((n, width), table.dtype), mesh=mesh, name="gather_rows",
               scratch_types=[pltpu.VMEM((SC_WINDOW,), jnp.int32), rows_t, rows_t] + [pltpu.SemaphoreType.DMA] * 4)
    def gather(table_hbm, idx_hbm, out_hbm, idx_v, rows_a, rows_b, gsem_a, gsem_b, wsem_a, wsem_b):
        worker = _sc_worker(mesh)
        first = worker * per_worker
        halves = ((rows_a, gsem_a, wsem_a, 0), (rows_b, gsem_b, wsem_b, half))

        def write(buf, wsem, row):
            return pltpu.make_async_copy(buf, out_hbm.at[pl.ds(pl.multiple_of(row, half), half)], wsem)

        @pl.loop(0, steps)
        def _(j):
            base = pl.multiple_of(first + j * SC_WINDOW, SC_WINDOW)
            pltpu.sync_copy(idx_hbm.at[pl.ds(base, SC_WINDOW)], idx_v)
            gathers = []
            for buf, gsem, wsem, off in halves:
                @pl.when(j > 0)
                def _():
                    write(buf, wsem, base).wait()
                gathers.append(pltpu.async_copy(table_hbm.at[idx_v.at[pl.ds(off, half)]], buf, gsem))
            for (buf, gsem, wsem, off), g in zip(halves, gathers):
                g.wait()
                write(buf, wsem, base + off).start()

        for buf, gsem, wsem, off in halves:
            write(buf, wsem, first).wait()

    return gather(table, indices)


def _combine_kernel(y_ref, wt_ref, shd_ref, x_ref, gt_ref, o_ref):
    wt = wt_ref[...]
    acc_lo = acc_hi = None
    for k in range(TOP_K):
        lo, hi = _unpack_halves(y_ref[k])
        w = wt[:, k:k + 1]
        acc_lo = w * lo if k == 0 else acc_lo + w * lo
        acc_hi = w * hi if k == 0 else acc_hi + w * hi
    routed = jnp.concatenate([acc_lo, acc_hi], axis=1)
    o_ref[...] = x_ref[...] + gt_ref[0] * (routed + shd_ref[...])


def _combine_call(yk, wts, shared, x1, gt, tm):
    T, D = x1.shape
    B = gt.shape[0]
    per_b = T // B // tm
    tok = lambda w: pl.BlockSpec((tm, w), lambda i: (i, 0))
    return pl.pallas_call(
        _combine_kernel,
        grid=(T // tm,),
        in_specs=[pl.BlockSpec((TOP_K, tm, yk.shape[2]), lambda i: (0, i, 0)),
                  tok(LANES), tok(D), tok(D),
                  pl.BlockSpec((1, 1, D), lambda i: (i // per_b, 0, 0))],
        out_specs=tok(D),
        out_shape=jax.ShapeDtypeStruct((T, D), F32),
        compiler_params=_cparams("arbitrary"),
        name="combine",
    )(yk, wts, shared, x1, gt)


def _rope_tables(positions):
    half = HEAD_DIM // 2
    inv = ROPE_THETA ** (-jnp.arange(half, dtype=F32) / half)
    ang = positions.astype(F32)[..., None] * inv
    cos, sin, zero = jnp.cos(ang), jnp.sin(ang), jnp.zeros_like(ang)
    cos_t = jnp.concatenate([cos, cos] * 2, axis=-1)
    sin_lo = jnp.concatenate([-sin, zero] * 2, axis=-1)
    sin_hi = jnp.concatenate([zero, sin] * 2, axis=-1)
    return cos_t, sin_lo, sin_hi


def _reorder_w_in(w):
    o = np.cumsum([0, Q_WIDTH] + [KV_WIDTH] * 6 + [NSA_Q_HEADS * 3, GM_WIDTH, GM_WIDTH, D_MODEL, D_MODEL])
    q, kc, vc, ks, vs, kw, vw, g, u, v, ga, gb = [w[:, o[i]:o[i + 1]] for i in range(12)]
    per = NSA_GROUP * 3
    pad = jnp.zeros((w.shape[0], LANES - per), w.dtype)
    return jnp.concatenate([q, ks, kw, kc, vc, vs, vw, g[:, :per], pad, g[:, per:], pad, u, v, ga, gb], axis=1)


def _owner(ends, pos):
    return jnp.minimum(jnp.sum((ends[None, :] <= pos[:, None]).astype(jnp.int32), axis=1), ends.shape[0] - 1)


def _lookup(table, idx):
    hit = idx[:, None] == jnp.arange(table.shape[0], dtype=jnp.int32)[None, :]
    return jnp.sum(jnp.where(hit, table[None, :], 0), axis=1)


def _expert_plan(counts, n_items_max):
    per_item = MOE_ROWS // MOE_GRAN
    counts = counts.astype(jnp.int32)
    gran = (counts + MOE_GRAN - 1) // MOE_GRAN
    gran_start = jnp.cumsum(gran) - gran
    items = (gran + per_item - 1) // per_item
    item_end = jnp.cumsum(items)
    w = jnp.arange(n_items_max, dtype=jnp.int32)
    ie = _owner(item_end, w)
    part = w - (_lookup(item_end, ie) - _lookup(items, ie))
    live = w < item_end[-1]
    item_ng = jnp.where(live, jnp.clip(_lookup(gran, ie) - part * per_item, 0, per_item), 0).astype(jnp.int32)
    item_row = jnp.where(live, (_lookup(gran_start, ie) + part * per_item) * MOE_GRAN, 0).astype(jnp.int32)
    return gran_start * MOE_GRAN, ie.astype(jnp.int32), item_row, item_ng, item_end[-1:].astype(jnp.int32)


def kernel(x, c, positions, w_mod, b_mod, w_in, q_gain, k_gain, cmp_pos_k, cmp_pos_v, cmp_w1_k, cmp_w2_k, cmp_w1_v, cmp_w2_v, gm_ln_g, gm_ln_b, gm_ws, gm_bs, w_proj_nsa, w_proj_gm, w_out, w_router, b_router, w_gate_e, w_up_e, w_down_e, w_gate_sh, w_up_sh, w_down_sh):
    B, S, D = x.shape
    L = w_mod.shape[0]
    T = B * S
    tm = 512
    tm_dense = 512
    scale = HEAD_DIM ** -0.5
    cos_t, sin_lo, sin_hi = _rope_tables(positions)
    mod = _mod_call(c, w_mod, b_mod)
    n_alloc = T * TOP_K + N_EXPERTS * MOE_GRAN + MOE_ROWS
    n_items_max = N_EXPERTS + T * TOP_K // MOE_ROWS + 1
    rows_buf = jnp.zeros((n_alloc, D // 2), jnp.uint32)

    for l in range(L):
        sh_a, sc_a, gt_a, sh_f, sc_f, gt_f = [mod[l, :, i * D:(i + 1) * D].reshape(B, 1, D) for i in range(6)]
        qg = (jnp.tile(q_gain[l], NSA_Q_HEADS) * scale).reshape(1, Q_WIDTH)
        kg = jnp.tile(k_gain[l], 2 * NSA_KV_HEADS).reshape(1, 2 * KV_WIDTH)
        (qn, qr, ks, kw, vs, vw, kc_raw, vc_raw, gates, u, v, ga, gb) = _inproj_call(
            x, sc_a, sh_a, _reorder_w_in(w_in[l]).astype(BF16), qg, kg, cos_t, sin_lo, sin_hi,
            gm_ln_g[l].reshape(1, GM_WIDTH), gm_ln_b[l].reshape(1, GM_WIDTH), tm_dense)
        kc, vc = _compress_call(
            kc_raw, vc_raw, cmp_w1_k[l].astype(BF16), cmp_w2_k[l].astype(BF16), cmp_pos_k[l].reshape(1, -1),
            cmp_w1_v[l].astype(BF16), cmp_w2_v[l].astype(BF16), cmp_pos_v[l].reshape(1, -1),
            k_gain[l].reshape(1, HEAD_DIM))
        o_cmp, q_aug = _nsa_cmp_call(qn, qr, kc, vc, WINDOW // 2)
        o_nsa = _nsa_flash_call(q_aug, ks, vs, kw, vw, o_cmp, gates)
        bs_full = jnp.repeat(gm_bs[l].T, GM_GROUP_DIM, axis=1)
        o_gm = _gmlp_call(u, v, gm_ws[l], bs_full, 512)
        x1, h2, logits, shared = _mixout_call(
            o_nsa, o_gm, ga, gb, x, gt_a, sc_f, sh_f,
            w_proj_nsa[l].astype(BF16), w_proj_gm[l].astype(BF16), w_out[l].astype(BF16), w_router[l].T.astype(BF16),
            w_gate_sh[l].astype(BF16), w_up_sh[l].astype(BF16), w_down_sh[l].astype(BF16), tm_dense)
        idx, wts, rank, counts = _route_call(logits, b_router[l], tm)
        row_start, item_e, item_row, item_ng, n_items = _expert_plan(counts[:, 0], n_items_max)
        slots_by_k = _slot_call(idx, rank, row_start, tm).reshape(TOP_K * T)
        xs_ref = jax.new_ref(rows_buf)
        _scatter_rows(h2.reshape(T, D // 2), slots_by_k, xs_ref)
        y = _expert_call(l, item_e, item_row, item_ng, n_items, jax.freeze(xs_ref), w_gate_e, w_up_e, w_down_e)
        yk = _gather_rows(y, slots_by_k).reshape(TOP_K, T, D // 2)
        rows_buf = y
        x = _combine_call(yk, wts, shared.reshape(T, D), x1.reshape(T, D), gt_f, tm).reshape(B, S, D)
    return x
```
